```python
import math
import jax
import jax.numpy as jnp
from jax import lax
import numpy as np

D_MODEL = 1024
BATCH = 8
SEQ = 2048
DEPTH = 4
DEC_BATCH = 128
DEC_SEQ = 8
PAST_LEN = 16384
PAGE_SIZE = 128

D_MIX = D_MODEL
D_A = D_MIX // 2
D_B = D_MIX - D_A
DH_A = 128
H_A = D_A // DH_A
DH_B = 64
H_B = D_B // DH_B
CONV_W = 4
CHUNK = 64
R_DECAY = 64
R_AAA = 64
R_GATE = 128
D_A_IN = 4 * D_A + 2 * H_A
D_B_IN = 3 * D_B + R_DECAY + R_AAA + R_GATE
D_IN = D_A_IN + D_B_IN
N_EXPERTS = 16
N_GROUPS = 4
EXPERTS_PER_GROUP = N_EXPERTS // N_GROUPS
TOP_K = 2
D_EXP = 512
ALPHA = (2 * DEPTH) ** 0.25
BETA = (8 * DEPTH) ** -0.25
LN_EPS = 1e-5
GN_EPS_A = 1e-6
GN_EPS_B = 64e-5

kernel_name = 'hybrid_mlstm_rwkv7_moe_step'


def layer_norm(x, g, b, eps=LN_EPS):
    xf = x.astype(jnp.float32)
    mu = xf.mean(-1, keepdims=True)
    var = jnp.square(xf - mu).mean(-1, keepdims=True)
    return ((xf - mu) * lax.rsqrt(var + eps)).astype(x.dtype) * g + b


def head_norm(x, eps):
    xf = x.astype(jnp.float32)
    mu = xf.mean(-1, keepdims=True)
    var = jnp.square(xf - mu).mean(-1, keepdims=True)
    return (xf - mu) * lax.rsqrt(var + eps)


def causal_conv(u, buf, w, b):
    L = u.shape[1]
    full = jnp.concatenate([buf.astype(u.dtype), u], axis=1)
    out = b + sum(full[:, j:j + L] * w[j] for j in range(CONV_W))
    return out, full[:, L:]


def token_shift(p, prev, mu):
    prev_seq = jnp.concatenate([prev[:, None].astype(p.dtype), p[:, :-1]], axis=1)
    return p + (prev_seq - p) * mu, p[:, -1]


def mlstm_chunked(q, k, v, li, lf, C0, n0, m0):
    f32 = jnp.float32
    B, L, H, dh = q.shape
    c = math.gcd(L, CHUNK)
    nc = L // c
    to_chunks = lambda t: t.astype(f32).reshape(B, nc, c, H, -1).transpose(1, 0, 3, 2, 4)
    gate_chunks = lambda t: t.astype(f32).reshape(B, nc, c, H).transpose(1, 0, 3, 2)
    causal = jnp.tril(jnp.ones((c, c), bool))

    def step(carry, inp):
        C, n, m = carry
        qc, kc, vc, lic, lfc = inp
        b = jnp.cumsum(lfc, -1)
        dmat = jnp.where(causal, b[..., :, None] - b[..., None, :] + lic[..., None, :], -jnp.inf)
        m_inter = b + m[..., None]
        m_t = jnp.maximum(m_inter, dmat.max(-1))
        s = jnp.einsum('bhtd,bhsd->bhts', qc, kc) * jnp.exp(dmat - m_t[..., None])
        w_inter = jnp.exp(m_inter - m_t)
        num = w_inter[..., None] * jnp.einsum('bhtd,bhde->bhte', qc, C) + jnp.einsum('bhts,bhse->bhte', s, vc)
        den = w_inter * jnp.einsum('bhtd,bhd->bht', qc, n) + s.sum(-1)
        h = num / jnp.maximum(jnp.abs(den), jnp.exp(-m_t))[..., None]
        b_last = b[..., -1]
        g_s = b_last[..., None] - b + lic
        m_new = jnp.maximum(b_last + m, g_s.max(-1))
        w_s = jnp.exp(g_s - m_new[..., None])
        w_old = jnp.exp(b_last + m - m_new)
        C = w_old[..., None, None] * C + jnp.einsum('bhs,bhsd,bhse->bhde', w_s, kc, vc)
        n = w_old[..., None] * n + jnp.einsum('bhs,bhsd->bhd', w_s, kc)
        return (C, n, m_new), h

    (C, n, m), h = lax.scan(step, (C0.astype(f32), n0.astype(f32), m0.astype(f32)),
                            (to_chunks(q), to_chunks(k), to_chunks(v), gate_chunks(li), gate_chunks(lf)))
    h = h.transpose(1, 0, 3, 2, 4).reshape(B, L, H, dh)
    return h, C, n, m


def rwkv7_recurrence(r, decay, k, v, kk, a, S0):
    f32 = jnp.float32
    seq = tuple(jnp.moveaxis(t.astype(f32), 1, 0) for t in (r, decay, k, v, kk, a))

    def step(S, inp):
        r_t, w_t, k_t, v_t, kk_t, a_t = inp
        sa = jnp.einsum('bhvk,bhk->bhv', S, -kk_t)
        S = (S * w_t[..., None, :] + sa[..., :, None] * (kk_t * a_t)[..., None, :]
             + v_t[..., :, None] * k_t[..., None, :])
        return S, jnp.einsum('bhvk,bhk->bhv', S, r_t)

    S, out = lax.scan(step, S0.astype(f32), seq)
    return jnp.moveaxis(out, 0, 1), S


def mixer_block(x, C0, n0, m0, conv0, S0, shift0, w_in, conv_w, conv_b, b_i, b_f, gn_a_g,
                mu_shift, w0, w2, a0, a2, g2, k_k, k_a, r_k, lnx_g, lnx_b, w_out):
    f32 = jnp.float32
    B, L, _ = x.shape
    p = x @ w_in
    pa = p[..., :D_A_IN]
    qk, conv_new = causal_conv(pa[..., :2 * D_A], conv0, conv_w, conv_b)
    qk = jax.nn.silu(qk)
    q = qk[..., :D_A].reshape(B, L, H_A, DH_A)
    k = qk[..., D_A:].reshape(B, L, H_A, DH_A) * (DH_A ** -0.5)
    v = pa[..., 2 * D_A:3 * D_A].reshape(B, L, H_A, DH_A)
    o_pre = pa[..., 3 * D_A:4 * D_A]
    i_pre = (pa[..., 4 * D_A:4 * D_A + H_A] + b_i).astype(f32)
    f_pre = (pa[..., 4 * D_A + H_A:] + b_f).astype(f32)
    h, C, n, m = mlstm_chunked(q, k, v, i_pre, jax.nn.log_sigmoid(f_pre), C0, n0, m0)
    h_a = (head_norm(h, GN_EPS_A).reshape(B, L, D_A).astype(x.dtype) * gn_a_g) * jax.nn.sigmoid(o_pre)
    pb, shift_new = token_shift(p[..., D_A_IN:], shift0, mu_shift)
    r = pb[..., :D_B]
    kr = pb[..., D_B:2 * D_B]
    vr = pb[..., 2 * D_B:3 * D_B]
    o = 3 * D_B
    wl = pb[..., o:o + R_DECAY]
    al = pb[..., o + R_DECAY:o + R_DECAY + R_AAA]
    gl = pb[..., o + R_DECAY + R_AAA:]
    wlog = (-jax.nn.softplus(-(w0 + jnp.tanh(wl) @ w2)) - 0.5).astype(f32)
    decay = jnp.exp(-jnp.exp(wlog))
    a = jax.nn.sigmoid(a0 + al @ a2)
    g = jax.nn.sigmoid(gl) @ g2
    heads = lambda t: t.reshape(B, L, H_B, DH_B)
    kk = heads(kr * k_k).astype(f32)
    kk = kk / jnp.maximum(jnp.sqrt(jnp.sum(kk * kk, -1, keepdims=True)), 1e-12)
    kr = kr * (1 + (a - 1) * k_a)
    rh, kh, vh = heads(r), heads(kr), heads(vr)
    out, S = rwkv7_recurrence(rh, heads(decay), kh, vh, kk, heads(a), S0)
    out = (head_norm(out, GN_EPS_B).reshape(B, L, D_B).astype(x.dtype) * lnx_g + lnx_b)
    bonus = (jnp.sum(rh * kh * r_k, -1, keepdims=True) * vh).reshape(B, L, D_B)
    h_b = (out + bonus) * g
    y = jnp.concatenate([h_a, h_b], axis=-1) @ w_out
    sd = C0.dtype
    return y, (C.astype(sd), n.astype(sd), m.astype(sd), conv_new.astype(sd), S.astype(sd), shift_new.astype(sd))


def moe(x, w_router, b_router, wg, wu, wd):
    f32 = jnp.float32
    B, L, D = x.shape
    xf = x.reshape(B * L, D)
    probs = jax.nn.softmax((xf @ w_router).astype(f32), axis=-1)
    sel = (probs + b_router.astype(f32)).reshape(-1, N_GROUPS, EXPERTS_PER_GROUP)
    grp_score = lax.top_k(sel, TOP_K)[0].sum(-1)
    g_idx = jnp.argmax(grp_score, axis=-1)
    in_grp = jnp.take_along_axis(sel, g_idx[:, None, None], axis=1)[:, 0]
    _, local = lax.top_k(in_grp, TOP_K)
    e_idx = g_idx[:, None] * EXPERTS_PER_GROUP + local
    wts = jnp.take_along_axis(probs, e_idx, axis=1)
    wts = wts / wts.sum(-1, keepdims=True)
    gate = jnp.einsum('nk,nke->ne', wts, jax.nn.one_hot(e_idx, N_EXPERTS, dtype=f32)).astype(x.dtype)
    y = jnp.zeros_like(xf)
    for e in range(N_EXPERTS):
        hid = jax.nn.silu(xf @ wg[e]) * (xf @ wu[e])
        y = y + gate[:, e:e + 1] * (hid @ wd[e])
    return y.reshape(B, L, D)


def setup_inputs(seed: int = 0) -> dict:
    key = jax.random.key(seed)
    ks = iter(jax.random.split(key, 48))
    f32 = jnp.float32
    nrm = lambda shape, s=1.0: jax.random.normal(next(ks), shape, f32) * s
    return {
        'x_prompt': nrm((BATCH, SEQ, D_MODEL)),
        'x_sample': nrm((DEC_BATCH, DEC_SEQ, D_MODEL)),
        'state_mlstm_C': nrm((DEPTH, DEC_BATCH, H_A, DH_A, DH_A), 0.1),
        'state_mlstm_n': nrm((DEPTH, DEC_BATCH, H_A, DH_A), 0.1),
        'state_mlstm_m': nrm((DEPTH, DEC_BATCH, H_A)),
        'state_mlstm_conv': nrm((DEPTH, DEC_BATCH, CONV_W - 1, 2 * D_A)),
        'state_rwkv_S': nrm((DEPTH, DEC_BATCH, H_B, DH_B, DH_B), 0.1),
        'state_rwkv_shift': nrm((DEPTH, DEC_BATCH, D_B_IN)),
        'ln0_g': 1.0 + nrm((D_MODEL,), 0.02),
        'ln0_b': nrm((D_MODEL,), 0.02),
        'w_in': nrm((DEPTH, D_MODEL, D_IN), D_MODEL ** -0.5),
        'conv_w': nrm((DEPTH, CONV_W, 2 * D_A), CONV_W ** -0.5),
        'conv_b': nrm((DEPTH, 2 * D_A), 0.02),
        'b_i': nrm((DEPTH, H_A), 0.1),
        'b_f': 3.0 + nrm((DEPTH, H_A), 0.5),
        'gn_a_g': 1.0 + nrm((DEPTH, D_A), 0.02),
        'mu_shift': jax.random.uniform(next(ks), (DEPTH, D_B_IN), f32),
        'w0': jax.random.uniform(next(ks), (DEPTH, D_B), f32, minval=-4.0, maxval=0.0),
        'w2': nrm((DEPTH, R_DECAY, D_B), 0.5 * R_DECAY ** -0.5),
        'a0': nrm((DEPTH, D_B), 0.1),
        'a2': nrm((DEPTH, R_AAA, D_B), 0.5 * R_AAA ** -0.5),
        'g2': nrm((DEPTH, R_GATE, D_B), R_GATE ** -0.5),
        'k_k': 1.0 + nrm((DEPTH, D_B), 0.1),
        'k_a': 1.0 + nrm((DEPTH, D_B), 0.1),
        'r_k': nrm((DEPTH, H_B, DH_B), 0.1),
        'lnx_g': 1.0 + nrm((DEPTH, D_B), 0.02),
        'lnx_b': nrm((DEPTH, D_B), 0.02),
        'w_out': nrm((DEPTH, D_MIX, D_MODEL), BETA * D_MIX ** -0.5),
        'ln1_g': 1.0 + nrm((DEPTH, D_MODEL), 0.02),
        'ln1_b': nrm((DEPTH, D_MODEL), 0.02),
        'w_router': nrm((D_MODEL, N_EXPERTS), D_MODEL ** -0.5),
        'b_router': nrm((N_EXPERTS,), 0.01),
        'we_gate': nrm((DEPTH, N_EXPERTS, D_MODEL, D_EXP), D_MODEL ** -0.5),
        'we_up': nrm((DEPTH, N_EXPERTS, D_MODEL, D_EXP), D_MODEL ** -0.5),
        'we_down': nrm((DEPTH, N_EXPERTS, D_EXP, D_MODEL), BETA * D_EXP ** -0.5),
        'ln2_g': 1.0 + nrm((DEPTH, D_MODEL), 0.02),
        'ln2_b': nrm((DEPTH, D_MODEL), 0.02),
    }


def reference(x_prompt, x_sample, state_mlstm_C, state_mlstm_n, state_mlstm_m, state_mlstm_conv,
              state_rwkv_S, state_rwkv_shift, ln0_g, ln0_b, w_in, conv_w, conv_b, b_i, b_f, gn_a_g,
              mu_shift, w0, w2, a0, a2, g2, k_k, k_a, r_k, lnx_g, lnx_b, w_out, ln1_g, ln1_b,
              w_router, b_router, we_gate, we_up, we_down, ln2_g, ln2_b):
    def run(x, states):
        nb = x.shape[0]
        dt = x.dtype
        x = layer_norm(x, ln0_g, ln0_b)
        outs = []
        for l in range(DEPTH):
            if states is None:
                st = (jnp.zeros((nb, H_A, DH_A, DH_A), dt), jnp.zeros((nb, H_A, DH_A), dt),
                      jnp.zeros((nb, H_A), dt), jnp.zeros((nb, CONV_W - 1, 2 * D_A), dt),
                      jnp.zeros((nb, H_B, DH_B, DH_B), dt), jnp.zeros((nb, D_B_IN), dt))
            else:
                st = tuple(s[l] for s in states)
            mix, new_st = mixer_block(x, *st, w_in[l], conv_w[l], conv_b[l], b_i[l], b_f[l], gn_a_g[l],
                                      mu_shift[l], w0[l], w2[l], a0[l], a2[l], g2[l], k_k[l], k_a[l],
                                      r_k[l], lnx_g[l], lnx_b[l], w_out[l])
            x = layer_norm(ALPHA * x + mix, ln1_g[l], ln1_b[l])
            x = layer_norm(ALPHA * x + moe(x, w_router, b_router, we_gate[l], we_up[l], we_down[l]),
                           ln2_g[l], ln2_b[l])
            outs.append(new_st)
        return x, [jnp.stack(s) for s in zip(*outs)]

    y_prompt, (p_C, p_n, p_m, p_conv, p_S, p_shift) = run(x_prompt, None)
    y_sample, (s_C, s_n, s_m, s_conv, s_S, s_shift) = run(
        x_sample, (state_mlstm_C, state_mlstm_n, state_mlstm_m, state_mlstm_conv, state_rwkv_S, state_rwkv_shift))
    return (y_prompt, y_sample, p_C, p_n, p_m, p_conv, p_S, p_shift, s_C, s_n, s_m, s_conv, s_S, s_shift)
```

```python
import functools
import math

import jax
import jax.numpy as jnp
from jax import lax
from jax.experimental import pallas as pl
from jax.experimental.pallas import tpu as pltpu

F32 = jnp.float32
BF16 = jnp.bfloat16

D_MODEL = 1024
DEPTH = 4
D_A = 512
D_B = 512
DH_A = 128
H_A = 4
DH_B = 64
H_B = 8
N_PAIRS = H_B // 2
CONV_W = 4
CHUNK = 64
R_DECAY = 64
R_AAA = 64
R_GATE = 128
D_B_IN = 3 * D_B + R_DECAY + R_AAA + R_GATE
N_EXPERTS = 16
N_GROUPS = 4
EXPERTS_PER_GROUP = 4
D_EXP = 512
ALPHA = (2 * DEPTH) ** 0.25
LN_EPS = 1e-5
GN_EPS_A = 1e-6
GN_EPS_B = 64e-5

LANES = 128
SOLVE_BLOCK = 16
VMEM_LIMIT = 48 * 1024 * 1024

NN = (((1,), (0,)), ((), ()))
NT = (((1,), (1,)), ((), ()))
TN = (((0,), (0,)), ((), ()))


def _dot(a, b, dims=NN):
    return lax.dot_general(a.astype(BF16), b.astype(BF16), dims, preferred_element_type=F32)


def _split3(x):
    hi = x.astype(BF16)
    r1 = x - hi.astype(F32)
    mid = r1.astype(BF16)
    lo = (r1 - mid.astype(F32)).astype(BF16)
    return hi, mid, lo


def _dot_exact_lhs(a, x, dims=NN):
    a = a.astype(BF16)
    hi, mid, lo = _split3(x)
    d = lambda p: lax.dot_general(a, p, dims, preferred_element_type=F32)
    return d(hi) + d(mid) + d(lo)


def _dot_exact_rhs(x, b, dims=NN):
    b = b.astype(BF16)
    hi, mid, lo = _split3(x)
    d = lambda p: lax.dot_general(p, b, dims, preferred_element_type=F32)
    return d(hi) + d(mid) + d(lo)


def _layer_norm(x, g, b, eps):
    mu = jnp.mean(x, axis=-1, keepdims=True)
    xc = x - mu
    var = jnp.mean(xc * xc, axis=-1, keepdims=True)
    return xc * lax.rsqrt(var + eps) * g + b


def _sigmoid(x):
    return 1.0 / (1.0 + jnp.exp(-x))


def _log_sigmoid(x):
    return jnp.minimum(x, 0.0) - jnp.log(1.0 + jnp.exp(-jnp.abs(x)))


def _silu(x):
    return x * _sigmoid(x)


def _ln_kernel(x_ref, g_ref, b_ref, o_ref):
    o_ref[...] = _layer_norm(x_ref[...], g_ref[...], b_ref[...], LN_EPS)


def _ln_call(x, g, b, tm):
    n, d = x.shape
    row = pl.BlockSpec((tm, d), lambda i: (i, 0))
    vec = pl.BlockSpec((1, d), lambda i: (0, 0))
    return pl.pallas_call(
        _ln_kernel,
        grid=(n // tm,),
        in_specs=[row, vec, vec],
        out_specs=row,
        out_shape=jax.ShapeDtypeStruct((n, d), F32),
        compiler_params=pltpu.CompilerParams(dimension_semantics=("parallel",)),
        name="ln0",
    )(x, g, b)


W_A_COLS = 4 * D_A
W_IF_COLS = 2 * LANES
W_IN_COLS = W_A_COLS + D_B_IN + W_IF_COLS


def _inproj_kernel(x_ref, w_ref, pa_ref, pb_ref, pif_ref):
    xb = x_ref[...].astype(BF16)
    pa_ref[...] = jnp.dot(xb, w_ref[:, 0:W_A_COLS], preferred_element_type=F32)
    pb_ref[...] = jnp.dot(xb, w_ref[:, W_A_COLS:W_A_COLS + D_B_IN], preferred_element_type=F32)
    pif_ref[...] = jnp.dot(xb, w_ref[:, W_A_COLS + D_B_IN:W_IN_COLS], preferred_element_type=F32)


def _inproj_call(x, w, tm):
    n = x.shape[0]
    return pl.pallas_call(
        _inproj_kernel,
        grid=(n // tm,),
        in_specs=[pl.BlockSpec((tm, D_MODEL), lambda i: (i, 0)),
                  pl.BlockSpec((D_MODEL, W_IN_COLS), lambda i: (0, 0))],
        out_specs=[pl.BlockSpec((tm, W_A_COLS), lambda i: (i, 0)),
                   pl.BlockSpec((tm, D_B_IN), lambda i: (i, 0)),
                   pl.BlockSpec((tm, W_IF_COLS), lambda i: (i, 0))],
        out_shape=[jax.ShapeDtypeStruct((n, W_A_COLS), F32),
                   jax.ShapeDtypeStruct((n, D_B_IN), F32),
                   jax.ShapeDtypeStruct((n, W_IF_COLS), F32)],
        compiler_params=pltpu.CompilerParams(dimension_semantics=("parallel",),
                                             vmem_limit_bytes=VMEM_LIMIT),
        name="inproj",
    )(x, w)


CONV_PAD = 8


def _mlstm_kernel(pa_ref, pif_ref, conv0_ref, c0_ref, n0_ref, m0_ref, convw_ref, convb_ref,
                  bif_ref, gn_ref, h_ref, c_out_ref, n_out_ref, m_out_ref,
                  qk_scr, c_scr, n_scr, m_scr, *, c, nc):
    ci = pl.program_id(1)

    @pl.when(ci == 0)
    def _():
        c_scr[...] = c0_ref[0]
        n_scr[...] = n0_ref[0]
        m_scr[...] = m0_ref[0]
        qk_scr[CONV_PAD - (CONV_W - 1):CONV_PAD, :] = conv0_ref[0]

    u = pa_ref[:, 0:2 * D_A]
    qk_scr[CONV_PAD:CONV_PAD + c, :] = u
    acc = convb_ref[...]
    for j in range(CONV_W):
        off = CONV_PAD - (CONV_W - 1) + j
        acc = acc + qk_scr[off:off + c, :] * convw_ref[j:j + 1, :]
    qk_scr[CONV_PAD - (CONV_W - 1):CONV_PAD, :] = u[c - (CONV_W - 1):c, :]
    qk = _silu(acc)

    gates = pif_ref[...] + bif_ref[...]
    li_all = gates[:, 0:LANES]
    lf_all = _log_sigmoid(gates[:, LANES:2 * LANES])
    row = lax.broadcasted_iota(jnp.int32, (c, c), 0)
    col = lax.broadcasted_iota(jnp.int32, (c, c), 1)
    causal = col <= row
    tril = jnp.where(causal, 1.0, 0.0).astype(BF16)
    b_all = _dot_exact_lhs(tril, lf_all)
    z_all = li_all - b_all
    lane = lax.broadcasted_iota(jnp.int32, (c, LANES), 1)
    lane1 = lax.broadcasted_iota(jnp.int32, (1, LANES), 1)
    m_all = m_scr[...]
    m_new_all = m_all

    for h in range(H_A):
        sl = slice(h * DH_A, (h + 1) * DH_A)
        q = qk[:, h * DH_A:(h + 1) * DH_A]
        k = qk[:, D_A + h * DH_A:D_A + (h + 1) * DH_A] * (DH_A ** -0.5)
        v = pa_ref[:, 2 * D_A + h * DH_A:2 * D_A + (h + 1) * DH_A]
        b_col = b_all[:, h:h + 1]
        li_col = li_all[:, h:h + 1]
        m_prev = m_all[:, h:h + 1]
        sel = jnp.where(lane == h, 1.0, 0.0).astype(BF16)
        z_row = _dot_exact_lhs(sel, z_all, NT)
        dmat = jnp.where(causal, b_col + z_row, -jnp.inf)
        m_inter = b_col + m_prev
        m_t = jnp.maximum(m_inter, jnp.max(dmat, axis=-1, keepdims=True))
        s = _dot(q, k, NT) * jnp.exp(dmat - m_t)
        w_inter = jnp.exp(m_inter - m_t)
        c_h = c_scr[h]
        n_h = n_scr[h:h + 1, :]
        num = w_inter * _dot(q, c_h) + _dot(s, v)
        den = w_inter * jnp.sum(q * n_h, axis=-1, keepdims=True) + jnp.sum(s, axis=-1, keepdims=True)
        hh = num / jnp.maximum(jnp.abs(den), jnp.exp(-m_t))
        b_last = b_col[c - 1:c, :]
        g_s = b_last - b_col + li_col
        m_new = jnp.maximum(b_last + m_prev, jnp.max(g_s, axis=0, keepdims=True))
        w_s = jnp.exp(g_s - m_new)
        w_old = jnp.exp(b_last + m_prev - m_new)
        wk = w_s * k
        c_scr[h] = w_old * c_h + _dot(wk, v, TN)
        n_scr[h:h + 1, :] = w_old * n_h + jnp.sum(wk, axis=0, keepdims=True)
        m_new_all = jnp.where(lane1 == h, m_new, m_new_all)
        mu = jnp.mean(hh, axis=-1, keepdims=True)
        hc = hh - mu
        var = jnp.mean(hc * hc, axis=-1, keepdims=True)
        hn = hc * lax.rsqrt(var + GN_EPS_A)
        o_pre = pa_ref[:, 3 * D_A + h * DH_A:3 * D_A + (h + 1) * DH_A]
        h_ref[:, sl] = hn * gn_ref[:, sl] * _sigmoid(o_pre)

    m_scr[...] = m_new_all

    @pl.when(ci == nc - 1)
    def _():
        c_out_ref[0] = c_scr[...]
        n_out_ref[0] = n_scr[...]
        m_out_ref[0] = m_scr[...]


def _mlstm_call(pa, pif, conv0, c0, n0, m0, convw, convb, bif, gn, *, batch, seq):
    c = math.gcd(seq, CHUNK)
    nc = seq // c
    n = batch * seq
    tok = lambda w: pl.BlockSpec((c, w), lambda b, i: (b * nc + i, 0))
    vec = lambda r, w: pl.BlockSpec((r, w), lambda b, i: (0, 0))
    st4 = pl.BlockSpec((1, H_A, DH_A, DH_A), lambda b, i: (b, 0, 0, 0))
    st3 = lambda r, w: pl.BlockSpec((1, r, w), lambda b, i: (b, 0, 0))
    return pl.pallas_call(
        functools.partial(_mlstm_kernel, c=c, nc=nc),
        grid=(batch, nc),
        in_specs=[tok(W_A_COLS), tok(W_IF_COLS), st3(CONV_W - 1, 2 * D_A), st4, st3(H_A, DH_A),
                  st3(1, LANES), vec(CONV_W, 2 * D_A), vec(1, 2 * D_A), vec(1, W_IF_COLS), vec(1, D_A)],
        out_specs=[tok(D_A), st4, st3(H_A, DH_A), st3(1, LANES)],
        out_shape=[jax.ShapeDtypeStruct((n, D_A), F32),
                   jax.ShapeDtypeStruct((batch, H_A, DH_A, DH_A), F32),
                   jax.ShapeDtypeStruct((batch, H_A, DH_A), F32),
                   jax.ShapeDtypeStruct((batch, 1, LANES), F32)],
        scratch_shapes=[pltpu.VMEM((CONV_PAD + c, 2 * D_A), F32),
                        pltpu.VMEM((H_A, DH_A, DH_A), F32),
                        pltpu.VMEM((H_A, DH_A), F32),
                        pltpu.VMEM((1, LANES), F32)],
        compiler_params=pltpu.CompilerParams(dimension_semantics=("parallel", "arbitrary"),
                                             vmem_limit_bytes=VMEM_LIMIT),
        name="mlstm",
    )(pa, pif, conv0, c0, n0, m0, convw, convb, bif, gn)


SHIFT_PAD = 8


def _neumann_inverse(m_strict, size, t):
    row = lax.broadcasted_iota(jnp.int32, (size, size), 0)
    col = lax.broadcasted_iota(jnp.int32, (size, size), 1)
    eye = jnp.where(row == col, 1.0, 0.0)
    blk = min(SOLVE_BLOCK, t)
    shift = int(math.log2(blk))
    same = (row >> shift) == (col >> shift)
    d = jnp.where(same, m_strict, 0.0)
    p = eye + d
    x = d
    for _ in range(int(math.log2(blk)) - 1):
        x = _dot(x, x)
        p = p + _dot(p, x)
    nb = t // blk
    if nb == 1:
        return p
    nmat = _dot(p, m_strict - d)
    q = eye + nmat
    y = nmat
    for _ in range(int(math.log2(nb)) - 1):
        y = _dot(y, y)
        q = q + _dot(q, y)
    return _dot(q, p)


def _rwkv_kernel(pb_ref, shift0_ref, s0_ref, mu_ref, w0a0_ref, wlora_ref, g2_ref, kk_ref, ka_ref,
                 rk_ref, lng_ref, lnb_ref, bd_ref, h_ref, s_out_ref, sh_scr, s_scr, *, t, nc):
    ci = pl.program_id(1)

    @pl.when(ci == 0)
    def _():
        s_scr[...] = s0_ref[0]
        sh_scr[SHIFT_PAD - 1:SHIFT_PAD, :] = shift0_ref[0]

    p = pb_ref[...]
    sh_scr[SHIFT_PAD:SHIFT_PAD + t, :] = p
    prev = sh_scr[SHIFT_PAD - 1:SHIFT_PAD - 1 + t, :]
    sh_scr[SHIFT_PAD - 1:SHIFT_PAD, :] = p[t - 1:t, :]
    pb = p + (prev - p) * mu_ref[...]

    r = pb[:, 0:D_B]
    kr = pb[:, D_B:2 * D_B]
    vr = pb[:, 2 * D_B:3 * D_B]
    lora_in = pb[:, 3 * D_B:3 * D_B + LANES]
    gl = pb[:, 3 * D_B + LANES:3 * D_B + 2 * LANES]
    lane = lax.broadcasted_iota(jnp.int32, (t, LANES), 1)
    lora_act = jnp.where(lane < R_DECAY, jnp.tanh(lora_in), lora_in)
    z = _dot(lora_act, wlora_ref[...]) + w0a0_ref[...]
    zw = z[:, 0:D_B]
    ld = -jnp.exp(_log_sigmoid(zw) - 0.5)
    a = _sigmoid(z[:, D_B:2 * D_B])
    g = _dot(_sigmoid(gl), g2_ref[...])
    bd = bd_ref[...]
    kk_raw = kr * kk_ref[...]
    ss = _dot_exact_rhs(kk_raw * kk_raw, bd)
    kk = kk_raw / jnp.maximum(jnp.sqrt(ss), 1e-12)
    k2 = kr * (1.0 + (a - 1.0) * ka_ref[...])
    bonus = _dot_exact_rhs(r * k2 * rk_ref[...], bd) * vr

    row = lax.broadcasted_iota(jnp.int32, (t, t), 0)
    col = lax.broadcasted_iota(jnp.int32, (t, t), 1)
    tril = jnp.where(col <= row, 1.0, 0.0).astype(BF16)
    lc = _dot_exact_lhs(tril, ld)
    lp = lc - ld
    lend = lc[t - 1:t, :]
    e_c = jnp.exp(lc)
    e_nc = jnp.exp(-lc)
    e_end = jnp.exp(lend - lc)
    at = -kk * jnp.exp(lp)
    rt = r * e_c
    kt = k2 * e_nc
    b_raw = kk * a
    bt = b_raw * e_nc
    kend = k2 * e_end
    bend = b_raw * e_end
    dec_end = jnp.exp(lend)

    t2 = 2 * t
    srow = lax.broadcasted_iota(jnp.int32, (t2, t2), 0)
    scol = lax.broadcasted_iota(jnp.int32, (t2, t2), 1)
    strict = scol < srow
    incl = scol <= srow
    lane2 = lax.broadcasted_iota(jnp.int32, (t2, LANES), 1)
    row2 = lax.broadcasted_iota(jnp.int32, (t2, LANES), 0)
    head_sel = (lane2 >= DH_B) == (row2 >= t)
    lane_t = lax.broadcasted_iota(jnp.int32, (t, LANES), 1)
    first_head = lane_t < DH_B
    brow = lax.broadcasted_iota(jnp.int32, (LANES, LANES), 0)
    bcol = lax.broadcasted_iota(jnp.int32, (LANES, LANES), 1)
    blockdiag = (brow >= DH_B) == (bcol >= DH_B)

    def stack(x):
        return jnp.where(head_sel, jnp.concatenate([x, x], axis=0), 0.0)

    for pr in range(N_PAIRS):
        sl = slice(pr * LANES, (pr + 1) * LANES)
        la = stack(at[:, sl])
        lr = stack(rt[:, sl])
        rk_s = stack(kt[:, sl])
        rb_s = stack(bt[:, sl])
        v_p = vr[:, sl]
        v_s = jnp.concatenate([v_p, v_p], axis=0)
        s_p = s_scr[pr]
        m_ak = jnp.where(strict, _dot(la, rk_s, NT), 0.0)
        m_ab = jnp.where(strict, _dot(la, rb_s, NT), 0.0)
        n_rk = jnp.where(incl, _dot(lr, rk_s, NT), 0.0)
        n_rb = jnp.where(incl, _dot(lr, rb_s, NT), 0.0)
        a_s = _dot(la, s_p, NT)
        r_s = _dot(lr, s_p, NT)
        rhs = a_s + _dot(m_ak, v_s)
        inv = _neumann_inverse(m_ab, t2, t)
        sa_s = _dot(inv, rhs)
        o_s = r_s + _dot(n_rk, v_s) + _dot(n_rb, sa_s)
        o_p = jnp.where(first_head, o_s[0:t], o_s[t:t2])
        sa_p = jnp.where(first_head, sa_s[0:t], sa_s[t:t2])
        upd = _dot(v_p, kend[:, sl], TN) + _dot(sa_p, bend[:, sl], TN)
        s_scr[pr] = s_p * dec_end[:, sl] + jnp.where(blockdiag, upd, 0.0)
        h_ref[:, sl] = o_p

    o = h_ref[...]
    mean = _dot_exact_rhs(o, bd) * (1.0 / DH_B)
    oc = o - mean
    var = _dot_exact_rhs(oc * oc, bd) * (1.0 / DH_B)
    out = oc * lax.rsqrt(var + GN_EPS_B) * lng_ref[...] + lnb_ref[...]
    h_ref[...] = (out + bonus) * g

    @pl.when(ci == nc - 1)
    def _():
        s_out_ref[0] = s_scr[...]


def _rwkv_call(pb, shift0, s0, mu, w0a0, wlora, g2, kk, ka, rk, lng, lnb, bd, *, batch, seq):
    t = math.gcd(seq, CHUNK)
    nc = seq // t
    n = batch * seq
    tok = lambda w: pl.BlockSpec((t, w), lambda b, i: (b * nc + i, 0))
    vec = lambda r, w: pl.BlockSpec((r, w), lambda b, i: (0, 0))
    st4 = pl.BlockSpec((1, N_PAIRS, LANES, LANES), lambda b, i: (b, 0, 0, 0))
    return pl.pallas_call(
        functools.partial(_rwkv_kernel, t=t, nc=nc),
        grid=(batch, nc),
        in_specs=[tok(D_B_IN), pl.BlockSpec((1, 1, D_B_IN), lambda b, i: (b, 0, 0)), st4,
                  vec(1, D_B_IN), vec(1, 2 * D_B), vec(LANES, 2 * D_B), vec(R_GATE, D_B),
                  vec(1, D_B), vec(1, D_B), vec(1, D_B), vec(1, D_B), vec(1, D_B), vec(D_B, D_B)],
        out_specs=[tok(D_B), st4],
        out_shape=[jax.ShapeDtypeStruct((n, D_B), F32),
                   jax.ShapeDtypeStruct((batch, N_PAIRS, LANES, LANES), F32)],
        scratch_shapes=[pltpu.VMEM((SHIFT_PAD + t, D_B_IN), F32),
                        pltpu.VMEM((N_PAIRS, LANES, LANES), F32)],
        compiler_params=pltpu.CompilerParams(dimension_semantics=("parallel", "arbitrary"),
                                             vmem_limit_bytes=VMEM_LIMIT),
        name="rwkv",
    )(pb, shift0, s0, mu, w0a0, wlora, g2, kk, ka, rk, lng, lnb, bd)


def _outproj_kernel(x_ref, ha_ref, hb_ref, wo_ref, g_ref, b_ref, wr_ref, br_ref, x1_ref, gate_ref, *, tm):
    y = _dot(ha_ref[...], wo_ref[0:D_A, :]) + _dot(hb_ref[...], wo_ref[D_A:D_A + D_B, :])
    x1 = _layer_norm(ALPHA * x_ref[...] + y, g_ref[...], b_ref[...], LN_EPS)
    x1_ref[...] = x1
    logits = lax.dot_general(wr_ref[...], x1, NT, precision=lax.Precision.HIGHEST,
                             preferred_element_type=F32)
    mx = jnp.max(logits, axis=0, keepdims=True)
    ex = jnp.exp(logits - mx)
    probs = ex / jnp.sum(ex, axis=0, keepdims=True)
    sel = probs + br_ref[...]
    neg = -jnp.inf

    def top2(rows):
        m1 = functools.reduce(jnp.maximum, rows)
        i1 = jnp.full(m1.shape, len(rows) - 1, jnp.int32)
        for j in range(len(rows) - 2, -1, -1):
            i1 = jnp.where(rows[j] == m1, j, i1)
        rest = [jnp.where(i1 == j, neg, rows[j]) for j in range(len(rows))]
        m2 = functools.reduce(jnp.maximum, rest)
        i2 = jnp.full(m2.shape, len(rows) - 1, jnp.int32)
        for j in range(len(rows) - 2, -1, -1):
            i2 = jnp.where(rest[j] == m2, j, i2)
        return m1, i1, m2, i2

    g_score, g_i1, g_i2 = [], [], []
    for gidx in range(N_GROUPS):
        rows = [sel[gidx * EXPERTS_PER_GROUP + j:gidx * EXPERTS_PER_GROUP + j + 1, :]
                for j in range(EXPERTS_PER_GROUP)]
        m1, i1, m2, i2 = top2(rows)
        g_score.append(m1 + m2)
        g_i1.append(i1)
        g_i2.append(i2)
    best = functools.reduce(jnp.maximum, g_score)
    grp = jnp.full(best.shape, N_GROUPS - 1, jnp.int32)
    for gidx in range(N_GROUPS - 2, -1, -1):
        grp = jnp.where(g_score[gidx] == best, gidx, grp)
    l1 = g_i1[N_GROUPS - 1]
    l2 = g_i2[N_GROUPS - 1]
    for gidx in range(N_GROUPS - 2, -1, -1):
        l1 = jnp.where(grp == gidx, g_i1[gidx], l1)
        l2 = jnp.where(grp == gidx, g_i2[gidx], l2)
    e1 = grp * EXPERTS_PER_GROUP + l1
    e2 = grp * EXPERTS_PER_GROUP + l2
    e_iota = lax.broadcasted_iota(jnp.int32, (N_EXPERTS, tm), 0)
    p1 = jnp.sum(jnp.where(e_iota == e1, probs, 0.0), axis=0, keepdims=True)
    p2 = jnp.sum(jnp.where(e_iota == e2, probs, 0.0), axis=0, keepdims=True)
    tot = p1 + p2
    gate_t = jnp.where(e_iota == e1, p1 / tot, 0.0) + jnp.where(e_iota == e2, p2 / tot, 0.0)
    gate_pad = jnp.concatenate([gate_t, jnp.zeros((LANES - N_EXPERTS, tm), F32)], axis=0)
    gate_ref[...] = gate_pad.T


def _outproj_call(x, ha, hb, wo, g, b, wr_t, br, tm):
    n = x.shape[0]
    row = lambda w: pl.BlockSpec((tm, w), lambda i: (i, 0))
    full = lambda r, w: pl.BlockSpec((r, w), lambda i: (0, 0))
    return pl.pallas_call(
        functools.partial(_outproj_kernel, tm=tm),
        grid=(n // tm,),
        in_specs=[row(D_MODEL), row(D_A), row(D_B), full(D_MODEL, D_MODEL), full(1, D_MODEL),
                  full(1, D_MODEL), full(N_EXPERTS, D_MODEL), full(N_EXPERTS, 1)],
        out_specs=[row(D_MODEL), row(LANES)],
        out_shape=[jax.ShapeDtypeStruct((n, D_MODEL), F32), jax.ShapeDtypeStruct((n, LANES), F32)],
        compiler_params=pltpu.CompilerParams(dimension_semantics=("parallel",),
                                             vmem_limit_bytes=VMEM_LIMIT),
        name="outproj",
    )(x, ha, hb, wo, g, b, wr_t, br)


def _moe_kernel(x_ref, gate_ref, wg_ref, wu_ref, wd_ref, g_ref, b_ref, o_ref, xb_scr, acc_scr):
    e = pl.program_id(1)

    @pl.when(e == 0)
    def _():
        xb_scr[...] = x_ref[...].astype(BF16)
        acc_scr[...] = jnp.zeros_like(acc_scr)

    xb = xb_scr[...]
    hid = _silu(jnp.dot(xb, wg_ref[0], preferred_element_type=F32)) * jnp.dot(
        xb, wu_ref[0], preferred_element_type=F32)
    lane = lax.broadcasted_iota(jnp.int32, gate_ref.shape, 1)
    gcol = jnp.sum(jnp.where(lane == e, gate_ref[...], 0.0), axis=1, keepdims=True)
    acc_scr[...] += _dot(hid * gcol, wd_ref[0])

    @pl.when(e == N_EXPERTS - 1)
    def _():
        o_ref[...] = _layer_norm(ALPHA * x_ref[...] + acc_scr[...], g_ref[...], b_ref[...], LN_EPS)


def _moe_call(x, gate, wg, wu, wd, g, b, tm):
    n = x.shape[0]
    row = lambda w: pl.BlockSpec((tm, w), lambda i, e: (i, 0))
    vec = pl.BlockSpec((1, D_MODEL), lambda i, e: (0, 0))
    return pl.pallas_call(
        _moe_kernel,
        grid=(n // tm, N_EXPERTS),
        in_specs=[row(D_MODEL), row(LANES),
                  pl.BlockSpec((1, D_MODEL, D_EXP), lambda i, e: (e, 0, 0)),
                  pl.BlockSpec((1, D_MODEL, D_EXP), lambda i, e: (e, 0, 0)),
                  pl.BlockSpec((1, D_EXP, D_MODEL), lambda i, e: (e, 0, 0)),
                  vec, vec],
        out_specs=row(D_MODEL),
        out_shape=jax.ShapeDtypeStruct((n, D_MODEL), F32),
        scratch_shapes=[pltpu.VMEM((tm, D_MODEL), BF16), pltpu.VMEM((tm, D_MODEL), F32)],
        compiler_params=pltpu.CompilerParams(dimension_semantics=("parallel", "arbitrary"),
                                             vmem_limit_bytes=VMEM_LIMIT),
        name="moe",
    )(x, gate, wg, wu, wd, g, b)


def _pair_states(s):
    b = s.shape[0]
    s = s.reshape(b, N_PAIRS, 2, DH_B, DH_B)
    z = jnp.zeros_like(s[:, :, 0])
    top = jnp.concatenate([s[:, :, 0], z], axis=-1)
    bot = jnp.concatenate([z, s[:, :, 1]], axis=-1)
    return jnp.concatenate([top, bot], axis=-2)


def _unpair_states(sp):
    b = sp.shape[0]
    h0 = sp[:, :, 0:DH_B, 0:DH_B]
    h1 = sp[:, :, DH_B:, DH_B:]
    return jnp.stack([h0, h1], axis=2).reshape(b, H_B, DH_B, DH_B)


def _row_tile(n):
    return 512 if n % 512 == 0 else n


def kernel(x_prompt, x_sample, state_mlstm_C, state_mlstm_n, state_mlstm_m, state_mlstm_conv, state_rwkv_S, state_rwkv_shift, ln0_g, ln0_b, w_in, conv_w, conv_b, b_i, b_f, gn_a_g, mu_shift, w0, w2, a0, a2, g2, k_k, k_a, r_k, lnx_g, lnx_b, w_out, ln1_g, ln1_b, w_router, b_router, we_gate, we_up, we_down, ln2_g, ln2_b):
    d_a_in = 4 * D_A + 2 * H_A
    w_if = jnp.zeros((DEPTH, D_MODEL, W_IF_COLS), F32)
    w_if = w_if.at[:, :, 0:H_A].set(w_in[:, :, 4 * D_A:4 * D_A + H_A])
    w_if = w_if.at[:, :, LANES:LANES + H_A].set(w_in[:, :, 4 * D_A + H_A:d_a_in])
    w_cat = jnp.concatenate([w_in[:, :, 0:4 * D_A], w_in[:, :, d_a_in:], w_if], axis=-1).astype(BF16)
    bif = jnp.zeros((DEPTH, 1, W_IF_COLS), F32)
    bif = bif.at[:, 0, 0:H_A].set(b_i).at[:, 0, LANES:LANES + H_A].set(b_f)
    wlora = jnp.zeros((DEPTH, LANES, 2 * D_B), F32)
    wlora = wlora.at[:, 0:R_DECAY, 0:D_B].set(w2).at[:, R_DECAY:, D_B:].set(a2).astype(BF16)
    w0a0 = jnp.concatenate([w0, a0], axis=-1)[:, None, :]
    g2b = g2.astype(BF16)
    wob = w_out.astype(BF16)
    wgb = we_gate.astype(BF16)
    wub = we_up.astype(BF16)
    wdb = we_down.astype(BF16)
    wr_t = w_router.T
    br = b_router[:, None]
    hid = jnp.arange(D_B) // DH_B
    bd = (hid[:, None] == hid[None, :]).astype(BF16)
    r1 = lambda v: v[None, :]

    def run(x3, states):
        nb, seq, _ = x3.shape
        n = nb * seq
        tm = _row_tile(n)
        x = _ln_call(x3.reshape(n, D_MODEL), r1(ln0_g), r1(ln0_b), tm)
        outs = []
        for l in range(DEPTH):
            if states is None:
                c0 = jnp.zeros((nb, H_A, DH_A, DH_A), F32)
                n0 = jnp.zeros((nb, H_A, DH_A), F32)
                m0 = jnp.zeros((nb, H_A), F32)
                conv0 = jnp.zeros((nb, CONV_W - 1, 2 * D_A), F32)
                s0 = jnp.zeros((nb, H_B, DH_B, DH_B), F32)
                shift0 = jnp.zeros((nb, D_B_IN), F32)
            else:
                c0, n0, m0, conv0, s0, shift0 = (s[l] for s in states)
            m0p = jnp.zeros((nb, 1, LANES), F32).at[:, 0, 0:H_A].set(m0)
            pa, pb, pif = _inproj_call(x, w_cat[l], tm)
            ha, c_new, n_new, m_new = _mlstm_call(
                pa, pif, conv0, c0, n0, m0p, conv_w[l], r1(conv_b[l]), bif[l], r1(gn_a_g[l]),
                batch=nb, seq=seq)
            hb, s_new = _rwkv_call(
                pb, shift0[:, None, :], _pair_states(s0), r1(mu_shift[l]), w0a0[l], wlora[l], g2b[l],
                r1(k_k[l]), r1(k_a[l]), r1(r_k[l].reshape(D_B)), r1(lnx_g[l]), r1(lnx_b[l]), bd,
                batch=nb, seq=seq)
            x1, gate = _outproj_call(x, ha, hb, wob[l], r1(ln1_g[l]), r1(ln1_b[l]), wr_t, br, tm)
            x = _moe_call(x1, gate, wgb[l], wub[l], wdb[l], r1(ln2_g[l]), r1(ln2_b[l]), tm)
            pa3 = pa.reshape(nb, seq, W_A_COLS)
            full = jnp.concatenate([conv0, pa3[:, :, 0:2 * D_A]], axis=1) if seq < CONV_W - 1 else pa3[:, :, 0:2 * D_A]
            conv_new = full[:, -(CONV_W - 1):, :]
            shift_new = pb.reshape(nb, seq, D_B_IN)[:, -1, :]
            outs.append((c_new, n_new, m_new[:, 0, 0:H_A], conv_new, _unpair_states(s_new), shift_new))
        return x.reshape(nb, seq, D_MODEL), [jnp.stack(s) for s in zip(*outs)]

    y_prompt, (p_c, p_n, p_m, p_conv, p_s, p_shift) = run(x_prompt, None)
    y_sample, (s_c, s_n, s_m, s_conv, s_s, s_shift) = run(
        x_sample, (state_mlstm_C, state_mlstm_n, state_mlstm_m, state_mlstm_conv, state_rwkv_S,
                   state_rwkv_shift))
    return (y_prompt, y_sample, p_c, p_n, p_m, p_conv, p_s, p_shift, s_c, s_n, s_m, s_conv, s_s, s_shift)
```

```python
import functools
import math

import jax
import jax.numpy as jnp
from jax import lax
from jax.experimental import pallas as pl
from jax.experimental.pallas import tpu as pltpu

F32 = jnp.float32
BF16 = jnp.bfloat16

D_MODEL = 1024
DEPTH = 4
D_A = 512
D_B = 512
DH_A = 128
H_A = 4
DH_B = 64
H_B = 8
N_PAIRS = H_B // 2
CONV_W = 4
CHUNK = 64
R_DECAY = 64
R_AAA = 64
R_GATE = 128
D_B_IN = 3 * D_B + R_DECAY + R_AAA + R_GATE
N_EXPERTS = 16
N_GROUPS = 4
EXPERTS_PER_GROUP = 4
D_EXP = 512
ALPHA = (2 * DEPTH) ** 0.25
LN_EPS = 1e-5
GN_EPS_A = 1e-6
GN_EPS_B = 64e-5

LANES = 128
SOLVE_BLOCK = 16
VMEM_LIMIT = 48 * 1024 * 1024

NN = (((1,), (0,)), ((), ()))
NT = (((1,), (1,)), ((), ()))
TN = (((0,), (0,)), ((), ()))


def _dot(a, b, dims=NN):
    return lax.dot_general(a.astype(BF16), b.astype(BF16), dims, preferred_element_type=F32)


def _split3(x):
    hi = x.astype(BF16)
    r1 = x - hi.astype(F32)
    mid = r1.astype(BF16)
    lo = (r1 - mid.astype(F32)).astype(BF16)
    return hi, mid, lo


def _dot_exact_lhs(a, x, dims=NN):
    a = a.astype(BF16)
    hi, mid, lo = _split3(x)
    d = lambda p: lax.dot_general(a, p, dims, preferred_element_type=F32)
    return d(hi) + d(mid) + d(lo)


def _dot_exact_rhs(x, b, dims=NN):
    b = b.astype(BF16)
    hi, mid, lo = _split3(x)
    d = lambda p: lax.dot_general(p, b, dims, preferred_element_type=F32)
    return d(hi) + d(mid) + d(lo)


def _layer_norm(x, g, b, eps):
    mu = jnp.mean(x, axis=-1, keepdims=True)
    xc = x - mu
    var = jnp.mean(xc * xc, axis=-1, keepdims=True)
    return xc * lax.rsqrt(var + eps) * g + b


def _sigmoid(x):
    return 1.0 / (1.0 + jnp.exp(-x))


def _log_sigmoid(x):
    return jnp.minimum(x, 0.0) - jnp.log(1.0 + jnp.exp(-jnp.abs(x)))


def _silu(x):
    return x * _sigmoid(x)


def _ln_kernel(x_ref, g_ref, b_ref, o_ref):
    o_ref[...] = _layer_norm(x_ref[...], g_ref[...], b_ref[...], LN_EPS)


def _ln_call(x, g, b, tm):
    n, d = x.shape
    row = pl.BlockSpec((tm, d), lambda i: (i, 0))
    vec = pl.BlockSpec((1, d), lambda i: (0, 0))
    return pl.pallas_call(
        _ln_kernel,
        grid=(n // tm,),
        in_specs=[row, vec, vec],
        out_specs=row,
        out_shape=jax.ShapeDtypeStruct((n, d), F32),
        compiler_params=pltpu.CompilerParams(dimension_semantics=("parallel",)),
        name="ln0",
    )(x, g, b)


W_A_COLS = 4 * D_A
W_IF_COLS = 2 * LANES
W_IN_COLS = W_A_COLS + D_B_IN + W_IF_COLS


def _inproj_kernel(x_ref, w_ref, pa_ref, pb_ref, pif_ref):
    xb = x_ref[...].astype(BF16)
    pa_ref[...] = jnp.dot(xb, w_ref[:, 0:W_A_COLS], preferred_element_type=F32)
    pb_ref[...] = jnp.dot(xb, w_ref[:, W_A_COLS:W_A_COLS + D_B_IN], preferred_element_type=F32)
    pif_ref[...] = jnp.dot(xb, w_ref[:, W_A_COLS + D_B_IN:W_IN_COLS], preferred_element_type=F32)


def _inproj_call(x, w, tm):
    n = x.shape[0]
    return pl.pallas_call(
        _inproj_kernel,
        grid=(n // tm,),
        in_specs=[pl.BlockSpec((tm, D_MODEL), lambda i: (i, 0)),
                  pl.BlockSpec((D_MODEL, W_IN_COLS), lambda i: (0, 0))],
        out_specs=[pl.BlockSpec((tm, W_A_COLS), lambda i: (i, 0)),
                   pl.BlockSpec((tm, D_B_IN), lambda i: (i, 0)),
                   pl.BlockSpec((tm, W_IF_COLS), lambda i: (i, 0))],
        out_shape=[jax.ShapeDtypeStruct((n, W_A_COLS), F32),
                   jax.ShapeDtypeStruct((n, D_B_IN), F32),
                   jax.ShapeDtypeStruct((n, W_IF_COLS), F32)],
        compiler_params=pltpu.CompilerParams(dimension_semantics=("parallel",),
                                             vmem_limit_bytes=VMEM_LIMIT),
        name="inproj",
    )(x, w)


CONV_PAD = 8


def _mlstm_kernel(pa_ref, pif_ref, conv0_ref, c0_ref, n0_ref, m0_ref, convw_ref, convb_ref,
                  bif_ref, gn_ref, h_ref, c_out_ref, n_out_ref, m_out_ref,
                  qk_scr, c_scr, n_scr, m_scr, *, c, nc, nbb):
    ci = pl.program_id(1)
    rows = nbb * c
    cshift = int(math.log2(c))
    chains = [(nb, h) for nb in range(nbb) for h in range(H_A)]
    prev0 = CONV_PAD - (CONV_W - 1)

    @pl.when(ci == 0)
    def _():
        c_scr[...] = c0_ref[...]
        n_scr[...] = n0_ref[...]
        m_scr[...] = m0_ref[...]
        for nb in range(nbb):
            qk_scr[nb, prev0:CONV_PAD, :] = conv0_ref[nb]

    cat0 = lambda xs: xs[0] if len(xs) == 1 else jnp.concatenate(xs, axis=0)
    acc_l = []
    for nb in range(nbb):
        u = pa_ref[nb, :, 0:2 * D_A]
        qk_scr[nb, CONV_PAD:CONV_PAD + c, :] = u
        acc = convb_ref[...]
        for j in range(CONV_W):
            acc = acc + qk_scr[nb, prev0 + j:prev0 + j + c, :] * convw_ref[j:j + 1, :]
        qk_scr[nb, prev0:CONV_PAD, :] = u[c - (CONV_W - 1):c, :]
        acc_l.append(acc)
    qk = _silu(cat0(acc_l))

    gates = cat0([pif_ref[nb] for nb in range(nbb)]) + bif_ref[...]
    li_all = gates[:, 0:LANES]
    lf_all = _log_sigmoid(gates[:, LANES:2 * LANES])
    row = lax.broadcasted_iota(jnp.int32, (rows, rows), 0)
    col = lax.broadcasted_iota(jnp.int32, (rows, rows), 1)
    tril = jnp.where((col <= row) & ((row >> cshift) == (col >> cshift)), 1.0, 0.0)
    b_all = _dot_exact_lhs(tril, lf_all)
    z_all = li_all - b_all
    crow = lax.broadcasted_iota(jnp.int32, (c, c), 0)
    ccol = lax.broadcasted_iota(jnp.int32, (c, c), 1)
    causal = ccol <= crow
    hrow = lax.broadcasted_iota(jnp.int32, (H_A * c, LANES), 0)
    hlane = lax.broadcasted_iota(jnp.int32, (H_A * c, LANES), 1)
    head_pick = jnp.where((hrow >> cshift) == hlane, 1.0, 0.0)
    z_rows = [_dot_exact_lhs(head_pick, z_all[nb * c:(nb + 1) * c], NT) for nb in range(nbb)]
    lane1 = lax.broadcasted_iota(jnp.int32, (1, LANES), 1)

    rs = lambda nb: slice(nb * c, (nb + 1) * c)
    q_l = [qk[rs(nb), h * DH_A:(h + 1) * DH_A] for nb, h in chains]
    k_l = [qk[rs(nb), D_A + h * DH_A:D_A + (h + 1) * DH_A] * (DH_A ** -0.5) for nb, h in chains]
    v_l = [pa_ref[nb, :, 2 * D_A + h * DH_A:2 * D_A + (h + 1) * DH_A] for nb, h in chains]
    c_l = [c_scr[nb, h] for nb, h in chains]
    n_l = [n_scr[nb, h:h + 1, :] for nb, h in chains]
    b_col = [b_all[rs(nb), h:h + 1] for nb, h in chains]
    li_col = [li_all[rs(nb), h:h + 1] for nb, h in chains]
    m_prev = [m_scr[nb][:, h:h + 1] for nb, h in chains]
    dmat = [jnp.where(causal, bc + z_rows[nb][h * c:(h + 1) * c], -jnp.inf)
            for bc, (nb, h) in zip(b_col, chains)]
    m_inter = [bc + mp for bc, mp in zip(b_col, m_prev)]
    m_t = [jnp.maximum(mi, jnp.max(d, axis=-1, keepdims=True)) for mi, d in zip(m_inter, dmat)]
    qk_dot = [_dot(q, k, NT) for q, k in zip(q_l, k_l)]
    qc = [_dot(q, cm) for q, cm in zip(q_l, c_l)]
    s_l = [x * jnp.exp(d - mt) for x, d, mt in zip(qk_dot, dmat, m_t)]
    sv = [_dot(s, v) for s, v in zip(s_l, v_l)]
    b_last = [bc[c - 1:c, :] for bc in b_col]
    g_s = [bl - bc + li for bl, bc, li in zip(b_last, b_col, li_col)]
    m_new = [jnp.maximum(bl + mp, jnp.max(gs, axis=0, keepdims=True))
             for bl, mp, gs in zip(b_last, m_prev, g_s)]
    wk = [jnp.exp(gs - mn) * k for gs, mn, k in zip(g_s, m_new, k_l)]
    w_old = [jnp.exp(bl + mp - mn) for bl, mp, mn in zip(b_last, m_prev, m_new)]
    kv = [_dot(w, v, TN) for w, v in zip(wk, v_l)]
    for i, (nb, h) in enumerate(chains):
        c_scr[nb, h] = w_old[i] * c_l[i] + kv[i]
        n_scr[nb, h:h + 1, :] = w_old[i] * n_l[i] + jnp.sum(wk[i], axis=0, keepdims=True)
    for nb in range(nbb):
        m_row = m_scr[nb]
        for h in range(H_A):
            m_row = jnp.where(lane1 == h, m_new[nb * H_A + h], m_row)
        m_scr[nb] = m_row
    w_inter = [jnp.exp(mi - mt) for mi, mt in zip(m_inter, m_t)]
    qn = [jnp.sum(q * nv, axis=-1, keepdims=True) for q, nv in zip(q_l, n_l)]
    s_sum = [jnp.sum(s, axis=-1, keepdims=True) for s in s_l]
    den = [w * a + b for w, a, b in zip(w_inter, qn, s_sum)]
    hh = [(w * a + b) / jnp.maximum(jnp.abs(d), jnp.exp(-mt))
          for w, a, b, d, mt in zip(w_inter, qc, sv, den, m_t)]
    mu = [jnp.mean(x, axis=-1, keepdims=True) for x in hh]
    hc = [x - m for x, m in zip(hh, mu)]
    var = [jnp.mean(x * x, axis=-1, keepdims=True) for x in hc]
    for i, (nb, h) in enumerate(chains):
        sl = slice(h * DH_A, (h + 1) * DH_A)
        hn = hc[i] * lax.rsqrt(var[i] + GN_EPS_A)
        o_pre = pa_ref[nb, :, 3 * D_A + h * DH_A:3 * D_A + (h + 1) * DH_A]
        h_ref[nb, :, sl] = hn * gn_ref[:, sl] * _sigmoid(o_pre)

    @pl.when(ci == nc - 1)
    def _():
        c_out_ref[...] = c_scr[...]
        n_out_ref[...] = n_scr[...]
        m_out_ref[...] = m_scr[...]


def _mlstm_call(pa, pif, conv0, c0, n0, m0, convw, convb, bif, gn, *, batch, seq, nbb):
    c = math.gcd(seq, CHUNK)
    nc = seq // c
    tok = lambda w: pl.BlockSpec((nbb, c, w), lambda b, i: (b, i, 0))
    vec = lambda r, w: pl.BlockSpec((r, w), lambda b, i: (0, 0))
    st4 = pl.BlockSpec((nbb, H_A, DH_A, DH_A), lambda b, i: (b, 0, 0, 0))
    st3 = lambda r, w: pl.BlockSpec((nbb, r, w), lambda b, i: (b, 0, 0))
    return pl.pallas_call(
        functools.partial(_mlstm_kernel, c=c, nc=nc, nbb=nbb),
        grid=(batch // nbb, nc),
        in_specs=[tok(W_A_COLS), tok(W_IF_COLS), st3(CONV_W - 1, 2 * D_A), st4, st3(H_A, DH_A),
                  st3(1, LANES), vec(CONV_W, 2 * D_A), vec(1, 2 * D_A), vec(1, W_IF_COLS), vec(1, D_A)],
        out_specs=[tok(D_A), st4, st3(H_A, DH_A), st3(1, LANES)],
        out_shape=[jax.ShapeDtypeStruct((batch, seq, D_A), F32),
                   jax.ShapeDtypeStruct((batch, H_A, DH_A, DH_A), F32),
                   jax.ShapeDtypeStruct((batch, H_A, DH_A), F32),
                   jax.ShapeDtypeStruct((batch, 1, LANES), F32)],
        scratch_shapes=[pltpu.VMEM((nbb, CONV_PAD + c, 2 * D_A), F32),
                        pltpu.VMEM((nbb, H_A, DH_A, DH_A), F32),
                        pltpu.VMEM((nbb, H_A, DH_A), F32),
                        pltpu.VMEM((nbb, 1, LANES), F32)],
        compiler_params=pltpu.CompilerParams(dimension_semantics=("parallel", "arbitrary"),
                                             vmem_limit_bytes=VMEM_LIMIT),
        name="mlstm",
    )(pa, pif, conv0, c0, n0, m0, convw, convb, bif, gn)


SHIFT_PAD = 8


def _neumann_inverses(ms, size, t):
    row = lax.broadcasted_iota(jnp.int32, (size, size), 0)
    col = lax.broadcasted_iota(jnp.int32, (size, size), 1)
    eye = jnp.where(row == col, 1.0, 0.0)
    blk = min(SOLVE_BLOCK, t)
    shift = int(math.log2(blk))
    same = (row >> shift) == (col >> shift)
    ds = [jnp.where(same, m, 0.0) for m in ms]
    ps = [eye + d for d in ds]
    xs = ds
    for _ in range(shift - 1):
        xs = [_dot(x, x) for x in xs]
        ps = [p + _dot(p, x) for p, x in zip(ps, xs)]
    nblk = t // blk
    if nblk == 1:
        return ps, None
    ns = [_dot(p, m - d) for p, m, d in zip(ps, ms, ds)]
    qs = [eye + n for n in ns]
    ys = ns
    for _ in range(int(math.log2(nblk)) - 1):
        ys = [_dot(y, y) for y in ys]
        qs = [q + _dot(q, y) for q, y in zip(qs, ys)]
    return ps, qs


def _rwkv_kernel(pb_ref, shift0_ref, s0_ref, mu_ref, w0a0_ref, wlora_ref, g2_ref, kk_ref, ka_ref,
                 rk_ref, lng_ref, lnb_ref, h_ref, s_out_ref, sh_scr, s_scr, *, t, nc, nbb):
    ci = pl.program_id(1)
    rows = nbb * t
    t2 = 2 * t
    groups = [(nb, pr) for nb in range(nbb) for pr in range(N_PAIRS)]

    brow = lax.broadcasted_iota(jnp.int32, (LANES, LANES), 0)
    bcol = lax.broadcasted_iota(jnp.int32, (LANES, LANES), 1)
    blockdiag = (brow >= DH_B) == (bcol >= DH_B)
    bd2 = jnp.where(blockdiag, 1.0, 0.0).astype(BF16)

    @pl.when(ci == 0)
    def _():
        er = lax.broadcasted_iota(jnp.int32, (DH_B, LANES), 0)
        ec = lax.broadcasted_iota(jnp.int32, (DH_B, LANES), 1)
        dup_cols = jnp.where((ec & (DH_B - 1)) == er, 1.0, 0.0)
        for nb in range(nbb):
            sh_scr[nb, SHIFT_PAD - 1:SHIFT_PAD, :] = shift0_ref[nb]
            for pr in range(N_PAIRS):
                x = s0_ref[nb, 2 * pr:2 * pr + 2].reshape(LANES, DH_B)
                s_scr[nb * N_PAIRS + pr] = jnp.where(blockdiag, _dot_exact_rhs(x, dup_cols), 0.0)

    p_l, prev_l = [], []
    for nb in range(nbb):
        p_nb = pb_ref[nb]
        sh_scr[nb, SHIFT_PAD:SHIFT_PAD + t, :] = p_nb
        prev_l.append(sh_scr[nb, SHIFT_PAD - 1:SHIFT_PAD - 1 + t, :])
        sh_scr[nb, SHIFT_PAD - 1:SHIFT_PAD, :] = p_nb[t - 1:t, :]
        p_l.append(p_nb)
    cat0 = lambda xs: xs[0] if len(xs) == 1 else jnp.concatenate(xs, axis=0)
    p = cat0(p_l)
    prev = cat0(prev_l)
    pb = p + (prev - p) * mu_ref[...]

    r = pb[:, 0:D_B]
    kr = pb[:, D_B:2 * D_B]
    vr = pb[:, 2 * D_B:3 * D_B]
    lora_in = pb[:, 3 * D_B:3 * D_B + LANES]
    gl = pb[:, 3 * D_B + LANES:3 * D_B + 2 * LANES]
    lane = lax.broadcasted_iota(jnp.int32, (rows, LANES), 1)
    lora_act = jnp.where(lane < R_DECAY, jnp.tanh(lora_in), lora_in)
    z = _dot(lora_act, wlora_ref[...]) + w0a0_ref[...]
    ld = -jnp.exp(_log_sigmoid(z[:, 0:D_B]) - 0.5)
    a = _sigmoid(z[:, D_B:2 * D_B])
    g = _dot(_sigmoid(gl), g2_ref[...])

    def seg_sum(x):
        xr = jnp.concatenate([x[:, q * LANES:(q + 1) * LANES] for q in range(N_PAIRS)], axis=0)
        hi = xr.astype(BF16)
        lo = (xr - hi.astype(F32)).astype(BF16)
        s = (jnp.dot(hi, bd2, preferred_element_type=F32) + jnp.dot(lo, bd2, preferred_element_type=F32))
        return jnp.concatenate([s[q * rows:(q + 1) * rows] for q in range(N_PAIRS)], axis=1)

    kk_raw = kr * kk_ref[...]
    kk = kk_raw / jnp.maximum(jnp.sqrt(seg_sum(kk_raw * kk_raw)), 1e-12)
    k2 = kr * (1.0 + (a - 1.0) * ka_ref[...])
    bonus = seg_sum(r * k2 * rk_ref[...]) * vr

    row = lax.broadcasted_iota(jnp.int32, (rows, rows), 0)
    col = lax.broadcasted_iota(jnp.int32, (rows, rows), 1)
    tshift = int(math.log2(t))
    tril = jnp.where((col <= row) & ((row >> tshift) == (col >> tshift)), 1.0, 0.0)
    lc = _dot_exact_lhs(tril, ld)
    lends = [lc[(nb + 1) * t - 1:(nb + 1) * t, :] for nb in range(nbb)]
    lend_rows = cat0([jnp.broadcast_to(le, (t, D_B)) for le in lends])
    e_nc = jnp.exp(-lc)
    e_end = jnp.exp(lend_rows - lc)
    b_raw = kk * a
    at = -kk * jnp.exp(lc - ld)
    rt = r * jnp.exp(lc)
    kt = k2 * e_nc
    bt = b_raw * e_nc
    kend = k2 * e_end
    bend = b_raw * e_end

    srow = lax.broadcasted_iota(jnp.int32, (t2, t2), 0)
    scol = lax.broadcasted_iota(jnp.int32, (t2, t2), 1)
    strict = scol < srow
    incl = scol <= srow
    lane2 = lax.broadcasted_iota(jnp.int32, (t2, LANES), 1)
    row2 = lax.broadcasted_iota(jnp.int32, (t2, LANES), 0)
    head_sel = (lane2 >= DH_B) == (row2 >= t)
    first_head = lax.broadcasted_iota(jnp.int32, (t, LANES), 1) < DH_B

    def blk(x, nb, pr):
        return x[nb * t:(nb + 1) * t, pr * LANES:(pr + 1) * LANES]

    def stack(x):
        return jnp.where(head_sel, jnp.concatenate([x, x], axis=0), 0.0)

    lhs = [jnp.concatenate([stack(blk(at, *gp)), stack(blk(rt, *gp))], axis=0) for gp in groups]
    rk_s = [stack(blk(kt, *gp)) for gp in groups]
    rb_s = [stack(blk(bt, *gp)) for gp in groups]
    v_p = [blk(vr, *gp) for gp in groups]
    v_s = [jnp.concatenate([v, v], axis=0) for v in v_p]
    s_p = [s_scr[i] for i in range(len(groups))]
    gk = [_dot(l, x, NT) for l, x in zip(lhs, rk_s)]
    gb = [_dot(l, x, NT) for l, x in zip(lhs, rb_s)]
    xs = [_dot(l, s, NT) for l, s in zip(lhs, s_p)]
    m_ab = [jnp.where(strict, x[0:t2], 0.0) for x in gb]
    ps, qs = _neumann_inverses(m_ab, t2, t)
    rhs = [x[0:t2] + _dot(jnp.where(strict, y[0:t2], 0.0), v) for x, y, v in zip(xs, gk, v_s)]
    sa_s = [_dot(pm, x) for pm, x in zip(ps, rhs)]
    if qs is not None:
        sa_s = [_dot(qm, x) for qm, x in zip(qs, sa_s)]
    o_s = [x[t2:] + _dot(jnp.where(incl, y[t2:], 0.0), v) + _dot(jnp.where(incl, w[t2:], 0.0), sa)
           for x, y, w, v, sa in zip(xs, gk, gb, v_s, sa_s)]
    o_p = [jnp.where(first_head, o[0:t], o[t:t2]) for o in o_s]
    sa_p = [jnp.where(first_head, sa[0:t], sa[t:t2]) for sa in sa_s]
    upd = [_dot(jnp.concatenate([v, sa], axis=0),
                jnp.concatenate([blk(kend, *gp), blk(bend, *gp)], axis=0), TN)
           for v, sa, gp in zip(v_p, sa_p, groups)]
    for i, (nb, pr) in enumerate(groups):
        dec = jnp.exp(lends[nb][:, pr * LANES:(pr + 1) * LANES])
        s_scr[i] = s_p[i] * dec + jnp.where(blockdiag, upd[i], 0.0)

    o = cat0([jnp.concatenate(o_p[nb * N_PAIRS:(nb + 1) * N_PAIRS], axis=1) for nb in range(nbb)])
    mean = seg_sum(o) * (1.0 / DH_B)
    oc = o - mean
    var = seg_sum(oc * oc) * (1.0 / DH_B)
    out = oc * lax.rsqrt(var + GN_EPS_B) * lng_ref[...] + lnb_ref[...]
    res = (out + bonus) * g
    for nb in range(nbb):
        h_ref[nb] = res[nb * t:(nb + 1) * t, :]

    @pl.when(ci == nc - 1)
    def _():
        fr = lax.broadcasted_iota(jnp.int32, (LANES, DH_B), 0)
        fc = lax.broadcasted_iota(jnp.int32, (LANES, DH_B), 1)
        dup_rows = jnp.where((fr & (DH_B - 1)) == fc, 1.0, 0.0)
        for i, (nb, pr) in enumerate(groups):
            packed = _dot_exact_rhs(s_scr[i], dup_rows)
            s_out_ref[nb, 2 * pr:2 * pr + 2] = packed.reshape(2, DH_B, DH_B)


def _rwkv_call(pb, shift0, s0, mu, w0a0, wlora, g2, kk, ka, rk, lng, lnb, *, batch, seq, nbb):
    t = math.gcd(seq, CHUNK)
    nc = seq // t
    tok = lambda w: pl.BlockSpec((nbb, t, w), lambda b, i: (b, i, 0))
    vec = lambda r, w: pl.BlockSpec((r, w), lambda b, i: (0, 0))
    st4 = pl.BlockSpec((nbb, H_B, DH_B, DH_B), lambda b, i: (b, 0, 0, 0))
    return pl.pallas_call(
        functools.partial(_rwkv_kernel, t=t, nc=nc, nbb=nbb),
        grid=(batch // nbb, nc),
        in_specs=[tok(D_B_IN), pl.BlockSpec((nbb, 1, D_B_IN), lambda b, i: (b, 0, 0)), st4,
                  vec(1, D_B_IN), vec(1, 2 * D_B), vec(LANES, 2 * D_B), vec(R_GATE, D_B),
                  vec(1, D_B), vec(1, D_B), vec(1, D_B), vec(1, D_B), vec(1, D_B)],
        out_specs=[tok(D_B), st4],
        out_shape=[jax.ShapeDtypeStruct((batch, seq, D_B), F32),
                   jax.ShapeDtypeStruct((batch, H_B, DH_B, DH_B), F32)],
        scratch_shapes=[pltpu.VMEM((nbb, SHIFT_PAD + t, D_B_IN), F32),
                        pltpu.VMEM((nbb * N_PAIRS, LANES, LANES), F32)],
        compiler_params=pltpu.CompilerParams(dimension_semantics=("parallel", "arbitrary"),
                                             vmem_limit_bytes=VMEM_LIMIT),
        name="rwkv",
    )(pb, shift0, s0, mu, w0a0, wlora, g2, kk, ka, rk, lng, lnb)


def _outproj_kernel(x_ref, ha_ref, hb_ref, wo_ref, g_ref, b_ref, wr_ref, br_ref, x1_ref, gate_ref, *, tm):
    y = _dot(ha_ref[...], wo_ref[0:D_A, :]) + _dot(hb_ref[...], wo_ref[D_A:D_A + D_B, :])
    x1 = _layer_norm(ALPHA * x_ref[...] + y, g_ref[...], b_ref[...], LN_EPS)
    x1_ref[...] = x1
    logits = lax.dot_general(wr_ref[...], x1, NT, precision=lax.Precision.HIGHEST,
                             preferred_element_type=F32)
    mx = jnp.max(logits, axis=0, keepdims=True)
    ex = jnp.exp(logits - mx)
    probs = ex / jnp.sum(ex, axis=0, keepdims=True)
    sel = probs + br_ref[...]
    neg = -jnp.inf

    def top2(rows):
        m1 = functools.reduce(jnp.maximum, rows)
        i1 = jnp.full(m1.shape, len(rows) - 1, jnp.int32)
        for j in range(len(rows) - 2, -1, -1):
            i1 = jnp.where(rows[j] == m1, j, i1)
        rest = [jnp.where(i1 == j, neg, rows[j]) for j in range(len(rows))]
        m2 = functools.reduce(jnp.maximum, rest)
        i2 = jnp.full(m2.shape, len(rows) - 1, jnp.int32)
        for j in range(len(rows) - 2, -1, -1):
            i2 = jnp.where(rest[j] == m2, j, i2)
        return m1, i1, m2, i2

    g_score, g_i1, g_i2 = [], [], []
    for gidx in range(N_GROUPS):
        rows = [sel[gidx * EXPERTS_PER_GROUP + j:gidx * EXPERTS_PER_GROUP + j + 1, :]
                for j in range(EXPERTS_PER_GROUP)]
        m1, i1, m2, i2 = top2(rows)
        g_score.append(m1 + m2)
        g_i1.append(i1)
        g_i2.append(i2)
    best = functools.reduce(jnp.maximum, g_score)
    grp = jnp.full(best.shape, N_GROUPS - 1, jnp.int32)
    for gidx in range(N_GROUPS - 2, -1, -1):
        grp = jnp.where(g_score[gidx] == best, gidx, grp)
    l1 = g_i1[N_GROUPS - 1]
    l2 = g_i2[N_GROUPS - 1]
    for gidx in range(N_GROUPS - 2, -1, -1):
        l1 = jnp.where(grp == gidx, g_i1[gidx], l1)
        l2 = jnp.where(grp == gidx, g_i2[gidx], l2)
    e1 = grp * EXPERTS_PER_GROUP + l1
    e2 = grp * EXPERTS_PER_GROUP + l2
    e_iota = lax.broadcasted_iota(jnp.int32, (N_EXPERTS, tm), 0)
    p1 = jnp.sum(jnp.where(e_iota == e1, probs, 0.0), axis=0, keepdims=True)
    p2 = jnp.sum(jnp.where(e_iota == e2, probs, 0.0), axis=0, keepdims=True)
    tot = p1 + p2
    gate_t = jnp.where(e_iota == e1, p1 / tot, 0.0) + jnp.where(e_iota == e2, p2 / tot, 0.0)
    gate_pad = jnp.concatenate([gate_t, jnp.zeros((LANES - N_EXPERTS, tm), F32)], axis=0)
    gate_ref[...] = gate_pad.T


def _outproj_call(x, ha, hb, wo, g, b, wr_t, br, tm):
    n = x.shape[0]
    row = lambda w: pl.BlockSpec((tm, w), lambda i: (i, 0))
    full = lambda r, w: pl.BlockSpec((r, w), lambda i: (0, 0))
    return pl.pallas_call(
        functools.partial(_outproj_kernel, tm=tm),
        grid=(n // tm,),
        in_specs=[row(D_MODEL), row(D_A), row(D_B), full(D_MODEL, D_MODEL), full(1, D_MODEL),
                  full(1, D_MODEL), full(N_EXPERTS, D_MODEL), full(N_EXPERTS, 1)],
        out_specs=[row(D_MODEL), row(LANES)],
        out_shape=[jax.ShapeDtypeStruct((n, D_MODEL), F32), jax.ShapeDtypeStruct((n, LANES), F32)],
        compiler_params=pltpu.CompilerParams(dimension_semantics=("parallel",),
                                             vmem_limit_bytes=VMEM_LIMIT),
        name="outproj",
    )(x, ha, hb, wo, g, b, wr_t, br)


def _moe_kernel(x_ref, gate_ref, wg_ref, wu_ref, wd_ref, g_ref, b_ref, o_ref, xb_scr, acc_scr):
    e = pl.program_id(1)

    @pl.when(e == 0)
    def _():
        xb_scr[...] = x_ref[...].astype(BF16)
        acc_scr[...] = jnp.zeros_like(acc_scr)

    xb = xb_scr[...]
    hid = _silu(jnp.dot(xb, wg_ref[0], preferred_element_type=F32)) * jnp.dot(
        xb, wu_ref[0], preferred_element_type=F32)
    lane = lax.broadcasted_iota(jnp.int32, gate_ref.shape, 1)
    gcol = jnp.sum(jnp.where(lane == e, gate_ref[...], 0.0), axis=1, keepdims=True)
    acc_scr[...] += _dot(hid * gcol, wd_ref[0])

    @pl.when(e == N_EXPERTS - 1)
    def _():
        o_ref[...] = _layer_norm(ALPHA * x_ref[...] + acc_scr[...], g_ref[...], b_ref[...], LN_EPS)


def _moe_call(x, gate, wg, wu, wd, g, b, tm):
    n = x.shape[0]
    row = lambda w: pl.BlockSpec((tm, w), lambda i, e: (i, 0))
    vec = pl.BlockSpec((1, D_MODEL), lambda i, e: (0, 0))
    return pl.pallas_call(
        _moe_kernel,
        grid=(n // tm, N_EXPERTS),
        in_specs=[row(D_MODEL), row(LANES),
                  pl.BlockSpec((1, D_MODEL, D_EXP), lambda i, e: (e, 0, 0)),
                  pl.BlockSpec((1, D_MODEL, D_EXP), lambda i, e: (e, 0, 0)),
                  pl.BlockSpec((1, D_EXP, D_MODEL), lambda i, e: (e, 0, 0)),
                  vec, vec],
        out_specs=row(D_MODEL),
        out_shape=jax.ShapeDtypeStruct((n, D_MODEL), F32),
        scratch_shapes=[pltpu.VMEM((tm, D_MODEL), BF16), pltpu.VMEM((tm, D_MODEL), F32)],
        compiler_params=pltpu.CompilerParams(dimension_semantics=("parallel", "arbitrary"),
                                             vmem_limit_bytes=VMEM_LIMIT),
        name="moe",
    )(x, gate, wg, wu, wd, g, b)


def _row_tile(n):
    return 512 if n % 512 == 0 else n


def _seqs_per_step(batch, seq):
    want = 2 if seq >= CHUNK else 4
    return want if batch % want == 0 else 1


def kernel(x_prompt, x_sample, state_mlstm_C, state_mlstm_n, state_mlstm_m, state_mlstm_conv, state_rwkv_S, state_rwkv_shift, ln0_g, ln0_b, w_in, conv_w, conv_b, b_i, b_f, gn_a_g, mu_shift, w0, w2, a0, a2, g2, k_k, k_a, r_k, lnx_g, lnx_b, w_out, ln1_g, ln1_b, w_router, b_router, we_gate, we_up, we_down, ln2_g, ln2_b):
    d_a_in = 4 * D_A + 2 * H_A
    w_if = jnp.zeros((DEPTH, D_MODEL, W_IF_COLS), F32)
    w_if = w_if.at[:, :, 0:H_A].set(w_in[:, :, 4 * D_A:4 * D_A + H_A])
    w_if = w_if.at[:, :, LANES:LANES + H_A].set(w_in[:, :, 4 * D_A + H_A:d_a_in])
    w_cat = jnp.concatenate([w_in[:, :, 0:4 * D_A], w_in[:, :, d_a_in:], w_if], axis=-1).astype(BF16)
    bif = jnp.zeros((DEPTH, 1, W_IF_COLS), F32)
    bif = bif.at[:, 0, 0:H_A].set(b_i).at[:, 0, LANES:LANES + H_A].set(b_f)
    wlora = jnp.zeros((DEPTH, LANES, 2 * D_B), F32)
    wlora = wlora.at[:, 0:R_DECAY, 0:D_B].set(w2).at[:, R_DECAY:, D_B:].set(a2).astype(BF16)
    w0a0 = jnp.concatenate([w0, a0], axis=-1)[:, None, :]
    g2b = g2.astype(BF16)
    wob = w_out.astype(BF16)
    wgb = we_gate.astype(BF16)
    wub = we_up.astype(BF16)
    wdb = we_down.astype(BF16)
    wr_t = w_router.T
    br = b_router[:, None]
    r1 = lambda v: v[None, :]

    def run(x3, states):
        nb, seq, _ = x3.shape
        n = nb * seq
        tm = _row_tile(n)
        rwkv_nbb = _seqs_per_step(nb, seq)
        mlstm_nbb = rwkv_nbb
        x = _ln_call(x3.reshape(n, D_MODEL), r1(ln0_g), r1(ln0_b), tm)
        outs = []
        for l in range(DEPTH):
            if states is None:
                c0 = jnp.zeros((nb, H_A, DH_A, DH_A), F32)
                n0 = jnp.zeros((nb, H_A, DH_A), F32)
                m0 = jnp.zeros((nb, H_A), F32)
                conv0 = jnp.zeros((nb, CONV_W - 1, 2 * D_A), F32)
                s0 = jnp.zeros((nb, H_B, DH_B, DH_B), F32)
                shift0 = jnp.zeros((nb, D_B_IN), F32)
            else:
                c0, n0, m0, conv0, s0, shift0 = (s[l] for s in states)
            m0p = jnp.zeros((nb, 1, LANES), F32).at[:, 0, 0:H_A].set(m0)
            pa, pb, pif = _inproj_call(x, w_cat[l], tm)
            ha, c_new, n_new, m_new = _mlstm_call(
                pa.reshape(nb, seq, W_A_COLS), pif.reshape(nb, seq, W_IF_COLS), conv0, c0, n0, m0p,
                conv_w[l], r1(conv_b[l]), bif[l], r1(gn_a_g[l]), batch=nb, seq=seq, nbb=mlstm_nbb)
            ha = ha.reshape(n, D_A)
            hb, s_new = _rwkv_call(
                pb.reshape(nb, seq, D_B_IN), shift0[:, None, :], s0, r1(mu_shift[l]), w0a0[l], wlora[l],
                g2b[l], r1(k_k[l]), r1(k_a[l]), r1(r_k[l].reshape(D_B)), r1(lnx_g[l]), r1(lnx_b[l]),
                batch=nb, seq=seq, nbb=rwkv_nbb)
            hb = hb.reshape(n, D_B)
            x1, gate = _outproj_call(x, ha, hb, wob[l], r1(ln1_g[l]), r1(ln1_b[l]), wr_t, br, tm)
            x = _moe_call(x1, gate, wgb[l], wub[l], wdb[l], r1(ln2_g[l]), r1(ln2_b[l]), tm)
            pa3 = pa.reshape(nb, seq, W_A_COLS)
            full = jnp.concatenate([conv0, pa3[:, :, 0:2 * D_A]], axis=1) if seq < CONV_W - 1 else pa3[:, :, 0:2 * D_A]
            conv_new = full[:, -(CONV_W - 1):, :]
            shift_new = pb.reshape(nb, seq, D_B_IN)[:, -1, :]
            outs.append((c_new, n_new, m_new[:, 0, 0:H_A], conv_new, s_new, shift_new))
        return x.reshape(nb, seq, D_MODEL), [jnp.stack(s) for s in zip(*outs)]

    y_prompt, (p_c, p_n, p_m, p_conv, p_s, p_shift) = run(x_prompt, None)
    y_sample, (s_c, s_n, s_m, s_conv, s_s, s_shift) = run(
        x_sample, (state_mlstm_C, state_mlstm_n, state_mlstm_m, state_mlstm_conv, state_rwkv_S,
                   state_rwkv_shift))
    return (y_prompt, y_sample, p_c, p_n, p_m, p_conv, p_s, p_shift, s_c, s_n, s_m, s_conv, s_s, s_shift)
```

```python
import functools
import math

import jax
import jax.numpy as jnp
from jax import lax
from jax.experimental import pallas as pl
from jax.experimental.pallas import tpu as pltpu

F32 = jnp.float32
BF16 = jnp.bfloat16

D_MODEL = 1024
DEPTH = 4
D_A = 512
D_B = 512
DH_A = 128
H_A = 4
DH_B = 64
H_B = 8
N_PAIRS = H_B // 2
CONV_W = 4
CHUNK = 64
R_DECAY = 64
R_AAA = 64
R_GATE = 128
D_B_IN = 3 * D_B + R_DECAY + R_AAA + R_GATE
N_EXPERTS = 16
N_GROUPS = 4
EXPERTS_PER_GROUP = 4
D_EXP = 512
ALPHA = (2 * DEPTH) ** 0.25
LN_EPS = 1e-5
GN_EPS_A = 1e-6
GN_EPS_B = 64e-5

LANES = 128
SOLVE_BLOCK = 16
VMEM_LIMIT = 48 * 1024 * 1024
MOE_VMEM_LIMIT = 56 * 1024 * 1024

NN = (((1,), (0,)), ((), ()))
NT = (((1,), (1,)), ((), ()))
TN = (((0,), (0,)), ((), ()))


def _dot(a, b, dims=NN):
    return lax.dot_general(a.astype(BF16), b.astype(BF16), dims, preferred_element_type=F32)


def _split3(x):
    hi = x.astype(BF16)
    r1 = x - hi.astype(F32)
    mid = r1.astype(BF16)
    lo = (r1 - mid.astype(F32)).astype(BF16)
    return hi, mid, lo


def _dot_exact_lhs(a, x, dims=NN):
    a = a.astype(BF16)
    hi, mid, lo = _split3(x)
    d = lambda p: lax.dot_general(a, p, dims, preferred_element_type=F32)
    return d(hi) + d(mid) + d(lo)


def _dot_exact_rhs(x, b, dims=NN):
    b = b.astype(BF16)
    hi, mid, lo = _split3(x)
    d = lambda p: lax.dot_general(p, b, dims, preferred_element_type=F32)
    return d(hi) + d(mid) + d(lo)


def _layer_norm(x, g, b, eps):
    mu = jnp.mean(x, axis=-1, keepdims=True)
    xc = x - mu
    var = jnp.mean(xc * xc, axis=-1, keepdims=True)
    return xc * lax.rsqrt(var + eps) * g + b


def _sigmoid(x):
    return 1.0 / (1.0 + jnp.exp(-x))


def _log_sigmoid(x):
    return jnp.minimum(x, 0.0) - jnp.log(1.0 + jnp.exp(-jnp.abs(x)))


def _silu(x):
    return x * _sigmoid(x)


def _ln_kernel(x_ref, g_ref, b_ref, o_ref):
    o_ref[...] = _layer_norm(x_ref[...], g_ref[...], b_ref[...], LN_EPS)


def _ln_call(x, g, b, tm):
    n, d = x.shape
    row = pl.BlockSpec((tm, d), lambda i: (i, 0))
    vec = pl.BlockSpec((1, d), lambda i: (0, 0))
    return pl.pallas_call(
        _ln_kernel,
        grid=(n // tm,),
        in_specs=[row, vec, vec],
        out_specs=row,
        out_shape=jax.ShapeDtypeStruct((n, d), F32),
        compiler_params=pltpu.CompilerParams(dimension_semantics=("parallel",)),
        name="ln0",
    )(x, g, b)


W_A_COLS = 4 * D_A
W_IF_COLS = 2 * LANES
W_IN_COLS = W_A_COLS + D_B_IN + W_IF_COLS


def _inproj_kernel(x_ref, w_ref, pa_ref, pb_ref, pif_ref):
    xb = x_ref[...].astype(BF16)
    pa_ref[...] = jnp.dot(xb, w_ref[:, 0:W_A_COLS], preferred_element_type=F32)
    pb_ref[...] = jnp.dot(xb, w_ref[:, W_A_COLS:W_A_COLS + D_B_IN], preferred_element_type=F32)
    pif_ref[...] = jnp.dot(xb, w_ref[:, W_A_COLS + D_B_IN:W_IN_COLS], preferred_element_type=F32)


def _inproj_call(x, w, tm):
    n = x.shape[0]
    return pl.pallas_call(
        _inproj_kernel,
        grid=(n // tm,),
        in_specs=[pl.BlockSpec((tm, D_MODEL), lambda i: (i, 0)),
                  pl.BlockSpec((D_MODEL, W_IN_COLS), lambda i: (0, 0))],
        out_specs=[pl.BlockSpec((tm, W_A_COLS), lambda i: (i, 0)),
                   pl.BlockSpec((tm, D_B_IN), lambda i: (i, 0)),
                   pl.BlockSpec((tm, W_IF_COLS), lambda i: (i, 0))],
        out_shape=[jax.ShapeDtypeStruct((n, W_A_COLS), F32),
                   jax.ShapeDtypeStruct((n, D_B_IN), F32),
                   jax.ShapeDtypeStruct((n, W_IF_COLS), F32)],
        compiler_params=pltpu.CompilerParams(dimension_semantics=("parallel",),
                                             vmem_limit_bytes=VMEM_LIMIT),
        name="inproj",
    )(x, w)


CONV_PAD = 8


def _mlstm_kernel(pa_ref, pif_ref, conv0_ref, c0_ref, n0_ref, m0_ref, convw_ref, convb_ref,
                  bif_ref, gn_ref, h_ref, c_out_ref, n_out_ref, m_out_ref,
                  qk_scr, c_scr, n_scr, m_scr, *, c, nc, nbb):
    ci = pl.program_id(1)
    rows = nbb * c
    cshift = int(math.log2(c))
    chains = [(nb, h) for nb in range(nbb) for h in range(H_A)]
    prev0 = CONV_PAD - (CONV_W - 1)

    @pl.when(ci == 0)
    def _():
        c_scr[...] = c0_ref[...]
        n_scr[...] = n0_ref[...]
        m_scr[...] = m0_ref[...]
        for nb in range(nbb):
            qk_scr[nb, prev0:CONV_PAD, :] = conv0_ref[nb]

    cat0 = lambda xs: xs[0] if len(xs) == 1 else jnp.concatenate(xs, axis=0)
    acc_l = []
    for nb in range(nbb):
        u = pa_ref[nb, :, 0:2 * D_A]
        qk_scr[nb, CONV_PAD:CONV_PAD + c, :] = u
        acc = convb_ref[...]
        for j in range(CONV_W):
            acc = acc + qk_scr[nb, prev0 + j:prev0 + j + c, :] * convw_ref[j:j + 1, :]
        qk_scr[nb, prev0:CONV_PAD, :] = u[c - (CONV_W - 1):c, :]
        acc_l.append(acc)
    qk = _silu(cat0(acc_l))

    gates = cat0([pif_ref[nb] for nb in range(nbb)]) + bif_ref[...]
    li_all = gates[:, 0:LANES]
    lf_all = _log_sigmoid(gates[:, LANES:2 * LANES])
    row = lax.broadcasted_iota(jnp.int32, (rows, rows), 0)
    col = lax.broadcasted_iota(jnp.int32, (rows, rows), 1)
    tril = jnp.where((col <= row) & ((row >> cshift) == (col >> cshift)), 1.0, 0.0)
    b_all = _dot_exact_lhs(tril, lf_all)
    z_all = li_all - b_all
    crow = lax.broadcasted_iota(jnp.int32, (c, c), 0)
    ccol = lax.broadcasted_iota(jnp.int32, (c, c), 1)
    causal = ccol <= crow
    hrow = lax.broadcasted_iota(jnp.int32, (H_A * c, LANES), 0)
    hlane = lax.broadcasted_iota(jnp.int32, (H_A * c, LANES), 1)
    head_pick = jnp.where((hrow >> cshift) == hlane, 1.0, 0.0)
    z_rows = [_dot_exact_lhs(head_pick, z_all[nb * c:(nb + 1) * c], NT) for nb in range(nbb)]
    lane1 = lax.broadcasted_iota(jnp.int32, (1, LANES), 1)

    rs = lambda nb: slice(nb * c, (nb + 1) * c)
    q_l = [qk[rs(nb), h * DH_A:(h + 1) * DH_A] for nb, h in chains]
    k_l = [qk[rs(nb), D_A + h * DH_A:D_A + (h + 1) * DH_A] * (DH_A ** -0.5) for nb, h in chains]
    v_l = [pa_ref[nb, :, 2 * D_A + h * DH_A:2 * D_A + (h + 1) * DH_A] for nb, h in chains]
    c_l = [c_scr[nb, h] for nb, h in chains]
    n_l = [n_scr[nb, h:h + 1, :] for nb, h in chains]
    b_col = [b_all[rs(nb), h:h + 1] for nb, h in chains]
    li_col = [li_all[rs(nb), h:h + 1] for nb, h in chains]
    m_prev = [m_scr[nb][:, h:h + 1] for nb, h in chains]
    dmat = [jnp.where(causal, bc + z_rows[nb][h * c:(h + 1) * c], -jnp.inf)
            for bc, (nb, h) in zip(b_col, chains)]
    m_inter = [bc + mp for bc, mp in zip(b_col, m_prev)]
    m_t = [jnp.maximum(mi, jnp.max(d, axis=-1, keepdims=True)) for mi, d in zip(m_inter, dmat)]
    qk_dot = [_dot(q, k, NT) for q, k in zip(q_l, k_l)]
    qc = [_dot(q, cm) for q, cm in zip(q_l, c_l)]
    s_l = [x * jnp.exp(d - mt) for x, d, mt in zip(qk_dot, dmat, m_t)]
    sv = [_dot(s, v) for s, v in zip(s_l, v_l)]
    b_last = [bc[c - 1:c, :] for bc in b_col]
    g_s = [bl - bc + li for bl, bc, li in zip(b_last, b_col, li_col)]
    m_new = [jnp.maximum(bl + mp, jnp.max(gs, axis=0, keepdims=True))
             for bl, mp, gs in zip(b_last, m_prev, g_s)]
    wk = [jnp.exp(gs - mn) * k for gs, mn, k in zip(g_s, m_new, k_l)]
    w_old = [jnp.exp(bl + mp - mn) for bl, mp, mn in zip(b_last, m_prev, m_new)]
    kv = [_dot(w, v, TN) for w, v in zip(wk, v_l)]
    for i, (nb, h) in enumerate(chains):
        c_scr[nb, h] = w_old[i] * c_l[i] + kv[i]
        n_scr[nb, h:h + 1, :] = w_old[i] * n_l[i] + jnp.sum(wk[i], axis=0, keepdims=True)
    for nb in range(nbb):
        m_row = m_scr[nb]
        for h in range(H_A):
            m_row = jnp.where(lane1 == h, m_new[nb * H_A + h], m_row)
        m_scr[nb] = m_row
    w_inter = [jnp.exp(mi - mt) for mi, mt in zip(m_inter, m_t)]
    qn = [jnp.sum(q * nv, axis=-1, keepdims=True) for q, nv in zip(q_l, n_l)]
    s_sum = [jnp.sum(s, axis=-1, keepdims=True) for s in s_l]
    den = [w * a + b for w, a, b in zip(w_inter, qn, s_sum)]
    hh = [(w * a + b) / jnp.maximum(jnp.abs(d), jnp.exp(-mt))
          for w, a, b, d, mt in zip(w_inter, qc, sv, den, m_t)]
    mu = [jnp.mean(x, axis=-1, keepdims=True) for x in hh]
    hc = [x - m for x, m in zip(hh, mu)]
    var = [jnp.mean(x * x, axis=-1, keepdims=True) for x in hc]
    for i, (nb, h) in enumerate(chains):
        sl = slice(h * DH_A, (h + 1) * DH_A)
        hn = hc[i] * lax.rsqrt(var[i] + GN_EPS_A)
        o_pre = pa_ref[nb, :, 3 * D_A + h * DH_A:3 * D_A + (h + 1) * DH_A]
        h_ref[nb, :, sl] = hn * gn_ref[:, sl] * _sigmoid(o_pre)

    @pl.when(ci == nc - 1)
    def _():
        c_out_ref[...] = c_scr[...]
        n_out_ref[...] = n_scr[...]
        m_out_ref[...] = m_scr[...]


def _mlstm_call(pa, pif, conv0, c0, n0, m0, convw, convb, bif, gn, *, batch, seq, nbb):
    c = math.gcd(seq, CHUNK)
    nc = seq // c
    tok = lambda w: pl.BlockSpec((nbb, c, w), lambda b, i: (b, i, 0))
    vec = lambda r, w: pl.BlockSpec((r, w), lambda b, i: (0, 0))
    st4 = pl.BlockSpec((nbb, H_A, DH_A, DH_A), lambda b, i: (b, 0, 0, 0))
    st3 = lambda r, w: pl.BlockSpec((nbb, r, w), lambda b, i: (b, 0, 0))
    return pl.pallas_call(
        functools.partial(_mlstm_kernel, c=c, nc=nc, nbb=nbb),
        grid=(batch // nbb, nc),
        in_specs=[tok(W_A_COLS), tok(W_IF_COLS), st3(CONV_W - 1, 2 * D_A), st4, st3(H_A, DH_A),
                  st3(1, LANES), vec(CONV_W, 2 * D_A), vec(1, 2 * D_A), vec(1, W_IF_COLS), vec(1, D_A)],
        out_specs=[tok(D_A), st4, st3(H_A, DH_A), st3(1, LANES)],
        out_shape=[jax.ShapeDtypeStruct((batch, seq, D_A), F32),
                   jax.ShapeDtypeStruct((batch, H_A, DH_A, DH_A), F32),
                   jax.ShapeDtypeStruct((batch, H_A, DH_A), F32),
                   jax.ShapeDtypeStruct((batch, 1, LANES), F32)],
        scratch_shapes=[pltpu.VMEM((nbb, CONV_PAD + c, 2 * D_A), F32),
                        pltpu.VMEM((nbb, H_A, DH_A, DH_A), F32),
                        pltpu.VMEM((nbb, H_A, DH_A), F32),
                        pltpu.VMEM((nbb, 1, LANES), F32)],
        compiler_params=pltpu.CompilerParams(dimension_semantics=("parallel", "arbitrary"),
                                             vmem_limit_bytes=VMEM_LIMIT),
        name="mlstm",
    )(pa, pif, conv0, c0, n0, m0, convw, convb, bif, gn)


SHIFT_PAD = 8


def _neumann_inverses(ms, size, t):
    row = lax.broadcasted_iota(jnp.int32, (size, size), 0)
    col = lax.broadcasted_iota(jnp.int32, (size, size), 1)
    eye = jnp.where(row == col, 1.0, 0.0)
    blk = min(SOLVE_BLOCK, t)
    shift = int(math.log2(blk))
    same = (row >> shift) == (col >> shift)
    ds = [jnp.where(same, m, 0.0) for m in ms]
    ps = [eye + d for d in ds]
    xs = ds
    for _ in range(shift - 1):
        xs = [_dot(x, x) for x in xs]
        ps = [p + _dot(p, x) for p, x in zip(ps, xs)]
    nblk = t // blk
    if nblk == 1:
        return ps, None
    ns = [_dot(p, m - d) for p, m, d in zip(ps, ms, ds)]
    qs = [eye + n for n in ns]
    ys = ns
    for _ in range(int(math.log2(nblk)) - 1):
        ys = [_dot(y, y) for y in ys]
        qs = [q + _dot(q, y) for q, y in zip(qs, ys)]
    return ps, qs


def _rwkv_kernel(pb_ref, shift0_ref, s0_ref, mu_ref, w0a0_ref, wlora_ref, g2_ref, kk_ref, ka_ref,
                 rk_ref, lng_ref, lnb_ref, h_ref, s_out_ref, sh_scr, s_scr, *, t, nc, nbb):
    ci = pl.program_id(1)
    rows = nbb * t
    t2 = 2 * t
    groups = [(nb, pr) for nb in range(nbb) for pr in range(N_PAIRS)]

    brow = lax.broadcasted_iota(jnp.int32, (LANES, LANES), 0)
    bcol = lax.broadcasted_iota(jnp.int32, (LANES, LANES), 1)
    blockdiag = (brow >= DH_B) == (bcol >= DH_B)
    bd2 = jnp.where(blockdiag, 1.0, 0.0).astype(BF16)

    @pl.when(ci == 0)
    def _():
        er = lax.broadcasted_iota(jnp.int32, (DH_B, LANES), 0)
        ec = lax.broadcasted_iota(jnp.int32, (DH_B, LANES), 1)
        dup_cols = jnp.where((ec & (DH_B - 1)) == er, 1.0, 0.0)
        for nb in range(nbb):
            sh_scr[nb, SHIFT_PAD - 1:SHIFT_PAD, :] = shift0_ref[nb]
            for pr in range(N_PAIRS):
                x = s0_ref[nb, 2 * pr:2 * pr + 2].reshape(LANES, DH_B)
                s_scr[nb * N_PAIRS + pr] = jnp.where(blockdiag, _dot_exact_rhs(x, dup_cols), 0.0)

    p_l, prev_l = [], []
    for nb in range(nbb):
        p_nb = pb_ref[nb]
        sh_scr[nb, SHIFT_PAD:SHIFT_PAD + t, :] = p_nb
        prev_l.append(sh_scr[nb, SHIFT_PAD - 1:SHIFT_PAD - 1 + t, :])
        sh_scr[nb, SHIFT_PAD - 1:SHIFT_PAD, :] = p_nb[t - 1:t, :]
        p_l.append(p_nb)
    cat0 = lambda xs: xs[0] if len(xs) == 1 else jnp.concatenate(xs, axis=0)
    p = cat0(p_l)
    prev = cat0(prev_l)
    pb = p + (prev - p) * mu_ref[...]

    r = pb[:, 0:D_B]
    kr = pb[:, D_B:2 * D_B]
    vr = pb[:, 2 * D_B:3 * D_B]
    lora_in = pb[:, 3 * D_B:3 * D_B + LANES]
    gl = pb[:, 3 * D_B + LANES:3 * D_B + 2 * LANES]
    lane = lax.broadcasted_iota(jnp.int32, (rows, LANES), 1)
    lora_act = jnp.where(lane < R_DECAY, jnp.tanh(lora_in), lora_in)
    z = _dot(lora_act, wlora_ref[...]) + w0a0_ref[...]
    ld = -jnp.exp(_log_sigmoid(z[:, 0:D_B]) - 0.5)
    a = _sigmoid(z[:, D_B:2 * D_B])
    g = _dot(_sigmoid(gl), g2_ref[...])

    def seg_sum(x):
        xr = jnp.concatenate([x[:, q * LANES:(q + 1) * LANES] for q in range(N_PAIRS)], axis=0)
        hi = xr.astype(BF16)
        lo = (xr - hi.astype(F32)).astype(BF16)
        s = (jnp.dot(hi, bd2, preferred_element_type=F32) + jnp.dot(lo, bd2, preferred_element_type=F32))
        return jnp.concatenate([s[q * rows:(q + 1) * rows] for q in range(N_PAIRS)], axis=1)

    kk_raw = kr * kk_ref[...]
    kk = kk_raw / jnp.maximum(jnp.sqrt(seg_sum(kk_raw * kk_raw)), 1e-12)
    k2 = kr * (1.0 + (a - 1.0) * ka_ref[...])
    bonus = seg_sum(r * k2 * rk_ref[...]) * vr

    row = lax.broadcasted_iota(jnp.int32, (rows, rows), 0)
    col = lax.broadcasted_iota(jnp.int32, (rows, rows), 1)
    tshift = int(math.log2(t))
    tril = jnp.where((col <= row) & ((row >> tshift) == (col >> tshift)), 1.0, 0.0)
    lc = _dot_exact_lhs(tril, ld)
    lends = [lc[(nb + 1) * t - 1:(nb + 1) * t, :] for nb in range(nbb)]
    lend_rows = cat0([jnp.broadcast_to(le, (t, D_B)) for le in lends])
    e_nc = jnp.exp(-lc)
    e_end = jnp.exp(lend_rows - lc)
    b_raw = kk * a
    at = -kk * jnp.exp(lc - ld)
    rt = r * jnp.exp(lc)
    kt = k2 * e_nc
    bt = b_raw * e_nc
    kend = k2 * e_end
    bend = b_raw * e_end

    srow = lax.broadcasted_iota(jnp.int32, (t2, t2), 0)
    scol = lax.broadcasted_iota(jnp.int32, (t2, t2), 1)
    strict = scol < srow
    incl = scol <= srow
    lane2 = lax.broadcasted_iota(jnp.int32, (t2, LANES), 1)
    row2 = lax.broadcasted_iota(jnp.int32, (t2, LANES), 0)
    head_sel = (lane2 >= DH_B) == (row2 >= t)
    first_head = lax.broadcasted_iota(jnp.int32, (t, LANES), 1) < DH_B

    def blk(x, nb, pr):
        return x[nb * t:(nb + 1) * t, pr * LANES:(pr + 1) * LANES]

    def stack(x):
        return jnp.where(head_sel, jnp.concatenate([x, x], axis=0), 0.0)

    lhs = [jnp.concatenate([stack(blk(at, *gp)), stack(blk(rt, *gp))], axis=0) for gp in groups]
    rk_s = [stack(blk(kt, *gp)) for gp in groups]
    rb_s = [stack(blk(bt, *gp)) for gp in groups]
    v_p = [blk(vr, *gp) for gp in groups]
    v_s = [jnp.concatenate([v, v], axis=0) for v in v_p]
    s_p = [s_scr[i] for i in range(len(groups))]
    gk = [_dot(l, x, NT) for l, x in zip(lhs, rk_s)]
    gb = [_dot(l, x, NT) for l, x in zip(lhs, rb_s)]
    xs = [_dot(l, s, NT) for l, s in zip(lhs, s_p)]
    m_ab = [jnp.where(strict, x[0:t2], 0.0) for x in gb]
    ps, qs = _neumann_inverses(m_ab, t2, t)
    rhs = [x[0:t2] + _dot(jnp.where(strict, y[0:t2], 0.0), v) for x, y, v in zip(xs, gk, v_s)]
    sa_s = [_dot(pm, x) for pm, x in zip(ps, rhs)]
    if qs is not None:
        sa_s = [_dot(qm, x) for qm, x in zip(qs, sa_s)]
    o_s = [x[t2:] + _dot(jnp.where(incl, y[t2:], 0.0), v) + _dot(jnp.where(incl, w[t2:], 0.0), sa)
           for x, y, w, v, sa in zip(xs, gk, gb, v_s, sa_s)]
    o_p = [jnp.where(first_head, o[0:t], o[t:t2]) for o in o_s]
    sa_p = [jnp.where(first_head, sa[0:t], sa[t:t2]) for sa in sa_s]
    upd = [_dot(jnp.concatenate([v, sa], axis=0),
                jnp.concatenate([blk(kend, *gp), blk(bend, *gp)], axis=0), TN)
           for v, sa, gp in zip(v_p, sa_p, groups)]
    for i, (nb, pr) in enumerate(groups):
        dec = jnp.exp(lends[nb][:, pr * LANES:(pr + 1) * LANES])
        s_scr[i] = s_p[i] * dec + jnp.where(blockdiag, upd[i], 0.0)

    o = cat0([jnp.concatenate(o_p[nb * N_PAIRS:(nb + 1) * N_PAIRS], axis=1) for nb in range(nbb)])
    mean = seg_sum(o) * (1.0 / DH_B)
    oc = o - mean
    var = seg_sum(oc * oc) * (1.0 / DH_B)
    out = oc * lax.rsqrt(var + GN_EPS_B) * lng_ref[...] + lnb_ref[...]
    res = (out + bonus) * g
    for nb in range(nbb):
        h_ref[nb] = res[nb * t:(nb + 1) * t, :]

    @pl.when(ci == nc - 1)
    def _():
        fr = lax.broadcasted_iota(jnp.int32, (LANES, DH_B), 0)
        fc = lax.broadcasted_iota(jnp.int32, (LANES, DH_B), 1)
        dup_rows = jnp.where((fr & (DH_B - 1)) == fc, 1.0, 0.0)
        for i, (nb, pr) in enumerate(groups):
            packed = _dot_exact_rhs(s_scr[i], dup_rows)
            s_out_ref[nb, 2 * pr:2 * pr + 2] = packed.reshape(2, DH_B, DH_B)


def _rwkv_call(pb, shift0, s0, mu, w0a0, wlora, g2, kk, ka, rk, lng, lnb, *, batch, seq, nbb):
    t = math.gcd(seq, CHUNK)
    nc = seq // t
    tok = lambda w: pl.BlockSpec((nbb, t, w), lambda b, i: (b, i, 0))
    vec = lambda r, w: pl.BlockSpec((r, w), lambda b, i: (0, 0))
    st4 = pl.BlockSpec((nbb, H_B, DH_B, DH_B), lambda b, i: (b, 0, 0, 0))
    return pl.pallas_call(
        functools.partial(_rwkv_kernel, t=t, nc=nc, nbb=nbb),
        grid=(batch // nbb, nc),
        in_specs=[tok(D_B_IN), pl.BlockSpec((nbb, 1, D_B_IN), lambda b, i: (b, 0, 0)), st4,
                  vec(1, D_B_IN), vec(1, 2 * D_B), vec(LANES, 2 * D_B), vec(R_GATE, D_B),
                  vec(1, D_B), vec(1, D_B), vec(1, D_B), vec(1, D_B), vec(1, D_B)],
        out_specs=[tok(D_B), st4],
        out_shape=[jax.ShapeDtypeStruct((batch, seq, D_B), F32),
                   jax.ShapeDtypeStruct((batch, H_B, DH_B, DH_B), F32)],
        scratch_shapes=[pltpu.VMEM((nbb, SHIFT_PAD + t, D_B_IN), F32),
                        pltpu.VMEM((nbb * N_PAIRS, LANES, LANES), F32)],
        compiler_params=pltpu.CompilerParams(dimension_semantics=("parallel", "arbitrary"),
                                             vmem_limit_bytes=VMEM_LIMIT),
        name="rwkv",
    )(pb, shift0, s0, mu, w0a0, wlora, g2, kk, ka, rk, lng, lnb)


ROUTE_ROWS = 8


def _outproj_kernel(x_ref, ha_ref, hb_ref, wo_ref, g_ref, b_ref, wr_ref, br_ref, x1_ref, route_ref,
                    cnt_ref, *, tm):
    y = _dot(ha_ref[...], wo_ref[0:D_A, :]) + _dot(hb_ref[...], wo_ref[D_A:D_A + D_B, :])
    x1 = _layer_norm(ALPHA * x_ref[...] + y, g_ref[...], b_ref[...], LN_EPS)
    x1_ref[...] = x1
    logits = lax.dot_general(wr_ref[...], x1, NT, precision=lax.Precision.HIGHEST,
                             preferred_element_type=F32)
    mx = jnp.max(logits, axis=0, keepdims=True)
    ex = jnp.exp(logits - mx)
    probs = ex / jnp.sum(ex, axis=0, keepdims=True)
    sel = probs + br_ref[...]
    neg = -jnp.inf

    def top2(rows):
        m1 = functools.reduce(jnp.maximum, rows)
        i1 = jnp.full(m1.shape, len(rows) - 1, jnp.int32)
        for j in range(len(rows) - 2, -1, -1):
            i1 = jnp.where(rows[j] == m1, j, i1)
        rest = [jnp.where(i1 == j, neg, rows[j]) for j in range(len(rows))]
        m2 = functools.reduce(jnp.maximum, rest)
        i2 = jnp.full(m2.shape, len(rows) - 1, jnp.int32)
        for j in range(len(rows) - 2, -1, -1):
            i2 = jnp.where(rest[j] == m2, j, i2)
        return m1, i1, m2, i2

    g_score, g_i1, g_i2 = [], [], []
    for gidx in range(N_GROUPS):
        rows = [sel[gidx * EXPERTS_PER_GROUP + j:gidx * EXPERTS_PER_GROUP + j + 1, :]
                for j in range(EXPERTS_PER_GROUP)]
        m1, i1, m2, i2 = top2(rows)
        g_score.append(m1 + m2)
        g_i1.append(i1)
        g_i2.append(i2)
    best = functools.reduce(jnp.maximum, g_score)
    grp = jnp.full(best.shape, N_GROUPS - 1, jnp.int32)
    for gidx in range(N_GROUPS - 2, -1, -1):
        grp = jnp.where(g_score[gidx] == best, gidx, grp)
    l1 = g_i1[N_GROUPS - 1]
    l2 = g_i2[N_GROUPS - 1]
    for gidx in range(N_GROUPS - 2, -1, -1):
        l1 = jnp.where(grp == gidx, g_i1[gidx], l1)
        l2 = jnp.where(grp == gidx, g_i2[gidx], l2)
    e1 = grp * EXPERTS_PER_GROUP + l1
    e2 = grp * EXPERTS_PER_GROUP + l2
    e_iota = lax.broadcasted_iota(jnp.int32, (N_EXPERTS, tm), 0)
    p1 = jnp.sum(jnp.where(e_iota == e1, probs, 0.0), axis=0, keepdims=True)
    p2 = jnp.sum(jnp.where(e_iota == e2, probs, 0.0), axis=0, keepdims=True)
    tot = p1 + p2
    r8 = lax.broadcasted_iota(jnp.int32, (ROUTE_ROWS, tm), 0)
    route_ref[...] = jnp.where(r8 == 0, e1.astype(F32),
                               jnp.where(r8 == 1, e2.astype(F32),
                                         jnp.where(r8 == 2, p1 / tot, jnp.where(r8 == 3, p2 / tot, 0.0))))
    picked = jnp.where((e_iota == e1) | (e_iota == e2), 1.0, 0.0)
    cnt_ref[0] = jnp.broadcast_to(jnp.sum(picked, axis=1, keepdims=True), (N_EXPERTS, LANES))


def _outproj_call(x, ha, hb, wo, g, b, wr_t, br, tm):
    n = x.shape[0]
    row = lambda w: pl.BlockSpec((tm, w), lambda i: (i, 0))
    full = lambda r, w: pl.BlockSpec((r, w), lambda i: (0, 0))
    return pl.pallas_call(
        functools.partial(_outproj_kernel, tm=tm),
        grid=(n // tm,),
        in_specs=[row(D_MODEL), row(D_A), row(D_B), full(D_MODEL, D_MODEL), full(1, D_MODEL),
                  full(1, D_MODEL), full(N_EXPERTS, D_MODEL), full(N_EXPERTS, 1)],
        out_specs=[row(D_MODEL), pl.BlockSpec((ROUTE_ROWS, tm), lambda i: (0, i)),
                   pl.BlockSpec((1, N_EXPERTS, LANES), lambda i: (i, 0, 0))],
        out_shape=[jax.ShapeDtypeStruct((n, D_MODEL), F32), jax.ShapeDtypeStruct((ROUTE_ROWS, n), F32),
                   jax.ShapeDtypeStruct((n // tm, N_EXPERTS, LANES), F32)],
        compiler_params=pltpu.CompilerParams(dimension_semantics=("parallel",),
                                             vmem_limit_bytes=VMEM_LIMIT),
        name="outproj",
    )(x, ha, hb, wo, g, b, wr_t, br)


MOE_BM = 128
MOE_CH = 512


def _moe_max_blocks(t):
    return -(-2 * t // MOE_BM) + N_EXPERTS - 1


def _moe_kernel(blk_e_ref, nblk_ref, x_ref, route_ref, padoff_ref, tri_ref, wg_ref, wu_ref, wd_ref,
                g_ref, b_ref, o_ref, xb_scr, xs_scr, cols_scr, acc_scr, *, t, maxb, n_chunks):
    i = pl.program_id(0)
    j = pl.program_id(1)
    nblk = nblk_ref[i]
    s_used = nblk * MOE_BM

    @pl.when(j == 0)
    def _():
        xb = x_ref[...].astype(BF16)
        xb_scr[...] = xb
        route = route_ref[...]
        e_iota = lax.broadcasted_iota(jnp.int32, (N_EXPERTS, t), 0)
        a1 = e_iota == route[0:1].astype(jnp.int32)
        a2 = e_iota == route[1:2].astype(jnp.int32)
        picked = jnp.where(a1 | a2, 1.0, 0.0).astype(BF16)
        rank = jnp.dot(picked, tri_ref[...], preferred_element_type=F32)
        base = padoff_ref[0][:, 0:1] + rank
        slot1 = jnp.sum(jnp.where(a1, base, 0.0), axis=0, keepdims=True)
        slot2 = jnp.sum(jnp.where(a2, base, 0.0), axis=0, keepdims=True)
        r8 = lax.broadcasted_iota(jnp.int32, (ROUTE_ROWS, t), 0)
        rows = jnp.where(r8 == 0, slot1, jnp.where(r8 == 1, slot2,
                                                   jnp.where(r8 == 2, route[2:3],
                                                             jnp.where(r8 == 3, route[3:4], 0.0))))
        padded = jnp.concatenate([rows, jnp.zeros((LANES - ROUTE_ROWS, t), F32)], axis=0)
        cols_scr[...] = padded.T
        s1 = slot1.astype(jnp.int32)
        s2 = slot2.astype(jnp.int32)
        for c in range(n_chunks):
            @pl.when(c * MOE_CH < s_used)
            def _():
                s_iota = lax.broadcasted_iota(jnp.int32, (MOE_CH, t), 0) + c * MOE_CH
                onehot = jnp.where((s_iota == s1) | (s_iota == s2), 1.0, 0.0).astype(BF16)
                xs_scr[c * MOE_CH:(c + 1) * MOE_CH, :] = jnp.dot(
                    onehot, xb, preferred_element_type=F32).astype(BF16)

    @pl.when(j < nblk)
    def _():
        r0 = pl.multiple_of(j * MOE_BM, MOE_BM)
        xblk = xs_scr[pl.ds(r0, MOE_BM), :]
        hid = _silu(jnp.dot(xblk, wg_ref[0, 0], preferred_element_type=F32)) * jnp.dot(
            xblk, wu_ref[0, 0], preferred_element_type=F32)
        y = jnp.dot(hid.astype(BF16), wd_ref[0, 0], preferred_element_type=F32)
        xs_scr[pl.ds(r0, MOE_BM), :] = y.astype(BF16)

    @pl.when(j == maxb - 1)
    def _():
        cols = cols_scr[...]
        s1c = cols[:, 0:1].astype(jnp.int32)
        s2c = cols[:, 1:2].astype(jnp.int32)
        w1c = cols[:, 2:3]
        w2c = cols[:, 3:4]
        acc_scr[...] = jnp.zeros_like(acc_scr)
        for c in range(n_chunks):
            @pl.when(c * MOE_CH < s_used)
            def _():
                l_iota = lax.broadcasted_iota(jnp.int32, (t, MOE_CH), 1) + c * MOE_CH
                weighted = (jnp.where(l_iota == s1c, w1c, 0.0) + jnp.where(l_iota == s2c, w2c, 0.0)).astype(BF16)
                acc_scr[...] += jnp.dot(weighted, xs_scr[c * MOE_CH:(c + 1) * MOE_CH, :],
                                        preferred_element_type=F32)
        o_ref[...] = _layer_norm(ALPHA * x_ref[...] + acc_scr[...], g_ref[...], b_ref[...], LN_EPS)


def _moe_schedule(counts, tiles_per, maxb):
    cnt = counts[:, :, 0].astype(jnp.int32)
    cnt = cnt.reshape(-1, tiles_per, N_EXPERTS).sum(axis=1)
    nblk = (cnt + MOE_BM - 1) // MOE_BM
    end = jnp.cumsum(nblk, axis=-1)
    start = end - nblk
    steps = jnp.arange(maxb, dtype=jnp.int32)
    blk_e = jnp.sum((steps[None, :, None] >= end[:, None, :]).astype(jnp.int32), axis=-1)
    last_e = jnp.max(jnp.where(nblk > 0, jnp.arange(N_EXPERTS, dtype=jnp.int32)[None, :], 0), axis=-1)
    blk_e = jnp.minimum(blk_e, last_e[:, None])
    padoff = jnp.broadcast_to((start * MOE_BM).astype(F32)[:, :, None], (cnt.shape[0], N_EXPERTS, LANES))
    return blk_e, end[:, -1], padoff


def _moe_call(x, route, counts, wg, wu, wd, g, b, *, layer, tm):
    n = x.shape[0]
    t = 1024 if n % 1024 == 0 else n
    maxb = _moe_max_blocks(t)
    n_chunks = -(-maxb * MOE_BM // MOE_CH)
    blk_e, nblk, padoff = _moe_schedule(counts, t // tm, maxb)
    tri = jnp.triu(jnp.ones((t, t), BF16), k=1)
    wspec = lambda r, c: pl.BlockSpec((1, 1, r, c), lambda i, j, be, nb: (layer, be[i, j], 0, 0))
    vec = pl.BlockSpec((1, D_MODEL), lambda i, j, be, nb: (0, 0))
    return pl.pallas_call(
        functools.partial(_moe_kernel, t=t, maxb=maxb, n_chunks=n_chunks),
        grid_spec=pltpu.PrefetchScalarGridSpec(
            num_scalar_prefetch=2,
            grid=(n // t, maxb),
            in_specs=[pl.BlockSpec((t, D_MODEL), lambda i, j, be, nb: (i, 0)),
                      pl.BlockSpec((ROUTE_ROWS, t), lambda i, j, be, nb: (0, i)),
                      pl.BlockSpec((1, N_EXPERTS, LANES), lambda i, j, be, nb: (i, 0, 0)),
                      pl.BlockSpec((t, t), lambda i, j, be, nb: (0, 0)),
                      wspec(D_MODEL, D_EXP), wspec(D_MODEL, D_EXP), wspec(D_EXP, D_MODEL), vec, vec],
            out_specs=pl.BlockSpec((t, D_MODEL), lambda i, j, be, nb: (i, 0)),
            scratch_shapes=[pltpu.VMEM((t, D_MODEL), BF16),
                            pltpu.VMEM((n_chunks * MOE_CH, D_MODEL), BF16),
                            pltpu.VMEM((t, LANES), F32),
                            pltpu.VMEM((t, D_MODEL), F32)]),
        out_shape=jax.ShapeDtypeStruct((n, D_MODEL), F32),
        compiler_params=pltpu.CompilerParams(dimension_semantics=("parallel", "arbitrary"),
                                             vmem_limit_bytes=MOE_VMEM_LIMIT),
        name="moe",
    )(blk_e, nblk, x, route, padoff, tri, wg, wu, wd, g, b)


def _row_tile(n):
    return 512 if n % 512 == 0 else n


def _seqs_per_step(batch, seq):
    want = 2 if seq >= CHUNK else 4
    return want if batch % want == 0 else 1


def kernel(x_prompt, x_sample, state_mlstm_C, state_mlstm_n, state_mlstm_m, state_mlstm_conv, state_rwkv_S, state_rwkv_shift, ln0_g, ln0_b, w_in, conv_w, conv_b, b_i, b_f, gn_a_g, mu_shift, w0, w2, a0, a2, g2, k_k, k_a, r_k, lnx_g, lnx_b, w_out, ln1_g, ln1_b, w_router, b_router, we_gate, we_up, we_down, ln2_g, ln2_b):
    d_a_in = 4 * D_A + 2 * H_A
    w_if = jnp.zeros((DEPTH, D_MODEL, W_IF_COLS), F32)
    w_if = w_if.at[:, :, 0:H_A].set(w_in[:, :, 4 * D_A:4 * D_A + H_A])
    w_if = w_if.at[:, :, LANES:LANES + H_A].set(w_in[:, :, 4 * D_A + H_A:d_a_in])
    w_cat = jnp.concatenate([w_in[:, :, 0:4 * D_A], w_in[:, :, d_a_in:], w_if], axis=-1).astype(BF16)
    bif = jnp.zeros((DEPTH, 1, W_IF_COLS), F32)
    bif = bif.at[:, 0, 0:H_A].set(b_i).at[:, 0, LANES:LANES + H_A].set(b_f)
    wlora = jnp.zeros((DEPTH, LANES, 2 * D_B), F32)
    wlora = wlora.at[:, 0:R_DECAY, 0:D_B].set(w2).at[:, R_DECAY:, D_B:].set(a2).astype(BF16)
    w0a0 = jnp.concatenate([w0, a0], axis=-1)[:, None, :]
    g2b = g2.astype(BF16)
    wob = w_out.astype(BF16)
    wgb = we_gate.astype(BF16)
    wub = we_up.astype(BF16)
    wdb = we_down.astype(BF16)
    wr_t = w_router.T
    br = b_router[:, None]
    r1 = lambda v: v[None, :]

    def run(x3, states):
        nb, seq, _ = x3.shape
        n = nb * seq
        tm = _row_tile(n)
        rwkv_nbb = _seqs_per_step(nb, seq)
        mlstm_nbb = rwkv_nbb
        x = _ln_call(x3.reshape(n, D_MODEL), r1(ln0_g), r1(ln0_b), tm)
        outs = []
        for l in range(DEPTH):
            if states is None:
                c0 = jnp.zeros((nb, H_A, DH_A, DH_A), F32)
                n0 = jnp.zeros((nb, H_A, DH_A), F32)
                m0 = jnp.zeros((nb, H_A), F32)
                conv0 = jnp.zeros((nb, CONV_W - 1, 2 * D_A), F32)
                s0 = jnp.zeros((nb, H_B, DH_B, DH_B), F32)
                shift0 = jnp.zeros((nb, D_B_IN), F32)
            else:
                c0, n0, m0, conv0, s0, shift0 = (s[l] for s in states)
            m0p = jnp.zeros((nb, 1, LANES), F32).at[:, 0, 0:H_A].set(m0)
            pa, pb, pif = _inproj_call(x, w_cat[l], tm)
            ha, c_new, n_new, m_new = _mlstm_call(
                pa.reshape(nb, seq, W_A_COLS), pif.reshape(nb, seq, W_IF_COLS), conv0, c0, n0, m0p,
                conv_w[l], r1(conv_b[l]), bif[l], r1(gn_a_g[l]), batch=nb, seq=seq, nbb=mlstm_nbb)
            ha = ha.reshape(n, D_A)
            hb, s_new = _rwkv_call(
                pb.reshape(nb, seq, D_B_IN), shift0[:, None, :], s0, r1(mu_shift[l]), w0a0[l], wlora[l],
                g2b[l], r1(k_k[l]), r1(k_a[l]), r1(r_k[l].reshape(D_B)), r1(lnx_g[l]), r1(lnx_b[l]),
                batch=nb, seq=seq, nbb=rwkv_nbb)
            hb = hb.reshape(n, D_B)
            x1, route, counts = _outproj_call(x, ha, hb, wob[l], r1(ln1_g[l]), r1(ln1_b[l]), wr_t, br, tm)
            x = _moe_call(x1, route, counts, wgb, wub, wdb, r1(ln2_g[l]), r1(ln2_b[l]), layer=l, tm=tm)
            pa3 = pa.reshape(nb, seq, W_A_COLS)
            full = jnp.concatenate([conv0, pa3[:, :, 0:2 * D_A]], axis=1) if seq < CONV_W - 1 else pa3[:, :, 0:2 * D_A]
            conv_new = full[:, -(CONV_W - 1):, :]
            shift_new = pb.reshape(nb, seq, D_B_IN)[:, -1, :]
            outs.append((c_new, n_new, m_new[:, 0, 0:H_A], conv_new, s_new, shift_new))
        return x.reshape(nb, seq, D_MODEL), [jnp.stack(s) for s in zip(*outs)]

    y_prompt, (p_c, p_n, p_m, p_conv, p_s, p_shift) = run(x_prompt, None)
    y_sample, (s_c, s_n, s_m, s_conv, s_s, s_shift) = run(
        x_sample, (state_mlstm_C, state_mlstm_n, state_mlstm_m, state_mlstm_conv, state_rwkv_S,
                   state_rwkv_shift))
    return (y_prompt, y_sample, p_c, p_n, p_m, p_conv, p_s, p_shift, s_c, s_n, s_m, s_conv, s_s, s_shift)
```

```python
import functools
import math

import jax
import jax.numpy as jnp
from jax import lax
from jax.experimental import pallas as pl
from jax.experimental.pallas import tpu as pltpu

F32 = jnp.float32
BF16 = jnp.bfloat16

D_MODEL = 1024
DEPTH = 4
D_A = 512
D_B = 512
DH_A = 128
H_A = 4
DH_B = 64
H_B = 8
N_PAIRS = H_B // 2
CONV_W = 4
CHUNK = 64
R_DECAY = 64
R_AAA = 64
R_GATE = 128
D_B_IN = 3 * D_B + R_DECAY + R_AAA + R_GATE
N_EXPERTS = 16
N_GROUPS = 4
EXPERTS_PER_GROUP = 4
D_EXP = 512
ALPHA = (2 * DEPTH) ** 0.25
LN_EPS = 1e-5
GN_EPS_A = 1e-6
GN_EPS_B = 64e-5

LANES = 128
SOLVE_BLOCK = 16
VMEM_LIMIT = 48 * 1024 * 1024
MOE_VMEM_LIMIT = 56 * 1024 * 1024

NN = (((1,), (0,)), ((), ()))
NT = (((1,), (1,)), ((), ()))
TN = (((0,), (0,)), ((), ()))


def _dot(a, b, dims=NN):
    return lax.dot_general(a.astype(BF16), b.astype(BF16), dims, preferred_element_type=F32)


def _split3(x):
    hi = x.astype(BF16)
    r1 = x - hi.astype(F32)
    mid = r1.astype(BF16)
    lo = (r1 - mid.astype(F32)).astype(BF16)
    return hi, mid, lo


def _dot_exact_lhs(a, x, dims=NN):
    a = a.astype(BF16)
    hi, mid, lo = _split3(x)
    d = lambda p: lax.dot_general(a, p, dims, preferred_element_type=F32)
    return d(hi) + d(mid) + d(lo)


def _dot_exact_rhs(x, b, dims=NN):
    b = b.astype(BF16)
    hi, mid, lo = _split3(x)
    d = lambda p: lax.dot_general(p, b, dims, preferred_element_type=F32)
    return d(hi) + d(mid) + d(lo)


def _layer_norm(x, g, b, eps):
    mu = jnp.mean(x, axis=-1, keepdims=True)
    xc = x - mu
    var = jnp.mean(xc * xc, axis=-1, keepdims=True)
    return xc * lax.rsqrt(var + eps) * g + b


def _sigmoid(x):
    return 1.0 / (1.0 + jnp.exp(-x))


def _log_sigmoid(x):
    return jnp.minimum(x, 0.0) - jnp.log(1.0 + jnp.exp(-jnp.abs(x)))


def _silu(x):
    return x * _sigmoid(x)


def _ln_kernel(x_ref, g_ref, b_ref, o_ref):
    o_ref[...] = _layer_norm(x_ref[...], g_ref[...], b_ref[...], LN_EPS)


def _ln_call(x, g, b, tm):
    n, d = x.shape
    row = pl.BlockSpec((tm, d), lambda i: (i, 0))
    vec = pl.BlockSpec((1, d), lambda i: (0, 0))
    return pl.pallas_call(
        _ln_kernel,
        grid=(n // tm,),
        in_specs=[row, vec, vec],
        out_specs=row,
        out_shape=jax.ShapeDtypeStruct((n, d), F32),
        compiler_params=pltpu.CompilerParams(dimension_semantics=("parallel",)),
        name="ln0",
    )(x, g, b)


W_A_COLS = 4 * D_A
W_IF_COLS = 2 * LANES
W_IN_COLS = W_A_COLS + D_B_IN + W_IF_COLS


def _inproj_kernel(x_ref, w_ref, pa_ref, pb_ref, pif_ref):
    xb = x_ref[...].astype(BF16)
    pa_ref[...] = jnp.dot(xb, w_ref[0, :, 0:W_A_COLS], preferred_element_type=F32)
    pb_ref[...] = jnp.dot(xb, w_ref[0, :, W_A_COLS:W_A_COLS + D_B_IN], preferred_element_type=F32)
    pif_ref[...] = jnp.dot(xb, w_ref[0, :, W_A_COLS + D_B_IN:W_IN_COLS], preferred_element_type=F32)


def _inproj_call(x, w, tm, layer):
    n = x.shape[0]
    return pl.pallas_call(
        _inproj_kernel,
        grid=(n // tm,),
        in_specs=[pl.BlockSpec((tm, D_MODEL), lambda i: (i, 0)),
                  pl.BlockSpec((1, D_MODEL, W_IN_COLS), lambda i: (layer, 0, 0))],
        out_specs=[pl.BlockSpec((tm, W_A_COLS), lambda i: (i, 0)),
                   pl.BlockSpec((tm, D_B_IN), lambda i: (i, 0)),
                   pl.BlockSpec((tm, W_IF_COLS), lambda i: (i, 0))],
        out_shape=[jax.ShapeDtypeStruct((n, W_A_COLS), F32),
                   jax.ShapeDtypeStruct((n, D_B_IN), F32),
                   jax.ShapeDtypeStruct((n, W_IF_COLS), F32)],
        compiler_params=pltpu.CompilerParams(dimension_semantics=("parallel",),
                                             vmem_limit_bytes=VMEM_LIMIT),
        name="inproj",
    )(x, w)


CONV_PAD = 8


def _mlstm_kernel(pa_ref, pif_ref, conv0_ref, c0_ref, n0_ref, m0_ref, convw_ref, convb_ref,
                  bif_ref, gn_ref, *rest, c, nc, nbb, chained):
    h_ref, c_out_ref, n_out_ref, m_out_ref, qk_scr, c_scr, n_scr, m_scr = rest[1:] if chained else rest
    ci = pl.program_id(1)
    rows = nbb * c
    cshift = int(math.log2(c))
    chains = [(nb, h) for nb in range(nbb) for h in range(H_A)]
    prev0 = CONV_PAD - (CONV_W - 1)

    @pl.when(ci == 0)
    def _():
        c_scr[...] = c0_ref[0]
        n_scr[...] = n0_ref[...]
        m_scr[...] = m0_ref[...]
        for nb in range(nbb):
            qk_scr[nb, prev0:CONV_PAD, :] = conv0_ref[nb]

    cat0 = lambda xs: xs[0] if len(xs) == 1 else jnp.concatenate(xs, axis=0)
    acc_l = []
    for nb in range(nbb):
        u = pa_ref[nb, :, 0:2 * D_A]
        qk_scr[nb, CONV_PAD:CONV_PAD + c, :] = u
        acc = convb_ref[...]
        for j in range(CONV_W):
            acc = acc + qk_scr[nb, prev0 + j:prev0 + j + c, :] * convw_ref[j:j + 1, :]
        qk_scr[nb, prev0:CONV_PAD, :] = u[c - (CONV_W - 1):c, :]
        acc_l.append(acc)
    qk = _silu(cat0(acc_l))

    gates = cat0([pif_ref[nb] for nb in range(nbb)]) + bif_ref[...]
    li_all = gates[:, 0:LANES]
    lf_all = _log_sigmoid(gates[:, LANES:2 * LANES])
    row = lax.broadcasted_iota(jnp.int32, (rows, rows), 0)
    col = lax.broadcasted_iota(jnp.int32, (rows, rows), 1)
    tril = jnp.where((col <= row) & ((row >> cshift) == (col >> cshift)), 1.0, 0.0)
    b_all = _dot_exact_lhs(tril, lf_all)
    z_all = li_all - b_all
    crow = lax.broadcasted_iota(jnp.int32, (c, c), 0)
    ccol = lax.broadcasted_iota(jnp.int32, (c, c), 1)
    causal = ccol <= crow
    hrow = lax.broadcasted_iota(jnp.int32, (H_A * c, LANES), 0)
    hlane = lax.broadcasted_iota(jnp.int32, (H_A * c, LANES), 1)
    head_pick = jnp.where((hrow >> cshift) == hlane, 1.0, 0.0)
    z_rows = [_dot_exact_lhs(head_pick, z_all[nb * c:(nb + 1) * c], NT) for nb in range(nbb)]
    lane1 = lax.broadcasted_iota(jnp.int32, (1, LANES), 1)

    rs = lambda nb: slice(nb * c, (nb + 1) * c)
    q_l = [qk[rs(nb), h * DH_A:(h + 1) * DH_A] for nb, h in chains]
    k_l = [qk[rs(nb), D_A + h * DH_A:D_A + (h + 1) * DH_A] * (DH_A ** -0.5) for nb, h in chains]
    v_l = [pa_ref[nb, :, 2 * D_A + h * DH_A:2 * D_A + (h + 1) * DH_A] for nb, h in chains]
    c_l = [c_scr[nb, h] for nb, h in chains]
    n_l = [n_scr[nb, h:h + 1, :] for nb, h in chains]
    b_col = [b_all[rs(nb), h:h + 1] for nb, h in chains]
    li_col = [li_all[rs(nb), h:h + 1] for nb, h in chains]
    m_prev = [m_scr[nb][:, h:h + 1] for nb, h in chains]
    dmat = [jnp.where(causal, bc + z_rows[nb][h * c:(h + 1) * c], -jnp.inf)
            for bc, (nb, h) in zip(b_col, chains)]
    m_inter = [bc + mp for bc, mp in zip(b_col, m_prev)]
    m_t = [jnp.maximum(mi, jnp.max(d, axis=-1, keepdims=True)) for mi, d in zip(m_inter, dmat)]
    qk_dot = [_dot(q, k, NT) for q, k in zip(q_l, k_l)]
    qc = [_dot(q, cm) for q, cm in zip(q_l, c_l)]
    s_l = [x * jnp.exp(d - mt) for x, d, mt in zip(qk_dot, dmat, m_t)]
    sv = [_dot(s, v) for s, v in zip(s_l, v_l)]
    b_last = [bc[c - 1:c, :] for bc in b_col]
    g_s = [bl - bc + li for bl, bc, li in zip(b_last, b_col, li_col)]
    m_new = [jnp.maximum(bl + mp, jnp.max(gs, axis=0, keepdims=True))
             for bl, mp, gs in zip(b_last, m_prev, g_s)]
    wk = [jnp.exp(gs - mn) * k for gs, mn, k in zip(g_s, m_new, k_l)]
    w_old = [jnp.exp(bl + mp - mn) for bl, mp, mn in zip(b_last, m_prev, m_new)]
    kv = [_dot(w, v, TN) for w, v in zip(wk, v_l)]
    for i, (nb, h) in enumerate(chains):
        c_scr[nb, h] = w_old[i] * c_l[i] + kv[i]
        n_scr[nb, h:h + 1, :] = w_old[i] * n_l[i] + jnp.sum(wk[i], axis=0, keepdims=True)
    for nb in range(nbb):
        m_row = m_scr[nb]
        for h in range(H_A):
            m_row = jnp.where(lane1 == h, m_new[nb * H_A + h], m_row)
        m_scr[nb] = m_row
    w_inter = [jnp.exp(mi - mt) for mi, mt in zip(m_inter, m_t)]
    qn = [jnp.sum(q * nv, axis=-1, keepdims=True) for q, nv in zip(q_l, n_l)]
    s_sum = [jnp.sum(s, axis=-1, keepdims=True) for s in s_l]
    den = [w * a + b for w, a, b in zip(w_inter, qn, s_sum)]
    hh = [(w * a + b) / jnp.maximum(jnp.abs(d), jnp.exp(-mt))
          for w, a, b, d, mt in zip(w_inter, qc, sv, den, m_t)]
    mu = [jnp.mean(x, axis=-1, keepdims=True) for x in hh]
    hc = [x - m for x, m in zip(hh, mu)]
    var = [jnp.mean(x * x, axis=-1, keepdims=True) for x in hc]
    for i, (nb, h) in enumerate(chains):
        sl = slice(h * DH_A, (h + 1) * DH_A)
        hn = hc[i] * lax.rsqrt(var[i] + GN_EPS_A)
        o_pre = pa_ref[nb, :, 3 * D_A + h * DH_A:3 * D_A + (h + 1) * DH_A]
        h_ref[nb, :, sl] = hn * gn_ref[:, sl] * _sigmoid(o_pre)

    @pl.when(ci == nc - 1)
    def _():
        c_out_ref[0] = c_scr[...]
        n_out_ref[...] = n_scr[...]
        m_out_ref[...] = m_scr[...]


def _mlstm_call(pa, pif, conv0, c_in, n0, m0, convw, convb, bif, gn, c_stack, *, batch, seq, nbb,
                layer_in, layer):
    c = math.gcd(seq, CHUNK)
    nc = seq // c
    tok = lambda w: pl.BlockSpec((nbb, c, w), lambda b, i: (b, i, 0))
    vec = lambda r, w: pl.BlockSpec((r, w), lambda b, i: (0, 0))
    st5 = lambda l: pl.BlockSpec((1, nbb, H_A, DH_A, DH_A), lambda b, i: (l, b, 0, 0, 0))
    st3 = lambda r, w: pl.BlockSpec((nbb, r, w), lambda b, i: (b, 0, 0))
    chained = c_stack is not None
    in_specs = [tok(W_A_COLS), tok(W_IF_COLS), st3(CONV_W - 1, 2 * D_A), st5(layer_in), st3(H_A, DH_A),
                st3(1, LANES), vec(CONV_W, 2 * D_A), vec(1, 2 * D_A), vec(1, W_IF_COLS), vec(1, D_A)]
    args = [pa, pif, conv0, c_in, n0, m0, convw, convb, bif, gn]
    if chained:
        in_specs.append(pl.BlockSpec(memory_space=pl.ANY))
        args.append(c_stack)
    return pl.pallas_call(
        functools.partial(_mlstm_kernel, c=c, nc=nc, nbb=nbb, chained=chained),
        grid=(batch // nbb, nc),
        in_specs=in_specs,
        out_specs=[tok(D_A), st5(layer), st3(H_A, DH_A), st3(1, LANES)],
        out_shape=[jax.ShapeDtypeStruct((batch, seq, D_A), F32),
                   jax.ShapeDtypeStruct((DEPTH, batch, H_A, DH_A, DH_A), F32),
                   jax.ShapeDtypeStruct((batch, H_A, DH_A), F32),
                   jax.ShapeDtypeStruct((batch, 1, LANES), F32)],
        scratch_shapes=[pltpu.VMEM((nbb, CONV_PAD + c, 2 * D_A), F32),
                        pltpu.VMEM((nbb, H_A, DH_A, DH_A), F32),
                        pltpu.VMEM((nbb, H_A, DH_A), F32),
                        pltpu.VMEM((nbb, 1, LANES), F32)],
        input_output_aliases={len(args) - 1: 1} if chained else {},
        compiler_params=pltpu.CompilerParams(dimension_semantics=("parallel", "arbitrary"),
                                             vmem_limit_bytes=VMEM_LIMIT),
        name="mlstm",
    )(*args)


SHIFT_PAD = 8


def _neumann_inverses(ms, size, t):
    row = lax.broadcasted_iota(jnp.int32, (size, size), 0)
    col = lax.broadcasted_iota(jnp.int32, (size, size), 1)
    eye = jnp.where(row == col, 1.0, 0.0)
    blk = min(SOLVE_BLOCK, t)
    shift = int(math.log2(blk))
    same = (row >> shift) == (col >> shift)
    ds = [jnp.where(same, m, 0.0) for m in ms]
    ps = [eye + d for d in ds]
    xs = ds
    for _ in range(shift - 1):
        xs = [_dot(x, x) for x in xs]
        ps = [p + _dot(p, x) for p, x in zip(ps, xs)]
    nblk = t // blk
    if nblk == 1:
        return ps, None
    ns = [_dot(p, m - d) for p, m, d in zip(ps, ms, ds)]
    qs = [eye + n for n in ns]
    ys = ns
    for _ in range(int(math.log2(nblk)) - 1):
        ys = [_dot(y, y) for y in ys]
        qs = [q + _dot(q, y) for q, y in zip(qs, ys)]
    return ps, qs


def _rwkv_kernel(pb_ref, shift0_ref, s0_ref, mu_ref, w0a0_ref, wlora_ref, g2_ref, kk_ref, ka_ref,
                 rk_ref, lng_ref, lnb_ref, *rest, t, nc, nbb, chained):
    h_ref, s_out_ref, sh_scr, s_scr = rest[1:] if chained else rest
    ci = pl.program_id(1)
    rows = nbb * t
    t2 = 2 * t
    groups = [(nb, pr) for nb in range(nbb) for pr in range(N_PAIRS)]

    brow = lax.broadcasted_iota(jnp.int32, (LANES, LANES), 0)
    bcol = lax.broadcasted_iota(jnp.int32, (LANES, LANES), 1)
    blockdiag = (brow >= DH_B) == (bcol >= DH_B)
    bd2 = jnp.where(blockdiag, 1.0, 0.0).astype(BF16)

    @pl.when(ci == 0)
    def _():
        er = lax.broadcasted_iota(jnp.int32, (DH_B, LANES), 0)
        ec = lax.broadcasted_iota(jnp.int32, (DH_B, LANES), 1)
        dup_cols = jnp.where((ec & (DH_B - 1)) == er, 1.0, 0.0)
        for nb in range(nbb):
            sh_scr[nb, SHIFT_PAD - 1:SHIFT_PAD, :] = shift0_ref[nb]
            for pr in range(N_PAIRS):
                x = s0_ref[0, nb, 2 * pr:2 * pr + 2].reshape(LANES, DH_B)
                s_scr[nb * N_PAIRS + pr] = jnp.where(blockdiag, _dot_exact_rhs(x, dup_cols), 0.0)

    p_l, prev_l = [], []
    for nb in range(nbb):
        p_nb = pb_ref[nb]
        sh_scr[nb, SHIFT_PAD:SHIFT_PAD + t, :] = p_nb
        prev_l.append(sh_scr[nb, SHIFT_PAD - 1:SHIFT_PAD - 1 + t, :])
        sh_scr[nb, SHIFT_PAD - 1:SHIFT_PAD, :] = p_nb[t - 1:t, :]
        p_l.append(p_nb)
    cat0 = lambda xs: xs[0] if len(xs) == 1 else jnp.concatenate(xs, axis=0)
    p = cat0(p_l)
    prev = cat0(prev_l)
    pb = p + (prev - p) * mu_ref[...]

    r = pb[:, 0:D_B]
    kr = pb[:, D_B:2 * D_B]
    vr = pb[:, 2 * D_B:3 * D_B]
    lora_in = pb[:, 3 * D_B:3 * D_B + LANES]
    gl = pb[:, 3 * D_B + LANES:3 * D_B + 2 * LANES]
    lane = lax.broadcasted_iota(jnp.int32, (rows, LANES), 1)
    lora_act = jnp.where(lane < R_DECAY, jnp.tanh(lora_in), lora_in)
    z = _dot(lora_act, wlora_ref[...]) + w0a0_ref[...]
    ld = -jnp.exp(_log_sigmoid(z[:, 0:D_B]) - 0.5)
    a = _sigmoid(z[:, D_B:2 * D_B])
    g = _dot(_sigmoid(gl), g2_ref[...])

    def seg_sum(x):
        xr = jnp.concatenate([x[:, q * LANES:(q + 1) * LANES] for q in range(N_PAIRS)], axis=0)
        hi = xr.astype(BF16)
        lo = (xr - hi.astype(F32)).astype(BF16)
        s = (jnp.dot(hi, bd2, preferred_element_type=F32) + jnp.dot(lo, bd2, preferred_element_type=F32))
        return jnp.concatenate([s[q * rows:(q + 1) * rows] for q in range(N_PAIRS)], axis=1)

    kk_raw = kr * kk_ref[...]
    kk = kk_raw / jnp.maximum(jnp.sqrt(seg_sum(kk_raw * kk_raw)), 1e-12)
    k2 = kr * (1.0 + (a - 1.0) * ka_ref[...])
    bonus = seg_sum(r * k2 * rk_ref[...]) * vr

    row = lax.broadcasted_iota(jnp.int32, (rows, rows), 0)
    col = lax.broadcasted_iota(jnp.int32, (rows, rows), 1)
    tshift = int(math.log2(t))
    tril = jnp.where((col <= row) & ((row >> tshift) == (col >> tshift)), 1.0, 0.0)
    lc = _dot_exact_lhs(tril, ld)
    lends = [lc[(nb + 1) * t - 1:(nb + 1) * t, :] for nb in range(nbb)]
    lend_rows = cat0([jnp.broadcast_to(le, (t, D_B)) for le in lends])
    e_nc = jnp.exp(-lc)
    e_end = jnp.exp(lend_rows - lc)
    b_raw = kk * a
    at = -kk * jnp.exp(lc - ld)
    rt = r * jnp.exp(lc)
    kt = k2 * e_nc
    bt = b_raw * e_nc
    kend = k2 * e_end
    bend = b_raw * e_end

    srow = lax.broadcasted_iota(jnp.int32, (t2, t2), 0)
    scol = lax.broadcasted_iota(jnp.int32, (t2, t2), 1)
    strict = scol < srow
    incl = scol <= srow
    lane2 = lax.broadcasted_iota(jnp.int32, (t2, LANES), 1)
    row2 = lax.broadcasted_iota(jnp.int32, (t2, LANES), 0)
    head_sel = (lane2 >= DH_B) == (row2 >= t)
    first_head = lax.broadcasted_iota(jnp.int32, (t, LANES), 1) < DH_B

    def blk(x, nb, pr):
        return x[nb * t:(nb + 1) * t, pr * LANES:(pr + 1) * LANES]

    def stack(x):
        return jnp.where(head_sel, jnp.concatenate([x, x], axis=0), 0.0)

    lhs = [jnp.concatenate([stack(blk(at, *gp)), stack(blk(rt, *gp))], axis=0) for gp in groups]
    rk_s = [stack(blk(kt, *gp)) for gp in groups]
    rb_s = [stack(blk(bt, *gp)) for gp in groups]
    v_p = [blk(vr, *gp) for gp in groups]
    v_s = [jnp.concatenate([v, v], axis=0) for v in v_p]
    s_p = [s_scr[i] for i in range(len(groups))]
    gk = [_dot(l, x, NT) for l, x in zip(lhs, rk_s)]
    gb = [_dot(l, x, NT) for l, x in zip(lhs, rb_s)]
    xs = [_dot(l, s, NT) for l, s in zip(lhs, s_p)]
    m_ab = [jnp.where(strict, x[0:t2], 0.0) for x in gb]
    ps, qs = _neumann_inverses(m_ab, t2, t)
    rhs = [x[0:t2] + _dot(jnp.where(strict, y[0:t2], 0.0), v) for x, y, v in zip(xs, gk, v_s)]
    sa_s = [_dot(pm, x) for pm, x in zip(ps, rhs)]
    if qs is not None:
        sa_s = [_dot(qm, x) for qm, x in zip(qs, sa_s)]
    o_s = [x[t2:] + _dot(jnp.where(incl, y[t2:], 0.0), v) + _dot(jnp.where(incl, w[t2:], 0.0), sa)
           for x, y, w, v, sa in zip(xs, gk, gb, v_s, sa_s)]
    o_p = [jnp.where(first_head, o[0:t], o[t:t2]) for o in o_s]
    sa_p = [jnp.where(first_head, sa[0:t], sa[t:t2]) for sa in sa_s]
    upd = [_dot(jnp.concatenate([v, sa], axis=0),
                jnp.concatenate([blk(kend, *gp), blk(bend, *gp)], axis=0), TN)
           for v, sa, gp in zip(v_p, sa_p, groups)]
    for i, (nb, pr) in enumerate(groups):
        dec = jnp.exp(lends[nb][:, pr * LANES:(pr + 1) * LANES])
        s_scr[i] = s_p[i] * dec + jnp.where(blockdiag, upd[i], 0.0)

    o = cat0([jnp.concatenate(o_p[nb * N_PAIRS:(nb + 1) * N_PAIRS], axis=1) for nb in range(nbb)])
    mean = seg_sum(o) * (1.0 / DH_B)
    oc = o - mean
    var = seg_sum(oc * oc) * (1.0 / DH_B)
    out = oc * lax.rsqrt(var + GN_EPS_B) * lng_ref[...] + lnb_ref[...]
    res = (out + bonus) * g
    for nb in range(nbb):
        h_ref[nb] = res[nb * t:(nb + 1) * t, :]

    @pl.when(ci == nc - 1)
    def _():
        fr = lax.broadcasted_iota(jnp.int32, (LANES, DH_B), 0)
        fc = lax.broadcasted_iota(jnp.int32, (LANES, DH_B), 1)
        dup_rows = jnp.where((fr & (DH_B - 1)) == fc, 1.0, 0.0)
        for i, (nb, pr) in enumerate(groups):
            packed = _dot_exact_rhs(s_scr[i], dup_rows)
            s_out_ref[0, nb, 2 * pr:2 * pr + 2] = packed.reshape(2, DH_B, DH_B)


def _rwkv_call(pb, shift0, s_in, mu, w0a0, wlora, g2, kk, ka, rk, lng, lnb, s_stack, *, batch, seq, nbb,
               layer_in, layer):
    t = math.gcd(seq, CHUNK)
    nc = seq // t
    tok = lambda w: pl.BlockSpec((nbb, t, w), lambda b, i: (b, i, 0))
    vec = lambda r, w: pl.BlockSpec((r, w), lambda b, i: (0, 0))
    st5 = lambda l: pl.BlockSpec((1, nbb, H_B, DH_B, DH_B), lambda b, i: (l, b, 0, 0, 0))
    chained = s_stack is not None
    in_specs = [tok(D_B_IN), pl.BlockSpec((nbb, 1, D_B_IN), lambda b, i: (b, 0, 0)), st5(layer_in),
                vec(1, D_B_IN), vec(1, 2 * D_B), vec(LANES, 2 * D_B), vec(R_GATE, D_B),
                vec(1, D_B), vec(1, D_B), vec(1, D_B), vec(1, D_B), vec(1, D_B)]
    args = [pb, shift0, s_in, mu, w0a0, wlora, g2, kk, ka, rk, lng, lnb]
    if chained:
        in_specs.append(pl.BlockSpec(memory_space=pl.ANY))
        args.append(s_stack)
    return pl.pallas_call(
        functools.partial(_rwkv_kernel, t=t, nc=nc, nbb=nbb, chained=chained),
        grid=(batch // nbb, nc),
        in_specs=in_specs,
        out_specs=[tok(D_B), st5(layer)],
        out_shape=[jax.ShapeDtypeStruct((batch, seq, D_B), F32),
                   jax.ShapeDtypeStruct((DEPTH, batch, H_B, DH_B, DH_B), F32)],
        scratch_shapes=[pltpu.VMEM((nbb, SHIFT_PAD + t, D_B_IN), F32),
                        pltpu.VMEM((nbb * N_PAIRS, LANES, LANES), F32)],
        input_output_aliases={len(args) - 1: 1} if chained else {},
        compiler_params=pltpu.CompilerParams(dimension_semantics=("parallel", "arbitrary"),
                                             vmem_limit_bytes=VMEM_LIMIT),
        name="rwkv",
    )(*args)


ROUTE_ROWS = 8


def _outproj_kernel(x_ref, ha_ref, hb_ref, wo_ref, g_ref, b_ref, wr_ref, br_ref, x1_ref, route_ref,
                    cnt_ref, *, tm):
    y = _dot(ha_ref[...], wo_ref[0, 0:D_A, :]) + _dot(hb_ref[...], wo_ref[0, D_A:D_A + D_B, :])
    x1 = _layer_norm(ALPHA * x_ref[...] + y, g_ref[...], b_ref[...], LN_EPS)
    x1_ref[...] = x1
    logits = lax.dot_general(wr_ref[...], x1, NT, precision=lax.Precision.HIGHEST,
                             preferred_element_type=F32)
    mx = jnp.max(logits, axis=0, keepdims=True)
    ex = jnp.exp(logits - mx)
    probs = ex / jnp.sum(ex, axis=0, keepdims=True)
    sel = probs + br_ref[...]
    neg = -jnp.inf

    def top2(rows):
        m1 = functools.reduce(jnp.maximum, rows)
        i1 = jnp.full(m1.shape, len(rows) - 1, jnp.int32)
        for j in range(len(rows) - 2, -1, -1):
            i1 = jnp.where(rows[j] == m1, j, i1)
        rest = [jnp.where(i1 == j, neg, rows[j]) for j in range(len(rows))]
        m2 = functools.reduce(jnp.maximum, rest)
        i2 = jnp.full(m2.shape, len(rows) - 1, jnp.int32)
        for j in range(len(rows) - 2, -1, -1):
            i2 = jnp.where(rest[j] == m2, j, i2)
        return m1, i1, m2, i2

    g_score, g_i1, g_i2 = [], [], []
    for gidx in range(N_GROUPS):
        rows = [sel[gidx * EXPERTS_PER_GROUP + j:gidx * EXPERTS_PER_GROUP + j + 1, :]
                for j in range(EXPERTS_PER_GROUP)]
        m1, i1, m2, i2 = top2(rows)
        g_score.append(m1 + m2)
        g_i1.append(i1)
        g_i2.append(i2)
    best = functools.reduce(jnp.maximum, g_score)
    grp = jnp.full(best.shape, N_GROUPS - 1, jnp.int32)
    for gidx in range(N_GROUPS - 2, -1, -1):
        grp = jnp.where(g_score[gidx] == best, gidx, grp)
    l1 = g_i1[N_GROUPS - 1]
    l2 = g_i2[N_GROUPS - 1]
    for gidx in range(N_GROUPS - 2, -1, -1):
        l1 = jnp.where(grp == gidx, g_i1[gidx], l1)
        l2 = jnp.where(grp == gidx, g_i2[gidx], l2)
    e1 = grp * EXPERTS_PER_GROUP + l1
    e2 = grp * EXPERTS_PER_GROUP + l2
    e_iota = lax.broadcasted_iota(jnp.int32, (N_EXPERTS, tm), 0)
    p1 = jnp.sum(jnp.where(e_iota == e1, probs, 0.0), axis=0, keepdims=True)
    p2 = jnp.sum(jnp.where(e_iota == e2, probs, 0.0), axis=0, keepdims=True)
    tot = p1 + p2
    r8 = lax.broadcasted_iota(jnp.int32, (ROUTE_ROWS, tm), 0)
    route_ref[...] = jnp.where(r8 == 0, e1.astype(F32),
                               jnp.where(r8 == 1, e2.astype(F32),
                                         jnp.where(r8 == 2, p1 / tot, jnp.where(r8 == 3, p2 / tot, 0.0))))
    picked = jnp.where((e_iota == e1) | (e_iota == e2), 1.0, 0.0)
    cnt_ref[0] = jnp.broadcast_to(jnp.sum(picked, axis=1, keepdims=True), (N_EXPERTS, LANES))


def _outproj_call(x, ha, hb, wo, g, b, wr_t, br, tm, layer):
    n = x.shape[0]
    row = lambda w: pl.BlockSpec((tm, w), lambda i: (i, 0))
    full = lambda r, w: pl.BlockSpec((r, w), lambda i: (0, 0))
    return pl.pallas_call(
        functools.partial(_outproj_kernel, tm=tm),
        grid=(n // tm,),
        in_specs=[row(D_MODEL), row(D_A), row(D_B),
                  pl.BlockSpec((1, D_MODEL, D_MODEL), lambda i: (layer, 0, 0)), full(1, D_MODEL),
                  full(1, D_MODEL), full(N_EXPERTS, D_MODEL), full(N_EXPERTS, 1)],
        out_specs=[row(D_MODEL), pl.BlockSpec((ROUTE_ROWS, tm), lambda i: (0, i)),
                   pl.BlockSpec((1, N_EXPERTS, LANES), lambda i: (i, 0, 0))],
        out_shape=[jax.ShapeDtypeStruct((n, D_MODEL), F32), jax.ShapeDtypeStruct((ROUTE_ROWS, n), F32),
                   jax.ShapeDtypeStruct((n // tm, N_EXPERTS, LANES), F32)],
        compiler_params=pltpu.CompilerParams(dimension_semantics=("parallel",),
                                             vmem_limit_bytes=VMEM_LIMIT),
        name="outproj",
    )(x, ha, hb, wo, g, b, wr_t, br)


MOE_BM = 128
MOE_CH = 512


def _moe_max_blocks(t):
    return -(-2 * t // MOE_BM) + N_EXPERTS - 1


def _moe_kernel(blk_e_ref, nblk_ref, x_ref, route_ref, padoff_ref, tri_ref, wg_ref, wu_ref, wd_ref,
                g_ref, b_ref, o_ref, xb_scr, xs_scr, cols_scr, acc_scr, *, t, maxb, n_chunks):
    i = pl.program_id(0)
    j = pl.program_id(1)
    nblk = nblk_ref[i]
    s_used = nblk * MOE_BM

    @pl.when(j == 0)
    def _():
        xb = x_ref[...].astype(BF16)
        xb_scr[...] = xb
        route = route_ref[...]
        e_iota = lax.broadcasted_iota(jnp.int32, (N_EXPERTS, t), 0)
        a1 = e_iota == route[0:1].astype(jnp.int32)
        a2 = e_iota == route[1:2].astype(jnp.int32)
        picked = jnp.where(a1 | a2, 1.0, 0.0).astype(BF16)
        rank = jnp.dot(picked, tri_ref[...], preferred_element_type=F32)
        base = padoff_ref[0][:, 0:1] + rank
        slot1 = jnp.sum(jnp.where(a1, base, 0.0), axis=0, keepdims=True)
        slot2 = jnp.sum(jnp.where(a2, base, 0.0), axis=0, keepdims=True)
        r8 = lax.broadcasted_iota(jnp.int32, (ROUTE_ROWS, t), 0)
        rows = jnp.where(r8 == 0, slot1, jnp.where(r8 == 1, slot2,
                                                   jnp.where(r8 == 2, route[2:3],
                                                             jnp.where(r8 == 3, route[3:4], 0.0))))
        padded = jnp.concatenate([rows, jnp.zeros((LANES - ROUTE_ROWS, t), F32)], axis=0)
        cols_scr[...] = padded.T
        s1 = slot1.astype(jnp.int32)
        s2 = slot2.astype(jnp.int32)
        for c in range(n_chunks):
            @pl.when(c * MOE_CH < s_used)
            def _():
                s_iota = lax.broadcasted_iota(jnp.int32, (MOE_CH, t), 0) + c * MOE_CH
                onehot = jnp.where((s_iota == s1) | (s_iota == s2), 1.0, 0.0).astype(BF16)
                xs_scr[c * MOE_CH:(c + 1) * MOE_CH, :] = jnp.dot(
                    onehot, xb, preferred_element_type=F32).astype(BF16)

    @pl.when(j < nblk)
    def _():
        r0 = pl.multiple_of(j * MOE_BM, MOE_BM)
        xblk = xs_scr[pl.ds(r0, MOE_BM), :]
        hid = _silu(jnp.dot(xblk, wg_ref[0, 0], preferred_element_type=F32)) * jnp.dot(
            xblk, wu_ref[0, 0], preferred_element_type=F32)
        y = jnp.dot(hid.astype(BF16), wd_ref[0, 0], preferred_element_type=F32)
        xs_scr[pl.ds(r0, MOE_BM), :] = y.astype(BF16)

    @pl.when(j == maxb - 1)
    def _():
        cols = cols_scr[...]
        s1c = cols[:, 0:1].astype(jnp.int32)
        s2c = cols[:, 1:2].astype(jnp.int32)
        w1c = cols[:, 2:3]
        w2c = cols[:, 3:4]
        for c in range(n_chunks):
            @pl.when(c * MOE_CH < s_used)
            def _():
                l_iota = lax.broadcasted_iota(jnp.int32, (t, MOE_CH), 1) + c * MOE_CH
                weighted = (jnp.where(l_iota == s1c, w1c, 0.0) + jnp.where(l_iota == s2c, w2c, 0.0)).astype(BF16)
                part = jnp.dot(weighted, xs_scr[c * MOE_CH:(c + 1) * MOE_CH, :], preferred_element_type=F32)
                if c == 0:
                    acc_scr[...] = part
                else:
                    acc_scr[...] += part
        o_ref[...] = _layer_norm(ALPHA * x_ref[...] + acc_scr[...], g_ref[...], b_ref[...], LN_EPS)


def _moe_schedule(counts, tiles_per, maxb):
    cnt = counts[:, :, 0].astype(jnp.int32)
    cnt = cnt.reshape(-1, tiles_per, N_EXPERTS).sum(axis=1)
    nblk = (cnt + MOE_BM - 1) // MOE_BM
    end = jnp.cumsum(nblk, axis=-1)
    start = end - nblk
    steps = jnp.arange(maxb, dtype=jnp.int32)
    blk_e = jnp.sum((steps[None, :, None] >= end[:, None, :]).astype(jnp.int32), axis=-1)
    last_e = jnp.max(jnp.where(nblk > 0, jnp.arange(N_EXPERTS, dtype=jnp.int32)[None, :], 0), axis=-1)
    blk_e = jnp.minimum(blk_e, last_e[:, None])
    padoff = jnp.broadcast_to((start * MOE_BM).astype(F32)[:, :, None], (cnt.shape[0], N_EXPERTS, LANES))
    return blk_e, end[:, -1], padoff


def _moe_call(x, route, counts, wg, wu, wd, g, b, *, layer, tm):
    n = x.shape[0]
    t = 1024 if n % 1024 == 0 else n
    maxb = _moe_max_blocks(t)
    n_chunks = -(-maxb * MOE_BM // MOE_CH)
    blk_e, nblk, padoff = _moe_schedule(counts, t // tm, maxb)
    tri = jnp.triu(jnp.ones((t, t), BF16), k=1)
    wspec = lambda r, c: pl.BlockSpec((1, 1, r, c), lambda i, j, be, nb: (layer, be[i, j], 0, 0))
    vec = pl.BlockSpec((1, D_MODEL), lambda i, j, be, nb: (0, 0))
    return pl.pallas_call(
        functools.partial(_moe_kernel, t=t, maxb=maxb, n_chunks=n_chunks),
        grid_spec=pltpu.PrefetchScalarGridSpec(
            num_scalar_prefetch=2,
            grid=(n // t, maxb),
            in_specs=[pl.BlockSpec((t, D_MODEL), lambda i, j, be, nb: (i, 0)),
                      pl.BlockSpec((ROUTE_ROWS, t), lambda i, j, be, nb: (0, i)),
                      pl.BlockSpec((1, N_EXPERTS, LANES), lambda i, j, be, nb: (i, 0, 0)),
                      pl.BlockSpec((t, t), lambda i, j, be, nb: (0, 0)),
                      wspec(D_MODEL, D_EXP), wspec(D_MODEL, D_EXP), wspec(D_EXP, D_MODEL), vec, vec],
            out_specs=pl.BlockSpec((t, D_MODEL), lambda i, j, be, nb: (i, 0)),
            scratch_shapes=[pltpu.VMEM((t, D_MODEL), BF16),
                            pltpu.VMEM((n_chunks * MOE_CH, D_MODEL), BF16),
                            pltpu.VMEM((t, LANES), F32),
                            pltpu.VMEM((t, D_MODEL), F32)]),
        out_shape=jax.ShapeDtypeStruct((n, D_MODEL), F32),
        compiler_params=pltpu.CompilerParams(dimension_semantics=("parallel", "arbitrary"),
                                             vmem_limit_bytes=MOE_VMEM_LIMIT),
        name="moe",
    )(blk_e, nblk, x, route, padoff, tri, wg, wu, wd, g, b)


def _row_tile(n):
    return 512 if n % 512 == 0 else n


def _seqs_per_step(batch, want):
    return want if batch % want == 0 else 1


def kernel(x_prompt, x_sample, state_mlstm_C, state_mlstm_n, state_mlstm_m, state_mlstm_conv, state_rwkv_S, state_rwkv_shift, ln0_g, ln0_b, w_in, conv_w, conv_b, b_i, b_f, gn_a_g, mu_shift, w0, w2, a0, a2, g2, k_k, k_a, r_k, lnx_g, lnx_b, w_out, ln1_g, ln1_b, w_router, b_router, we_gate, we_up, we_down, ln2_g, ln2_b):
    d_a_in = 4 * D_A + 2 * H_A
    zpad = jnp.zeros((DEPTH, D_MODEL, LANES - H_A), F32)
    w_cat = jnp.concatenate(
        [w_in[:, :, 0:4 * D_A], w_in[:, :, d_a_in:], w_in[:, :, 4 * D_A:4 * D_A + H_A], zpad,
         w_in[:, :, 4 * D_A + H_A:d_a_in], zpad], axis=-1).astype(BF16)
    bif = jnp.zeros((DEPTH, 1, W_IF_COLS), F32)
    bif = bif.at[:, 0, 0:H_A].set(b_i).at[:, 0, LANES:LANES + H_A].set(b_f)
    wlora = jnp.zeros((DEPTH, LANES, 2 * D_B), F32)
    wlora = wlora.at[:, 0:R_DECAY, 0:D_B].set(w2).at[:, R_DECAY:, D_B:].set(a2).astype(BF16)
    w0a0 = jnp.concatenate([w0, a0], axis=-1)[:, None, :]
    g2b = g2.astype(BF16)
    wob = w_out.astype(BF16)
    wgb = we_gate.astype(BF16)
    wub = we_up.astype(BF16)
    wdb = we_down.astype(BF16)
    wr_t = w_router.T
    br = b_router[:, None]
    r1 = lambda v: v[None, :]

    def run(x3, states):
        nb, seq, _ = x3.shape
        n = nb * seq
        tm = _row_tile(n)
        rwkv_nbb = _seqs_per_step(nb, 4)
        mlstm_nbb = _seqs_per_step(nb, 2 if seq >= CHUNK else 4)
        x = _ln_call(x3.reshape(n, D_MODEL), r1(ln0_g), r1(ln0_b), tm)
        outs = []
        c_stack = s_stack = None
        for l in range(DEPTH):
            if states is None:
                c_in = jnp.zeros((1, nb, H_A, DH_A, DH_A), F32)
                s_in = jnp.zeros((1, nb, H_B, DH_B, DH_B), F32)
                layer_in = 0
                n0 = jnp.zeros((nb, H_A, DH_A), F32)
                m0 = jnp.zeros((nb, H_A), F32)
                conv0 = jnp.zeros((nb, CONV_W - 1, 2 * D_A), F32)
                shift0 = jnp.zeros((nb, D_B_IN), F32)
            else:
                c_in, s_in, layer_in = states[0], states[4], l
                n0, m0, conv0, shift0 = (states[k][l] for k in (1, 2, 3, 5))
            m0p = jnp.zeros((nb, 1, LANES), F32).at[:, 0, 0:H_A].set(m0)
            pa, pb, pif = _inproj_call(x, w_cat, tm, l)
            ha, c_stack, n_new, m_new = _mlstm_call(
                pa.reshape(nb, seq, W_A_COLS), pif.reshape(nb, seq, W_IF_COLS), conv0, c_in, n0, m0p,
                conv_w[l], r1(conv_b[l]), bif[l], r1(gn_a_g[l]), c_stack, batch=nb, seq=seq,
                nbb=mlstm_nbb, layer_in=layer_in, layer=l)
            ha = ha.reshape(n, D_A)
            hb, s_stack = _rwkv_call(
                pb.reshape(nb, seq, D_B_IN), shift0[:, None, :], s_in, r1(mu_shift[l]), w0a0[l], wlora[l],
                g2b[l], r1(k_k[l]), r1(k_a[l]), r1(r_k[l].reshape(D_B)), r1(lnx_g[l]), r1(lnx_b[l]),
                s_stack, batch=nb, seq=seq, nbb=rwkv_nbb, layer_in=layer_in, layer=l)
            hb = hb.reshape(n, D_B)
            x1, route, counts = _outproj_call(x, ha, hb, wob, r1(ln1_g[l]), r1(ln1_b[l]), wr_t, br, tm, l)
            x = _moe_call(x1, route, counts, wgb, wub, wdb, r1(ln2_g[l]), r1(ln2_b[l]), layer=l, tm=tm)
            pa3 = pa.reshape(nb, seq, W_A_COLS)
            full = jnp.concatenate([conv0, pa3[:, :, 0:2 * D_A]], axis=1) if seq < CONV_W - 1 else pa3[:, :, 0:2 * D_A]
            conv_new = full[:, -(CONV_W - 1):, :]
            shift_new = pb.reshape(nb, seq, D_B_IN)[:, -1, :]
            outs.append((n_new, m_new[:, 0, 0:H_A], conv_new, shift_new))
        n_all, m_all, conv_all, shift_all = (jnp.stack(s) for s in zip(*outs))
        return x.reshape(nb, seq, D_MODEL), (c_stack, n_all, m_all, conv_all, s_stack, shift_all)

    y_prompt, (p_c, p_n, p_m, p_conv, p_s, p_shift) = run(x_prompt, None)
    y_sample, (s_c, s_n, s_m, s_conv, s_s, s_shift) = run(
        x_sample, (state_mlstm_C, state_mlstm_n, state_mlstm_m, state_mlstm_conv, state_rwkv_S,
                   state_rwkv_shift))
    return (y_prompt, y_sample, p_c, p_n, p_m, p_conv, p_s, p_shift, s_c, s_n, s_m, s_conv, s_s, s_shift)
```

```python
import functools
import math

import jax
import jax.numpy as jnp
from jax import lax
from jax.experimental import pallas as pl
from jax.experimental.pallas import tpu as pltpu

F32 = jnp.float32
BF16 = jnp.bfloat16

D_MODEL = 1024
DEPTH = 4
D_A = 512
D_B = 512
DH_A = 128
H_A = 4
DH_B = 64
H_B = 8
N_PAIRS = H_B // 2
CONV_W = 4
CHUNK = 64
R_DECAY = 64
R_AAA = 64
R_GATE = 128
D_B_IN = 3 * D_B + R_DECAY + R_AAA + R_GATE
N_EXPERTS = 16
N_GROUPS = 4
EXPERTS_PER_GROUP = 4
D_EXP = 512
ALPHA = (2 * DEPTH) ** 0.25
LN_EPS = 1e-5
GN_EPS_A = 1e-6
GN_EPS_B = 64e-5

LANES = 128
SOLVE_BLOCK = 16
VMEM_LIMIT = 48 * 1024 * 1024

NN = (((1,), (0,)), ((), ()))
NT = (((1,), (1,)), ((), ()))
TN = (((0,), (0,)), ((), ()))


def _dot(a, b, dims=NN):
    return lax.dot_general(a.astype(BF16), b.astype(BF16), dims, preferred_element_type=F32)


def _split3(x):
    hi = x.astype(BF16)
    r1 = x - hi.astype(F32)
    mid = r1.astype(BF16)
    lo = (r1 - mid.astype(F32)).astype(BF16)
    return hi, mid, lo


def _dot_exact_lhs(a, x, dims=NN):
    a = a.astype(BF16)
    hi, mid, lo = _split3(x)
    d = lambda p: lax.dot_general(a, p, dims, preferred_element_type=F32)
    return d(hi) + d(mid) + d(lo)


def _dot_exact_rhs(x, b, dims=NN):
    b = b.astype(BF16)
    hi, mid, lo = _split3(x)
    d = lambda p: lax.dot_general(p, b, dims, preferred_element_type=F32)
    return d(hi) + d(mid) + d(lo)


def _layer_norm(x, g, b, eps):
    mu = jnp.mean(x, axis=-1, keepdims=True)
    xc = x - mu
    var = jnp.mean(xc * xc, axis=-1, keepdims=True)
    return xc * lax.rsqrt(var + eps) * g + b


def _sigmoid(x):
    return 1.0 / (1.0 + jnp.exp(-x))


def _log_sigmoid(x):
    return jnp.minimum(x, 0.0) - jnp.log(1.0 + jnp.exp(-jnp.abs(x)))


def _silu(x):
    return x * _sigmoid(x)


def _ln_kernel(x_ref, g_ref, b_ref, o_ref):
    o_ref[...] = _layer_norm(x_ref[...], g_ref[...], b_ref[...], LN_EPS)


def _ln_call(x, g, b, tm):
    n, d = x.shape
    row = pl.BlockSpec((tm, d), lambda i: (i, 0))
    vec = pl.BlockSpec((1, d), lambda i: (0, 0))
    return pl.pallas_call(
        _ln_kernel,
        grid=(n // tm,),
        in_specs=[row, vec, vec],
        out_specs=row,
        out_shape=jax.ShapeDtypeStruct((n, d), F32),
        compiler_params=pltpu.CompilerParams(dimension_semantics=("parallel",)),
        name="ln0",
    )(x, g, b)


W_A_COLS = 4 * D_A
W_IF_COLS = 2 * LANES
W_IN_COLS = W_A_COLS + D_B_IN + W_IF_COLS


def _inproj_kernel(x_ref, w_ref, pa_ref, pb_ref, pif_ref):
    xb = x_ref[...].astype(BF16)
    pa_ref[...] = jnp.dot(xb, w_ref[0, :, 0:W_A_COLS], preferred_element_type=F32)
    pb_ref[...] = jnp.dot(xb, w_ref[0, :, W_A_COLS:W_A_COLS + D_B_IN], preferred_element_type=F32)
    pif_ref[...] = jnp.dot(xb, w_ref[0, :, W_A_COLS + D_B_IN:W_IN_COLS], preferred_element_type=F32)


def _inproj_call(x, w, tm, layer):
    n = x.shape[0]
    return pl.pallas_call(
        _inproj_kernel,
        grid=(n // tm,),
        in_specs=[pl.BlockSpec((tm, D_MODEL), lambda i: (i, 0)),
                  pl.BlockSpec((1, D_MODEL, W_IN_COLS), lambda i: (layer, 0, 0))],
        out_specs=[pl.BlockSpec((tm, W_A_COLS), lambda i: (i, 0)),
                   pl.BlockSpec((tm, D_B_IN), lambda i: (i, 0)),
                   pl.BlockSpec((tm, W_IF_COLS), lambda i: (i, 0))],
        out_shape=[jax.ShapeDtypeStruct((n, W_A_COLS), F32),
                   jax.ShapeDtypeStruct((n, D_B_IN), F32),
                   jax.ShapeDtypeStruct((n, W_IF_COLS), F32)],
        compiler_params=pltpu.CompilerParams(dimension_semantics=("parallel",),
                                             vmem_limit_bytes=VMEM_LIMIT),
        name="inproj",
    )(x, w)


CONV_PAD = 8


def _mlstm_kernel(pa_ref, pif_ref, conv0_ref, c0_ref, n0_ref, m0_ref, convw_ref, convb_ref,
                  bif_ref, gn_ref, *rest, c, nc, nbb, chained):
    h_ref, c_out_ref, n_out_ref, m_out_ref, qk_scr, c_scr, n_scr, m_scr = rest[1:] if chained else rest
    ci = pl.program_id(1)
    rows = nbb * c
    cshift = int(math.log2(c))
    chains = [(nb, h) for nb in range(nbb) for h in range(H_A)]
    prev0 = CONV_PAD - (CONV_W - 1)

    @pl.when(ci == 0)
    def _():
        c_scr[...] = c0_ref[0]
        n_scr[...] = n0_ref[...]
        m_scr[...] = m0_ref[...]
        for nb in range(nbb):
            qk_scr[nb, prev0:CONV_PAD, :] = conv0_ref[nb]

    cat0 = lambda xs: xs[0] if len(xs) == 1 else jnp.concatenate(xs, axis=0)
    acc_l = []
    for nb in range(nbb):
        u = pa_ref[nb, :, 0:2 * D_A]
        qk_scr[nb, CONV_PAD:CONV_PAD + c, :] = u
        acc = convb_ref[...]
        for j in range(CONV_W):
            acc = acc + qk_scr[nb, prev0 + j:prev0 + j + c, :] * convw_ref[j:j + 1, :]
        qk_scr[nb, prev0:CONV_PAD, :] = u[c - (CONV_W - 1):c, :]
        acc_l.append(acc)
    qk = _silu(cat0(acc_l))

    gates = cat0([pif_ref[nb] for nb in range(nbb)]) + bif_ref[...]
    li_all = gates[:, 0:LANES]
    lf_all = _log_sigmoid(gates[:, LANES:2 * LANES])
    row = lax.broadcasted_iota(jnp.int32, (rows, rows), 0)
    col = lax.broadcasted_iota(jnp.int32, (rows, rows), 1)
    tril = jnp.where((col <= row) & ((row >> cshift) == (col >> cshift)), 1.0, 0.0)
    b_all = _dot_exact_lhs(tril, lf_all)
    z_all = li_all - b_all
    crow = lax.broadcasted_iota(jnp.int32, (c, c), 0)
    ccol = lax.broadcasted_iota(jnp.int32, (c, c), 1)
    causal = ccol <= crow
    hrow = lax.broadcasted_iota(jnp.int32, (H_A * c, LANES), 0)
    hlane = lax.broadcasted_iota(jnp.int32, (H_A * c, LANES), 1)
    head_pick = jnp.where((hrow >> cshift) == hlane, 1.0, 0.0)
    z_rows = [_dot_exact_lhs(head_pick, z_all[nb * c:(nb + 1) * c], NT) for nb in range(nbb)]
    lane1 = lax.broadcasted_iota(jnp.int32, (1, LANES), 1)

    rs = lambda nb: slice(nb * c, (nb + 1) * c)
    q_l = [qk[rs(nb), h * DH_A:(h + 1) * DH_A] for nb, h in chains]
    k_l = [qk[rs(nb), D_A + h * DH_A:D_A + (h + 1) * DH_A] * (DH_A ** -0.5) for nb, h in chains]
    v_l = [pa_ref[nb, :, 2 * D_A + h * DH_A:2 * D_A + (h + 1) * DH_A] for nb, h in chains]
    c_l = [c_scr[nb, h] for nb, h in chains]
    n_l = [n_scr[nb, h:h + 1, :] for nb, h in chains]
    b_col = [b_all[rs(nb), h:h + 1] for nb, h in chains]
    li_col = [li_all[rs(nb), h:h + 1] for nb, h in chains]
    m_prev = [m_scr[nb][:, h:h + 1] for nb, h in chains]
    dmat = [jnp.where(causal, bc + z_rows[nb][h * c:(h + 1) * c], -jnp.inf)
            for bc, (nb, h) in zip(b_col, chains)]
    m_inter = [bc + mp for bc, mp in zip(b_col, m_prev)]
    m_t = [jnp.maximum(mi, jnp.max(d, axis=-1, keepdims=True)) for mi, d in zip(m_inter, dmat)]
    qk_dot = [_dot(q, k, NT) for q, k in zip(q_l, k_l)]
    qc = [_dot(q, cm) for q, cm in zip(q_l, c_l)]
    s_l = [x * jnp.exp(d - mt) for x, d, mt in zip(qk_dot, dmat, m_t)]
    sv = [_dot(s, v) for s, v in zip(s_l, v_l)]
    b_last = [bc[c - 1:c, :] for bc in b_col]
    g_s = [bl - bc + li for bl, bc, li in zip(b_last, b_col, li_col)]
    m_new = [jnp.maximum(bl + mp, jnp.max(gs, axis=0, keepdims=True))
             for bl, mp, gs in zip(b_last, m_prev, g_s)]
    wk = [jnp.exp(gs - mn) * k for gs, mn, k in zip(g_s, m_new, k_l)]
    w_old = [jnp.exp(bl + mp - mn) for bl, mp, mn in zip(b_last, m_prev, m_new)]
    kv = [_dot(w, v, TN) for w, v in zip(wk, v_l)]
    for i, (nb, h) in enumerate(chains):
        c_scr[nb, h] = w_old[i] * c_l[i] + kv[i]
        n_scr[nb, h:h + 1, :] = w_old[i] * n_l[i] + jnp.sum(wk[i], axis=0, keepdims=True)
    for nb in range(nbb):
        m_row = m_scr[nb]
        for h in range(H_A):
            m_row = jnp.where(lane1 == h, m_new[nb * H_A + h], m_row)
        m_scr[nb] = m_row
    w_inter = [jnp.exp(mi - mt) for mi, mt in zip(m_inter, m_t)]
    qn = [jnp.sum(q * nv, axis=-1, keepdims=True) for q, nv in zip(q_l, n_l)]
    s_sum = [jnp.sum(s, axis=-1, keepdims=True) for s in s_l]
    den = [w * a + b for w, a, b in zip(w_inter, qn, s_sum)]
    hh = [(w * a + b) / jnp.maximum(jnp.abs(d), jnp.exp(-mt))
          for w, a, b, d, mt in zip(w_inter, qc, sv, den, m_t)]
    mu = [jnp.mean(x, axis=-1, keepdims=True) for x in hh]
    hc = [x - m for x, m in zip(hh, mu)]
    var = [jnp.mean(x * x, axis=-1, keepdims=True) for x in hc]
    for i, (nb, h) in enumerate(chains):
        sl = slice(h * DH_A, (h + 1) * DH_A)
        hn = hc[i] * lax.rsqrt(var[i] + GN_EPS_A)
        o_pre = pa_ref[nb, :, 3 * D_A + h * DH_A:3 * D_A + (h + 1) * DH_A]
        h_ref[nb, :, sl] = hn * gn_ref[:, sl] * _sigmoid(o_pre)

    @pl.when(ci == nc - 1)
    def _():
        c_out_ref[0] = c_scr[...]
        n_out_ref[...] = n_scr[...]
        m_out_ref[...] = m_scr[...]


def _mlstm_call(pa, pif, conv0, c_in, n0, m0, convw, convb, bif, gn, c_stack, *, batch, seq, nbb,
                layer_in, layer):
    c = math.gcd(seq, CHUNK)
    nc = seq // c
    tok = lambda w: pl.BlockSpec((nbb, c, w), lambda b, i: (b, i, 0))
    vec = lambda r, w: pl.BlockSpec((r, w), lambda b, i: (0, 0))
    st5 = lambda l: pl.BlockSpec((1, nbb, H_A, DH_A, DH_A), lambda b, i: (l, b, 0, 0, 0))
    st3 = lambda r, w: pl.BlockSpec((nbb, r, w), lambda b, i: (b, 0, 0))
    chained = c_stack is not None
    in_specs = [tok(W_A_COLS), tok(W_IF_COLS), st3(CONV_W - 1, 2 * D_A), st5(layer_in), st3(H_A, DH_A),
                st3(1, LANES), vec(CONV_W, 2 * D_A), vec(1, 2 * D_A), vec(1, W_IF_COLS), vec(1, D_A)]
    args = [pa, pif, conv0, c_in, n0, m0, convw, convb, bif, gn]
    if chained:
        in_specs.append(pl.BlockSpec(memory_space=pl.ANY))
        args.append(c_stack)
    return pl.pallas_call(
        functools.partial(_mlstm_kernel, c=c, nc=nc, nbb=nbb, chained=chained),
        grid=(batch // nbb, nc),
        in_specs=in_specs,
        out_specs=[tok(D_A), st5(layer), st3(H_A, DH_A), st3(1, LANES)],
        out_shape=[jax.ShapeDtypeStruct((batch, seq, D_A), F32),
                   jax.ShapeDtypeStruct((DEPTH, batch, H_A, DH_A, DH_A), F32),
                   jax.ShapeDtypeStruct((batch, H_A, DH_A), F32),
                   jax.ShapeDtypeStruct((batch, 1, LANES), F32)],
        scratch_shapes=[pltpu.VMEM((nbb, CONV_PAD + c, 2 * D_A), F32),
                        pltpu.VMEM((nbb, H_A, DH_A, DH_A), F32),
                        pltpu.VMEM((nbb, H_A, DH_A), F32),
                        pltpu.VMEM((nbb, 1, LANES), F32)],
        input_output_aliases={len(args) - 1: 1} if chained else {},
        compiler_params=pltpu.CompilerParams(dimension_semantics=("parallel", "arbitrary"),
                                             vmem_limit_bytes=VMEM_LIMIT),
        name="mlstm",
    )(*args)


SHIFT_PAD = 8


def _neumann_inverses(ms, size, t):
    row = lax.broadcasted_iota(jnp.int32, (size, size), 0)
    col = lax.broadcasted_iota(jnp.int32, (size, size), 1)
    eye = jnp.where(row == col, 1.0, 0.0)
    blk = min(SOLVE_BLOCK, t)
    shift = int(math.log2(blk))
    same = (row >> shift) == (col >> shift)
    ds = [jnp.where(same, m, 0.0) for m in ms]
    ps = [eye + d for d in ds]
    xs = ds
    for _ in range(shift - 1):
        xs = [_dot(x, x) for x in xs]
        ps = [p + _dot(p, x) for p, x in zip(ps, xs)]
    nblk = t // blk
    if nblk == 1:
        return ps, None
    ns = [_dot(p, m - d) for p, m, d in zip(ps, ms, ds)]
    qs = [eye + n for n in ns]
    ys = ns
    for _ in range(int(math.log2(nblk)) - 1):
        ys = [_dot(y, y) for y in ys]
        qs = [q + _dot(q, y) for q, y in zip(qs, ys)]
    return ps, qs


def _rwkv_kernel(pb_ref, shift0_ref, s0_ref, mu_ref, w0a0_ref, wlora_ref, g2_ref, kk_ref, ka_ref,
                 rk_ref, lng_ref, lnb_ref, *rest, t, nc, nbb, chained):
    h_ref, s_out_ref, sh_scr, s_scr = rest[1:] if chained else rest
    ci = pl.program_id(1)
    rows = nbb * t
    t2 = 2 * t
    groups = [(nb, pr) for nb in range(nbb) for pr in range(N_PAIRS)]

    brow = lax.broadcasted_iota(jnp.int32, (LANES, LANES), 0)
    bcol = lax.broadcasted_iota(jnp.int32, (LANES, LANES), 1)
    blockdiag = (brow >= DH_B) == (bcol >= DH_B)
    bd2 = jnp.where(blockdiag, 1.0, 0.0).astype(BF16)

    @pl.when(ci == 0)
    def _():
        er = lax.broadcasted_iota(jnp.int32, (DH_B, LANES), 0)
        ec = lax.broadcasted_iota(jnp.int32, (DH_B, LANES), 1)
        dup_cols = jnp.where((ec & (DH_B - 1)) == er, 1.0, 0.0)
        for nb in range(nbb):
            sh_scr[nb, SHIFT_PAD - 1:SHIFT_PAD, :] = shift0_ref[nb]
            for pr in range(N_PAIRS):
                x = s0_ref[0, nb, 2 * pr:2 * pr + 2].reshape(LANES, DH_B)
                s_scr[nb * N_PAIRS + pr] = jnp.where(blockdiag, _dot_exact_rhs(x, dup_cols), 0.0)

    p_l, prev_l = [], []
    for nb in range(nbb):
        p_nb = pb_ref[nb]
        sh_scr[nb, SHIFT_PAD:SHIFT_PAD + t, :] = p_nb
        prev_l.append(sh_scr[nb, SHIFT_PAD - 1:SHIFT_PAD - 1 + t, :])
        sh_scr[nb, SHIFT_PAD - 1:SHIFT_PAD, :] = p_nb[t - 1:t, :]
        p_l.append(p_nb)
    cat0 = lambda xs: xs[0] if len(xs) == 1 else jnp.concatenate(xs, axis=0)
    p = cat0(p_l)
    prev = cat0(prev_l)
    pb = p + (prev - p) * mu_ref[...]

    r = pb[:, 0:D_B]
    kr = pb[:, D_B:2 * D_B]
    vr = pb[:, 2 * D_B:3 * D_B]
    lora_in = pb[:, 3 * D_B:3 * D_B + LANES]
    gl = pb[:, 3 * D_B + LANES:3 * D_B + 2 * LANES]
    lane = lax.broadcasted_iota(jnp.int32, (rows, LANES), 1)
    lora_act = jnp.where(lane < R_DECAY, jnp.tanh(lora_in), lora_in)
    z = _dot(lora_act, wlora_ref[...]) + w0a0_ref[...]
    ld = -jnp.exp(_log_sigmoid(z[:, 0:D_B]) - 0.5)
    a = _sigmoid(z[:, D_B:2 * D_B])
    g = _dot(_sigmoid(gl), g2_ref[...])

    def seg_sum(x):
        xr = jnp.concatenate([x[:, q * LANES:(q + 1) * LANES] for q in range(N_PAIRS)], axis=0)
        hi = xr.astype(BF16)
        lo = (xr - hi.astype(F32)).astype(BF16)
        s = (jnp.dot(hi, bd2, preferred_element_type=F32) + jnp.dot(lo, bd2, preferred_element_type=F32))
        return jnp.concatenate([s[q * rows:(q + 1) * rows] for q in range(N_PAIRS)], axis=1)

    kk_raw = kr * kk_ref[...]
    kk = kk_raw / jnp.maximum(jnp.sqrt(seg_sum(kk_raw * kk_raw)), 1e-12)
    k2 = kr * (1.0 + (a - 1.0) * ka_ref[...])
    bonus = seg_sum(r * k2 * rk_ref[...]) * vr

    row = lax.broadcasted_iota(jnp.int32, (rows, rows), 0)
    col = lax.broadcasted_iota(jnp.int32, (rows, rows), 1)
    tshift = int(math.log2(t))
    tril = jnp.where((col <= row) & ((row >> tshift) == (col >> tshift)), 1.0, 0.0)
    lc = _dot_exact_lhs(tril, ld)
    lends = [lc[(nb + 1) * t - 1:(nb + 1) * t, :] for nb in range(nbb)]
    lend_rows = cat0([jnp.broadcast_to(le, (t, D_B)) for le in lends])
    e_nc = jnp.exp(-lc)
    e_end = jnp.exp(lend_rows - lc)
    b_raw = kk * a
    at = -kk * jnp.exp(lc - ld)
    rt = r * jnp.exp(lc)
    kt = k2 * e_nc
    bt = b_raw * e_nc
    kend = k2 * e_end
    bend = b_raw * e_end

    srow = lax.broadcasted_iota(jnp.int32, (t2, t2), 0)
    scol = lax.broadcasted_iota(jnp.int32, (t2, t2), 1)
    strict = scol < srow
    incl = scol <= srow
    lane2 = lax.broadcasted_iota(jnp.int32, (t2, LANES), 1)
    row2 = lax.broadcasted_iota(jnp.int32, (t2, LANES), 0)
    head_sel = (lane2 >= DH_B) == (row2 >= t)
    first_head = lax.broadcasted_iota(jnp.int32, (t, LANES), 1) < DH_B

    def blk(x, nb, pr):
        return x[nb * t:(nb + 1) * t, pr * LANES:(pr + 1) * LANES]

    def stack(x):
        return jnp.where(head_sel, jnp.concatenate([x, x], axis=0), 0.0)

    lhs = [jnp.concatenate([stack(blk(at, *gp)), stack(blk(rt, *gp))], axis=0) for gp in groups]
    rk_s = [stack(blk(kt, *gp)) for gp in groups]
    rb_s = [stack(blk(bt, *gp)) for gp in groups]
    v_p = [blk(vr, *gp) for gp in groups]
    v_s = [jnp.concatenate([v, v], axis=0) for v in v_p]
    s_p = [s_scr[i] for i in range(len(groups))]
    gk = [_dot(l, x, NT) for l, x in zip(lhs, rk_s)]
    gb = [_dot(l, x, NT) for l, x in zip(lhs, rb_s)]
    xs = [_dot(l, s, NT) for l, s in zip(lhs, s_p)]
    m_ab = [jnp.where(strict, x[0:t2], 0.0) for x in gb]
    ps, qs = _neumann_inverses(m_ab, t2, t)
    rhs = [x[0:t2] + _dot(jnp.where(strict, y[0:t2], 0.0), v) for x, y, v in zip(xs, gk, v_s)]
    sa_s = [_dot(pm, x) for pm, x in zip(ps, rhs)]
    if qs is not None:
        sa_s = [_dot(qm, x) for qm, x in zip(qs, sa_s)]
    o_s = [x[t2:] + _dot(jnp.where(incl, y[t2:], 0.0), v) + _dot(jnp.where(incl, w[t2:], 0.0), sa)
           for x, y, w, v, sa in zip(xs, gk, gb, v_s, sa_s)]
    o_p = [jnp.where(first_head, o[0:t], o[t:t2]) for o in o_s]
    sa_p = [jnp.where(first_head, sa[0:t], sa[t:t2]) for sa in sa_s]
    upd = [_dot(jnp.concatenate([v, sa], axis=0),
                jnp.concatenate([blk(kend, *gp), blk(bend, *gp)], axis=0), TN)
           for v, sa, gp in zip(v_p, sa_p, groups)]
    for i, (nb, pr) in enumerate(groups):
        dec = jnp.exp(lends[nb][:, pr * LANES:(pr + 1) * LANES])
        s_scr[i] = s_p[i] * dec + jnp.where(blockdiag, upd[i], 0.0)

    o = cat0([jnp.concatenate(o_p[nb * N_PAIRS:(nb + 1) * N_PAIRS], axis=1) for nb in range(nbb)])
    mean = seg_sum(o) * (1.0 / DH_B)
    oc = o - mean
    var = seg_sum(oc * oc) * (1.0 / DH_B)
    out = oc * lax.rsqrt(var + GN_EPS_B) * lng_ref[...] + lnb_ref[...]
    res = (out + bonus) * g
    for nb in range(nbb):
        h_ref[nb] = res[nb * t:(nb + 1) * t, :]

    @pl.when(ci == nc - 1)
    def _():
        fr = lax.broadcasted_iota(jnp.int32, (LANES, DH_B), 0)
        fc = lax.broadcasted_iota(jnp.int32, (LANES, DH_B), 1)
        dup_rows = jnp.where((fr & (DH_B - 1)) == fc, 1.0, 0.0)
        for i, (nb, pr) in enumerate(groups):
            packed = _dot_exact_rhs(s_scr[i], dup_rows)
            s_out_ref[0, nb, 2 * pr:2 * pr + 2] = packed.reshape(2, DH_B, DH_B)


def _rwkv_call(pb, shift0, s_in, mu, w0a0, wlora, g2, kk, ka, rk, lng, lnb, s_stack, *, batch, seq, nbb,
               layer_in, layer):
    t = math.gcd(seq, CHUNK)
    nc = seq // t
    tok = lambda w: pl.BlockSpec((nbb, t, w), lambda b, i: (b, i, 0))
    vec = lambda r, w: pl.BlockSpec((r, w), lambda b, i: (0, 0))
    st5 = lambda l: pl.BlockSpec((1, nbb, H_B, DH_B, DH_B), lambda b, i: (l, b, 0, 0, 0))
    chained = s_stack is not None
    in_specs = [tok(D_B_IN), pl.BlockSpec((nbb, 1, D_B_IN), lambda b, i: (b, 0, 0)), st5(layer_in),
                vec(1, D_B_IN), vec(1, 2 * D_B), vec(LANES, 2 * D_B), vec(R_GATE, D_B),
                vec(1, D_B), vec(1, D_B), vec(1, D_B), vec(1, D_B), vec(1, D_B)]
    args = [pb, shift0, s_in, mu, w0a0, wlora, g2, kk, ka, rk, lng, lnb]
    if chained:
        in_specs.append(pl.BlockSpec(memory_space=pl.ANY))
        args.append(s_stack)
    return pl.pallas_call(
        functools.partial(_rwkv_kernel, t=t, nc=nc, nbb=nbb, chained=chained),
        grid=(batch // nbb, nc),
        in_specs=in_specs,
        out_specs=[tok(D_B), st5(layer)],
        out_shape=[jax.ShapeDtypeStruct((batch, seq, D_B), F32),
                   jax.ShapeDtypeStruct((DEPTH, batch, H_B, DH_B, DH_B), F32)],
        scratch_shapes=[pltpu.VMEM((nbb, SHIFT_PAD + t, D_B_IN), F32),
                        pltpu.VMEM((nbb * N_PAIRS, LANES, LANES), F32)],
        input_output_aliases={len(args) - 1: 1} if chained else {},
        compiler_params=pltpu.CompilerParams(dimension_semantics=("parallel", "arbitrary"),
                                             vmem_limit_bytes=VMEM_LIMIT),
        name="rwkv",
    )(*args)


ROUTE_ROWS = 8


def _outproj_kernel(x_ref, ha_ref, hb_ref, wo_ref, g_ref, b_ref, wr_ref, br_ref, x1_ref, route_ref,
                    cnt_ref, *, tm):
    y = _dot(ha_ref[...], wo_ref[0, 0:D_A, :]) + _dot(hb_ref[...], wo_ref[0, D_A:D_A + D_B, :])
    x1 = _layer_norm(ALPHA * x_ref[...] + y, g_ref[...], b_ref[...], LN_EPS)
    x1_ref[...] = x1
    logits = lax.dot_general(wr_ref[...], x1, NT, precision=lax.Precision.HIGHEST,
                             preferred_element_type=F32)
    mx = jnp.max(logits, axis=0, keepdims=True)
    ex = jnp.exp(logits - mx)
    probs = ex / jnp.sum(ex, axis=0, keepdims=True)
    sel = probs + br_ref[...]
    neg = -jnp.inf

    def top2(rows):
        m1 = functools.reduce(jnp.maximum, rows)
        i1 = jnp.full(m1.shape, len(rows) - 1, jnp.int32)
        for j in range(len(rows) - 2, -1, -1):
            i1 = jnp.where(rows[j] == m1, j, i1)
        rest = [jnp.where(i1 == j, neg, rows[j]) for j in range(len(rows))]
        m2 = functools.reduce(jnp.maximum, rest)
        i2 = jnp.full(m2.shape, len(rows) - 1, jnp.int32)
        for j in range(len(rows) - 2, -1, -1):
            i2 = jnp.where(rest[j] == m2, j, i2)
        return m1, i1, m2, i2

    g_score, g_i1, g_i2 = [], [], []
    for gidx in range(N_GROUPS):
        rows = [sel[gidx * EXPERTS_PER_GROUP + j:gidx * EXPERTS_PER_GROUP + j + 1, :]
                for j in range(EXPERTS_PER_GROUP)]
        m1, i1, m2, i2 = top2(rows)
        g_score.append(m1 + m2)
        g_i1.append(i1)
        g_i2.append(i2)
    best = functools.reduce(jnp.maximum, g_score)
    grp = jnp.full(best.shape, N_GROUPS - 1, jnp.int32)
    for gidx in range(N_GROUPS - 2, -1, -1):
        grp = jnp.where(g_score[gidx] == best, gidx, grp)
    l1 = g_i1[N_GROUPS - 1]
    l2 = g_i2[N_GROUPS - 1]
    for gidx in range(N_GROUPS - 2, -1, -1):
        l1 = jnp.where(grp == gidx, g_i1[gidx], l1)
        l2 = jnp.where(grp == gidx, g_i2[gidx], l2)
    e1 = grp * EXPERTS_PER_GROUP + l1
    e2 = grp * EXPERTS_PER_GROUP + l2
    e_iota = lax.broadcasted_iota(jnp.int32, (N_EXPERTS, tm), 0)
    p1 = jnp.sum(jnp.where(e_iota == e1, probs, 0.0), axis=0, keepdims=True)
    p2 = jnp.sum(jnp.where(e_iota == e2, probs, 0.0), axis=0, keepdims=True)
    tot = p1 + p2
    r8 = lax.broadcasted_iota(jnp.int32, (ROUTE_ROWS, tm), 0)
    route_ref[...] = jnp.where(r8 == 0, e1.astype(F32),
                               jnp.where(r8 == 1, e2.astype(F32),
                                         jnp.where(r8 == 2, p1 / tot, jnp.where(r8 == 3, p2 / tot, 0.0))))
    picked = jnp.where((e_iota == e1) | (e_iota == e2), 1.0, 0.0)
    cnt_ref[0] = jnp.broadcast_to(jnp.sum(picked, axis=1, keepdims=True), (N_EXPERTS, LANES))


def _outproj_call(x, ha, hb, wo, g, b, wr_t, br, tm, layer):
    n = x.shape[0]
    row = lambda w: pl.BlockSpec((tm, w), lambda i: (i, 0))
    full = lambda r, w: pl.BlockSpec((r, w), lambda i: (0, 0))
    return pl.pallas_call(
        functools.partial(_outproj_kernel, tm=tm),
        grid=(n // tm,),
        in_specs=[row(D_MODEL), row(D_A), row(D_B),
                  pl.BlockSpec((1, D_MODEL, D_MODEL), lambda i: (layer, 0, 0)), full(1, D_MODEL),
                  full(1, D_MODEL), full(N_EXPERTS, D_MODEL), full(N_EXPERTS, 1)],
        out_specs=[row(D_MODEL), pl.BlockSpec((ROUTE_ROWS, tm), lambda i: (0, i)),
                   pl.BlockSpec((1, N_EXPERTS, LANES), lambda i: (i, 0, 0))],
        out_shape=[jax.ShapeDtypeStruct((n, D_MODEL), F32), jax.ShapeDtypeStruct((ROUTE_ROWS, n), F32),
                   jax.ShapeDtypeStruct((n // tm, N_EXPERTS, LANES), F32)],
        compiler_params=pltpu.CompilerParams(dimension_semantics=("parallel",),
                                             vmem_limit_bytes=VMEM_LIMIT),
        name="outproj",
    )(x, ha, hb, wo, g, b, wr_t, br)


MOE_BM = 128
MOE_CH = 512


def _moe_max_blocks(t):
    return -(-2 * t // MOE_BM) + N_EXPERTS - 1


def _moe_sort_kernel(nblk_ref, x_ref, route_ref, padoff_ref, tri_ref, xs_ref, cols_ref, *, t, n_tiles,
                     n_chunks):
    i = pl.program_id(0)

    @pl.when(i >= n_tiles)
    def _():
        xs_ref[...] = jnp.zeros_like(xs_ref)

    @pl.when(i < n_tiles)
    def _():
        s_used = nblk_ref[jnp.minimum(i, n_tiles - 1)] * MOE_BM
        xb = x_ref[...].astype(BF16)
        route = route_ref[...]
        e_iota = lax.broadcasted_iota(jnp.int32, (N_EXPERTS, t), 0)
        a1 = e_iota == route[0:1].astype(jnp.int32)
        a2 = e_iota == route[1:2].astype(jnp.int32)
        picked = jnp.where(a1 | a2, 1.0, 0.0).astype(BF16)
        rank = jnp.dot(picked, tri_ref[...], preferred_element_type=F32)
        base = padoff_ref[0][:, 0:1] + rank
        slot1 = jnp.sum(jnp.where(a1, base, 0.0), axis=0, keepdims=True)
        slot2 = jnp.sum(jnp.where(a2, base, 0.0), axis=0, keepdims=True)
        r8 = lax.broadcasted_iota(jnp.int32, (ROUTE_ROWS, t), 0)
        rows = jnp.where(r8 == 0, slot1, jnp.where(r8 == 1, slot2,
                                                   jnp.where(r8 == 2, route[2:3],
                                                             jnp.where(r8 == 3, route[3:4], 0.0))))
        padded = jnp.concatenate([rows, jnp.zeros((LANES - ROUTE_ROWS, t), F32)], axis=0)
        cols_ref[...] = padded.T
        s1 = slot1.astype(jnp.int32)
        s2 = slot2.astype(jnp.int32)
        for c in range(n_chunks):
            used = c * MOE_CH < s_used

            @pl.when(used)
            def _():
                s_iota = lax.broadcasted_iota(jnp.int32, (MOE_CH, t), 0) + c * MOE_CH
                onehot = jnp.where((s_iota == s1) | (s_iota == s2), 1.0, 0.0).astype(BF16)
                xs_ref[0, c * MOE_CH:(c + 1) * MOE_CH, :] = jnp.dot(
                    onehot, xb, preferred_element_type=F32).astype(BF16)

            @pl.when(jnp.logical_not(used))
            def _():
                xs_ref[0, c * MOE_CH:(c + 1) * MOE_CH, :] = jnp.zeros((MOE_CH, D_MODEL), BF16)


def _moe_expert_kernel(se_ref, sblk_ref, nsteps_ref, xs_ref, wg_ref, wu_ref, wd_ref, ys_ref,
                       wg_scr, wu_scr, wd_scr):
    s = pl.program_id(0)
    prev = se_ref[jnp.maximum(s - 1, 0)]

    @pl.when((s == 0) | (se_ref[s] != prev))
    def _():
        wg_scr[...] = wg_ref[0, 0].astype(BF16)
        wu_scr[...] = wu_ref[0, 0].astype(BF16)
        wd_scr[...] = wd_ref[0, 0].astype(BF16)

    @pl.when(s < nsteps_ref[0])
    def _():
        xblk = xs_ref[...]
        hid = _silu(jnp.dot(xblk, wg_scr[...], preferred_element_type=F32)) * jnp.dot(
            xblk, wu_scr[...], preferred_element_type=F32)
        y = jnp.dot(hid.astype(BF16), wd_scr[...], preferred_element_type=F32)
        ys_ref[...] = y.astype(BF16)

    @pl.when(s >= nsteps_ref[0])
    def _():
        ys_ref[...] = xs_ref[...]


def _moe_combine_kernel(nblk_ref, x_ref, cols_ref, ys_ref, g_ref, b_ref, o_ref, acc_scr, *, t, n_chunks):
    s_used = nblk_ref[pl.program_id(0)] * MOE_BM
    cols = cols_ref[...]
    s1c = cols[:, 0:1].astype(jnp.int32)
    s2c = cols[:, 1:2].astype(jnp.int32)
    w1c = cols[:, 2:3]
    w2c = cols[:, 3:4]
    for c in range(n_chunks):
        @pl.when(c * MOE_CH < s_used)
        def _():
            l_iota = lax.broadcasted_iota(jnp.int32, (t, MOE_CH), 1) + c * MOE_CH
            weighted = (jnp.where(l_iota == s1c, w1c, 0.0) + jnp.where(l_iota == s2c, w2c, 0.0)).astype(BF16)
            part = jnp.dot(weighted, ys_ref[0, c * MOE_CH:(c + 1) * MOE_CH, :], preferred_element_type=F32)
            if c == 0:
                acc_scr[...] = part
            else:
                acc_scr[...] += part
    o_ref[...] = _layer_norm(ALPHA * x_ref[...] + acc_scr[...], g_ref[...], b_ref[...], LN_EPS)


def _moe_schedule(counts, tiles_per, maxb, bpt):
    cnt = counts[:, :, 0].astype(jnp.int32)
    cnt = cnt.reshape(-1, tiles_per, N_EXPERTS).sum(axis=1)
    n_tiles = cnt.shape[0]
    nblk = (cnt + MOE_BM - 1) // MOE_BM
    end = jnp.cumsum(nblk, axis=-1)
    start = end - nblk
    total = end[:, -1]
    padoff = jnp.broadcast_to((start * MOE_BM).astype(F32)[:, :, None], (n_tiles, N_EXPERTS, LANES))
    per_e = jnp.sum(nblk, axis=0)
    e_end = jnp.cumsum(per_e)
    e_base = e_end - per_e
    tile_prefix = jnp.cumsum(nblk, axis=0) - nblk
    j = jnp.arange(maxb, dtype=jnp.int32)
    blk_e = jnp.minimum(jnp.sum((j[None, :, None] >= end[:, None, :]).astype(jnp.int32), axis=-1),
                        N_EXPERTS - 1)
    take = lambda a: jnp.take_along_axis(a, blk_e, axis=1)
    pos = e_base[blk_e] + take(tile_prefix) + (j[None, :] - take(start))
    n_steps = n_tiles * maxb
    pos = jnp.where(j[None, :] < total[:, None], pos, n_steps)
    flat_id = jnp.arange(n_tiles, dtype=jnp.int32)[:, None] * bpt + j[None, :]
    sblk = jnp.full((n_steps,), n_tiles * bpt, jnp.int32).at[pos.reshape(-1)].set(
        flat_id.reshape(-1), mode="drop")
    s_idx = jnp.arange(n_steps, dtype=jnp.int32)
    last_e = jnp.max(jnp.where(per_e > 0, jnp.arange(N_EXPERTS, dtype=jnp.int32), 0))
    se = jnp.minimum(jnp.sum((s_idx[:, None] >= e_end[None, :]).astype(jnp.int32), axis=-1), last_e)
    return total, padoff, se, sblk, e_end[-1:]


def _moe_call(x, route, counts, wg, wu, wd, g, b, *, layer, tm):
    n = x.shape[0]
    t = 1024 if n % 1024 == 0 else n
    n_tiles = n // t
    maxb = _moe_max_blocks(t)
    n_chunks = -(-maxb * MOE_BM // MOE_CH)
    s_alloc = n_chunks * MOE_CH
    bpt = s_alloc // MOE_BM
    nblk, padoff, se, sblk, nsteps = _moe_schedule(counts, t // tm, maxb, bpt)
    tri = jnp.triu(jnp.ones((t, t), BF16), k=1)
    last = n_tiles - 1
    xs, cols = pl.pallas_call(
        functools.partial(_moe_sort_kernel, t=t, n_tiles=n_tiles, n_chunks=n_chunks),
        grid_spec=pltpu.PrefetchScalarGridSpec(
            num_scalar_prefetch=1,
            grid=(n_tiles + 1,),
            in_specs=[pl.BlockSpec((t, D_MODEL), lambda i, nb: (jnp.minimum(i, last), 0)),
                      pl.BlockSpec((ROUTE_ROWS, t), lambda i, nb: (0, jnp.minimum(i, last))),
                      pl.BlockSpec((1, N_EXPERTS, LANES), lambda i, nb: (jnp.minimum(i, last), 0, 0)),
                      pl.BlockSpec((t, t), lambda i, nb: (0, 0))],
            out_specs=[pl.BlockSpec((1, s_alloc, D_MODEL), lambda i, nb: (i, 0, 0)),
                       pl.BlockSpec((t, LANES), lambda i, nb: (jnp.minimum(i, last), 0))]),
        out_shape=[jax.ShapeDtypeStruct((n_tiles + 1, s_alloc, D_MODEL), BF16),
                   jax.ShapeDtypeStruct((n, LANES), F32)],
        compiler_params=pltpu.CompilerParams(dimension_semantics=("arbitrary",),
                                             vmem_limit_bytes=VMEM_LIMIT),
        name="moe_sort",
    )(nblk, x, route, padoff, tri)
    wspec = lambda r, c: pl.BlockSpec((1, 1, r, c), lambda s, se_, sb, ns: (layer, se_[s], 0, 0))
    blk = pl.BlockSpec((MOE_BM, D_MODEL), lambda s, se_, sb, ns: (sb[s], 0))
    ys = pl.pallas_call(
        _moe_expert_kernel,
        grid_spec=pltpu.PrefetchScalarGridSpec(
            num_scalar_prefetch=3,
            grid=(n_tiles * maxb,),
            in_specs=[blk, wspec(D_MODEL, D_EXP), wspec(D_MODEL, D_EXP), wspec(D_EXP, D_MODEL)],
            out_specs=blk,
            scratch_shapes=[pltpu.VMEM((D_MODEL, D_EXP), BF16), pltpu.VMEM((D_MODEL, D_EXP), BF16),
                            pltpu.VMEM((D_EXP, D_MODEL), BF16)]),
        out_shape=jax.ShapeDtypeStruct(((n_tiles + 1) * s_alloc, D_MODEL), BF16),
        input_output_aliases={3: 0},
        compiler_params=pltpu.CompilerParams(dimension_semantics=("arbitrary",),
                                             vmem_limit_bytes=VMEM_LIMIT),
        name="moe_experts",
    )(se, sblk, nsteps, xs.reshape((n_tiles + 1) * s_alloc, D_MODEL), wg, wu, wd)
    vec = pl.BlockSpec((1, D_MODEL), lambda i, nb: (0, 0))
    return pl.pallas_call(
        functools.partial(_moe_combine_kernel, t=t, n_chunks=n_chunks),
        grid_spec=pltpu.PrefetchScalarGridSpec(
            num_scalar_prefetch=1,
            grid=(n_tiles,),
            in_specs=[pl.BlockSpec((t, D_MODEL), lambda i, nb: (i, 0)),
                      pl.BlockSpec((t, LANES), lambda i, nb: (i, 0)),
                      pl.BlockSpec((1, s_alloc, D_MODEL), lambda i, nb: (i, 0, 0)), vec, vec],
            out_specs=pl.BlockSpec((t, D_MODEL), lambda i, nb: (i, 0)),
            scratch_shapes=[pltpu.VMEM((t, D_MODEL), F32)]),
        out_shape=jax.ShapeDtypeStruct((n, D_MODEL), F32),
        compiler_params=pltpu.CompilerParams(dimension_semantics=("parallel",),
                                             vmem_limit_bytes=VMEM_LIMIT),
        name="moe_combine",
    )(nblk, x, cols, ys.reshape(n_tiles + 1, s_alloc, D_MODEL), g, b)


def _row_tile(n):
    return 512 if n % 512 == 0 else n


def _seqs_per_step(batch, want):
    return want if batch % want == 0 else 1


def kernel(x_prompt, x_sample, state_mlstm_C, state_mlstm_n, state_mlstm_m, state_mlstm_conv, state_rwkv_S, state_rwkv_shift, ln0_g, ln0_b, w_in, conv_w, conv_b, b_i, b_f, gn_a_g, mu_shift, w0, w2, a0, a2, g2, k_k, k_a, r_k, lnx_g, lnx_b, w_out, ln1_g, ln1_b, w_router, b_router, we_gate, we_up, we_down, ln2_g, ln2_b):
    d_a_in = 4 * D_A + 2 * H_A
    zpad = jnp.zeros((DEPTH, D_MODEL, LANES - H_A), F32)
    w_cat = jnp.concatenate(
        [w_in[:, :, 0:4 * D_A], w_in[:, :, d_a_in:], w_in[:, :, 4 * D_A:4 * D_A + H_A], zpad,
         w_in[:, :, 4 * D_A + H_A:d_a_in], zpad], axis=-1).astype(BF16)
    bif = jnp.zeros((DEPTH, 1, W_IF_COLS), F32)
    bif = bif.at[:, 0, 0:H_A].set(b_i).at[:, 0, LANES:LANES + H_A].set(b_f)
    wlora = jnp.zeros((DEPTH, LANES, 2 * D_B), F32)
    wlora = wlora.at[:, 0:R_DECAY, 0:D_B].set(w2).at[:, R_DECAY:, D_B:].set(a2).astype(BF16)
    w0a0 = jnp.concatenate([w0, a0], axis=-1)[:, None, :]
    g2b = g2.astype(BF16)
    wob = w_out.astype(BF16)
    wr_t = w_router.T
    br = b_router[:, None]
    r1 = lambda v: v[None, :]

    def run(x3, states):
        nb, seq, _ = x3.shape
        n = nb * seq
        tm = _row_tile(n)
        rwkv_nbb = _seqs_per_step(nb, 4)
        mlstm_nbb = _seqs_per_step(nb, 2 if seq >= CHUNK else 4)
        x = _ln_call(x3.reshape(n, D_MODEL), r1(ln0_g), r1(ln0_b), tm)
        outs = []
        c_stack = jnp.zeros((DEPTH, nb, H_A, DH_A, DH_A), F32)
        s_stack = jnp.zeros((DEPTH, nb, H_B, DH_B, DH_B), F32)
        for l in range(DEPTH):
            if states is None:
                c_in = jnp.zeros((1, nb, H_A, DH_A, DH_A), F32)
                s_in = jnp.zeros((1, nb, H_B, DH_B, DH_B), F32)
                layer_in = 0
                n0 = jnp.zeros((nb, H_A, DH_A), F32)
                m0 = jnp.zeros((nb, H_A), F32)
                conv0 = jnp.zeros((nb, CONV_W - 1, 2 * D_A), F32)
                shift0 = jnp.zeros((nb, D_B_IN), F32)
            else:
                c_in, s_in, layer_in = states[0], states[4], l
                n0, m0, conv0, shift0 = (states[k][l] for k in (1, 2, 3, 5))
            m0p = jnp.zeros((nb, 1, LANES), F32).at[:, 0, 0:H_A].set(m0)
            pa, pb, pif = _inproj_call(x, w_cat, tm, l)
            ha, c_stack, n_new, m_new = _mlstm_call(
                pa.reshape(nb, seq, W_A_COLS), pif.reshape(nb, seq, W_IF_COLS), conv0, c_in, n0, m0p,
                conv_w[l], r1(conv_b[l]), bif[l], r1(gn_a_g[l]), c_stack, batch=nb, seq=seq,
                nbb=mlstm_nbb, layer_in=layer_in, layer=l)
            ha = ha.reshape(n, D_A)
            hb, s_stack = _rwkv_call(
                pb.reshape(nb, seq, D_B_IN), shift0[:, None, :], s_in, r1(mu_shift[l]), w0a0[l], wlora[l],
                g2b[l], r1(k_k[l]), r1(k_a[l]), r1(r_k[l].reshape(D_B)), r1(lnx_g[l]), r1(lnx_b[l]),
                s_stack, batch=nb, seq=seq, nbb=rwkv_nbb, layer_in=layer_in, layer=l)
            hb = hb.reshape(n, D_B)
            x1, route, counts = _outproj_call(x, ha, hb, wob, r1(ln1_g[l]), r1(ln1_b[l]), wr_t, br, tm, l)
            x = _moe_call(x1, route, counts, we_gate, we_up, we_down, r1(ln2_g[l]), r1(ln2_b[l]),
                          layer=l, tm=tm)
            pa3 = pa.reshape(nb, seq, W_A_COLS)
            full = jnp.concatenate([conv0, pa3[:, :, 0:2 * D_A]], axis=1) if seq < CONV_W - 1 else pa3[:, :, 0:2 * D_A]
            conv_new = full[:, -(CONV_W - 1):, :]
            shift_new = pb.reshape(nb, seq, D_B_IN)[:, -1, :]
            outs.append((n_new, m_new[:, 0, 0:H_A], conv_new, shift_new))
        n_all, m_all, conv_all, shift_all = (jnp.stack(s) for s in zip(*outs))
        return x.reshape(nb, seq, D_MODEL), (c_stack, n_all, m_all, conv_all, s_stack, shift_all)

    y_prompt, (p_c, p_n, p_m, p_conv, p_s, p_shift) = run(x_prompt, None)
    y_sample, (s_c, s_n, s_m, s_conv, s_s, s_shift) = run(
        x_sample, (state_mlstm_C, state_mlstm_n, state_mlstm_m, state_mlstm_conv, state_rwkv_S,
                   state_rwkv_shift))
    return (y_prompt, y_sample, p_c, p_n, p_m, p_conv, p_s, p_shift, s_c, s_n, s_m, s_conv, s_s, s_shift)
```

```python
import functools
import math

import jax
import jax.numpy as jnp
from jax import lax
from jax.experimental import pallas as pl
from jax.experimental.pallas import tpu as pltpu

F32 = jnp.float32
BF16 = jnp.bfloat16

D_MODEL = 1024
DEPTH = 4
D_A = 512
D_B = 512
DH_A = 128
H_A = 4
DH_B = 64
H_B = 8
N_PAIRS = H_B // 2
CONV_W = 4
CHUNK = 64
R_DECAY = 64
R_AAA = 64
R_GATE = 128
D_B_IN = 3 * D_B + R_DECAY + R_AAA + R_GATE
N_EXPERTS = 16
N_GROUPS = 4
EXPERTS_PER_GROUP = 4
D_EXP = 512
ALPHA = (2 * DEPTH) ** 0.25
LN_EPS = 1e-5
GN_EPS_A = 1e-6
GN_EPS_B = 64e-5

LANES = 128
SOLVE_BLOCK = 16
VMEM_LIMIT = 48 * 1024 * 1024

NN = (((1,), (0,)), ((), ()))
NT = (((1,), (1,)), ((), ()))
TN = (((0,), (0,)), ((), ()))


def _dot(a, b, dims=NN):
    return lax.dot_general(a.astype(BF16), b.astype(BF16), dims, preferred_element_type=F32)


def _split3(x):
    hi = x.astype(BF16)
    r1 = x - hi.astype(F32)
    mid = r1.astype(BF16)
    lo = (r1 - mid.astype(F32)).astype(BF16)
    return hi, mid, lo


def _dot_exact_lhs(a, x, dims=NN):
    a = a.astype(BF16)
    hi, mid, lo = _split3(x)
    d = lambda p: lax.dot_general(a, p, dims, preferred_element_type=F32)
    return d(hi) + d(mid) + d(lo)


def _dot_exact_rhs(x, b, dims=NN):
    b = b.astype(BF16)
    hi, mid, lo = _split3(x)
    d = lambda p: lax.dot_general(p, b, dims, preferred_element_type=F32)
    return d(hi) + d(mid) + d(lo)


def _layer_norm(x, g, b, eps):
    mu = jnp.mean(x, axis=-1, keepdims=True)
    xc = x - mu
    var = jnp.mean(xc * xc, axis=-1, keepdims=True)
    return xc * lax.rsqrt(var + eps) * g + b


def _sigmoid(x):
    return 1.0 / (1.0 + jnp.exp(-x))


def _log_sigmoid(x):
    return jnp.minimum(x, 0.0) - jnp.log(1.0 + jnp.exp(-jnp.abs(x)))


def _silu(x):
    return x * _sigmoid(x)


def _ln_kernel(x_ref, g_ref, b_ref, o_ref):
    o_ref[...] = _layer_norm(x_ref[...], g_ref[...], b_ref[...], LN_EPS)


def _ln_call(x, g, b, tm):
    n, d = x.shape
    row = pl.BlockSpec((tm, d), lambda i: (i, 0))
    vec = pl.BlockSpec((1, d), lambda i: (0, 0))
    return pl.pallas_call(
        _ln_kernel,
        grid=(n // tm,),
        in_specs=[row, vec, vec],
        out_specs=row,
        out_shape=jax.ShapeDtypeStruct((n, d), F32),
        compiler_params=pltpu.CompilerParams(dimension_semantics=("parallel",)),
        name="ln0",
    )(x, g, b)


W_A_COLS = 4 * D_A
W_IF_COLS = 2 * LANES
W_IN_COLS = W_A_COLS + D_B_IN + W_IF_COLS


def _inproj_kernel(x_ref, w_ref, pa_ref, pb_ref, pif_ref):
    xb = x_ref[...].astype(BF16)
    pa_ref[...] = jnp.dot(xb, w_ref[0, :, 0:W_A_COLS], preferred_element_type=F32)
    pb_ref[...] = jnp.dot(xb, w_ref[0, :, W_A_COLS:W_A_COLS + D_B_IN], preferred_element_type=F32)
    pif_ref[...] = jnp.dot(xb, w_ref[0, :, W_A_COLS + D_B_IN:W_IN_COLS], preferred_element_type=F32)


def _inproj_call(x, w, tm, layer):
    n = x.shape[0]
    return pl.pallas_call(
        _inproj_kernel,
        grid=(n // tm,),
        in_specs=[pl.BlockSpec((tm, D_MODEL), lambda i: (i, 0)),
                  pl.BlockSpec((1, D_MODEL, W_IN_COLS), lambda i: (layer, 0, 0))],
        out_specs=[pl.BlockSpec((tm, W_A_COLS), lambda i: (i, 0)),
                   pl.BlockSpec((tm, D_B_IN), lambda i: (i, 0)),
                   pl.BlockSpec((tm, W_IF_COLS), lambda i: (i, 0))],
        out_shape=[jax.ShapeDtypeStruct((n, W_A_COLS), F32),
                   jax.ShapeDtypeStruct((n, D_B_IN), F32),
                   jax.ShapeDtypeStruct((n, W_IF_COLS), F32)],
        compiler_params=pltpu.CompilerParams(dimension_semantics=("parallel",),
                                             vmem_limit_bytes=VMEM_LIMIT),
        name="inproj",
    )(x, w)


CONV_PAD = 8


def _mlstm_kernel(pa_ref, pif_ref, conv0_ref, c0_ref, n0_ref, m0_ref, convw_ref, convb_ref,
                  bif_ref, gn_ref, *rest, c, nc, nbb, chained):
    h_ref, c_out_ref, n_out_ref, m_out_ref, qk_scr, c_scr, n_scr, m_scr = rest[1:] if chained else rest
    ci = pl.program_id(1)
    rows = nbb * c
    cshift = int(math.log2(c))
    chains = [(nb, h) for nb in range(nbb) for h in range(H_A)]
    prev0 = CONV_PAD - (CONV_W - 1)

    @pl.when(ci == 0)
    def _():
        c_scr[...] = c0_ref[0]
        n_scr[...] = n0_ref[...]
        m_scr[...] = m0_ref[...]
        for nb in range(nbb):
            qk_scr[nb, prev0:CONV_PAD, :] = conv0_ref[nb]

    cat0 = lambda xs: xs[0] if len(xs) == 1 else jnp.concatenate(xs, axis=0)
    acc_l = []
    for nb in range(nbb):
        u = pa_ref[nb, :, 0:2 * D_A]
        qk_scr[nb, CONV_PAD:CONV_PAD + c, :] = u
        acc = convb_ref[...]
        for j in range(CONV_W):
            acc = acc + qk_scr[nb, prev0 + j:prev0 + j + c, :] * convw_ref[j:j + 1, :]
        qk_scr[nb, prev0:CONV_PAD, :] = u[c - (CONV_W - 1):c, :]
        acc_l.append(acc)
    qk = _silu(cat0(acc_l))

    gates = cat0([pif_ref[nb] for nb in range(nbb)]) + bif_ref[...]
    li_all = gates[:, 0:LANES]
    lf_all = _log_sigmoid(gates[:, LANES:2 * LANES])
    row = lax.broadcasted_iota(jnp.int32, (rows, rows), 0)
    col = lax.broadcasted_iota(jnp.int32, (rows, rows), 1)
    tril = jnp.where((col <= row) & ((row >> cshift) == (col >> cshift)), 1.0, 0.0)
    b_all = _dot_exact_lhs(tril, lf_all)
    z_all = li_all - b_all
    crow = lax.broadcasted_iota(jnp.int32, (c, c), 0)
    ccol = lax.broadcasted_iota(jnp.int32, (c, c), 1)
    causal = ccol <= crow
    hrow = lax.broadcasted_iota(jnp.int32, (H_A * c, LANES), 0)
    hlane = lax.broadcasted_iota(jnp.int32, (H_A * c, LANES), 1)
    head_pick = jnp.where((hrow >> cshift) == hlane, 1.0, 0.0)
    z_rows = [_dot_exact_lhs(head_pick, z_all[nb * c:(nb + 1) * c], NT) for nb in range(nbb)]
    lane1 = lax.broadcasted_iota(jnp.int32, (1, LANES), 1)

    rs = lambda nb: slice(nb * c, (nb + 1) * c)
    q_l = [qk[rs(nb), h * DH_A:(h + 1) * DH_A] for nb, h in chains]
    k_l = [qk[rs(nb), D_A + h * DH_A:D_A + (h + 1) * DH_A] * (DH_A ** -0.5) for nb, h in chains]
    v_l = [pa_ref[nb, :, 2 * D_A + h * DH_A:2 * D_A + (h + 1) * DH_A] for nb, h in chains]
    c_l = [c_scr[nb, h] for nb, h in chains]
    n_l = [n_scr[nb, h:h + 1, :] for nb, h in chains]
    b_col = [b_all[rs(nb), h:h + 1] for nb, h in chains]
    li_col = [li_all[rs(nb), h:h + 1] for nb, h in chains]
    m_prev = [m_scr[nb][:, h:h + 1] for nb, h in chains]
    dmat = [jnp.where(causal, bc + z_rows[nb][h * c:(h + 1) * c], -jnp.inf)
            for bc, (nb, h) in zip(b_col, chains)]
    m_inter = [bc + mp for bc, mp in zip(b_col, m_prev)]
    m_t = [jnp.maximum(mi, jnp.max(d, axis=-1, keepdims=True)) for mi, d in zip(m_inter, dmat)]
    qk_dot = [_dot(q, k, NT) for q, k in zip(q_l, k_l)]
    qc = [_dot(q, cm) for q, cm in zip(q_l, c_l)]
    s_l = [x * jnp.exp(d - mt) for x, d, mt in zip(qk_dot, dmat, m_t)]
    sv = [_dot(s, v) for s, v in zip(s_l, v_l)]
    b_last = [bc[c - 1:c, :] for bc in b_col]
    g_s = [bl - bc + li for bl, bc, li in zip(b_last, b_col, li_col)]
    m_new = [jnp.maximum(bl + mp, jnp.max(gs, axis=0, keepdims=True))
             for bl, mp, gs in zip(b_last, m_prev, g_s)]
    wk = [jnp.exp(gs - mn) * k for gs, mn, k in zip(g_s, m_new, k_l)]
    w_old = [jnp.exp(bl + mp - mn) for bl, mp, mn in zip(b_last, m_prev, m_new)]
    kv = [_dot(w, v, TN) for w, v in zip(wk, v_l)]
    for i, (nb, h) in enumerate(chains):
        c_scr[nb, h] = w_old[i] * c_l[i] + kv[i]
        n_scr[nb, h:h + 1, :] = w_old[i] * n_l[i] + jnp.sum(wk[i], axis=0, keepdims=True)
    for nb in range(nbb):
        m_row = m_scr[nb]
        for h in range(H_A):
            m_row = jnp.where(lane1 == h, m_new[nb * H_A + h], m_row)
        m_scr[nb] = m_row
    w_inter = [jnp.exp(mi - mt) for mi, mt in zip(m_inter, m_t)]
    qn = [jnp.sum(q * nv, axis=-1, keepdims=True) for q, nv in zip(q_l, n_l)]
    s_sum = [jnp.sum(s, axis=-1, keepdims=True) for s in s_l]
    den = [w * a + b for w, a, b in zip(w_inter, qn, s_sum)]
    hh = [(w * a + b) / jnp.maximum(jnp.abs(d), jnp.exp(-mt))
          for w, a, b, d, mt in zip(w_inter, qc, sv, den, m_t)]
    mu = [jnp.mean(x, axis=-1, keepdims=True) for x in hh]
    hc = [x - m for x, m in zip(hh, mu)]
    var = [jnp.mean(x * x, axis=-1, keepdims=True) for x in hc]
    for i, (nb, h) in enumerate(chains):
        sl = slice(h * DH_A, (h + 1) * DH_A)
        hn = hc[i] * lax.rsqrt(var[i] + GN_EPS_A)
        o_pre = pa_ref[nb, :, 3 * D_A + h * DH_A:3 * D_A + (h + 1) * DH_A]
        h_ref[nb, :, sl] = hn * gn_ref[:, sl] * _sigmoid(o_pre)

    @pl.when(ci == nc - 1)
    def _():
        c_out_ref[0] = c_scr[...]
        n_out_ref[...] = n_scr[...]
        m_out_ref[...] = m_scr[...]


def _mlstm_call(pa, pif, conv0, c_in, n0, m0, convw, convb, bif, gn, c_stack, *, batch, seq, nbb,
                layer_in, layer):
    c = math.gcd(seq, CHUNK)
    nc = seq // c
    tok = lambda w: pl.BlockSpec((nbb, c, w), lambda b, i: (b, i, 0))
    vec = lambda r, w: pl.BlockSpec((r, w), lambda b, i: (0, 0))
    st5 = lambda l: pl.BlockSpec((1, nbb, H_A, DH_A, DH_A), lambda b, i: (l, b, 0, 0, 0))
    st3 = lambda r, w: pl.BlockSpec((nbb, r, w), lambda b, i: (b, 0, 0))
    chained = c_stack is not None
    in_specs = [tok(W_A_COLS), tok(W_IF_COLS), st3(CONV_W - 1, 2 * D_A), st5(layer_in), st3(H_A, DH_A),
                st3(1, LANES), vec(CONV_W, 2 * D_A), vec(1, 2 * D_A), vec(1, W_IF_COLS), vec(1, D_A)]
    args = [pa, pif, conv0, c_in, n0, m0, convw, convb, bif, gn]
    if chained:
        in_specs.append(pl.BlockSpec(memory_space=pl.ANY))
        args.append(c_stack)
    return pl.pallas_call(
        functools.partial(_mlstm_kernel, c=c, nc=nc, nbb=nbb, chained=chained),
        grid=(batch // nbb, nc),
        in_specs=in_specs,
        out_specs=[tok(D_A), st5(layer), st3(H_A, DH_A), st3(1, LANES)],
        out_shape=[jax.ShapeDtypeStruct((batch, seq, D_A), F32),
                   jax.ShapeDtypeStruct((DEPTH, batch, H_A, DH_A, DH_A), F32),
                   jax.ShapeDtypeStruct((batch, H_A, DH_A), F32),
                   jax.ShapeDtypeStruct((batch, 1, LANES), F32)],
        scratch_shapes=[pltpu.VMEM((nbb, CONV_PAD + c, 2 * D_A), F32),
                        pltpu.VMEM((nbb, H_A, DH_A, DH_A), F32),
                        pltpu.VMEM((nbb, H_A, DH_A), F32),
                        pltpu.VMEM((nbb, 1, LANES), F32)],
        input_output_aliases={len(args) - 1: 1} if chained else {},
        compiler_params=pltpu.CompilerParams(dimension_semantics=("parallel", "arbitrary"),
                                             vmem_limit_bytes=VMEM_LIMIT),
        name="mlstm",
    )(*args)


SHIFT_PAD = 8


def _neumann_inverses(ms, size, t):
    row = lax.broadcasted_iota(jnp.int32, (size, size), 0)
    col = lax.broadcasted_iota(jnp.int32, (size, size), 1)
    eye = jnp.where(row == col, 1.0, 0.0)
    blk = min(SOLVE_BLOCK, t)
    shift = int(math.log2(blk))
    same = (row >> shift) == (col >> shift)
    ds = [jnp.where(same, m, 0.0) for m in ms]
    ps = [eye + d for d in ds]
    xs = ds
    for _ in range(shift - 1):
        xs = [_dot(x, x) for x in xs]
        ps = [p + _dot(p, x) for p, x in zip(ps, xs)]
    nblk = t // blk
    if nblk == 1:
        return ps, None
    ns = [_dot(p, m - d) for p, m, d in zip(ps, ms, ds)]
    qs = [eye + n for n in ns]
    ys = ns
    for _ in range(int(math.log2(nblk)) - 1):
        ys = [_dot(y, y) for y in ys]
        qs = [q + _dot(q, y) for q, y in zip(qs, ys)]
    return ps, qs


def _rwkv_kernel(pb_ref, shift0_ref, s0_ref, mu_ref, w0a0_ref, wlora_ref, g2_ref, kk_ref, ka_ref,
                 rk_ref, lng_ref, lnb_ref, *rest, t, nc, nbb, chained):
    h_ref, s_out_ref, sh_scr, s_scr = rest[1:] if chained else rest
    ci = pl.program_id(1)
    rows = nbb * t
    t2 = 2 * t
    groups = [(nb, pr) for nb in range(nbb) for pr in range(N_PAIRS)]

    brow = lax.broadcasted_iota(jnp.int32, (LANES, LANES), 0)
    bcol = lax.broadcasted_iota(jnp.int32, (LANES, LANES), 1)
    blockdiag = (brow >= DH_B) == (bcol >= DH_B)
    bd2 = jnp.where(blockdiag, 1.0, 0.0).astype(BF16)

    @pl.when(ci == 0)
    def _():
        er = lax.broadcasted_iota(jnp.int32, (DH_B, LANES), 0)
        ec = lax.broadcasted_iota(jnp.int32, (DH_B, LANES), 1)
        dup_cols = jnp.where((ec & (DH_B - 1)) == er, 1.0, 0.0)
        for nb in range(nbb):
            sh_scr[nb, SHIFT_PAD - 1:SHIFT_PAD, :] = shift0_ref[nb]
            for pr in range(N_PAIRS):
                x = s0_ref[0, nb, 2 * pr:2 * pr + 2].reshape(LANES, DH_B)
                s_scr[nb * N_PAIRS + pr] = jnp.where(blockdiag, _dot_exact_rhs(x, dup_cols), 0.0)

    p_l, prev_l = [], []
    for nb in range(nbb):
        p_nb = pb_ref[nb]
        sh_scr[nb, SHIFT_PAD:SHIFT_PAD + t, :] = p_nb
        prev_l.append(sh_scr[nb, SHIFT_PAD - 1:SHIFT_PAD - 1 + t, :])
        sh_scr[nb, SHIFT_PAD - 1:SHIFT_PAD, :] = p_nb[t - 1:t, :]
        p_l.append(p_nb)
    cat0 = lambda xs: xs[0] if len(xs) == 1 else jnp.concatenate(xs, axis=0)
    p = cat0(p_l)
    prev = cat0(prev_l)
    pb = p + (prev - p) * mu_ref[...]

    r = pb[:, 0:D_B]
    kr = pb[:, D_B:2 * D_B]
    vr = pb[:, 2 * D_B:3 * D_B]
    lora_in = pb[:, 3 * D_B:3 * D_B + LANES]
    gl = pb[:, 3 * D_B + LANES:3 * D_B + 2 * LANES]
    lane = lax.broadcasted_iota(jnp.int32, (rows, LANES), 1)
    lora_act = jnp.where(lane < R_DECAY, jnp.tanh(lora_in), lora_in)
    z = _dot(lora_act, wlora_ref[...]) + w0a0_ref[...]
    ld = -jnp.exp(_log_sigmoid(z[:, 0:D_B]) - 0.5)
    a = _sigmoid(z[:, D_B:2 * D_B])
    g = _dot(_sigmoid(gl), g2_ref[...])

    def seg_sum(x):
        xr = jnp.concatenate([x[:, q * LANES:(q + 1) * LANES] for q in range(N_PAIRS)], axis=0)
        hi = xr.astype(BF16)
        lo = (xr - hi.astype(F32)).astype(BF16)
        s = (jnp.dot(hi, bd2, preferred_element_type=F32) + jnp.dot(lo, bd2, preferred_element_type=F32))
        return jnp.concatenate([s[q * rows:(q + 1) * rows] for q in range(N_PAIRS)], axis=1)

    kk_raw = kr * kk_ref[...]
    kk = kk_raw / jnp.maximum(jnp.sqrt(seg_sum(kk_raw * kk_raw)), 1e-12)
    k2 = kr * (1.0 + (a - 1.0) * ka_ref[...])
    bonus = seg_sum(r * k2 * rk_ref[...]) * vr

    row = lax.broadcasted_iota(jnp.int32, (rows, rows), 0)
    col = lax.broadcasted_iota(jnp.int32, (rows, rows), 1)
    tshift = int(math.log2(t))
    tril = jnp.where((col <= row) & ((row >> tshift) == (col >> tshift)), 1.0, 0.0)
    lc = _dot_exact_lhs(tril, ld)
    lends = [lc[(nb + 1) * t - 1:(nb + 1) * t, :] for nb in range(nbb)]
    lend_rows = cat0([jnp.broadcast_to(le, (t, D_B)) for le in lends])
    e_nc = jnp.exp(-lc)
    e_end = jnp.exp(lend_rows - lc)
    b_raw = kk * a
    at = -kk * jnp.exp(lc - ld)
    rt = r * jnp.exp(lc)
    kt = k2 * e_nc
    bt = b_raw * e_nc
    kend = k2 * e_end
    bend = b_raw * e_end

    srow = lax.broadcasted_iota(jnp.int32, (t2, t2), 0)
    scol = lax.broadcasted_iota(jnp.int32, (t2, t2), 1)
    strict = scol < srow
    incl = scol <= srow
    lane2 = lax.broadcasted_iota(jnp.int32, (t2, LANES), 1)
    row2 = lax.broadcasted_iota(jnp.int32, (t2, LANES), 0)
    head_sel = (lane2 >= DH_B) == (row2 >= t)
    first_head = lax.broadcasted_iota(jnp.int32, (t, LANES), 1) < DH_B

    def blk(x, nb, pr):
        return x[nb * t:(nb + 1) * t, pr * LANES:(pr + 1) * LANES]

    def stack(x):
        return jnp.where(head_sel, jnp.concatenate([x, x], axis=0), 0.0)

    lhs = [jnp.concatenate([stack(blk(at, *gp)), stack(blk(rt, *gp))], axis=0) for gp in groups]
    rk_s = [stack(blk(kt, *gp)) for gp in groups]
    rb_s = [stack(blk(bt, *gp)) for gp in groups]
    v_p = [blk(vr, *gp) for gp in groups]
    v_s = [jnp.concatenate([v, v], axis=0) for v in v_p]
    s_p = [s_scr[i] for i in range(len(groups))]
    gk = [_dot(l, x, NT) for l, x in zip(lhs, rk_s)]
    gb = [_dot(l, x, NT) for l, x in zip(lhs, rb_s)]
    xs = [_dot(l, s, NT) for l, s in zip(lhs, s_p)]
    m_ab = [jnp.where(strict, x[0:t2], 0.0) for x in gb]
    ps, qs = _neumann_inverses(m_ab, t2, t)
    rhs = [x[0:t2] + _dot(jnp.where(strict, y[0:t2], 0.0), v) for x, y, v in zip(xs, gk, v_s)]
    sa_s = [_dot(pm, x) for pm, x in zip(ps, rhs)]
    if qs is not None:
        sa_s = [_dot(qm, x) for qm, x in zip(qs, sa_s)]
    o_s = [x[t2:] + _dot(jnp.where(incl, y[t2:], 0.0), v) + _dot(jnp.where(incl, w[t2:], 0.0), sa)
           for x, y, w, v, sa in zip(xs, gk, gb, v_s, sa_s)]
    o_p = [jnp.where(first_head, o[0:t], o[t:t2]) for o in o_s]
    sa_p = [jnp.where(first_head, sa[0:t], sa[t:t2]) for sa in sa_s]
    upd = [_dot(jnp.concatenate([v, sa], axis=0),
                jnp.concatenate([blk(kend, *gp), blk(bend, *gp)], axis=0), TN)
           for v, sa, gp in zip(v_p, sa_p, groups)]
    for i, (nb, pr) in enumerate(groups):
        dec = jnp.exp(lends[nb][:, pr * LANES:(pr + 1) * LANES])
        s_scr[i] = s_p[i] * dec + jnp.where(blockdiag, upd[i], 0.0)

    o = cat0([jnp.concatenate(o_p[nb * N_PAIRS:(nb + 1) * N_PAIRS], axis=1) for nb in range(nbb)])
    mean = seg_sum(o) * (1.0 / DH_B)
    oc = o - mean
    var = seg_sum(oc * oc) * (1.0 / DH_B)
    out = oc * lax.rsqrt(var + GN_EPS_B) * lng_ref[...] + lnb_ref[...]
    res = (out + bonus) * g
    for nb in range(nbb):
        h_ref[nb] = res[nb * t:(nb + 1) * t, :]

    @pl.when(ci == nc - 1)
    def _():
        fr = lax.broadcasted_iota(jnp.int32, (LANES, DH_B), 0)
        fc = lax.broadcasted_iota(jnp.int32, (LANES, DH_B), 1)
        dup_rows = jnp.where((fr & (DH_B - 1)) == fc, 1.0, 0.0)
        for i, (nb, pr) in enumerate(groups):
            packed = _dot_exact_rhs(s_scr[i], dup_rows)
            s_out_ref[0, nb, 2 * pr:2 * pr + 2] = packed.reshape(2, DH_B, DH_B)


def _rwkv_call(pb, shift0, s_in, mu, w0a0, wlora, g2, kk, ka, rk, lng, lnb, s_stack, *, batch, seq, nbb,
               layer_in, layer):
    t = math.gcd(seq, CHUNK)
    nc = seq // t
    tok = lambda w: pl.BlockSpec((nbb, t, w), lambda b, i: (b, i, 0))
    vec = lambda r, w: pl.BlockSpec((r, w), lambda b, i: (0, 0))
    st5 = lambda l: pl.BlockSpec((1, nbb, H_B, DH_B, DH_B), lambda b, i: (l, b, 0, 0, 0))
    chained = s_stack is not None
    in_specs = [tok(D_B_IN), pl.BlockSpec((nbb, 1, D_B_IN), lambda b, i: (b, 0, 0)), st5(layer_in),
                vec(1, D_B_IN), vec(1, 2 * D_B), vec(LANES, 2 * D_B), vec(R_GATE, D_B),
                vec(1, D_B), vec(1, D_B), vec(1, D_B), vec(1, D_B), vec(1, D_B)]
    args = [pb, shift0, s_in, mu, w0a0, wlora, g2, kk, ka, rk, lng, lnb]
    if chained:
        in_specs.append(pl.BlockSpec(memory_space=pl.ANY))
        args.append(s_stack)
    return pl.pallas_call(
        functools.partial(_rwkv_kernel, t=t, nc=nc, nbb=nbb, chained=chained),
        grid=(batch // nbb, nc),
        in_specs=in_specs,
        out_specs=[tok(D_B), st5(layer)],
        out_shape=[jax.ShapeDtypeStruct((batch, seq, D_B), F32),
                   jax.ShapeDtypeStruct((DEPTH, batch, H_B, DH_B, DH_B), F32)],
        scratch_shapes=[pltpu.VMEM((nbb, SHIFT_PAD + t, D_B_IN), F32),
                        pltpu.VMEM((nbb * N_PAIRS, LANES, LANES), F32)],
        input_output_aliases={len(args) - 1: 1} if chained else {},
        compiler_params=pltpu.CompilerParams(dimension_semantics=("parallel", "arbitrary"),
                                             vmem_limit_bytes=VMEM_LIMIT),
        name="rwkv",
    )(*args)


ROUTE_ROWS = 8


def _outproj_kernel(x_ref, ha_ref, hb_ref, wo_ref, g_ref, b_ref, wr_ref, br_ref, x1_ref, route_ref,
                    cnt_ref, *, tm):
    y = _dot(ha_ref[...], wo_ref[0, 0:D_A, :]) + _dot(hb_ref[...], wo_ref[0, D_A:D_A + D_B, :])
    x1 = _layer_norm(ALPHA * x_ref[...] + y, g_ref[...], b_ref[...], LN_EPS)
    x1_ref[...] = x1
    logits = lax.dot_general(wr_ref[...], x1, NT, precision=lax.Precision.HIGHEST,
                             preferred_element_type=F32)
    mx = jnp.max(logits, axis=0, keepdims=True)
    ex = jnp.exp(logits - mx)
    probs = ex / jnp.sum(ex, axis=0, keepdims=True)
    sel = probs + br_ref[...]
    neg = -jnp.inf

    def top2(rows):
        m1 = functools.reduce(jnp.maximum, rows)
        i1 = jnp.full(m1.shape, len(rows) - 1, jnp.int32)
        for j in range(len(rows) - 2, -1, -1):
            i1 = jnp.where(rows[j] == m1, j, i1)
        rest = [jnp.where(i1 == j, neg, rows[j]) for j in range(len(rows))]
        m2 = functools.reduce(jnp.maximum, rest)
        i2 = jnp.full(m2.shape, len(rows) - 1, jnp.int32)
        for j in range(len(rows) - 2, -1, -1):
            i2 = jnp.where(rest[j] == m2, j, i2)
        return m1, i1, m2, i2

    g_score, g_i1, g_i2 = [], [], []
    for gidx in range(N_GROUPS):
        rows = [sel[gidx * EXPERTS_PER_GROUP + j:gidx * EXPERTS_PER_GROUP + j + 1, :]
                for j in range(EXPERTS_PER_GROUP)]
        m1, i1, m2, i2 = top2(rows)
        g_score.append(m1 + m2)
        g_i1.append(i1)
        g_i2.append(i2)
    best = functools.reduce(jnp.maximum, g_score)
    grp = jnp.full(best.shape, N_GROUPS - 1, jnp.int32)
    for gidx in range(N_GROUPS - 2, -1, -1):
        grp = jnp.where(g_score[gidx] == best, gidx, grp)
    l1 = g_i1[N_GROUPS - 1]
    l2 = g_i2[N_GROUPS - 1]
    for gidx in range(N_GROUPS - 2, -1, -1):
        l1 = jnp.where(grp == gidx, g_i1[gidx], l1)
        l2 = jnp.where(grp == gidx, g_i2[gidx], l2)
    e1 = grp * EXPERTS_PER_GROUP + l1
    e2 = grp * EXPERTS_PER_GROUP + l2
    e_iota = lax.broadcasted_iota(jnp.int32, (N_EXPERTS, tm), 0)
    p1 = jnp.sum(jnp.where(e_iota == e1, probs, 0.0), axis=0, keepdims=True)
    p2 = jnp.sum(jnp.where(e_iota == e2, probs, 0.0), axis=0, keepdims=True)
    tot = p1 + p2
    r8 = lax.broadcasted_iota(jnp.int32, (ROUTE_ROWS, tm), 0)
    route_ref[...] = jnp.where(r8 == 0, e1.astype(F32),
                               jnp.where(r8 == 1, e2.astype(F32),
                                         jnp.where(r8 == 2, p1 / tot, jnp.where(r8 == 3, p2 / tot, 0.0))))
    picked = jnp.where((e_iota == e1) | (e_iota == e2), 1.0, 0.0)
    cnt_ref[0] = jnp.broadcast_to(jnp.sum(picked, axis=1, keepdims=True), (N_EXPERTS, LANES))


def _outproj_call(x, ha, hb, wo, g, b, wr_t, br, tm, layer):
    n = x.shape[0]
    row = lambda w: pl.BlockSpec((tm, w), lambda i: (i, 0))
    full = lambda r, w: pl.BlockSpec((r, w), lambda i: (0, 0))
    return pl.pallas_call(
        functools.partial(_outproj_kernel, tm=tm),
        grid=(n // tm,),
        in_specs=[row(D_MODEL), row(D_A), row(D_B),
                  pl.BlockSpec((1, D_MODEL, D_MODEL), lambda i: (layer, 0, 0)), full(1, D_MODEL),
                  full(1, D_MODEL), full(N_EXPERTS, D_MODEL), full(N_EXPERTS, 1)],
        out_specs=[row(D_MODEL), pl.BlockSpec((ROUTE_ROWS, tm), lambda i: (0, i)),
                   pl.BlockSpec((1, N_EXPERTS, LANES), lambda i: (i, 0, 0))],
        out_shape=[jax.ShapeDtypeStruct((n, D_MODEL), F32), jax.ShapeDtypeStruct((ROUTE_ROWS, n), F32),
                   jax.ShapeDtypeStruct((n // tm, N_EXPERTS, LANES), F32)],
        compiler_params=pltpu.CompilerParams(dimension_semantics=("parallel",),
                                             vmem_limit_bytes=VMEM_LIMIT),
        name="outproj",
    )(x, ha, hb, wo, g, b, wr_t, br)


MOE_BM = 128
MOE_CH = 512
MOE_CCH = 1024


def _moe_max_blocks(t):
    return -(-2 * t // MOE_BM) + N_EXPERTS - 1


def _moe_sort_kernel(nblk_ref, x_ref, route_ref, padoff_ref, tri_ref, xs_ref, cols_ref, *, t, n_tiles,
                     n_chunks):
    i = pl.program_id(0)

    @pl.when(i >= n_tiles)
    def _():
        xs_ref[...] = jnp.zeros_like(xs_ref)

    @pl.when(i < n_tiles)
    def _():
        s_used = nblk_ref[jnp.minimum(i, n_tiles - 1)] * MOE_BM
        xb = x_ref[...].astype(BF16)
        route = route_ref[...]
        e_iota = lax.broadcasted_iota(jnp.int32, (N_EXPERTS, t), 0)
        a1 = e_iota == route[0:1].astype(jnp.int32)
        a2 = e_iota == route[1:2].astype(jnp.int32)
        picked = jnp.where(a1 | a2, 1.0, 0.0).astype(BF16)
        rank = jnp.dot(picked, tri_ref[...], preferred_element_type=F32)
        base = padoff_ref[0][:, 0:1] + rank
        slot1 = jnp.sum(jnp.where(a1, base, 0.0), axis=0, keepdims=True)
        slot2 = jnp.sum(jnp.where(a2, base, 0.0), axis=0, keepdims=True)
        r8 = lax.broadcasted_iota(jnp.int32, (ROUTE_ROWS, t), 0)
        rows = jnp.where(r8 == 0, slot1, jnp.where(r8 == 1, slot2,
                                                   jnp.where(r8 == 2, route[2:3],
                                                             jnp.where(r8 == 3, route[3:4], 0.0))))
        padded = jnp.concatenate([rows, jnp.zeros((LANES - ROUTE_ROWS, t), F32)], axis=0)
        cols_ref[...] = padded.T
        s1 = slot1.astype(jnp.int32)
        s2 = slot2.astype(jnp.int32)
        for c in range(n_chunks):
            used = c * MOE_CH < s_used

            @pl.when(used)
            def _():
                s_iota = lax.broadcasted_iota(jnp.int32, (MOE_CH, t), 0) + c * MOE_CH
                onehot = jnp.where((s_iota == s1) | (s_iota == s2), 1.0, 0.0).astype(BF16)
                xs_ref[0, c * MOE_CH:(c + 1) * MOE_CH, :] = jnp.dot(
                    onehot, xb, preferred_element_type=F32).astype(BF16)

            @pl.when(jnp.logical_not(used))
            def _():
                xs_ref[0, c * MOE_CH:(c + 1) * MOE_CH, :] = jnp.zeros((MOE_CH, D_MODEL), BF16)


def _moe_expert_kernel(se_ref, sblk_ref, nsteps_ref, xs_ref, wg_ref, wu_ref, wd_ref, ys_ref,
                       wg_scr, wu_scr, wd_scr):
    s = pl.program_id(0)
    prev = se_ref[jnp.maximum(s - 1, 0)]

    @pl.when((s == 0) | (se_ref[s] != prev))
    def _():
        wg_scr[...] = wg_ref[0, 0].astype(BF16)
        wu_scr[...] = wu_ref[0, 0].astype(BF16)
        wd_scr[...] = wd_ref[0, 0].astype(BF16)

    @pl.when(s < nsteps_ref[0])
    def _():
        xblk = xs_ref[...]
        hid = _silu(jnp.dot(xblk, wg_scr[...], preferred_element_type=F32)) * jnp.dot(
            xblk, wu_scr[...], preferred_element_type=F32)
        y = jnp.dot(hid.astype(BF16), wd_scr[...], preferred_element_type=F32)
        ys_ref[...] = y.astype(BF16)

    @pl.when(s >= nsteps_ref[0])
    def _():
        ys_ref[...] = xs_ref[...]


def _moe_combine_kernel(nblk_ref, x_ref, cols_ref, ys_ref, g_ref, b_ref, o_ref, acc_scr, *, t, n_chunks):
    s_used = nblk_ref[pl.program_id(0)] * MOE_BM
    cols = cols_ref[...]
    s1c = cols[:, 0:1].astype(jnp.int32)
    s2c = cols[:, 1:2].astype(jnp.int32)
    w1c = cols[:, 2:3]
    w2c = cols[:, 3:4]
    for c in range(n_chunks):
        @pl.when(c * MOE_CCH < s_used)
        def _():
            l_iota = lax.broadcasted_iota(jnp.int32, (t, MOE_CCH), 1) + c * MOE_CCH
            weighted = (jnp.where(l_iota == s1c, w1c, 0.0) + jnp.where(l_iota == s2c, w2c, 0.0)).astype(BF16)
            part = jnp.dot(weighted, ys_ref[0, c * MOE_CCH:(c + 1) * MOE_CCH, :],
                           preferred_element_type=F32)
            if c == 0:
                acc_scr[...] = part
            else:
                acc_scr[...] += part
    o_ref[...] = _layer_norm(ALPHA * x_ref[...] + acc_scr[...], g_ref[...], b_ref[...], LN_EPS)


def _moe_schedule(counts, tiles_per, maxb, bpt):
    cnt = counts[:, :, 0].astype(jnp.int32)
    cnt = cnt.reshape(-1, tiles_per, N_EXPERTS).sum(axis=1)
    n_tiles = cnt.shape[0]
    nblk = (cnt + MOE_BM - 1) // MOE_BM
    end = jnp.cumsum(nblk, axis=-1)
    start = end - nblk
    total = end[:, -1]
    padoff = jnp.broadcast_to((start * MOE_BM).astype(F32)[:, :, None], (n_tiles, N_EXPERTS, LANES))
    per_e = jnp.sum(nblk, axis=0)
    e_end = jnp.cumsum(per_e)
    e_base = e_end - per_e
    e_ids = jnp.arange(N_EXPERTS, dtype=jnp.int32)
    s_idx = jnp.arange(n_tiles * maxb, dtype=jnp.int32)
    last_e = jnp.max(jnp.where(per_e > 0, e_ids, 0))
    se = jnp.minimum(jnp.sum((s_idx[:, None] >= e_end[None, :]).astype(jnp.int32), axis=-1), last_e)
    oh_e = se[:, None] == e_ids[None, :]
    r = s_idx - jnp.sum(jnp.where(oh_e, e_base[None, :], 0), axis=-1)
    of_e = lambda a: jnp.sum(jnp.where(oh_e[:, None, :], a[None, :, :], 0), axis=-1)
    incl_s = of_e(jnp.cumsum(nblk, axis=0))
    tile = jnp.minimum(jnp.sum((incl_s <= r[:, None]).astype(jnp.int32), axis=-1), n_tiles - 1)
    oh_t = tile[:, None] == jnp.arange(n_tiles, dtype=jnp.int32)[None, :]
    of_t = lambda a: jnp.sum(jnp.where(oh_t, a, 0), axis=-1)
    j = of_t(of_e(start)) + r - (of_t(incl_s) - of_t(of_e(nblk)))
    sblk = jnp.where(s_idx < e_end[-1], tile * bpt + j, n_tiles * bpt)
    return total, padoff, se, sblk, e_end[-1:]


def _moe_call(x, route, counts, wg, wu, wd, g, b, *, layer, tm):
    n = x.shape[0]
    t = 1024 if n % 1024 == 0 else n
    n_tiles = n // t
    maxb = _moe_max_blocks(t)
    s_alloc = -(-maxb * MOE_BM // MOE_CCH) * MOE_CCH
    n_chunks = s_alloc // MOE_CH
    bpt = s_alloc // MOE_BM
    nblk, padoff, se, sblk, nsteps = _moe_schedule(counts, t // tm, maxb, bpt)
    tri = jnp.triu(jnp.ones((t, t), BF16), k=1)
    last = n_tiles - 1
    xs, cols = pl.pallas_call(
        functools.partial(_moe_sort_kernel, t=t, n_tiles=n_tiles, n_chunks=n_chunks),
        grid_spec=pltpu.PrefetchScalarGridSpec(
            num_scalar_prefetch=1,
            grid=(n_tiles + 1,),
            in_specs=[pl.BlockSpec((t, D_MODEL), lambda i, nb: (jnp.minimum(i, last), 0)),
                      pl.BlockSpec((ROUTE_ROWS, t), lambda i, nb: (0, jnp.minimum(i, last))),
                      pl.BlockSpec((1, N_EXPERTS, LANES), lambda i, nb: (jnp.minimum(i, last), 0, 0)),
                      pl.BlockSpec((t, t), lambda i, nb: (0, 0))],
            out_specs=[pl.BlockSpec((1, s_alloc, D_MODEL), lambda i, nb: (i, 0, 0)),
                       pl.BlockSpec((t, LANES), lambda i, nb: (jnp.minimum(i, last), 0))]),
        out_shape=[jax.ShapeDtypeStruct((n_tiles + 1, s_alloc, D_MODEL), BF16),
                   jax.ShapeDtypeStruct((n, LANES), F32)],
        compiler_params=pltpu.CompilerParams(dimension_semantics=("arbitrary",),
                                             vmem_limit_bytes=VMEM_LIMIT),
        name="moe_sort",
    )(nblk, x, route, padoff, tri)
    wspec = lambda r, c: pl.BlockSpec((1, 1, r, c), lambda s, se_, sb, ns: (layer, se_[s], 0, 0))
    blk = pl.BlockSpec((MOE_BM, D_MODEL), lambda s, se_, sb, ns: (sb[s], 0))
    ys = pl.pallas_call(
        _moe_expert_kernel,
        grid_spec=pltpu.PrefetchScalarGridSpec(
            num_scalar_prefetch=3,
            grid=(n_tiles * maxb,),
            in_specs=[blk, wspec(D_MODEL, D_EXP), wspec(D_MODEL, D_EXP), wspec(D_EXP, D_MODEL)],
            out_specs=blk,
            scratch_shapes=[pltpu.VMEM((D_MODEL, D_EXP), BF16), pltpu.VMEM((D_MODEL, D_EXP), BF16),
                            pltpu.VMEM((D_EXP, D_MODEL), BF16)]),
        out_shape=jax.ShapeDtypeStruct(((n_tiles + 1) * s_alloc, D_MODEL), BF16),
        input_output_aliases={3: 0},
        compiler_params=pltpu.CompilerParams(dimension_semantics=("arbitrary",),
                                             vmem_limit_bytes=VMEM_LIMIT),
        name="moe_experts",
    )(se, sblk, nsteps, xs.reshape((n_tiles + 1) * s_alloc, D_MODEL), wg, wu, wd)
    vec = pl.BlockSpec((1, D_MODEL), lambda i, nb: (0, 0))
    return pl.pallas_call(
        functools.partial(_moe_combine_kernel, t=t, n_chunks=s_alloc // MOE_CCH),
        grid_spec=pltpu.PrefetchScalarGridSpec(
            num_scalar_prefetch=1,
            grid=(n_tiles,),
            in_specs=[pl.BlockSpec((t, D_MODEL), lambda i, nb: (i, 0)),
                      pl.BlockSpec((t, LANES), lambda i, nb: (i, 0)),
                      pl.BlockSpec((1, s_alloc, D_MODEL), lambda i, nb: (i, 0, 0)), vec, vec],
            out_specs=pl.BlockSpec((t, D_MODEL), lambda i, nb: (i, 0)),
            scratch_shapes=[pltpu.VMEM((t, D_MODEL), F32)]),
        out_shape=jax.ShapeDtypeStruct((n, D_MODEL), F32),
        compiler_params=pltpu.CompilerParams(dimension_semantics=("parallel",),
                                             vmem_limit_bytes=VMEM_LIMIT),
        name="moe_combine",
    )(nblk, x, cols, ys.reshape(n_tiles + 1, s_alloc, D_MODEL), g, b)


def _row_tile(n):
    return 512 if n % 512 == 0 else n


def _seqs_per_step(batch, want):
    return want if batch % want == 0 else 1


def kernel(x_prompt, x_sample, state_mlstm_C, state_mlstm_n, state_mlstm_m, state_mlstm_conv, state_rwkv_S, state_rwkv_shift, ln0_g, ln0_b, w_in, conv_w, conv_b, b_i, b_f, gn_a_g, mu_shift, w0, w2, a0, a2, g2, k_k, k_a, r_k, lnx_g, lnx_b, w_out, ln1_g, ln1_b, w_router, b_router, we_gate, we_up, we_down, ln2_g, ln2_b):
    d_a_in = 4 * D_A + 2 * H_A
    zpad = jnp.zeros((DEPTH, D_MODEL, LANES - H_A), F32)
    w_cat = jnp.concatenate(
        [w_in[:, :, 0:4 * D_A], w_in[:, :, d_a_in:], w_in[:, :, 4 * D_A:4 * D_A + H_A], zpad,
         w_in[:, :, 4 * D_A + H_A:d_a_in], zpad], axis=-1).astype(BF16)
    bif = jnp.zeros((DEPTH, 1, W_IF_COLS), F32)
    bif = bif.at[:, 0, 0:H_A].set(b_i).at[:, 0, LANES:LANES + H_A].set(b_f)
    wlora = jnp.zeros((DEPTH, LANES, 2 * D_B), F32)
    wlora = wlora.at[:, 0:R_DECAY, 0:D_B].set(w2).at[:, R_DECAY:, D_B:].set(a2).astype(BF16)
    w0a0 = jnp.concatenate([w0, a0], axis=-1)[:, None, :]
    g2b = g2.astype(BF16)
    wob = w_out.astype(BF16)
    wr_t = w_router.T
    br = b_router[:, None]
    r1 = lambda v: v[None, :]

    def run(x3, states):
        nb, seq, _ = x3.shape
        n = nb * seq
        tm = _row_tile(n)
        rwkv_nbb = _seqs_per_step(nb, 4 if seq >= CHUNK else 8)
        mlstm_nbb = _seqs_per_step(nb, 2 if seq >= CHUNK else 8)
        x = _ln_call(x3.reshape(n, D_MODEL), r1(ln0_g), r1(ln0_b), tm)
        outs = []
        c_stack = jnp.zeros((DEPTH, nb, H_A, DH_A, DH_A), F32)
        s_stack = jnp.zeros((DEPTH, nb, H_B, DH_B, DH_B), F32)
        for l in range(DEPTH):
            if states is None:
                c_in = jnp.zeros((1, nb, H_A, DH_A, DH_A), F32)
                s_in = jnp.zeros((1, nb, H_B, DH_B, DH_B), F32)
                layer_in = 0
                n0 = jnp.zeros((nb, H_A, DH_A), F32)
                m0 = jnp.zeros((nb, H_A), F32)
                conv0 = jnp.zeros((nb, CONV_W - 1, 2 * D_A), F32)
                shift0 = jnp.zeros((nb, D_B_IN), F32)
            else:
                c_in, s_in, layer_in = states[0], states[4], l
                n0, m0, conv0, shift0 = (states[k][l] for k in (1, 2, 3, 5))
            m0p = jnp.zeros((nb, 1, LANES), F32).at[:, 0, 0:H_A].set(m0)
            pa, pb, pif = _inproj_call(x, w_cat, tm, l)
            ha, c_stack, n_new, m_new = _mlstm_call(
                pa.reshape(nb, seq, W_A_COLS), pif.reshape(nb, seq, W_IF_COLS), conv0, c_in, n0, m0p,
                conv_w[l], r1(conv_b[l]), bif[l], r1(gn_a_g[l]), c_stack, batch=nb, seq=seq,
                nbb=mlstm_nbb, layer_in=layer_in, layer=l)
            ha = ha.reshape(n, D_A)
            hb, s_stack = _rwkv_call(
                pb.reshape(nb, seq, D_B_IN), shift0[:, None, :], s_in, r1(mu_shift[l]), w0a0[l], wlora[l],
                g2b[l], r1(k_k[l]), r1(k_a[l]), r1(r_k[l].reshape(D_B)), r1(lnx_g[l]), r1(lnx_b[l]),
                s_stack, batch=nb, seq=seq, nbb=rwkv_nbb, layer_in=layer_in, layer=l)
            hb = hb.reshape(n, D_B)
            x1, route, counts = _outproj_call(x, ha, hb, wob, r1(ln1_g[l]), r1(ln1_b[l]), wr_t, br, tm, l)
            x = _moe_call(x1, route, counts, we_gate, we_up, we_down, r1(ln2_g[l]), r1(ln2_b[l]),
                          layer=l, tm=tm)
            pa3 = pa.reshape(nb, seq, W_A_COLS)
            full = jnp.concatenate([conv0, pa3[:, :, 0:2 * D_A]], axis=1) if seq < CONV_W - 1 else pa3[:, :, 0:2 * D_A]
            conv_new = full[:, -(CONV_W - 1):, :]
            shift_new = pb.reshape(nb, seq, D_B_IN)[:, -1, :]
            outs.append((n_new, m_new[:, 0, 0:H_A], conv_new, shift_new))
        n_all, m_all, conv_all, shift_all = (jnp.stack(s) for s in zip(*outs))
        return x.reshape(nb, seq, D_MODEL), (c_stack, n_all, m_all, conv_all, s_stack, shift_all)

    y_prompt, (p_c, p_n, p_m, p_conv, p_s, p_shift) = run(x_prompt, None)
    y_sample, (s_c, s_n, s_m, s_conv, s_s, s_shift) = run(
        x_sample, (state_mlstm_C, state_mlstm_n, state_mlstm_m, state_mlstm_conv, state_rwkv_S,
                   state_rwkv_shift))
    return (y_prompt, y_sample, p_c, p_n, p_m, p_conv, p_s, p_shift, s_c, s_n, s_m, s_conv, s_s, s_shift)
```

```python
import functools
import math

import jax
import jax.numpy as jnp
from jax import lax
from jax.experimental import pallas as pl
from jax.experimental.pallas import tpu as pltpu

F32 = jnp.float32
BF16 = jnp.bfloat16

D_MODEL = 1024
DEPTH = 4
D_A = 512
D_B = 512
DH_A = 128
H_A = 4
DH_B = 64
H_B = 8
N_PAIRS = H_B // 2
CONV_W = 4
CHUNK = 64
R_DECAY = 64
R_AAA = 64
R_GATE = 128
D_B_IN = 3 * D_B + R_DECAY + R_AAA + R_GATE
N_EXPERTS = 16
N_GROUPS = 4
EXPERTS_PER_GROUP = 4
D_EXP = 512
ALPHA = (2 * DEPTH) ** 0.25
LN_EPS = 1e-5
GN_EPS_A = 1e-6
GN_EPS_B = 64e-5

LANES = 128
SOLVE_BLOCK = 16
VMEM_LIMIT = 48 * 1024 * 1024

NN = (((1,), (0,)), ((), ()))
NT = (((1,), (1,)), ((), ()))
TN = (((0,), (0,)), ((), ()))


def _dot(a, b, dims=NN):
    return lax.dot_general(a.astype(BF16), b.astype(BF16), dims, preferred_element_type=F32)


def _split3(x):
    hi = x.astype(BF16)
    r1 = x - hi.astype(F32)
    mid = r1.astype(BF16)
    lo = (r1 - mid.astype(F32)).astype(BF16)
    return hi, mid, lo


def _dot_exact_lhs(a, x, dims=NN):
    a = a.astype(BF16)
    hi, mid, lo = _split3(x)
    d = lambda p: lax.dot_general(a, p, dims, preferred_element_type=F32)
    return d(hi) + d(mid) + d(lo)


def _dot_exact_rhs(x, b, dims=NN):
    b = b.astype(BF16)
    hi, mid, lo = _split3(x)
    d = lambda p: lax.dot_general(p, b, dims, preferred_element_type=F32)
    return d(hi) + d(mid) + d(lo)


def _layer_norm(x, g, b, eps):
    mu = jnp.mean(x, axis=-1, keepdims=True)
    xc = x - mu
    var = jnp.mean(xc * xc, axis=-1, keepdims=True)
    return xc * lax.rsqrt(var + eps) * g + b


def _sigmoid(x):
    return 1.0 / (1.0 + jnp.exp(-x))


def _log_sigmoid(x):
    return jnp.minimum(x, 0.0) - jnp.log(1.0 + jnp.exp(-jnp.abs(x)))


def _silu(x):
    return x * _sigmoid(x)


def _ln_kernel(x_ref, g_ref, b_ref, o_ref):
    o_ref[...] = _layer_norm(x_ref[...], g_ref[...], b_ref[...], LN_EPS)


def _ln_call(x, g, b, tm):
    n, d = x.shape
    row = pl.BlockSpec((tm, d), lambda i: (i, 0))
    vec = pl.BlockSpec((1, d), lambda i: (0, 0))
    return pl.pallas_call(
        _ln_kernel,
        grid=(n // tm,),
        in_specs=[row, vec, vec],
        out_specs=row,
        out_shape=jax.ShapeDtypeStruct((n, d), F32),
        compiler_params=pltpu.CompilerParams(dimension_semantics=("parallel",)),
        name="ln0",
    )(x, g, b)


W_A_COLS = 4 * D_A
W_IF_COLS = 2 * LANES
W_IN_COLS = W_A_COLS + D_B_IN + W_IF_COLS


def _inproj_kernel(x_ref, w_ref, pa_ref, pb_ref, pif_ref):
    xb = x_ref[...].astype(BF16)
    pa_ref[...] = jnp.dot(xb, w_ref[0, :, 0:W_A_COLS], preferred_element_type=F32)
    pb_ref[...] = jnp.dot(xb, w_ref[0, :, W_A_COLS:W_A_COLS + D_B_IN], preferred_element_type=F32)
    pif_ref[...] = jnp.dot(xb, w_ref[0, :, W_A_COLS + D_B_IN:W_IN_COLS], preferred_element_type=F32)


def _inproj_call(x, w, tm, layer):
    n = x.shape[0]
    return pl.pallas_call(
        _inproj_kernel,
        grid=(n // tm,),
        in_specs=[pl.BlockSpec((tm, D_MODEL), lambda i: (i, 0)),
                  pl.BlockSpec((1, D_MODEL, W_IN_COLS), lambda i: (layer, 0, 0))],
        out_specs=[pl.BlockSpec((tm, W_A_COLS), lambda i: (i, 0)),
                   pl.BlockSpec((tm, D_B_IN), lambda i: (i, 0)),
                   pl.BlockSpec((tm, W_IF_COLS), lambda i: (i, 0))],
        out_shape=[jax.ShapeDtypeStruct((n, W_A_COLS), F32),
                   jax.ShapeDtypeStruct((n, D_B_IN), F32),
                   jax.ShapeDtypeStruct((n, W_IF_COLS), F32)],
        compiler_params=pltpu.CompilerParams(dimension_semantics=("parallel",),
                                             vmem_limit_bytes=VMEM_LIMIT),
        name="inproj",
    )(x, w)


CONV_PAD = 8


def _mlstm_kernel(pa_ref, pif_ref, conv0_ref, c0_ref, n0_ref, m0_ref, convw_ref, convb_ref,
                  bif_ref, gn_ref, *rest, c, nc, nbb, chained):
    h_ref, c_out_ref, n_out_ref, m_out_ref, qk_scr, c_scr, n_scr, m_scr = rest[1:] if chained else rest
    ci = pl.program_id(1)
    rows = nbb * c
    cshift = int(math.log2(c))
    chains = [(nb, h) for nb in range(nbb) for h in range(H_A)]
    prev0 = CONV_PAD - (CONV_W - 1)

    @pl.when(ci == 0)
    def _():
        c_scr[...] = c0_ref[0]
        n_scr[...] = n0_ref[...]
        m_scr[...] = m0_ref[...]
        for nb in range(nbb):
            qk_scr[nb, prev0:CONV_PAD, :] = conv0_ref[nb]

    cat0 = lambda xs: xs[0] if len(xs) == 1 else jnp.concatenate(xs, axis=0)
    acc_l = []
    for nb in range(nbb):
        u = pa_ref[nb, :, 0:2 * D_A]
        qk_scr[nb, CONV_PAD:CONV_PAD + c, :] = u
        acc = convb_ref[...]
        for j in range(CONV_W):
            acc = acc + qk_scr[nb, prev0 + j:prev0 + j + c, :] * convw_ref[j:j + 1, :]
        qk_scr[nb, prev0:CONV_PAD, :] = u[c - (CONV_W - 1):c, :]
        acc_l.append(acc)
    qk = _silu(cat0(acc_l))

    gates = cat0([pif_ref[nb] for nb in range(nbb)]) + bif_ref[...]
    li_all = gates[:, 0:LANES]
    lf_all = _log_sigmoid(gates[:, LANES:2 * LANES])
    row = lax.broadcasted_iota(jnp.int32, (rows, rows), 0)
    col = lax.broadcasted_iota(jnp.int32, (rows, rows), 1)
    tril = jnp.where((col <= row) & ((row >> cshift) == (col >> cshift)), 1.0, 0.0)
    b_all = _dot_exact_lhs(tril, lf_all)
    z_all = li_all - b_all
    crow = lax.broadcasted_iota(jnp.int32, (c, c), 0)
    ccol = lax.broadcasted_iota(jnp.int32, (c, c), 1)
    causal = ccol <= crow
    hrow = lax.broadcasted_iota(jnp.int32, (H_A * c, LANES), 0)
    hlane = lax.broadcasted_iota(jnp.int32, (H_A * c, LANES), 1)
    head_pick = jnp.where((hrow >> cshift) == hlane, 1.0, 0.0)
    z_rows = [_dot_exact_lhs(head_pick, z_all[nb * c:(nb + 1) * c], NT) for nb in range(nbb)]
    lane1 = lax.broadcasted_iota(jnp.int32, (1, LANES), 1)

    rs = lambda nb: slice(nb * c, (nb + 1) * c)
    q_l = [qk[rs(nb), h * DH_A:(h + 1) * DH_A] for nb, h in chains]
    k_l = [qk[rs(nb), D_A + h * DH_A:D_A + (h + 1) * DH_A] * (DH_A ** -0.5) for nb, h in chains]
    v_l = [pa_ref[nb, :, 2 * D_A + h * DH_A:2 * D_A + (h + 1) * DH_A] for nb, h in chains]
    c_l = [c_scr[nb, h] for nb, h in chains]
    n_l = [n_scr[nb, h:h + 1, :] for nb, h in chains]
    b_col = [b_all[rs(nb), h:h + 1] for nb, h in chains]
    li_col = [li_all[rs(nb), h:h + 1] for nb, h in chains]
    m_prev = [m_scr[nb][:, h:h + 1] for nb, h in chains]
    dmat = [jnp.where(causal, bc + z_rows[nb][h * c:(h + 1) * c], -jnp.inf)
            for bc, (nb, h) in zip(b_col, chains)]
    m_inter = [bc + mp for bc, mp in zip(b_col, m_prev)]
    m_t = [jnp.maximum(mi, jnp.max(d, axis=-1, keepdims=True)) for mi, d in zip(m_inter, dmat)]
    qk_dot = [_dot(q, k, NT) for q, k in zip(q_l, k_l)]
    qc = [_dot(q, cm) for q, cm in zip(q_l, c_l)]
    s_l = [x * jnp.exp(d - mt) for x, d, mt in zip(qk_dot, dmat, m_t)]
    sv = [_dot(s, v) for s, v in zip(s_l, v_l)]
    b_last = [bc[c - 1:c, :] for bc in b_col]
    g_s = [bl - bc + li for bl, bc, li in zip(b_last, b_col, li_col)]
    m_new = [jnp.maximum(bl + mp, jnp.max(gs, axis=0, keepdims=True))
             for bl, mp, gs in zip(b_last, m_prev, g_s)]
    wk = [jnp.exp(gs - mn) * k for gs, mn, k in zip(g_s, m_new, k_l)]
    w_old = [jnp.exp(bl + mp - mn) for bl, mp, mn in zip(b_last, m_prev, m_new)]
    kv = [_dot(w, v, TN) for w, v in zip(wk, v_l)]
    for i, (nb, h) in enumerate(chains):
        c_scr[nb, h] = w_old[i] * c_l[i] + kv[i]
        n_scr[nb, h:h + 1, :] = w_old[i] * n_l[i] + jnp.sum(wk[i], axis=0, keepdims=True)
    for nb in range(nbb):
        m_row = m_scr[nb]
        for h in range(H_A):
            m_row = jnp.where(lane1 == h, m_new[nb * H_A + h], m_row)
        m_scr[nb] = m_row
    w_inter = [jnp.exp(mi - mt) for mi, mt in zip(m_inter, m_t)]
    qn = [jnp.sum(q * nv, axis=-1, keepdims=True) for q, nv in zip(q_l, n_l)]
    s_sum = [jnp.sum(s, axis=-1, keepdims=True) for s in s_l]
    den = [w * a + b for w, a, b in zip(w_inter, qn, s_sum)]
    hh = [(w * a + b) / jnp.maximum(jnp.abs(d), jnp.exp(-mt))
          for w, a, b, d, mt in zip(w_inter, qc, sv, den, m_t)]
    mu = [jnp.mean(x, axis=-1, keepdims=True) for x in hh]
    hc = [x - m for x, m in zip(hh, mu)]
    var = [jnp.mean(x * x, axis=-1, keepdims=True) for x in hc]
    for i, (nb, h) in enumerate(chains):
        sl = slice(h * DH_A, (h + 1) * DH_A)
        hn = hc[i] * lax.rsqrt(var[i] + GN_EPS_A)
        o_pre = pa_ref[nb, :, 3 * D_A + h * DH_A:3 * D_A + (h + 1) * DH_A]
        h_ref[nb, :, sl] = hn * gn_ref[:, sl] * _sigmoid(o_pre)

    @pl.when(ci == nc - 1)
    def _():
        c_out_ref[0] = c_scr[...]
        n_out_ref[...] = n_scr[...]
        m_out_ref[...] = m_scr[...]


def _mlstm_call(pa, pif, conv0, c_in, n0, m0, convw, convb, bif, gn, c_stack, *, batch, seq, nbb,
                layer_in, layer):
    c = math.gcd(seq, CHUNK)
    nc = seq // c
    tok = lambda w: pl.BlockSpec((nbb, c, w), lambda b, i: (b, i, 0))
    vec = lambda r, w: pl.BlockSpec((r, w), lambda b, i: (0, 0))
    st5 = lambda l: pl.BlockSpec((1, nbb, H_A, DH_A, DH_A), lambda b, i: (l, b, 0, 0, 0))
    st3 = lambda r, w: pl.BlockSpec((nbb, r, w), lambda b, i: (b, 0, 0))
    chained = c_stack is not None
    in_specs = [tok(W_A_COLS), tok(W_IF_COLS), st3(CONV_W - 1, 2 * D_A), st5(layer_in), st3(H_A, DH_A),
                st3(1, LANES), vec(CONV_W, 2 * D_A), vec(1, 2 * D_A), vec(1, W_IF_COLS), vec(1, D_A)]
    args = [pa, pif, conv0, c_in, n0, m0, convw, convb, bif, gn]
    if chained:
        in_specs.append(pl.BlockSpec(memory_space=pl.ANY))
        args.append(c_stack)
    return pl.pallas_call(
        functools.partial(_mlstm_kernel, c=c, nc=nc, nbb=nbb, chained=chained),
        grid=(batch // nbb, nc),
        in_specs=in_specs,
        out_specs=[tok(D_A), st5(layer), st3(H_A, DH_A), st3(1, LANES)],
        out_shape=[jax.ShapeDtypeStruct((batch, seq, D_A), F32),
                   jax.ShapeDtypeStruct((DEPTH, batch, H_A, DH_A, DH_A), F32),
                   jax.ShapeDtypeStruct((batch, H_A, DH_A), F32),
                   jax.ShapeDtypeStruct((batch, 1, LANES), F32)],
        scratch_shapes=[pltpu.VMEM((nbb, CONV_PAD + c, 2 * D_A), F32),
                        pltpu.VMEM((nbb, H_A, DH_A, DH_A), F32),
                        pltpu.VMEM((nbb, H_A, DH_A), F32),
                        pltpu.VMEM((nbb, 1, LANES), F32)],
        input_output_aliases={len(args) - 1: 1} if chained else {},
        compiler_params=pltpu.CompilerParams(dimension_semantics=("parallel", "arbitrary"),
                                             vmem_limit_bytes=VMEM_LIMIT),
        name="mlstm",
    )(*args)


SHIFT_PAD = 8


def _neumann_inverses(ms, size, t):
    row = lax.broadcasted_iota(jnp.int32, (size, size), 0)
    col = lax.broadcasted_iota(jnp.int32, (size, size), 1)
    eye = jnp.where(row == col, 1.0, 0.0)
    blk = min(SOLVE_BLOCK, t)
    shift = int(math.log2(blk))
    same = (row >> shift) == (col >> shift)
    ds = [jnp.where(same, m, 0.0) for m in ms]
    ps = [eye + d for d in ds]
    xs = ds
    for _ in range(shift - 1):
        xs = [_dot(x, x) for x in xs]
        ps = [p + _dot(p, x) for p, x in zip(ps, xs)]
    nblk = t // blk
    if nblk == 1:
        return ps, None
    ns = [_dot(p, m - d) for p, m, d in zip(ps, ms, ds)]
    qs = [eye + n for n in ns]
    ys = ns
    for _ in range(int(math.log2(nblk)) - 1):
        ys = [_dot(y, y) for y in ys]
        qs = [q + _dot(q, y) for q, y in zip(qs, ys)]
    return ps, qs


def _rwkv_kernel(pb_ref, shift0_ref, s0_ref, mu_ref, w0a0_ref, wlora_ref, g2_ref, kk_ref, ka_ref,
                 rk_ref, lng_ref, lnb_ref, *rest, t, nc, nbb, chained):
    h_ref, s_out_ref, sh_scr, s_scr = rest[1:] if chained else rest
    ci = pl.program_id(1)
    rows = nbb * t
    t2 = 2 * t
    groups = [(nb, pr) for nb in range(nbb) for pr in range(N_PAIRS)]

    brow = lax.broadcasted_iota(jnp.int32, (LANES, LANES), 0)
    bcol = lax.broadcasted_iota(jnp.int32, (LANES, LANES), 1)
    blockdiag = (brow >= DH_B) == (bcol >= DH_B)
    bd2 = jnp.where(blockdiag, 1.0, 0.0).astype(BF16)

    @pl.when(ci == 0)
    def _():
        er = lax.broadcasted_iota(jnp.int32, (DH_B, LANES), 0)
        ec = lax.broadcasted_iota(jnp.int32, (DH_B, LANES), 1)
        dup_cols = jnp.where((ec & (DH_B - 1)) == er, 1.0, 0.0)
        for nb in range(nbb):
            sh_scr[nb, SHIFT_PAD - 1:SHIFT_PAD, :] = shift0_ref[nb]
            for pr in range(N_PAIRS):
                x = s0_ref[0, nb, 2 * pr:2 * pr + 2].reshape(LANES, DH_B)
                s_scr[nb * N_PAIRS + pr] = jnp.where(blockdiag, _dot_exact_rhs(x, dup_cols), 0.0)

    p_l, prev_l = [], []
    for nb in range(nbb):
        p_nb = pb_ref[nb]
        sh_scr[nb, SHIFT_PAD:SHIFT_PAD + t, :] = p_nb
        prev_l.append(sh_scr[nb, SHIFT_PAD - 1:SHIFT_PAD - 1 + t, :])
        sh_scr[nb, SHIFT_PAD - 1:SHIFT_PAD, :] = p_nb[t - 1:t, :]
        p_l.append(p_nb)
    cat0 = lambda xs: xs[0] if len(xs) == 1 else jnp.concatenate(xs, axis=0)
    p = cat0(p_l)
    prev = cat0(prev_l)
    pb = p + (prev - p) * mu_ref[...]

    r = pb[:, 0:D_B]
    kr = pb[:, D_B:2 * D_B]
    vr = pb[:, 2 * D_B:3 * D_B]
    lora_in = pb[:, 3 * D_B:3 * D_B + LANES]
    gl = pb[:, 3 * D_B + LANES:3 * D_B + 2 * LANES]
    lane = lax.broadcasted_iota(jnp.int32, (rows, LANES), 1)
    lora_act = jnp.where(lane < R_DECAY, jnp.tanh(lora_in), lora_in)
    z = _dot(lora_act, wlora_ref[...]) + w0a0_ref[...]
    ld = -jnp.exp(_log_sigmoid(z[:, 0:D_B]) - 0.5)
    a = _sigmoid(z[:, D_B:2 * D_B])
    g = _dot(_sigmoid(gl), g2_ref[...])

    def seg_sum(x):
        xr = jnp.concatenate([x[:, q * LANES:(q + 1) * LANES] for q in range(N_PAIRS)], axis=0)
        hi = xr.astype(BF16)
        lo = (xr - hi.astype(F32)).astype(BF16)
        s = (jnp.dot(hi, bd2, preferred_element_type=F32) + jnp.dot(lo, bd2, preferred_element_type=F32))
        return jnp.concatenate([s[q * rows:(q + 1) * rows] for q in range(N_PAIRS)], axis=1)

    kk_raw = kr * kk_ref[...]
    kk = kk_raw / jnp.maximum(jnp.sqrt(seg_sum(kk_raw * kk_raw)), 1e-12)
    k2 = kr * (1.0 + (a - 1.0) * ka_ref[...])
    bonus = seg_sum(r * k2 * rk_ref[...]) * vr

    row = lax.broadcasted_iota(jnp.int32, (rows, rows), 0)
    col = lax.broadcasted_iota(jnp.int32, (rows, rows), 1)
    tshift = int(math.log2(t))
    tril = jnp.where((col <= row) & ((row >> tshift) == (col >> tshift)), 1.0, 0.0)
    lc = _dot_exact_lhs(tril, ld)
    lends = [lc[(nb + 1) * t - 1:(nb + 1) * t, :] for nb in range(nbb)]
    lend_rows = cat0([jnp.broadcast_to(le, (t, D_B)) for le in lends])
    e_nc = jnp.exp(-lc)
    e_end = jnp.exp(lend_rows - lc)
    b_raw = kk * a
    at = -kk * jnp.exp(lc - ld)
    rt = r * jnp.exp(lc)
    kt = k2 * e_nc
    bt = b_raw * e_nc
    kend = k2 * e_end
    bend = b_raw * e_end

    srow = lax.broadcasted_iota(jnp.int32, (t2, t2), 0)
    scol = lax.broadcasted_iota(jnp.int32, (t2, t2), 1)
    strict = scol < srow
    incl = scol <= srow
    lane2 = lax.broadcasted_iota(jnp.int32, (t2, LANES), 1)
    row2 = lax.broadcasted_iota(jnp.int32, (t2, LANES), 0)
    head_sel = (lane2 >= DH_B) == (row2 >= t)
    first_head = lax.broadcasted_iota(jnp.int32, (t, LANES), 1) < DH_B

    def blk(x, nb, pr):
        return x[nb * t:(nb + 1) * t, pr * LANES:(pr + 1) * LANES]

    def stack(x):
        return jnp.where(head_sel, jnp.concatenate([x, x], axis=0), 0.0)

    lhs = [jnp.concatenate([stack(blk(at, *gp)), stack(blk(rt, *gp))], axis=0) for gp in groups]
    rk_s = [stack(blk(kt, *gp)) for gp in groups]
    rb_s = [stack(blk(bt, *gp)) for gp in groups]
    v_p = [blk(vr, *gp) for gp in groups]
    v_s = [jnp.concatenate([v, v], axis=0) for v in v_p]
    s_p = [s_scr[i] for i in range(len(groups))]
    gk = [_dot(l, x, NT) for l, x in zip(lhs, rk_s)]
    gb = [_dot(l, x, NT) for l, x in zip(lhs, rb_s)]
    xs = [_dot(l, s, NT) for l, s in zip(lhs, s_p)]
    m_ab = [jnp.where(strict, x[0:t2], 0.0) for x in gb]
    ps, qs = _neumann_inverses(m_ab, t2, t)
    rhs = [x[0:t2] + _dot(jnp.where(strict, y[0:t2], 0.0), v) for x, y, v in zip(xs, gk, v_s)]
    sa_s = [_dot(pm, x) for pm, x in zip(ps, rhs)]
    if qs is not None:
        sa_s = [_dot(qm, x) for qm, x in zip(qs, sa_s)]
    o_s = [x[t2:] + _dot(jnp.where(incl, y[t2:], 0.0), v) + _dot(jnp.where(incl, w[t2:], 0.0), sa)
           for x, y, w, v, sa in zip(xs, gk, gb, v_s, sa_s)]
    o_p = [jnp.where(first_head, o[0:t], o[t:t2]) for o in o_s]
    sa_p = [jnp.where(first_head, sa[0:t], sa[t:t2]) for sa in sa_s]
    upd = [_dot(jnp.concatenate([v, sa], axis=0),
                jnp.concatenate([blk(kend, *gp), blk(bend, *gp)], axis=0), TN)
           for v, sa, gp in zip(v_p, sa_p, groups)]
    for i, (nb, pr) in enumerate(groups):
        dec = jnp.exp(lends[nb][:, pr * LANES:(pr + 1) * LANES])
        s_scr[i] = s_p[i] * dec + jnp.where(blockdiag, upd[i], 0.0)

    o = cat0([jnp.concatenate(o_p[nb * N_PAIRS:(nb + 1) * N_PAIRS], axis=1) for nb in range(nbb)])
    mean = seg_sum(o) * (1.0 / DH_B)
    oc = o - mean
    var = seg_sum(oc * oc) * (1.0 / DH_B)
    out = oc * lax.rsqrt(var + GN_EPS_B) * lng_ref[...] + lnb_ref[...]
    res = (out + bonus) * g
    for nb in range(nbb):
        h_ref[nb] = res[nb * t:(nb + 1) * t, :]

    @pl.when(ci == nc - 1)
    def _():
        fr = lax.broadcasted_iota(jnp.int32, (LANES, DH_B), 0)
        fc = lax.broadcasted_iota(jnp.int32, (LANES, DH_B), 1)
        dup_rows = jnp.where((fr & (DH_B - 1)) == fc, 1.0, 0.0)
        for i, (nb, pr) in enumerate(groups):
            packed = _dot_exact_rhs(s_scr[i], dup_rows)
            s_out_ref[0, nb, 2 * pr:2 * pr + 2] = packed.reshape(2, DH_B, DH_B)


def _rwkv_call(pb, shift0, s_in, mu, w0a0, wlora, g2, kk, ka, rk, lng, lnb, s_stack, *, batch, seq, nbb,
               layer_in, layer):
    t = math.gcd(seq, CHUNK)
    nc = seq // t
    tok = lambda w: pl.BlockSpec((nbb, t, w), lambda b, i: (b, i, 0))
    vec = lambda r, w: pl.BlockSpec((r, w), lambda b, i: (0, 0))
    st5 = lambda l: pl.BlockSpec((1, nbb, H_B, DH_B, DH_B), lambda b, i: (l, b, 0, 0, 0))
    chained = s_stack is not None
    in_specs = [tok(D_B_IN), pl.BlockSpec((nbb, 1, D_B_IN), lambda b, i: (b, 0, 0)), st5(layer_in),
                vec(1, D_B_IN), vec(1, 2 * D_B), vec(LANES, 2 * D_B), vec(R_GATE, D_B),
                vec(1, D_B), vec(1, D_B), vec(1, D_B), vec(1, D_B), vec(1, D_B)]
    args = [pb, shift0, s_in, mu, w0a0, wlora, g2, kk, ka, rk, lng, lnb]
    if chained:
        in_specs.append(pl.BlockSpec(memory_space=pl.ANY))
        args.append(s_stack)
    return pl.pallas_call(
        functools.partial(_rwkv_kernel, t=t, nc=nc, nbb=nbb, chained=chained),
        grid=(batch // nbb, nc),
        in_specs=in_specs,
        out_specs=[tok(D_B), st5(layer)],
        out_shape=[jax.ShapeDtypeStruct((batch, seq, D_B), F32),
                   jax.ShapeDtypeStruct((DEPTH, batch, H_B, DH_B, DH_B), F32)],
        scratch_shapes=[pltpu.VMEM((nbb, SHIFT_PAD + t, D_B_IN), F32),
                        pltpu.VMEM((nbb * N_PAIRS, LANES, LANES), F32)],
        input_output_aliases={len(args) - 1: 1} if chained else {},
        compiler_params=pltpu.CompilerParams(dimension_semantics=("parallel", "arbitrary"),
                                             vmem_limit_bytes=VMEM_LIMIT),
        name="rwkv",
    )(*args)


ROUTE_ROWS = 8


def _outproj_kernel(x_ref, ha_ref, hb_ref, wo_ref, g_ref, b_ref, wr_ref, br_ref, x1_ref, route_ref,
                    cnt_ref, *, tm):
    y = _dot(ha_ref[...], wo_ref[0, 0:D_A, :]) + _dot(hb_ref[...], wo_ref[0, D_A:D_A + D_B, :])
    x1 = _layer_norm(ALPHA * x_ref[...] + y, g_ref[...], b_ref[...], LN_EPS)
    x1_ref[...] = x1
    logits = lax.dot_general(wr_ref[...], x1, NT, precision=lax.Precision.HIGHEST,
                             preferred_element_type=F32)
    mx = jnp.max(logits, axis=0, keepdims=True)
    ex = jnp.exp(logits - mx)
    probs = ex / jnp.sum(ex, axis=0, keepdims=True)
    sel = probs + br_ref[...]
    neg = -jnp.inf

    def top2(rows):
        m1 = functools.reduce(jnp.maximum, rows)
        i1 = jnp.full(m1.shape, len(rows) - 1, jnp.int32)
        for j in range(len(rows) - 2, -1, -1):
            i1 = jnp.where(rows[j] == m1, j, i1)
        rest = [jnp.where(i1 == j, neg, rows[j]) for j in range(len(rows))]
        m2 = functools.reduce(jnp.maximum, rest)
        i2 = jnp.full(m2.shape, len(rows) - 1, jnp.int32)
        for j in range(len(rows) - 2, -1, -1):
            i2 = jnp.where(rest[j] == m2, j, i2)
        return m1, i1, m2, i2

    g_score, g_i1, g_i2 = [], [], []
    for gidx in range(N_GROUPS):
        rows = [sel[gidx * EXPERTS_PER_GROUP + j:gidx * EXPERTS_PER_GROUP + j + 1, :]
                for j in range(EXPERTS_PER_GROUP)]
        m1, i1, m2, i2 = top2(rows)
        g_score.append(m1 + m2)
        g_i1.append(i1)
        g_i2.append(i2)
    best = functools.reduce(jnp.maximum, g_score)
    grp = jnp.full(best.shape, N_GROUPS - 1, jnp.int32)
    for gidx in range(N_GROUPS - 2, -1, -1):
        grp = jnp.where(g_score[gidx] == best, gidx, grp)
    l1 = g_i1[N_GROUPS - 1]
    l2 = g_i2[N_GROUPS - 1]
    for gidx in range(N_GROUPS - 2, -1, -1):
        l1 = jnp.where(grp == gidx, g_i1[gidx], l1)
        l2 = jnp.where(grp == gidx, g_i2[gidx], l2)
    e1 = grp * EXPERTS_PER_GROUP + l1
    e2 = grp * EXPERTS_PER_GROUP + l2
    e_iota = lax.broadcasted_iota(jnp.int32, (N_EXPERTS, tm), 0)
    p1 = jnp.sum(jnp.where(e_iota == e1, probs, 0.0), axis=0, keepdims=True)
    p2 = jnp.sum(jnp.where(e_iota == e2, probs, 0.0), axis=0, keepdims=True)
    tot = p1 + p2
    r8 = lax.broadcasted_iota(jnp.int32, (ROUTE_ROWS, tm), 0)
    route_ref[...] = jnp.where(r8 == 0, e1.astype(F32),
                               jnp.where(r8 == 1, e2.astype(F32),
                                         jnp.where(r8 == 2, p1 / tot, jnp.where(r8 == 3, p2 / tot, 0.0))))
    picked = jnp.where((e_iota == e1) | (e_iota == e2), 1.0, 0.0)
    cnt_ref[0] = jnp.broadcast_to(jnp.sum(picked, axis=1, keepdims=True), (N_EXPERTS, LANES))


def _outproj_call(x, ha, hb, wo, g, b, wr_t, br, tm, layer):
    n = x.shape[0]
    row = lambda w: pl.BlockSpec((tm, w), lambda i: (i, 0))
    full = lambda r, w: pl.BlockSpec((r, w), lambda i: (0, 0))
    return pl.pallas_call(
        functools.partial(_outproj_kernel, tm=tm),
        grid=(n // tm,),
        in_specs=[row(D_MODEL), row(D_A), row(D_B),
                  pl.BlockSpec((1, D_MODEL, D_MODEL), lambda i: (layer, 0, 0)), full(1, D_MODEL),
                  full(1, D_MODEL), full(N_EXPERTS, D_MODEL), full(N_EXPERTS, 1)],
        out_specs=[row(D_MODEL), pl.BlockSpec((ROUTE_ROWS, tm), lambda i: (0, i)),
                   pl.BlockSpec((1, N_EXPERTS, LANES), lambda i: (i, 0, 0))],
        out_shape=[jax.ShapeDtypeStruct((n, D_MODEL), F32), jax.ShapeDtypeStruct((ROUTE_ROWS, n), F32),
                   jax.ShapeDtypeStruct((n // tm, N_EXPERTS, LANES), F32)],
        compiler_params=pltpu.CompilerParams(dimension_semantics=("parallel",),
                                             vmem_limit_bytes=VMEM_LIMIT),
        name="outproj",
    )(x, ha, hb, wo, g, b, wr_t, br)


MOE_BM = 128
MOE_CH = 512
MOE_CCH = 1024


def _moe_max_blocks(t):
    return -(-2 * t // MOE_BM) + N_EXPERTS - 1


def _moe_sort_kernel(nblk_ref, x_ref, route_ref, padoff_ref, tri_ref, xs_ref, cols_ref, *, t, n_tiles,
                     n_chunks):
    i = pl.program_id(0)

    @pl.when(i >= n_tiles)
    def _():
        xs_ref[...] = jnp.zeros_like(xs_ref)

    @pl.when(i < n_tiles)
    def _():
        s_used = nblk_ref[jnp.minimum(i, n_tiles - 1)] * MOE_BM
        xb = x_ref[...].astype(BF16)
        route = route_ref[...]
        e_iota = lax.broadcasted_iota(jnp.int32, (N_EXPERTS, t), 0)
        a1 = e_iota == route[0:1].astype(jnp.int32)
        a2 = e_iota == route[1:2].astype(jnp.int32)
        picked = jnp.where(a1 | a2, 1.0, 0.0).astype(BF16)
        rank = jnp.dot(picked, tri_ref[...], preferred_element_type=F32)
        base = padoff_ref[0][:, 0:1] + rank
        slot1 = jnp.sum(jnp.where(a1, base, 0.0), axis=0, keepdims=True)
        slot2 = jnp.sum(jnp.where(a2, base, 0.0), axis=0, keepdims=True)
        r8 = lax.broadcasted_iota(jnp.int32, (ROUTE_ROWS, t), 0)
        rows = jnp.where(r8 == 0, slot1, jnp.where(r8 == 1, slot2,
                                                   jnp.where(r8 == 2, route[2:3],
                                                             jnp.where(r8 == 3, route[3:4], 0.0))))
        padded = jnp.concatenate([rows, jnp.zeros((LANES - ROUTE_ROWS, t), F32)], axis=0)
        cols_ref[...] = padded.T
        s1 = slot1.astype(jnp.int32)
        s2 = slot2.astype(jnp.int32)
        for c in range(n_chunks):
            used = c * MOE_CH < s_used

            @pl.when(used)
            def _():
                s_iota = lax.broadcasted_iota(jnp.int32, (MOE_CH, t), 0) + c * MOE_CH
                onehot = jnp.where((s_iota == s1) | (s_iota == s2), 1.0, 0.0).astype(BF16)
                xs_ref[0, c * MOE_CH:(c + 1) * MOE_CH, :] = jnp.dot(
                    onehot, xb, preferred_element_type=F32).astype(BF16)

            @pl.when(jnp.logical_not(used))
            def _():
                xs_ref[0, c * MOE_CH:(c + 1) * MOE_CH, :] = jnp.zeros((MOE_CH, D_MODEL), BF16)


def _moe_expert_kernel(sblk_ref, ebase_ref, ecnt_ref, xs_hbm, wg_ref, wu_ref, wd_ref, ys_hbm,
                       wg_scr, wu_scr, wd_scr, xbuf, ybuf, in_sem, out_sem):
    e = pl.program_id(0)
    base = ebase_ref[e]
    cnt = ecnt_ref[e]
    wg_scr[...] = wg_ref[0, 0].astype(BF16)
    wu_scr[...] = wu_ref[0, 0].astype(BF16)
    wd_scr[...] = wd_ref[0, 0].astype(BF16)

    def rows(k):
        return pl.ds(pl.multiple_of(sblk_ref[base + k] * MOE_BM, MOE_BM), MOE_BM)

    def in_copy(k, slot):
        return pltpu.make_async_copy(xs_hbm.at[rows(k), :], xbuf.at[slot], in_sem.at[slot])

    def out_copy(k, slot):
        return pltpu.make_async_copy(ybuf.at[slot], ys_hbm.at[rows(k), :], out_sem.at[slot])

    @pl.when(cnt > 0)
    def _():
        in_copy(0, 0).start()

    def block(k, carry):
        slot = k % 2
        in_copy(k, slot).wait()

        @pl.when(k + 1 < cnt)
        def _():
            in_copy(k + 1, 1 - slot).start()

        @pl.when(k >= 2)
        def _():
            out_copy(k - 2, slot).wait()

        xblk = xbuf[slot]
        hid = _silu(jnp.dot(xblk, wg_scr[...], preferred_element_type=F32)) * jnp.dot(
            xblk, wu_scr[...], preferred_element_type=F32)
        y = jnp.dot(hid.astype(BF16), wd_scr[...], preferred_element_type=F32)
        ybuf[slot] = y.astype(BF16)
        out_copy(k, slot).start()
        return carry

    lax.fori_loop(0, cnt, block, 0)

    @pl.when(cnt >= 2)
    def _():
        out_copy(cnt - 2, cnt % 2).wait()

    @pl.when(cnt >= 1)
    def _():
        out_copy(cnt - 1, (cnt - 1) % 2).wait()


def _moe_combine_kernel(nblk_ref, x_ref, cols_ref, ys_ref, g_ref, b_ref, o_ref, acc_scr, *, t, n_chunks):
    s_used = nblk_ref[pl.program_id(0)] * MOE_BM
    cols = cols_ref[...]
    s1c = cols[:, 0:1].astype(jnp.int32)
    s2c = cols[:, 1:2].astype(jnp.int32)
    w1c = cols[:, 2:3]
    w2c = cols[:, 3:4]
    for c in range(n_chunks):
        @pl.when(c * MOE_CCH < s_used)
        def _():
            l_iota = lax.broadcasted_iota(jnp.int32, (t, MOE_CCH), 1) + c * MOE_CCH
            weighted = (jnp.where(l_iota == s1c, w1c, 0.0) + jnp.where(l_iota == s2c, w2c, 0.0)).astype(BF16)
            part = jnp.dot(weighted, ys_ref[0, c * MOE_CCH:(c + 1) * MOE_CCH, :],
                           preferred_element_type=F32)
            if c == 0:
                acc_scr[...] = part
            else:
                acc_scr[...] += part
    o_ref[...] = _layer_norm(ALPHA * x_ref[...] + acc_scr[...], g_ref[...], b_ref[...], LN_EPS)


def _moe_schedule(counts, tiles_per, maxb, bpt):
    cnt = counts[:, :, 0].astype(jnp.int32)
    cnt = cnt.reshape(-1, tiles_per, N_EXPERTS).sum(axis=1)
    n_tiles = cnt.shape[0]
    nblk = (cnt + MOE_BM - 1) // MOE_BM
    end = jnp.cumsum(nblk, axis=-1)
    start = end - nblk
    total = end[:, -1]
    padoff = jnp.broadcast_to((start * MOE_BM).astype(F32)[:, :, None], (n_tiles, N_EXPERTS, LANES))
    per_e = jnp.sum(nblk, axis=0)
    e_end = jnp.cumsum(per_e)
    e_base = e_end - per_e
    e_ids = jnp.arange(N_EXPERTS, dtype=jnp.int32)
    s_idx = jnp.arange(n_tiles * maxb, dtype=jnp.int32)
    last_e = jnp.max(jnp.where(per_e > 0, e_ids, 0))
    se = jnp.minimum(jnp.sum((s_idx[:, None] >= e_end[None, :]).astype(jnp.int32), axis=-1), last_e)
    oh_e = se[:, None] == e_ids[None, :]
    r = s_idx - jnp.sum(jnp.where(oh_e, e_base[None, :], 0), axis=-1)
    of_e = lambda a: jnp.sum(jnp.where(oh_e[:, None, :], a[None, :, :], 0), axis=-1)
    incl_s = of_e(jnp.cumsum(nblk, axis=0))
    tile = jnp.minimum(jnp.sum((incl_s <= r[:, None]).astype(jnp.int32), axis=-1), n_tiles - 1)
    oh_t = tile[:, None] == jnp.arange(n_tiles, dtype=jnp.int32)[None, :]
    of_t = lambda a: jnp.sum(jnp.where(oh_t, a, 0), axis=-1)
    j = of_t(of_e(start)) + r - (of_t(incl_s) - of_t(of_e(nblk)))
    sblk = jnp.where(s_idx < e_end[-1], tile * bpt + j, n_tiles * bpt)
    return total, padoff, sblk, e_base, per_e


def _moe_call(x, route, counts, wg, wu, wd, g, b, *, layer, tm):
    n = x.shape[0]
    t = 1024 if n % 1024 == 0 else n
    n_tiles = n // t
    maxb = _moe_max_blocks(t)
    s_alloc = -(-maxb * MOE_BM // MOE_CCH) * MOE_CCH
    n_chunks = s_alloc // MOE_CH
    bpt = s_alloc // MOE_BM
    nblk, padoff, sblk, e_base, e_cnt = _moe_schedule(counts, t // tm, maxb, bpt)
    tri = jnp.triu(jnp.ones((t, t), BF16), k=1)
    last = n_tiles - 1
    xs, cols = pl.pallas_call(
        functools.partial(_moe_sort_kernel, t=t, n_tiles=n_tiles, n_chunks=n_chunks),
        grid_spec=pltpu.PrefetchScalarGridSpec(
            num_scalar_prefetch=1,
            grid=(n_tiles + 1,),
            in_specs=[pl.BlockSpec((t, D_MODEL), lambda i, nb: (jnp.minimum(i, last), 0)),
                      pl.BlockSpec((ROUTE_ROWS, t), lambda i, nb: (0, jnp.minimum(i, last))),
                      pl.BlockSpec((1, N_EXPERTS, LANES), lambda i, nb: (jnp.minimum(i, last), 0, 0)),
                      pl.BlockSpec((t, t), lambda i, nb: (0, 0))],
            out_specs=[pl.BlockSpec((1, s_alloc, D_MODEL), lambda i, nb: (i, 0, 0)),
                       pl.BlockSpec((t, LANES), lambda i, nb: (jnp.minimum(i, last), 0))]),
        out_shape=[jax.ShapeDtypeStruct((n_tiles + 1, s_alloc, D_MODEL), BF16),
                   jax.ShapeDtypeStruct((n, LANES), F32)],
        compiler_params=pltpu.CompilerParams(dimension_semantics=("arbitrary",),
                                             vmem_limit_bytes=VMEM_LIMIT),
        name="moe_sort",
    )(nblk, x, route, padoff, tri)
    wspec = lambda r, c: pl.BlockSpec((1, 1, r, c), lambda e, sb, eb, ec: (layer, e, 0, 0))
    ys = pl.pallas_call(
        _moe_expert_kernel,
        grid_spec=pltpu.PrefetchScalarGridSpec(
            num_scalar_prefetch=3,
            grid=(N_EXPERTS,),
            in_specs=[pl.BlockSpec(memory_space=pl.ANY),
                      wspec(D_MODEL, D_EXP), wspec(D_MODEL, D_EXP), wspec(D_EXP, D_MODEL)],
            out_specs=pl.BlockSpec(memory_space=pl.ANY),
            scratch_shapes=[pltpu.VMEM((D_MODEL, D_EXP), BF16), pltpu.VMEM((D_MODEL, D_EXP), BF16),
                            pltpu.VMEM((D_EXP, D_MODEL), BF16),
                            pltpu.VMEM((2, MOE_BM, D_MODEL), BF16), pltpu.VMEM((2, MOE_BM, D_MODEL), BF16),
                            pltpu.SemaphoreType.DMA((2,)), pltpu.SemaphoreType.DMA((2,))]),
        out_shape=jax.ShapeDtypeStruct(((n_tiles + 1) * s_alloc, D_MODEL), BF16),
        input_output_aliases={3: 0},
        compiler_params=pltpu.CompilerParams(dimension_semantics=("arbitrary",),
                                             vmem_limit_bytes=VMEM_LIMIT),
        name="moe_experts",
    )(sblk, e_base, e_cnt, xs.reshape((n_tiles + 1) * s_alloc, D_MODEL), wg, wu, wd)
    vec = pl.BlockSpec((1, D_MODEL), lambda i, nb: (0, 0))
    return pl.pallas_call(
        functools.partial(_moe_combine_kernel, t=t, n_chunks=s_alloc // MOE_CCH),
        grid_spec=pltpu.PrefetchScalarGridSpec(
            num_scalar_prefetch=1,
            grid=(n_tiles,),
            in_specs=[pl.BlockSpec((t, D_MODEL), lambda i, nb: (i, 0)),
                      pl.BlockSpec((t, LANES), lambda i, nb: (i, 0)),
                      pl.BlockSpec((1, s_alloc, D_MODEL), lambda i, nb: (i, 0, 0)), vec, vec],
            out_specs=pl.BlockSpec((t, D_MODEL), lambda i, nb: (i, 0)),
            scratch_shapes=[pltpu.VMEM((t, D_MODEL), F32)]),
        out_shape=jax.ShapeDtypeStruct((n, D_MODEL), F32),
        compiler_params=pltpu.CompilerParams(dimension_semantics=("parallel",),
                                             vmem_limit_bytes=VMEM_LIMIT),
        name="moe_combine",
    )(nblk, x, cols, ys.reshape(n_tiles + 1, s_alloc, D_MODEL), g, b)


def _row_tile(n):
    return 512 if n % 512 == 0 else n


def _seqs_per_step(batch, want):
    return want if batch % want == 0 else 1


def kernel(x_prompt, x_sample, state_mlstm_C, state_mlstm_n, state_mlstm_m, state_mlstm_conv, state_rwkv_S, state_rwkv_shift, ln0_g, ln0_b, w_in, conv_w, conv_b, b_i, b_f, gn_a_g, mu_shift, w0, w2, a0, a2, g2, k_k, k_a, r_k, lnx_g, lnx_b, w_out, ln1_g, ln1_b, w_router, b_router, we_gate, we_up, we_down, ln2_g, ln2_b):
    d_a_in = 4 * D_A + 2 * H_A
    zpad = jnp.zeros((DEPTH, D_MODEL, LANES - H_A), F32)
    w_cat = jnp.concatenate(
        [w_in[:, :, 0:4 * D_A], w_in[:, :, d_a_in:], w_in[:, :, 4 * D_A:4 * D_A + H_A], zpad,
         w_in[:, :, 4 * D_A + H_A:d_a_in], zpad], axis=-1).astype(BF16)
    bif = jnp.zeros((DEPTH, 1, W_IF_COLS), F32)
    bif = bif.at[:, 0, 0:H_A].set(b_i).at[:, 0, LANES:LANES + H_A].set(b_f)
    wlora = jnp.zeros((DEPTH, LANES, 2 * D_B), F32)
    wlora = wlora.at[:, 0:R_DECAY, 0:D_B].set(w2).at[:, R_DECAY:, D_B:].set(a2).astype(BF16)
    w0a0 = jnp.concatenate([w0, a0], axis=-1)[:, None, :]
    g2b = g2.astype(BF16)
    wob = w_out.astype(BF16)
    wr_t = w_router.T
    br = b_router[:, None]
    r1 = lambda v: v[None, :]

    def run(x3, states):
        nb, seq, _ = x3.shape
        n = nb * seq
        tm = _row_tile(n)
        rwkv_nbb = _seqs_per_step(nb, 4 if seq >= CHUNK else 8)
        mlstm_nbb = _seqs_per_step(nb, 2 if seq >= CHUNK else 8)
        x = _ln_call(x3.reshape(n, D_MODEL), r1(ln0_g), r1(ln0_b), tm)
        outs = []
        c_stack = jnp.zeros((DEPTH, nb, H_A, DH_A, DH_A), F32)
        s_stack = jnp.zeros((DEPTH, nb, H_B, DH_B, DH_B), F32)
        for l in range(DEPTH):
            if states is None:
                c_in = jnp.zeros((1, nb, H_A, DH_A, DH_A), F32)
                s_in = jnp.zeros((1, nb, H_B, DH_B, DH_B), F32)
                layer_in = 0
                n0 = jnp.zeros((nb, H_A, DH_A), F32)
                m0 = jnp.zeros((nb, H_A), F32)
                conv0 = jnp.zeros((nb, CONV_W - 1, 2 * D_A), F32)
                shift0 = jnp.zeros((nb, D_B_IN), F32)
            else:
                c_in, s_in, layer_in = states[0], states[4], l
                n0, m0, conv0, shift0 = (states[k][l] for k in (1, 2, 3, 5))
            m0p = jnp.zeros((nb, 1, LANES), F32).at[:, 0, 0:H_A].set(m0)
            pa, pb, pif = _inproj_call(x, w_cat, tm, l)
            ha, c_stack, n_new, m_new = _mlstm_call(
                pa.reshape(nb, seq, W_A_COLS), pif.reshape(nb, seq, W_IF_COLS), conv0, c_in, n0, m0p,
                conv_w[l], r1(conv_b[l]), bif[l], r1(gn_a_g[l]), c_stack, batch=nb, seq=seq,
                nbb=mlstm_nbb, layer_in=layer_in, layer=l)
            ha = ha.reshape(n, D_A)
            hb, s_stack = _rwkv_call(
                pb.reshape(nb, seq, D_B_IN), shift0[:, None, :], s_in, r1(mu_shift[l]), w0a0[l], wlora[l],
                g2b[l], r1(k_k[l]), r1(k_a[l]), r1(r_k[l].reshape(D_B)), r1(lnx_g[l]), r1(lnx_b[l]),
                s_stack, batch=nb, seq=seq, nbb=rwkv_nbb, layer_in=layer_in, layer=l)
            hb = hb.reshape(n, D_B)
            x1, route, counts = _outproj_call(x, ha, hb, wob, r1(ln1_g[l]), r1(ln1_b[l]), wr_t, br, tm, l)
            x = _moe_call(x1, route, counts, we_gate, we_up, we_down, r1(ln2_g[l]), r1(ln2_b[l]),
                          layer=l, tm=tm)
            pa3 = pa.reshape(nb, seq, W_A_COLS)
            full = jnp.concatenate([conv0, pa3[:, :, 0:2 * D_A]], axis=1) if seq < CONV_W - 1 else pa3[:, :, 0:2 * D_A]
            conv_new = full[:, -(CONV_W - 1):, :]
            shift_new = pb.reshape(nb, seq, D_B_IN)[:, -1, :]
            outs.append((n_new, m_new[:, 0, 0:H_A], conv_new, shift_new))
        n_all, m_all, conv_all, shift_all = (jnp.stack(s) for s in zip(*outs))
        return x.reshape(nb, seq, D_MODEL), (c_stack, n_all, m_all, conv_all, s_stack, shift_all)

    y_prompt, (p_c, p_n, p_m, p_conv, p_s, p_shift) = run(x_prompt, None)
    y_sample, (s_c, s_n, s_m, s_conv, s_s, s_shift) = run(
        x_sample, (state_mlstm_C, state_mlstm_n, state_mlstm_m, state_mlstm_conv, state_rwkv_S,
                   state_rwkv_shift))
    return (y_prompt, y_sample, p_c, p_n, p_m, p_conv, p_s, p_shift, s_c, s_n, s_m, s_conv, s_s, s_shift)
```

```python
import functools
import math

import jax
import jax.numpy as jnp
from jax import lax
from jax.experimental import pallas as pl
from jax.experimental.pallas import tpu as pltpu

F32 = jnp.float32
BF16 = jnp.bfloat16

D_MODEL = 1024
DEPTH = 4
D_A = 512
D_B = 512
DH_A = 128
H_A = 4
DH_B = 64
H_B = 8
N_PAIRS = H_B // 2
CONV_W = 4
CHUNK = 64
R_DECAY = 64
R_AAA = 64
R_GATE = 128
D_B_IN = 3 * D_B + R_DECAY + R_AAA + R_GATE
N_EXPERTS = 16
N_GROUPS = 4
EXPERTS_PER_GROUP = 4
D_EXP = 512
ALPHA = (2 * DEPTH) ** 0.25
LN_EPS = 1e-5
GN_EPS_A = 1e-6
GN_EPS_B = 64e-5

LANES = 128
SOLVE_BLOCK = 16
VMEM_LIMIT = 48 * 1024 * 1024

NN = (((1,), (0,)), ((), ()))
NT = (((1,), (1,)), ((), ()))
TN = (((0,), (0,)), ((), ()))


def _dot(a, b, dims=NN):
    return lax.dot_general(a.astype(BF16), b.astype(BF16), dims, preferred_element_type=F32)


def _split3(x):
    hi = x.astype(BF16)
    r1 = x - hi.astype(F32)
    mid = r1.astype(BF16)
    lo = (r1 - mid.astype(F32)).astype(BF16)
    return hi, mid, lo


def _dot_exact_lhs(a, x, dims=NN):
    a = a.astype(BF16)
    hi, mid, lo = _split3(x)
    d = lambda p: lax.dot_general(a, p, dims, preferred_element_type=F32)
    return d(hi) + d(mid) + d(lo)


def _dot_exact_rhs(x, b, dims=NN):
    b = b.astype(BF16)
    hi, mid, lo = _split3(x)
    d = lambda p: lax.dot_general(p, b, dims, preferred_element_type=F32)
    return d(hi) + d(mid) + d(lo)


def _layer_norm(x, g, b, eps):
    mu = jnp.mean(x, axis=-1, keepdims=True)
    xc = x - mu
    var = jnp.mean(xc * xc, axis=-1, keepdims=True)
    return xc * lax.rsqrt(var + eps) * g + b


def _sigmoid(x):
    return 1.0 / (1.0 + jnp.exp(-x))


def _log_sigmoid(x):
    return jnp.minimum(x, 0.0) - jnp.log(1.0 + jnp.exp(-jnp.abs(x)))


def _silu(x):
    return x * _sigmoid(x)


def _ln_kernel(x_ref, g_ref, b_ref, o_ref):
    o_ref[...] = _layer_norm(x_ref[...], g_ref[...], b_ref[...], LN_EPS)


def _ln_call(x, g, b, tm):
    n, d = x.shape
    row = pl.BlockSpec((tm, d), lambda i: (i, 0))
    vec = pl.BlockSpec((1, d), lambda i: (0, 0))
    return pl.pallas_call(
        _ln_kernel,
        grid=(n // tm,),
        in_specs=[row, vec, vec],
        out_specs=row,
        out_shape=jax.ShapeDtypeStruct((n, d), F32),
        compiler_params=pltpu.CompilerParams(dimension_semantics=("parallel",)),
        name="ln0",
    )(x, g, b)


W_A_COLS = 4 * D_A
W_IF_COLS = 2 * LANES
W_IN_COLS = W_A_COLS + D_B_IN + W_IF_COLS


def _inproj_kernel(x_ref, w_ref, pa_ref, pb_ref, pif_ref):
    xb = x_ref[...].astype(BF16)
    pa_ref[...] = jnp.dot(xb, w_ref[0, :, 0:W_A_COLS], preferred_element_type=F32)
    pb_ref[...] = jnp.dot(xb, w_ref[0, :, W_A_COLS:W_A_COLS + D_B_IN], preferred_element_type=F32)
    pif_ref[...] = jnp.dot(xb, w_ref[0, :, W_A_COLS + D_B_IN:W_IN_COLS], preferred_element_type=F32)


def _inproj_call(x, w, tm, layer):
    n = x.shape[0]
    return pl.pallas_call(
        _inproj_kernel,
        grid=(n // tm,),
        in_specs=[pl.BlockSpec((tm, D_MODEL), lambda i: (i, 0)),
                  pl.BlockSpec((1, D_MODEL, W_IN_COLS), lambda i: (layer, 0, 0))],
        out_specs=[pl.BlockSpec((tm, W_A_COLS), lambda i: (i, 0)),
                   pl.BlockSpec((tm, D_B_IN), lambda i: (i, 0)),
                   pl.BlockSpec((tm, W_IF_COLS), lambda i: (i, 0))],
        out_shape=[jax.ShapeDtypeStruct((n, W_A_COLS), F32),
                   jax.ShapeDtypeStruct((n, D_B_IN), F32),
                   jax.ShapeDtypeStruct((n, W_IF_COLS), F32)],
        compiler_params=pltpu.CompilerParams(dimension_semantics=("parallel",),
                                             vmem_limit_bytes=VMEM_LIMIT),
        name="inproj",
    )(x, w)


CONV_PAD = 8


def _mlstm_kernel(pa_ref, pif_ref, conv0_ref, c0_ref, n0_ref, m0_ref, convw_ref, convb_ref,
                  bif_ref, gn_ref, *rest, c, nc, nbb, chained):
    h_ref, c_out_ref, n_out_ref, m_out_ref, qk_scr, c_scr, n_scr, m_scr = rest[1:] if chained else rest
    ci = pl.program_id(1)
    rows = nbb * c
    cshift = int(math.log2(c))
    chains = [(nb, h) for nb in range(nbb) for h in range(H_A)]
    prev0 = CONV_PAD - (CONV_W - 1)

    @pl.when(ci == 0)
    def _():
        c_scr[...] = c0_ref[0]
        n_scr[...] = n0_ref[...]
        m_scr[...] = m0_ref[...]
        for nb in range(nbb):
            qk_scr[nb, prev0:CONV_PAD, :] = conv0_ref[nb]

    cat0 = lambda xs: xs[0] if len(xs) == 1 else jnp.concatenate(xs, axis=0)
    acc_l = []
    for nb in range(nbb):
        u = pa_ref[nb, :, 0:2 * D_A]
        qk_scr[nb, CONV_PAD:CONV_PAD + c, :] = u
        acc = convb_ref[...]
        for j in range(CONV_W):
            acc = acc + qk_scr[nb, prev0 + j:prev0 + j + c, :] * convw_ref[j:j + 1, :]
        qk_scr[nb, prev0:CONV_PAD, :] = u[c - (CONV_W - 1):c, :]
        acc_l.append(acc)
    qk = _silu(cat0(acc_l))

    gates = cat0([pif_ref[nb] for nb in range(nbb)]) + bif_ref[...]
    li_all = gates[:, 0:LANES]
    lf_all = _log_sigmoid(gates[:, LANES:2 * LANES])
    row = lax.broadcasted_iota(jnp.int32, (rows, rows), 0)
    col = lax.broadcasted_iota(jnp.int32, (rows, rows), 1)
    tril = jnp.where((col <= row) & ((row >> cshift) == (col >> cshift)), 1.0, 0.0)
    b_all = _dot_exact_lhs(tril, lf_all)
    z_all = li_all - b_all
    crow = lax.broadcasted_iota(jnp.int32, (c, c), 0)
    ccol = lax.broadcasted_iota(jnp.int32, (c, c), 1)
    causal = ccol <= crow
    hrow = lax.broadcasted_iota(jnp.int32, (H_A * c, LANES), 0)
    hlane = lax.broadcasted_iota(jnp.int32, (H_A * c, LANES), 1)
    head_pick = jnp.where((hrow >> cshift) == hlane, 1.0, 0.0)
    z_rows = [_dot_exact_lhs(head_pick, z_all[nb * c:(nb + 1) * c], NT) for nb in range(nbb)]
    lane1 = lax.broadcasted_iota(jnp.int32, (1, LANES), 1)

    rs = lambda nb: slice(nb * c, (nb + 1) * c)
    q_l = [qk[rs(nb), h * DH_A:(h + 1) * DH_A] for nb, h in chains]
    k_l = [qk[rs(nb), D_A + h * DH_A:D_A + (h + 1) * DH_A] * (DH_A ** -0.5) for nb, h in chains]
    v_l = [pa_ref[nb, :, 2 * D_A + h * DH_A:2 * D_A + (h + 1) * DH_A] for nb, h in chains]
    c_l = [c_scr[nb, h] for nb, h in chains]
    n_l = [n_scr[nb, h:h + 1, :] for nb, h in chains]
    b_col = [b_all[rs(nb), h:h + 1] for nb, h in chains]
    li_col = [li_all[rs(nb), h:h + 1] for nb, h in chains]
    m_prev = [m_scr[nb][:, h:h + 1] for nb, h in chains]
    dmat = [jnp.where(causal, bc + z_rows[nb][h * c:(h + 1) * c], -jnp.inf)
            for bc, (nb, h) in zip(b_col, chains)]
    m_inter = [bc + mp for bc, mp in zip(b_col, m_prev)]
    m_t = [jnp.maximum(mi, jnp.max(d, axis=-1, keepdims=True)) for mi, d in zip(m_inter, dmat)]
    qk_dot = [_dot(q, k, NT) for q, k in zip(q_l, k_l)]
    qc = [_dot(q, cm) for q, cm in zip(q_l, c_l)]
    s_l = [x * jnp.exp(d - mt) for x, d, mt in zip(qk_dot, dmat, m_t)]
    sv = [_dot(s, v) for s, v in zip(s_l, v_l)]
    b_last = [bc[c - 1:c, :] for bc in b_col]
    g_s = [bl - bc + li for bl, bc, li in zip(b_last, b_col, li_col)]
    m_new = [jnp.maximum(bl + mp, jnp.max(gs, axis=0, keepdims=True))
             for bl, mp, gs in zip(b_last, m_prev, g_s)]
    wk = [jnp.exp(gs - mn) * k for gs, mn, k in zip(g_s, m_new, k_l)]
    w_old = [jnp.exp(bl + mp - mn) for bl, mp, mn in zip(b_last, m_prev, m_new)]
    kv = [_dot(w, v, TN) for w, v in zip(wk, v_l)]
    for i, (nb, h) in enumerate(chains):
        c_scr[nb, h] = w_old[i] * c_l[i] + kv[i]
        n_scr[nb, h:h + 1, :] = w_old[i] * n_l[i] + jnp.sum(wk[i], axis=0, keepdims=True)
    for nb in range(nbb):
        m_row = m_scr[nb]
        for h in range(H_A):
            m_row = jnp.where(lane1 == h, m_new[nb * H_A + h], m_row)
        m_scr[nb] = m_row
    w_inter = [jnp.exp(mi - mt) for mi, mt in zip(m_inter, m_t)]
    qn = [jnp.sum(q * nv, axis=-1, keepdims=True) for q, nv in zip(q_l, n_l)]
    s_sum = [jnp.sum(s, axis=-1, keepdims=True) for s in s_l]
    den = [w * a + b for w, a, b in zip(w_inter, qn, s_sum)]
    hh = [(w * a + b) / jnp.maximum(jnp.abs(d), jnp.exp(-mt))
          for w, a, b, d, mt in zip(w_inter, qc, sv, den, m_t)]
    mu = [jnp.mean(x, axis=-1, keepdims=True) for x in hh]
    hc = [x - m for x, m in zip(hh, mu)]
    var = [jnp.mean(x * x, axis=-1, keepdims=True) for x in hc]
    for i, (nb, h) in enumerate(chains):
        sl = slice(h * DH_A, (h + 1) * DH_A)
        hn = hc[i] * lax.rsqrt(var[i] + GN_EPS_A)
        o_pre = pa_ref[nb, :, 3 * D_A + h * DH_A:3 * D_A + (h + 1) * DH_A]
        h_ref[nb, :, sl] = hn * gn_ref[:, sl] * _sigmoid(o_pre)

    @pl.when(ci == nc - 1)
    def _():
        c_out_ref[0] = c_scr[...]
        n_out_ref[...] = n_scr[...]
        m_out_ref[...] = m_scr[...]


def _mlstm_call(pa, pif, conv0, c_in, n0, m0, convw, convb, bif, gn, c_stack, *, batch, seq, nbb,
                layer_in, layer):
    c = math.gcd(seq, CHUNK)
    nc = seq // c
    tok = lambda w: pl.BlockSpec((nbb, c, w), lambda b, i: (b, i, 0))
    vec = lambda r, w: pl.BlockSpec((r, w), lambda b, i: (0, 0))
    st5 = lambda l: pl.BlockSpec((1, nbb, H_A, DH_A, DH_A), lambda b, i: (l, b, 0, 0, 0))
    st3 = lambda r, w: pl.BlockSpec((nbb, r, w), lambda b, i: (b, 0, 0))
    chained = c_stack is not None
    in_specs = [tok(W_A_COLS), tok(W_IF_COLS), st3(CONV_W - 1, 2 * D_A), st5(layer_in), st3(H_A, DH_A),
                st3(1, LANES), vec(CONV_W, 2 * D_A), vec(1, 2 * D_A), vec(1, W_IF_COLS), vec(1, D_A)]
    args = [pa, pif, conv0, c_in, n0, m0, convw, convb, bif, gn]
    if chained:
        in_specs.append(pl.BlockSpec(memory_space=pl.ANY))
        args.append(c_stack)
    return pl.pallas_call(
        functools.partial(_mlstm_kernel, c=c, nc=nc, nbb=nbb, chained=chained),
        grid=(batch // nbb, nc),
        in_specs=in_specs,
        out_specs=[tok(D_A), st5(layer), st3(H_A, DH_A), st3(1, LANES)],
        out_shape=[jax.ShapeDtypeStruct((batch, seq, D_A), F32),
                   jax.ShapeDtypeStruct((DEPTH, batch, H_A, DH_A, DH_A), F32),
                   jax.ShapeDtypeStruct((batch, H_A, DH_A), F32),
                   jax.ShapeDtypeStruct((batch, 1, LANES), F32)],
        scratch_shapes=[pltpu.VMEM((nbb, CONV_PAD + c, 2 * D_A), F32),
                        pltpu.VMEM((nbb, H_A, DH_A, DH_A), F32),
                        pltpu.VMEM((nbb, H_A, DH_A), F32),
                        pltpu.VMEM((nbb, 1, LANES), F32)],
        input_output_aliases={len(args) - 1: 1} if chained else {},
        compiler_params=pltpu.CompilerParams(dimension_semantics=("parallel", "arbitrary"),
                                             vmem_limit_bytes=VMEM_LIMIT),
        name="mlstm",
    )(*args)


SHIFT_PAD = 8


def _neumann_inverses(ms, size, t):
    row = lax.broadcasted_iota(jnp.int32, (size, size), 0)
    col = lax.broadcasted_iota(jnp.int32, (size, size), 1)
    eye = jnp.where(row == col, 1.0, 0.0)
    blk = min(SOLVE_BLOCK, t)
    shift = int(math.log2(blk))
    same = (row >> shift) == (col >> shift)
    ds = [jnp.where(same, m, 0.0) for m in ms]
    ps = [eye + d for d in ds]
    xs = ds
    for _ in range(shift - 1):
        xs = [_dot(x, x) for x in xs]
        ps = [p + _dot(p, x) for p, x in zip(ps, xs)]
    nblk = t // blk
    if nblk == 1:
        return ps, None
    ns = [_dot(p, m - d) for p, m, d in zip(ps, ms, ds)]
    qs = [eye + n for n in ns]
    ys = ns
    for _ in range(int(math.log2(nblk)) - 1):
        ys = [_dot(y, y) for y in ys]
        qs = [q + _dot(q, y) for q, y in zip(qs, ys)]
    return ps, qs


def _rwkv_kernel(pb_ref, shift0_ref, s0_ref, mu_ref, w0a0_ref, wlora_ref, g2_ref, kk_ref, ka_ref,
                 rk_ref, lng_ref, lnb_ref, *rest, t, nc, nbb, chained):
    h_ref, s_out_ref, sh_scr, s_scr = rest[1:] if chained else rest
    ci = pl.program_id(1)
    rows = nbb * t
    t2 = 2 * t
    groups = [(nb, pr) for nb in range(nbb) for pr in range(N_PAIRS)]

    brow = lax.broadcasted_iota(jnp.int32, (LANES, LANES), 0)
    bcol = lax.broadcasted_iota(jnp.int32, (LANES, LANES), 1)
    blockdiag = (brow >= DH_B) == (bcol >= DH_B)
    bd2 = jnp.where(blockdiag, 1.0, 0.0).astype(BF16)

    @pl.when(ci == 0)
    def _():
        er = lax.broadcasted_iota(jnp.int32, (DH_B, LANES), 0)
        ec = lax.broadcasted_iota(jnp.int32, (DH_B, LANES), 1)
        dup_cols = jnp.where((ec & (DH_B - 1)) == er, 1.0, 0.0)
        for nb in range(nbb):
            sh_scr[nb, SHIFT_PAD - 1:SHIFT_PAD, :] = shift0_ref[nb]
            for pr in range(N_PAIRS):
                x = s0_ref[0, nb, 2 * pr:2 * pr + 2].reshape(LANES, DH_B)
                s_scr[nb * N_PAIRS + pr] = jnp.where(blockdiag, _dot_exact_rhs(x, dup_cols), 0.0)

    p_l, prev_l = [], []
    for nb in range(nbb):
        p_nb = pb_ref[nb]
        sh_scr[nb, SHIFT_PAD:SHIFT_PAD + t, :] = p_nb
        prev_l.append(sh_scr[nb, SHIFT_PAD - 1:SHIFT_PAD - 1 + t, :])
        sh_scr[nb, SHIFT_PAD - 1:SHIFT_PAD, :] = p_nb[t - 1:t, :]
        p_l.append(p_nb)
    cat0 = lambda xs: xs[0] if len(xs) == 1 else jnp.concatenate(xs, axis=0)
    p = cat0(p_l)
    prev = cat0(prev_l)
    pb = p + (prev - p) * mu_ref[...]

    r = pb[:, 0:D_B]
    kr = pb[:, D_B:2 * D_B]
    vr = pb[:, 2 * D_B:3 * D_B]
    lora_in = pb[:, 3 * D_B:3 * D_B + LANES]
    gl = pb[:, 3 * D_B + LANES:3 * D_B + 2 * LANES]
    lane = lax.broadcasted_iota(jnp.int32, (rows, LANES), 1)
    lora_act = jnp.where(lane < R_DECAY, jnp.tanh(lora_in), lora_in)
    z = _dot(lora_act, wlora_ref[...]) + w0a0_ref[...]
    ld = -jnp.exp(_log_sigmoid(z[:, 0:D_B]) - 0.5)
    a = _sigmoid(z[:, D_B:2 * D_B])
    g = _dot(_sigmoid(gl), g2_ref[...])

    def seg_sum(x):
        xr = jnp.concatenate([x[:, q * LANES:(q + 1) * LANES] for q in range(N_PAIRS)], axis=0)
        hi = xr.astype(BF16)
        lo = (xr - hi.astype(F32)).astype(BF16)
        s = (jnp.dot(hi, bd2, preferred_element_type=F32) + jnp.dot(lo, bd2, preferred_element_type=F32))
        return jnp.concatenate([s[q * rows:(q + 1) * rows] for q in range(N_PAIRS)], axis=1)

    kk_raw = kr * kk_ref[...]
    kk = kk_raw / jnp.maximum(jnp.sqrt(seg_sum(kk_raw * kk_raw)), 1e-12)
    k2 = kr * (1.0 + (a - 1.0) * ka_ref[...])
    bonus = seg_sum(r * k2 * rk_ref[...]) * vr

    row = lax.broadcasted_iota(jnp.int32, (rows, rows), 0)
    col = lax.broadcasted_iota(jnp.int32, (rows, rows), 1)
    tshift = int(math.log2(t))
    tril = jnp.where((col <= row) & ((row >> tshift) == (col >> tshift)), 1.0, 0.0)
    lc = _dot_exact_lhs(tril, ld)
    lends = [lc[(nb + 1) * t - 1:(nb + 1) * t, :] for nb in range(nbb)]
    lend_rows = cat0([jnp.broadcast_to(le, (t, D_B)) for le in lends])
    e_nc = jnp.exp(-lc)
    e_end = jnp.exp(lend_rows - lc)
    b_raw = kk * a
    at = -kk * jnp.exp(lc - ld)
    rt = r * jnp.exp(lc)
    kt = k2 * e_nc
    bt = b_raw * e_nc
    kend = k2 * e_end
    bend = b_raw * e_end

    srow = lax.broadcasted_iota(jnp.int32, (t2, t2), 0)
    scol = lax.broadcasted_iota(jnp.int32, (t2, t2), 1)
    strict = scol < srow
    incl = scol <= srow
    lane2 = lax.broadcasted_iota(jnp.int32, (t2, LANES), 1)
    row2 = lax.broadcasted_iota(jnp.int32, (t2, LANES), 0)
    head_sel = (lane2 >= DH_B) == (row2 >= t)
    first_head = lax.broadcasted_iota(jnp.int32, (t, LANES), 1) < DH_B

    def blk(x, nb, pr):
        return x[nb * t:(nb + 1) * t, pr * LANES:(pr + 1) * LANES]

    def stack(x):
        return jnp.where(head_sel, jnp.concatenate([x, x], axis=0), 0.0)

    lhs = [jnp.concatenate([stack(blk(at, *gp)), stack(blk(rt, *gp))], axis=0) for gp in groups]
    rk_s = [stack(blk(kt, *gp)) for gp in groups]
    rb_s = [stack(blk(bt, *gp)) for gp in groups]
    v_p = [blk(vr, *gp) for gp in groups]
    v_s = [jnp.concatenate([v, v], axis=0) for v in v_p]
    s_p = [s_scr[i] for i in range(len(groups))]
    gk = [_dot(l, x, NT) for l, x in zip(lhs, rk_s)]
    gb = [_dot(l, x, NT) for l, x in zip(lhs, rb_s)]
    xs = [_dot(l, s, NT) for l, s in zip(lhs, s_p)]
    m_ab = [jnp.where(strict, x[0:t2], 0.0) for x in gb]
    ps, qs = _neumann_inverses(m_ab, t2, t)
    rhs = [x[0:t2] + _dot(jnp.where(strict, y[0:t2], 0.0), v) for x, y, v in zip(xs, gk, v_s)]
    sa_s = [_dot(pm, x) for pm, x in zip(ps, rhs)]
    if qs is not None:
        sa_s = [_dot(qm, x) for qm, x in zip(qs, sa_s)]
    o_s = [x[t2:] + _dot(jnp.where(incl, y[t2:], 0.0), v) + _dot(jnp.where(incl, w[t2:], 0.0), sa)
           for x, y, w, v, sa in zip(xs, gk, gb, v_s, sa_s)]
    o_p = [jnp.where(first_head, o[0:t], o[t:t2]) for o in o_s]
    sa_p = [jnp.where(first_head, sa[0:t], sa[t:t2]) for sa in sa_s]
    upd = [_dot(jnp.concatenate([v, sa], axis=0),
                jnp.concatenate([blk(kend, *gp), blk(bend, *gp)], axis=0), TN)
           for v, sa, gp in zip(v_p, sa_p, groups)]
    for i, (nb, pr) in enumerate(groups):
        dec = jnp.exp(lends[nb][:, pr * LANES:(pr + 1) * LANES])
        s_scr[i] = s_p[i] * dec + jnp.where(blockdiag, upd[i], 0.0)

    o = cat0([jnp.concatenate(o_p[nb * N_PAIRS:(nb + 1) * N_PAIRS], axis=1) for nb in range(nbb)])
    mean = seg_sum(o) * (1.0 / DH_B)
    oc = o - mean
    var = seg_sum(oc * oc) * (1.0 / DH_B)
    out = oc * lax.rsqrt(var + GN_EPS_B) * lng_ref[...] + lnb_ref[...]
    res = (out + bonus) * g
    for nb in range(nbb):
        h_ref[nb] = res[nb * t:(nb + 1) * t, :]

    @pl.when(ci == nc - 1)
    def _():
        fr = lax.broadcasted_iota(jnp.int32, (LANES, DH_B), 0)
        fc = lax.broadcasted_iota(jnp.int32, (LANES, DH_B), 1)
        dup_rows = jnp.where((fr & (DH_B - 1)) == fc, 1.0, 0.0)
        for i, (nb, pr) in enumerate(groups):
            packed = _dot_exact_rhs(s_scr[i], dup_rows)
            s_out_ref[0, nb, 2 * pr:2 * pr + 2] = packed.reshape(2, DH_B, DH_B)


def _rwkv_call(pb, shift0, s_in, mu, w0a0, wlora, g2, kk, ka, rk, lng, lnb, s_stack, *, batch, seq, nbb,
               layer_in, layer):
    t = math.gcd(seq, CHUNK)
    nc = seq // t
    tok = lambda w: pl.BlockSpec((nbb, t, w), lambda b, i: (b, i, 0))
    vec = lambda r, w: pl.BlockSpec((r, w), lambda b, i: (0, 0))
    st5 = lambda l: pl.BlockSpec((1, nbb, H_B, DH_B, DH_B), lambda b, i: (l, b, 0, 0, 0))
    chained = s_stack is not None
    in_specs = [tok(D_B_IN), pl.BlockSpec((nbb, 1, D_B_IN), lambda b, i: (b, 0, 0)), st5(layer_in),
                vec(1, D_B_IN), vec(1, 2 * D_B), vec(LANES, 2 * D_B), vec(R_GATE, D_B),
                vec(1, D_B), vec(1, D_B), vec(1, D_B), vec(1, D_B), vec(1, D_B)]
    args = [pb, shift0, s_in, mu, w0a0, wlora, g2, kk, ka, rk, lng, lnb]
    if chained:
        in_specs.append(pl.BlockSpec(memory_space=pl.ANY))
        args.append(s_stack)
    return pl.pallas_call(
        functools.partial(_rwkv_kernel, t=t, nc=nc, nbb=nbb, chained=chained),
        grid=(batch // nbb, nc),
        in_specs=in_specs,
        out_specs=[tok(D_B), st5(layer)],
        out_shape=[jax.ShapeDtypeStruct((batch, seq, D_B), F32),
                   jax.ShapeDtypeStruct((DEPTH, batch, H_B, DH_B, DH_B), F32)],
        scratch_shapes=[pltpu.VMEM((nbb, SHIFT_PAD + t, D_B_IN), F32),
                        pltpu.VMEM((nbb * N_PAIRS, LANES, LANES), F32)],
        input_output_aliases={len(args) - 1: 1} if chained else {},
        compiler_params=pltpu.CompilerParams(dimension_semantics=("parallel", "arbitrary"),
                                             vmem_limit_bytes=VMEM_LIMIT),
        name="rwkv",
    )(*args)


ROUTE_ROWS = 8


def _outproj_kernel(x_ref, ha_ref, hb_ref, wo_ref, g_ref, b_ref, wr_ref, br_ref, x1_ref, route_ref,
                    cnt_ref, *, tm):
    y = _dot(ha_ref[...], wo_ref[0, 0:D_A, :]) + _dot(hb_ref[...], wo_ref[0, D_A:D_A + D_B, :])
    x1 = _layer_norm(ALPHA * x_ref[...] + y, g_ref[...], b_ref[...], LN_EPS)
    x1_ref[...] = x1
    logits = lax.dot_general(wr_ref[...], x1, NT, precision=lax.Precision.HIGHEST,
                             preferred_element_type=F32)
    mx = jnp.max(logits, axis=0, keepdims=True)
    ex = jnp.exp(logits - mx)
    probs = ex / jnp.sum(ex, axis=0, keepdims=True)
    sel = probs + br_ref[...]
    neg = -jnp.inf

    def top2(rows):
        m1 = functools.reduce(jnp.maximum, rows)
        i1 = jnp.full(m1.shape, len(rows) - 1, jnp.int32)
        for j in range(len(rows) - 2, -1, -1):
            i1 = jnp.where(rows[j] == m1, j, i1)
        rest = [jnp.where(i1 == j, neg, rows[j]) for j in range(len(rows))]
        m2 = functools.reduce(jnp.maximum, rest)
        i2 = jnp.full(m2.shape, len(rows) - 1, jnp.int32)
        for j in range(len(rows) - 2, -1, -1):
            i2 = jnp.where(rest[j] == m2, j, i2)
        return m1, i1, m2, i2

    g_score, g_i1, g_i2 = [], [], []
    for gidx in range(N_GROUPS):
        rows = [sel[gidx * EXPERTS_PER_GROUP + j:gidx * EXPERTS_PER_GROUP + j + 1, :]
                for j in range(EXPERTS_PER_GROUP)]
        m1, i1, m2, i2 = top2(rows)
        g_score.append(m1 + m2)
        g_i1.append(i1)
        g_i2.append(i2)
    best = functools.reduce(jnp.maximum, g_score)
    grp = jnp.full(best.shape, N_GROUPS - 1, jnp.int32)
    for gidx in range(N_GROUPS - 2, -1, -1):
        grp = jnp.where(g_score[gidx] == best, gidx, grp)
    l1 = g_i1[N_GROUPS - 1]
    l2 = g_i2[N_GROUPS - 1]
    for gidx in range(N_GROUPS - 2, -1, -1):
        l1 = jnp.where(grp == gidx, g_i1[gidx], l1)
        l2 = jnp.where(grp == gidx, g_i2[gidx], l2)
    e1 = grp * EXPERTS_PER_GROUP + l1
    e2 = grp * EXPERTS_PER_GROUP + l2
    e_iota = lax.broadcasted_iota(jnp.int32, (N_EXPERTS, tm), 0)
    p1 = jnp.sum(jnp.where(e_iota == e1, probs, 0.0), axis=0, keepdims=True)
    p2 = jnp.sum(jnp.where(e_iota == e2, probs, 0.0), axis=0, keepdims=True)
    tot = p1 + p2
    r8 = lax.broadcasted_iota(jnp.int32, (ROUTE_ROWS, tm), 0)
    route_ref[...] = jnp.where(r8 == 0, e1.astype(F32),
                               jnp.where(r8 == 1, e2.astype(F32),
                                         jnp.where(r8 == 2, p1 / tot, jnp.where(r8 == 3, p2 / tot, 0.0))))
    picked = jnp.where((e_iota == e1) | (e_iota == e2), 1.0, 0.0)
    cnt_ref[0] = jnp.broadcast_to(jnp.sum(picked, axis=1, keepdims=True), (N_EXPERTS, LANES))


def _outproj_call(x, ha, hb, wo, g, b, wr_t, br, tm, layer):
    n = x.shape[0]
    row = lambda w: pl.BlockSpec((tm, w), lambda i: (i, 0))
    full = lambda r, w: pl.BlockSpec((r, w), lambda i: (0, 0))
    return pl.pallas_call(
        functools.partial(_outproj_kernel, tm=tm),
        grid=(n // tm,),
        in_specs=[row(D_MODEL), row(D_A), row(D_B),
                  pl.BlockSpec((1, D_MODEL, D_MODEL), lambda i: (layer, 0, 0)), full(1, D_MODEL),
                  full(1, D_MODEL), full(N_EXPERTS, D_MODEL), full(N_EXPERTS, 1)],
        out_specs=[row(D_MODEL), pl.BlockSpec((ROUTE_ROWS, tm), lambda i: (0, i)),
                   pl.BlockSpec((1, N_EXPERTS, LANES), lambda i: (i, 0, 0))],
        out_shape=[jax.ShapeDtypeStruct((n, D_MODEL), F32), jax.ShapeDtypeStruct((ROUTE_ROWS, n), F32),
                   jax.ShapeDtypeStruct((n // tm, N_EXPERTS, LANES), F32)],
        compiler_params=pltpu.CompilerParams(dimension_semantics=("parallel",),
                                             vmem_limit_bytes=VMEM_LIMIT),
        name="outproj",
    )(x, ha, hb, wo, g, b, wr_t, br)


MOE_BM = 128
MOE_CH = 512
MOE_CCH = 1024
MOE_NBUF = 4


def _moe_max_blocks(t):
    return -(-2 * t // MOE_BM) + N_EXPERTS - 1


def _moe_sort_kernel(nblk_ref, x_ref, route_ref, padoff_ref, tri_ref, xs_ref, cols_ref, *, t, n_tiles,
                     n_chunks):
    i = pl.program_id(0)

    @pl.when(i >= n_tiles)
    def _():
        xs_ref[...] = jnp.zeros_like(xs_ref)

    @pl.when(i < n_tiles)
    def _():
        s_used = nblk_ref[jnp.minimum(i, n_tiles - 1)] * MOE_BM
        xb = x_ref[...].astype(BF16)
        route = route_ref[...]
        e_iota = lax.broadcasted_iota(jnp.int32, (N_EXPERTS, t), 0)
        a1 = e_iota == route[0:1].astype(jnp.int32)
        a2 = e_iota == route[1:2].astype(jnp.int32)
        picked = jnp.where(a1 | a2, 1.0, 0.0).astype(BF16)
        rank = jnp.dot(picked, tri_ref[...], preferred_element_type=F32)
        base = padoff_ref[0][:, 0:1] + rank
        slot1 = jnp.sum(jnp.where(a1, base, 0.0), axis=0, keepdims=True)
        slot2 = jnp.sum(jnp.where(a2, base, 0.0), axis=0, keepdims=True)
        r8 = lax.broadcasted_iota(jnp.int32, (ROUTE_ROWS, t), 0)
        rows = jnp.where(r8 == 0, slot1, jnp.where(r8 == 1, slot2,
                                                   jnp.where(r8 == 2, route[2:3],
                                                             jnp.where(r8 == 3, route[3:4], 0.0))))
        padded = jnp.concatenate([rows, jnp.zeros((LANES - ROUTE_ROWS, t), F32)], axis=0)
        cols_ref[...] = padded.T
        s1 = slot1.astype(jnp.int32)
        s2 = slot2.astype(jnp.int32)
        for c in range(n_chunks):
            used = c * MOE_CH < s_used

            @pl.when(used)
            def _():
                s_iota = lax.broadcasted_iota(jnp.int32, (MOE_CH, t), 0) + c * MOE_CH
                onehot = jnp.where((s_iota == s1) | (s_iota == s2), 1.0, 0.0).astype(BF16)
                xs_ref[0, c * MOE_CH:(c + 1) * MOE_CH, :] = jnp.dot(
                    onehot, xb, preferred_element_type=F32).astype(BF16)

            @pl.when(jnp.logical_not(used))
            def _():
                xs_ref[0, c * MOE_CH:(c + 1) * MOE_CH, :] = jnp.zeros((MOE_CH, D_MODEL), BF16)


def _moe_expert_kernel(sblk_ref, ebase_ref, ecnt_ref, xs_hbm, wg_ref, wu_ref, wd_ref, ys_hbm,
                       wg_scr, wu_scr, wd_scr, xbuf, ybuf, in_sem, out_sem):
    e = pl.program_id(0)
    base = ebase_ref[e]
    cnt = ecnt_ref[e]
    wg_scr[...] = wg_ref[0, 0].astype(BF16)
    wu_scr[...] = wu_ref[0, 0].astype(BF16)
    wd_scr[...] = wd_ref[0, 0].astype(BF16)

    def rows(k):
        return pl.ds(pl.multiple_of(sblk_ref[base + k] * MOE_BM, MOE_BM), MOE_BM)

    def in_copy(k, slot):
        return pltpu.make_async_copy(xs_hbm.at[rows(k), :], xbuf.at[slot], in_sem.at[slot])

    def out_copy(k, slot):
        return pltpu.make_async_copy(ybuf.at[slot], ys_hbm.at[rows(k), :], out_sem.at[slot])

    for i in range(MOE_NBUF - 1):
        @pl.when(i < cnt)
        def _():
            in_copy(i, i).start()

    def block(k, carry):
        slot = k % MOE_NBUF
        in_copy(k, slot).wait()
        ahead = k + MOE_NBUF - 1

        @pl.when(ahead < cnt)
        def _():
            in_copy(ahead, ahead % MOE_NBUF).start()

        @pl.when(k >= MOE_NBUF)
        def _():
            out_copy(k - MOE_NBUF, slot).wait()

        xblk = xbuf[slot]
        hid = _silu(jnp.dot(xblk, wg_scr[...], preferred_element_type=F32)) * jnp.dot(
            xblk, wu_scr[...], preferred_element_type=F32)
        y = jnp.dot(hid.astype(BF16), wd_scr[...], preferred_element_type=F32)
        ybuf[slot] = y.astype(BF16)
        out_copy(k, slot).start()
        return carry

    lax.fori_loop(0, cnt, block, 0)

    for i in range(MOE_NBUF):
        @pl.when(cnt > i)
        def _():
            out_copy(cnt - 1 - i, (cnt - 1 - i) % MOE_NBUF).wait()


def _moe_combine_kernel(nblk_ref, x_ref, cols_ref, ys_ref, g_ref, b_ref, o_ref, acc_scr, *, t, n_chunks):
    s_used = nblk_ref[pl.program_id(0)] * MOE_BM
    cols = cols_ref[...]
    s1c = cols[:, 0:1].astype(jnp.int32)
    s2c = cols[:, 1:2].astype(jnp.int32)
    w1c = cols[:, 2:3]
    w2c = cols[:, 3:4]
    for c in range(n_chunks):
        @pl.when(c * MOE_CCH < s_used)
        def _():
            l_iota = lax.broadcasted_iota(jnp.int32, (t, MOE_CCH), 1) + c * MOE_CCH
            weighted = (jnp.where(l_iota == s1c, w1c, 0.0) + jnp.where(l_iota == s2c, w2c, 0.0)).astype(BF16)
            part = jnp.dot(weighted, ys_ref[0, c * MOE_CCH:(c + 1) * MOE_CCH, :],
                           preferred_element_type=F32)
            if c == 0:
                acc_scr[...] = part
            else:
                acc_scr[...] += part
    o_ref[...] = _layer_norm(ALPHA * x_ref[...] + acc_scr[...], g_ref[...], b_ref[...], LN_EPS)


def _moe_schedule(counts, tiles_per, maxb, bpt):
    cnt = counts[:, :, 0].astype(jnp.int32)
    cnt = cnt.reshape(-1, tiles_per, N_EXPERTS).sum(axis=1)
    n_tiles = cnt.shape[0]
    nblk = (cnt + MOE_BM - 1) // MOE_BM
    end = jnp.cumsum(nblk, axis=-1)
    start = end - nblk
    total = end[:, -1]
    padoff = jnp.broadcast_to((start * MOE_BM).astype(F32)[:, :, None], (n_tiles, N_EXPERTS, LANES))
    per_e = jnp.sum(nblk, axis=0)
    e_end = jnp.cumsum(per_e)
    e_base = e_end - per_e
    e_ids = jnp.arange(N_EXPERTS, dtype=jnp.int32)
    s_idx = jnp.arange(n_tiles * maxb, dtype=jnp.int32)
    last_e = jnp.max(jnp.where(per_e > 0, e_ids, 0))
    se = jnp.minimum(jnp.sum((s_idx[:, None] >= e_end[None, :]).astype(jnp.int32), axis=-1), last_e)
    oh_e = se[:, None] == e_ids[None, :]
    r = s_idx - jnp.sum(jnp.where(oh_e, e_base[None, :], 0), axis=-1)
    of_e = lambda a: jnp.sum(jnp.where(oh_e[:, None, :], a[None, :, :], 0), axis=-1)
    incl_s = of_e(jnp.cumsum(nblk, axis=0))
    tile = jnp.minimum(jnp.sum((incl_s <= r[:, None]).astype(jnp.int32), axis=-1), n_tiles - 1)
    oh_t = tile[:, None] == jnp.arange(n_tiles, dtype=jnp.int32)[None, :]
    of_t = lambda a: jnp.sum(jnp.where(oh_t, a, 0), axis=-1)
    j = of_t(of_e(start)) + r - (of_t(incl_s) - of_t(of_e(nblk)))
    sblk = jnp.where(s_idx < e_end[-1], tile * bpt + j, n_tiles * bpt)
    return total, padoff, sblk, e_base, per_e


def _moe_call(x, route, counts, wg, wu, wd, g, b, *, layer, tm):
    n = x.shape[0]
    t = 1024 if n % 1024 == 0 else n
    n_tiles = n // t
    maxb = _moe_max_blocks(t)
    s_alloc = -(-maxb * MOE_BM // MOE_CCH) * MOE_CCH
    n_chunks = s_alloc // MOE_CH
    bpt = s_alloc // MOE_BM
    nblk, padoff, sblk, e_base, e_cnt = _moe_schedule(counts, t // tm, maxb, bpt)
    tri = jnp.triu(jnp.ones((t, t), BF16), k=1)
    last = n_tiles - 1
    xs, cols = pl.pallas_call(
        functools.partial(_moe_sort_kernel, t=t, n_tiles=n_tiles, n_chunks=n_chunks),
        grid_spec=pltpu.PrefetchScalarGridSpec(
            num_scalar_prefetch=1,
            grid=(n_tiles + 1,),
            in_specs=[pl.BlockSpec((t, D_MODEL), lambda i, nb: (jnp.minimum(i, last), 0)),
                      pl.BlockSpec((ROUTE_ROWS, t), lambda i, nb: (0, jnp.minimum(i, last))),
                      pl.BlockSpec((1, N_EXPERTS, LANES), lambda i, nb: (jnp.minimum(i, last), 0, 0)),
                      pl.BlockSpec((t, t), lambda i, nb: (0, 0))],
            out_specs=[pl.BlockSpec((1, s_alloc, D_MODEL), lambda i, nb: (i, 0, 0)),
                       pl.BlockSpec((t, LANES), lambda i, nb: (jnp.minimum(i, last), 0))]),
        out_shape=[jax.ShapeDtypeStruct((n_tiles + 1, s_alloc, D_MODEL), BF16),
                   jax.ShapeDtypeStruct((n, LANES), F32)],
        compiler_params=pltpu.CompilerParams(dimension_semantics=("arbitrary",),
                                             vmem_limit_bytes=VMEM_LIMIT),
        name="moe_sort",
    )(nblk, x, route, padoff, tri)
    wspec = lambda r, c: pl.BlockSpec((1, 1, r, c), lambda e, sb, eb, ec: (layer, e, 0, 0))
    ys = pl.pallas_call(
        _moe_expert_kernel,
        grid_spec=pltpu.PrefetchScalarGridSpec(
            num_scalar_prefetch=3,
            grid=(N_EXPERTS,),
            in_specs=[pl.BlockSpec(memory_space=pl.ANY),
                      wspec(D_MODEL, D_EXP), wspec(D_MODEL, D_EXP), wspec(D_EXP, D_MODEL)],
            out_specs=pl.BlockSpec(memory_space=pl.ANY),
            scratch_shapes=[pltpu.VMEM((D_MODEL, D_EXP), BF16), pltpu.VMEM((D_MODEL, D_EXP), BF16),
                            pltpu.VMEM((D_EXP, D_MODEL), BF16),
                            pltpu.VMEM((MOE_NBUF, MOE_BM, D_MODEL), BF16),
                            pltpu.VMEM((MOE_NBUF, MOE_BM, D_MODEL), BF16),
                            pltpu.SemaphoreType.DMA((MOE_NBUF,)), pltpu.SemaphoreType.DMA((MOE_NBUF,))]),
        out_shape=jax.ShapeDtypeStruct(((n_tiles + 1) * s_alloc, D_MODEL), BF16),
        input_output_aliases={3: 0},
        compiler_params=pltpu.CompilerParams(dimension_semantics=("arbitrary",),
                                             vmem_limit_bytes=VMEM_LIMIT),
        name="moe_experts",
    )(sblk, e_base, e_cnt, xs.reshape((n_tiles + 1) * s_alloc, D_MODEL), wg, wu, wd)
    vec = pl.BlockSpec((1, D_MODEL), lambda i, nb: (0, 0))
    return pl.pallas_call(
        functools.partial(_moe_combine_kernel, t=t, n_chunks=s_alloc // MOE_CCH),
        grid_spec=pltpu.PrefetchScalarGridSpec(
            num_scalar_prefetch=1,
            grid=(n_tiles,),
            in_specs=[pl.BlockSpec((t, D_MODEL), lambda i, nb: (i, 0)),
                      pl.BlockSpec((t, LANES), lambda i, nb: (i, 0)),
                      pl.BlockSpec((1, s_alloc, D_MODEL), lambda i, nb: (i, 0, 0)), vec, vec],
            out_specs=pl.BlockSpec((t, D_MODEL), lambda i, nb: (i, 0)),
            scratch_shapes=[pltpu.VMEM((t, D_MODEL), F32)]),
        out_shape=jax.ShapeDtypeStruct((n, D_MODEL), F32),
        compiler_params=pltpu.CompilerParams(dimension_semantics=("parallel",),
                                             vmem_limit_bytes=VMEM_LIMIT),
        name="moe_combine",
    )(nblk, x, cols, ys.reshape(n_tiles + 1, s_alloc, D_MODEL), g, b)


def _row_tile(n):
    return 512 if n % 512 == 0 else n


def _seqs_per_step(batch, want):
    return want if batch % want == 0 else 1


def kernel(x_prompt, x_sample, state_mlstm_C, state_mlstm_n, state_mlstm_m, state_mlstm_conv, state_rwkv_S, state_rwkv_shift, ln0_g, ln0_b, w_in, conv_w, conv_b, b_i, b_f, gn_a_g, mu_shift, w0, w2, a0, a2, g2, k_k, k_a, r_k, lnx_g, lnx_b, w_out, ln1_g, ln1_b, w_router, b_router, we_gate, we_up, we_down, ln2_g, ln2_b):
    d_a_in = 4 * D_A + 2 * H_A
    zpad = jnp.zeros((DEPTH, D_MODEL, LANES - H_A), F32)
    w_cat = jnp.concatenate(
        [w_in[:, :, 0:4 * D_A], w_in[:, :, d_a_in:], w_in[:, :, 4 * D_A:4 * D_A + H_A], zpad,
         w_in[:, :, 4 * D_A + H_A:d_a_in], zpad], axis=-1).astype(BF16)
    bif = jnp.zeros((DEPTH, 1, W_IF_COLS), F32)
    bif = bif.at[:, 0, 0:H_A].set(b_i).at[:, 0, LANES:LANES + H_A].set(b_f)
    wlora = jnp.zeros((DEPTH, LANES, 2 * D_B), F32)
    wlora = wlora.at[:, 0:R_DECAY, 0:D_B].set(w2).at[:, R_DECAY:, D_B:].set(a2).astype(BF16)
    w0a0 = jnp.concatenate([w0, a0], axis=-1)[:, None, :]
    g2b = g2.astype(BF16)
    wob = w_out.astype(BF16)
    wr_t = w_router.T
    br = b_router[:, None]
    r1 = lambda v: v[None, :]

    def run(x3, states):
        nb, seq, _ = x3.shape
        n = nb * seq
        tm = _row_tile(n)
        rwkv_nbb = _seqs_per_step(nb, 4 if seq >= CHUNK else 8)
        mlstm_nbb = _seqs_per_step(nb, 2 if seq >= CHUNK else 8)
        x = _ln_call(x3.reshape(n, D_MODEL), r1(ln0_g), r1(ln0_b), tm)
        outs = []
        c_stack = jnp.zeros((DEPTH, nb, H_A, DH_A, DH_A), F32)
        s_stack = jnp.zeros((DEPTH, nb, H_B, DH_B, DH_B), F32)
        for l in range(DEPTH):
            if states is None:
                c_in = jnp.zeros((1, nb, H_A, DH_A, DH_A), F32)
                s_in = jnp.zeros((1, nb, H_B, DH_B, DH_B), F32)
                layer_in = 0
                n0 = jnp.zeros((nb, H_A, DH_A), F32)
                m0 = jnp.zeros((nb, H_A), F32)
                conv0 = jnp.zeros((nb, CONV_W - 1, 2 * D_A), F32)
                shift0 = jnp.zeros((nb, D_B_IN), F32)
            else:
                c_in, s_in, layer_in = states[0], states[4], l
                n0, m0, conv0, shift0 = (states[k][l] for k in (1, 2, 3, 5))
            m0p = jnp.zeros((nb, 1, LANES), F32).at[:, 0, 0:H_A].set(m0)
            pa, pb, pif = _inproj_call(x, w_cat, tm, l)
            ha, c_stack, n_new, m_new = _mlstm_call(
                pa.reshape(nb, seq, W_A_COLS), pif.reshape(nb, seq, W_IF_COLS), conv0, c_in, n0, m0p,
                conv_w[l], r1(conv_b[l]), bif[l], r1(gn_a_g[l]), c_stack, batch=nb, seq=seq,
                nbb=mlstm_nbb, layer_in=layer_in, layer=l)
            ha = ha.reshape(n, D_A)
            hb, s_stack = _rwkv_call(
                pb.reshape(nb, seq, D_B_IN), shift0[:, None, :], s_in, r1(mu_shift[l]), w0a0[l], wlora[l],
                g2b[l], r1(k_k[l]), r1(k_a[l]), r1(r_k[l].reshape(D_B)), r1(lnx_g[l]), r1(lnx_b[l]),
                s_stack, batch=nb, seq=seq, nbb=rwkv_nbb, layer_in=layer_in, layer=l)
            hb = hb.reshape(n, D_B)
            x1, route, counts = _outproj_call(x, ha, hb, wob, r1(ln1_g[l]), r1(ln1_b[l]), wr_t, br, tm, l)
            x = _moe_call(x1, route, counts, we_gate, we_up, we_down, r1(ln2_g[l]), r1(ln2_b[l]),
                          layer=l, tm=tm)
            pa3 = pa.reshape(nb, seq, W_A_COLS)
            full = jnp.concatenate([conv0, pa3[:, :, 0:2 * D_A]], axis=1) if seq < CONV_W - 1 else pa3[:, :, 0:2 * D_A]
            conv_new = full[:, -(CONV_W - 1):, :]
            shift_new = pb.reshape(nb, seq, D_B_IN)[:, -1, :]
            outs.append((n_new, m_new[:, 0, 0:H_A], conv_new, shift_new))
        n_all, m_all, conv_all, shift_all = (jnp.stack(s) for s in zip(*outs))
        return x.reshape(nb, seq, D_MODEL), (c_stack, n_all, m_all, conv_all, s_stack, shift_all)

    y_prompt, (p_c, p_n, p_m, p_conv, p_s, p_shift) = run(x_prompt, None)
    y_sample, (s_c, s_n, s_m, s_conv, s_s, s_shift) = run(
        x_sample, (state_mlstm_C, state_mlstm_n, state_mlstm_m, state_mlstm_conv, state_rwkv_S,
                   state_rwkv_shift))
    return (y_prompt, y_sample, p_c, p_n, p_m, p_conv, p_s, p_shift, s_c, s_n, s_m, s_conv, s_s, s_shift)
```

```python
import functools
import math

import jax
import jax.numpy as jnp
from jax import lax
from jax.experimental import pallas as pl
from jax.experimental.pallas import tpu as pltpu

F32 = jnp.float32
BF16 = jnp.bfloat16

D_MODEL = 1024
DEPTH = 4
D_A = 512
D_B = 512
DH_A = 128
H_A = 4
DH_B = 64
H_B = 8
N_PAIRS = H_B // 2
CONV_W = 4
CHUNK = 64
R_DECAY = 64
R_AAA = 64
R_GATE = 128
D_B_IN = 3 * D_B + R_DECAY + R_AAA + R_GATE
N_EXPERTS = 16
N_GROUPS = 4
EXPERTS_PER_GROUP = 4
D_EXP = 512
ALPHA = (2 * DEPTH) ** 0.25
LN_EPS = 1e-5
GN_EPS_A = 1e-6
GN_EPS_B = 64e-5

LANES = 128
SOLVE_BLOCK = 16
VMEM_LIMIT = 48 * 1024 * 1024

NN = (((1,), (0,)), ((), ()))
NT = (((1,), (1,)), ((), ()))
TN = (((0,), (0,)), ((), ()))


def _dot(a, b, dims=NN):
    return lax.dot_general(a.astype(BF16), b.astype(BF16), dims, preferred_element_type=F32)


def _split3(x):
    hi = x.astype(BF16)
    r1 = x - hi.astype(F32)
    mid = r1.astype(BF16)
    lo = (r1 - mid.astype(F32)).astype(BF16)
    return hi, mid, lo


def _dot_exact_lhs(a, x, dims=NN):
    a = a.astype(BF16)
    hi, mid, lo = _split3(x)
    d = lambda p: lax.dot_general(a, p, dims, preferred_element_type=F32)
    return d(hi) + d(mid) + d(lo)


def _dot_exact_rhs(x, b, dims=NN):
    b = b.astype(BF16)
    hi, mid, lo = _split3(x)
    d = lambda p: lax.dot_general(p, b, dims, preferred_element_type=F32)
    return d(hi) + d(mid) + d(lo)


def _layer_norm(x, g, b, eps):
    mu = jnp.mean(x, axis=-1, keepdims=True)
    xc = x - mu
    var = jnp.mean(xc * xc, axis=-1, keepdims=True)
    return xc * lax.rsqrt(var + eps) * g + b


def _sigmoid(x):
    return 1.0 / (1.0 + jnp.exp(-x))


def _log_sigmoid(x):
    return jnp.minimum(x, 0.0) - jnp.log(1.0 + jnp.exp(-jnp.abs(x)))


def _silu(x):
    return x * _sigmoid(x)


def _ln_kernel(x_ref, g_ref, b_ref, o_ref):
    o_ref[...] = _layer_norm(x_ref[...], g_ref[...], b_ref[...], LN_EPS)


def _ln_call(x, g, b, tm):
    n, d = x.shape
    row = pl.BlockSpec((tm, d), lambda i: (i, 0))
    vec = pl.BlockSpec((1, d), lambda i: (0, 0))
    return pl.pallas_call(
        _ln_kernel,
        grid=(n // tm,),
        in_specs=[row, vec, vec],
        out_specs=row,
        out_shape=jax.ShapeDtypeStruct((n, d), F32),
        compiler_params=pltpu.CompilerParams(dimension_semantics=("parallel",)),
        name="ln0",
    )(x, g, b)


W_A_COLS = 4 * D_A
W_IF_COLS = 2 * LANES
W_IN_COLS = W_A_COLS + D_B_IN + W_IF_COLS


def _inproj_kernel(x_ref, w_ref, pa_ref, pb_ref, pif_ref):
    xb = x_ref[...].astype(BF16)
    pa_ref[...] = jnp.dot(xb, w_ref[0, :, 0:W_A_COLS], preferred_element_type=F32)
    pb_ref[...] = jnp.dot(xb, w_ref[0, :, W_A_COLS:W_A_COLS + D_B_IN], preferred_element_type=F32)
    pif_ref[...] = jnp.dot(xb, w_ref[0, :, W_A_COLS + D_B_IN:W_IN_COLS], preferred_element_type=F32)


def _inproj_call(x, w, tm, layer):
    n = x.shape[0]
    return pl.pallas_call(
        _inproj_kernel,
        grid=(n // tm,),
        in_specs=[pl.BlockSpec((tm, D_MODEL), lambda i: (i, 0)),
                  pl.BlockSpec((1, D_MODEL, W_IN_COLS), lambda i: (layer, 0, 0))],
        out_specs=[pl.BlockSpec((tm, W_A_COLS), lambda i: (i, 0)),
                   pl.BlockSpec((tm, D_B_IN), lambda i: (i, 0)),
                   pl.BlockSpec((tm, W_IF_COLS), lambda i: (i, 0))],
        out_shape=[jax.ShapeDtypeStruct((n, W_A_COLS), F32),
                   jax.ShapeDtypeStruct((n, D_B_IN), F32),
                   jax.ShapeDtypeStruct((n, W_IF_COLS), F32)],
        compiler_params=pltpu.CompilerParams(dimension_semantics=("parallel",),
                                             vmem_limit_bytes=VMEM_LIMIT),
        name="inproj",
    )(x, w)


CONV_PAD = 8


def _mlstm_kernel(pa_ref, pif_ref, conv0_ref, c0_ref, n0_ref, m0_ref, convw_ref, convb_ref,
                  bif_ref, gn_ref, *rest, c, nc, nbb, chained):
    h_ref, c_out_ref, n_out_ref, m_out_ref, qk_scr, c_scr, n_scr, m_scr = rest[1:] if chained else rest
    ci = pl.program_id(1)
    rows = nbb * c
    cshift = int(math.log2(c))
    chains = [(nb, h) for nb in range(nbb) for h in range(H_A)]
    prev0 = CONV_PAD - (CONV_W - 1)

    @pl.when(ci == 0)
    def _():
        c_scr[...] = c0_ref[0]
        n_scr[...] = n0_ref[...]
        m_scr[...] = m0_ref[...]
        for nb in range(nbb):
            qk_scr[nb, prev0:CONV_PAD, :] = conv0_ref[nb]

    cat0 = lambda xs: xs[0] if len(xs) == 1 else jnp.concatenate(xs, axis=0)
    acc_l = []
    for nb in range(nbb):
        u = pa_ref[nb, :, 0:2 * D_A]
        qk_scr[nb, CONV_PAD:CONV_PAD + c, :] = u
        acc = convb_ref[...]
        for j in range(CONV_W):
            acc = acc + qk_scr[nb, prev0 + j:prev0 + j + c, :] * convw_ref[j:j + 1, :]
        qk_scr[nb, prev0:CONV_PAD, :] = u[c - (CONV_W - 1):c, :]
        acc_l.append(acc)
    qk = _silu(cat0(acc_l))

    gates = cat0([pif_ref[nb] for nb in range(nbb)]) + bif_ref[...]
    li_all = gates[:, 0:LANES]
    lf_all = _log_sigmoid(gates[:, LANES:2 * LANES])
    row = lax.broadcasted_iota(jnp.int32, (rows, rows), 0)
    col = lax.broadcasted_iota(jnp.int32, (rows, rows), 1)
    tril = jnp.where((col <= row) & ((row >> cshift) == (col >> cshift)), 1.0, 0.0)
    b_all = _dot_exact_lhs(tril, lf_all)
    z_all = li_all - b_all
    crow = lax.broadcasted_iota(jnp.int32, (c, c), 0)
    ccol = lax.broadcasted_iota(jnp.int32, (c, c), 1)
    causal = ccol <= crow
    hrow = lax.broadcasted_iota(jnp.int32, (H_A * c, LANES), 0)
    hlane = lax.broadcasted_iota(jnp.int32, (H_A * c, LANES), 1)
    head_pick = jnp.where((hrow >> cshift) == hlane, 1.0, 0.0)
    z_rows = [_dot_exact_lhs(head_pick, z_all[nb * c:(nb + 1) * c], NT) for nb in range(nbb)]
    lane1 = lax.broadcasted_iota(jnp.int32, (1, LANES), 1)

    rs = lambda nb: slice(nb * c, (nb + 1) * c)
    q_l = [qk[rs(nb), h * DH_A:(h + 1) * DH_A] for nb, h in chains]
    k_l = [qk[rs(nb), D_A + h * DH_A:D_A + (h + 1) * DH_A] * (DH_A ** -0.5) for nb, h in chains]
    v_l = [pa_ref[nb, :, 2 * D_A + h * DH_A:2 * D_A + (h + 1) * DH_A] for nb, h in chains]
    c_l = [c_scr[nb, h] for nb, h in chains]
    n_l = [n_scr[nb, h:h + 1, :] for nb, h in chains]
    b_col = [b_all[rs(nb), h:h + 1] for nb, h in chains]
    li_col = [li_all[rs(nb), h:h + 1] for nb, h in chains]
    m_prev = [m_scr[nb][:, h:h + 1] for nb, h in chains]
    dmat = [jnp.where(causal, bc + z_rows[nb][h * c:(h + 1) * c], -jnp.inf)
            for bc, (nb, h) in zip(b_col, chains)]
    m_inter = [bc + mp for bc, mp in zip(b_col, m_prev)]
    m_t = [jnp.maximum(mi, jnp.max(d, axis=-1, keepdims=True)) for mi, d in zip(m_inter, dmat)]
    qk_dot = [_dot(q, k, NT) for q, k in zip(q_l, k_l)]
    qc = [_dot(q, cm) for q, cm in zip(q_l, c_l)]
    s_l = [x * jnp.exp(d - mt) for x, d, mt in zip(qk_dot, dmat, m_t)]
    sv = [_dot(s, v) for s, v in zip(s_l, v_l)]
    b_last = [bc[c - 1:c, :] for bc in b_col]
    g_s = [bl - bc + li for bl, bc, li in zip(b_last, b_col, li_col)]
    m_new = [jnp.maximum(bl + mp, jnp.max(gs, axis=0, keepdims=True))
             for bl, mp, gs in zip(b_last, m_prev, g_s)]
    wk = [jnp.exp(gs - mn) * k for gs, mn, k in zip(g_s, m_new, k_l)]
    w_old = [jnp.exp(bl + mp - mn) for bl, mp, mn in zip(b_last, m_prev, m_new)]
    kv = [_dot(w, v, TN) for w, v in zip(wk, v_l)]
    for i, (nb, h) in enumerate(chains):
        c_scr[nb, h] = w_old[i] * c_l[i] + kv[i]
        n_scr[nb, h:h + 1, :] = w_old[i] * n_l[i] + jnp.sum(wk[i], axis=0, keepdims=True)
    for nb in range(nbb):
        m_row = m_scr[nb]
        for h in range(H_A):
            m_row = jnp.where(lane1 == h, m_new[nb * H_A + h], m_row)
        m_scr[nb] = m_row
    w_inter = [jnp.exp(mi - mt) for mi, mt in zip(m_inter, m_t)]
    qn = [jnp.sum(q * nv, axis=-1, keepdims=True) for q, nv in zip(q_l, n_l)]
    s_sum = [jnp.sum(s, axis=-1, keepdims=True) for s in s_l]
    den = [w * a + b for w, a, b in zip(w_inter, qn, s_sum)]
    hh = [(w * a + b) / jnp.maximum(jnp.abs(d), jnp.exp(-mt))
          for w, a, b, d, mt in zip(w_inter, qc, sv, den, m_t)]
    mu = [jnp.mean(x, axis=-1, keepdims=True) for x in hh]
    hc = [x - m for x, m in zip(hh, mu)]
    var = [jnp.mean(x * x, axis=-1, keepdims=True) for x in hc]
    for i, (nb, h) in enumerate(chains):
        sl = slice(h * DH_A, (h + 1) * DH_A)
        hn = hc[i] * lax.rsqrt(var[i] + GN_EPS_A)
        o_pre = pa_ref[nb, :, 3 * D_A + h * DH_A:3 * D_A + (h + 1) * DH_A]
        h_ref[nb, :, sl] = hn * gn_ref[:, sl] * _sigmoid(o_pre)

    @pl.when(ci == nc - 1)
    def _():
        c_out_ref[0] = c_scr[...]
        n_out_ref[...] = n_scr[...]
        m_out_ref[...] = m_scr[...]


def _mlstm_call(pa, pif, conv0, c_in, n0, m0, convw, convb, bif, gn, c_stack, *, batch, seq, nbb,
                layer_in, layer):
    c = math.gcd(seq, CHUNK)
    nc = seq // c
    tok = lambda w: pl.BlockSpec((nbb, c, w), lambda b, i: (b, i, 0))
    vec = lambda r, w: pl.BlockSpec((r, w), lambda b, i: (0, 0))
    st5 = lambda l: pl.BlockSpec((1, nbb, H_A, DH_A, DH_A), lambda b, i: (l, b, 0, 0, 0))
    st3 = lambda r, w: pl.BlockSpec((nbb, r, w), lambda b, i: (b, 0, 0))
    chained = c_stack is not None
    in_specs = [tok(W_A_COLS), tok(W_IF_COLS), st3(CONV_W - 1, 2 * D_A), st5(layer_in), st3(H_A, DH_A),
                st3(1, LANES), vec(CONV_W, 2 * D_A), vec(1, 2 * D_A), vec(1, W_IF_COLS), vec(1, D_A)]
    args = [pa, pif, conv0, c_in, n0, m0, convw, convb, bif, gn]
    if chained:
        in_specs.append(pl.BlockSpec(memory_space=pl.ANY))
        args.append(c_stack)
    return pl.pallas_call(
        functools.partial(_mlstm_kernel, c=c, nc=nc, nbb=nbb, chained=chained),
        grid=(batch // nbb, nc),
        in_specs=in_specs,
        out_specs=[tok(D_A), st5(layer), st3(H_A, DH_A), st3(1, LANES)],
        out_shape=[jax.ShapeDtypeStruct((batch, seq, D_A), F32),
                   jax.ShapeDtypeStruct((DEPTH, batch, H_A, DH_A, DH_A), F32),
                   jax.ShapeDtypeStruct((batch, H_A, DH_A), F32),
                   jax.ShapeDtypeStruct((batch, 1, LANES), F32)],
        scratch_shapes=[pltpu.VMEM((nbb, CONV_PAD + c, 2 * D_A), F32),
                        pltpu.VMEM((nbb, H_A, DH_A, DH_A), F32),
                        pltpu.VMEM((nbb, H_A, DH_A), F32),
                        pltpu.VMEM((nbb, 1, LANES), F32)],
        input_output_aliases={len(args) - 1: 1} if chained else {},
        compiler_params=pltpu.CompilerParams(dimension_semantics=("parallel", "arbitrary"),
                                             vmem_limit_bytes=VMEM_LIMIT),
        name="mlstm",
    )(*args)


SHIFT_PAD = 8


def _neumann_inverses(ms, size, t):
    row = lax.broadcasted_iota(jnp.int32, (size, size), 0)
    col = lax.broadcasted_iota(jnp.int32, (size, size), 1)
    eye = jnp.where(row == col, 1.0, 0.0)
    blk = min(SOLVE_BLOCK, t)
    shift = int(math.log2(blk))
    same = (row >> shift) == (col >> shift)
    ds = [jnp.where(same, m, 0.0) for m in ms]
    ps = [eye + d for d in ds]
    xs = ds
    for _ in range(shift - 1):
        xs = [_dot(x, x) for x in xs]
        ps = [p + _dot(p, x) for p, x in zip(ps, xs)]
    nblk = t // blk
    if nblk == 1:
        return ps, None
    ns = [_dot(p, m - d) for p, m, d in zip(ps, ms, ds)]
    qs = [eye + n for n in ns]
    ys = ns
    for _ in range(int(math.log2(nblk)) - 1):
        ys = [_dot(y, y) for y in ys]
        qs = [q + _dot(q, y) for q, y in zip(qs, ys)]
    return ps, qs


def _rwkv_kernel(pb_ref, shift0_ref, s0_ref, mu_ref, w0a0_ref, wlora_ref, g2_ref, kk_ref, ka_ref,
                 rk_ref, lng_ref, lnb_ref, *rest, t, nc, nbb, chained):
    h_ref, s_out_ref, sh_scr, s_scr = rest[1:] if chained else rest
    ci = pl.program_id(1)
    rows = nbb * t
    t2 = 2 * t
    groups = [(nb, pr) for nb in range(nbb) for pr in range(N_PAIRS)]

    brow = lax.broadcasted_iota(jnp.int32, (LANES, LANES), 0)
    bcol = lax.broadcasted_iota(jnp.int32, (LANES, LANES), 1)
    blockdiag = (brow >= DH_B) == (bcol >= DH_B)
    bd2 = jnp.where(blockdiag, 1.0, 0.0).astype(BF16)

    @pl.when(ci == 0)
    def _():
        er = lax.broadcasted_iota(jnp.int32, (DH_B, LANES), 0)
        ec = lax.broadcasted_iota(jnp.int32, (DH_B, LANES), 1)
        dup_cols = jnp.where((ec & (DH_B - 1)) == er, 1.0, 0.0)
        for nb in range(nbb):
            sh_scr[nb, SHIFT_PAD - 1:SHIFT_PAD, :] = shift0_ref[nb]
            for pr in range(N_PAIRS):
                x = s0_ref[0, nb, 2 * pr:2 * pr + 2].reshape(LANES, DH_B)
                s_scr[nb * N_PAIRS + pr] = jnp.where(blockdiag, _dot_exact_rhs(x, dup_cols), 0.0)

    p_l, prev_l = [], []
    for nb in range(nbb):
        p_nb = pb_ref[nb]
        sh_scr[nb, SHIFT_PAD:SHIFT_PAD + t, :] = p_nb
        prev_l.append(sh_scr[nb, SHIFT_PAD - 1:SHIFT_PAD - 1 + t, :])
        sh_scr[nb, SHIFT_PAD - 1:SHIFT_PAD, :] = p_nb[t - 1:t, :]
        p_l.append(p_nb)
    cat0 = lambda xs: xs[0] if len(xs) == 1 else jnp.concatenate(xs, axis=0)
    p = cat0(p_l)
    prev = cat0(prev_l)
    pb = p + (prev - p) * mu_ref[...]

    r = pb[:, 0:D_B]
    kr = pb[:, D_B:2 * D_B]
    vr = pb[:, 2 * D_B:3 * D_B]
    lora_in = pb[:, 3 * D_B:3 * D_B + LANES]
    gl = pb[:, 3 * D_B + LANES:3 * D_B + 2 * LANES]
    lane = lax.broadcasted_iota(jnp.int32, (rows, LANES), 1)
    lora_act = jnp.where(lane < R_DECAY, jnp.tanh(lora_in), lora_in)
    z = _dot(lora_act, wlora_ref[...]) + w0a0_ref[...]
    ld = -jnp.exp(_log_sigmoid(z[:, 0:D_B]) - 0.5)
    a = _sigmoid(z[:, D_B:2 * D_B])
    g = _dot(_sigmoid(gl), g2_ref[...])

    def seg_sum(x):
        xr = jnp.concatenate([x[:, q * LANES:(q + 1) * LANES] for q in range(N_PAIRS)], axis=0)
        hi = xr.astype(BF16)
        lo = (xr - hi.astype(F32)).astype(BF16)
        s = (jnp.dot(hi, bd2, preferred_element_type=F32) + jnp.dot(lo, bd2, preferred_element_type=F32))
        return jnp.concatenate([s[q * rows:(q + 1) * rows] for q in range(N_PAIRS)], axis=1)

    kk_raw = kr * kk_ref[...]
    kk = kk_raw / jnp.maximum(jnp.sqrt(seg_sum(kk_raw * kk_raw)), 1e-12)
    k2 = kr * (1.0 + (a - 1.0) * ka_ref[...])
    bonus = seg_sum(r * k2 * rk_ref[...]) * vr

    row = lax.broadcasted_iota(jnp.int32, (rows, rows), 0)
    col = lax.broadcasted_iota(jnp.int32, (rows, rows), 1)
    tshift = int(math.log2(t))
    tril = jnp.where((col <= row) & ((row >> tshift) == (col >> tshift)), 1.0, 0.0)
    lc = _dot_exact_lhs(tril, ld)
    lends = [lc[(nb + 1) * t - 1:(nb + 1) * t, :] for nb in range(nbb)]
    lend_rows = cat0([jnp.broadcast_to(le, (t, D_B)) for le in lends])
    e_nc = jnp.exp(-lc)
    e_end = jnp.exp(lend_rows - lc)
    b_raw = kk * a
    at = -kk * jnp.exp(lc - ld)
    rt = r * jnp.exp(lc)
    kt = k2 * e_nc
    bt = b_raw * e_nc
    kend = k2 * e_end
    bend = b_raw * e_end

    srow = lax.broadcasted_iota(jnp.int32, (t2, t2), 0)
    scol = lax.broadcasted_iota(jnp.int32, (t2, t2), 1)
    strict = scol < srow
    incl = scol <= srow
    lane2 = lax.broadcasted_iota(jnp.int32, (t2, LANES), 1)
    row2 = lax.broadcasted_iota(jnp.int32, (t2, LANES), 0)
    head_sel = (lane2 >= DH_B) == (row2 >= t)
    first_head = lax.broadcasted_iota(jnp.int32, (t, LANES), 1) < DH_B

    def blk(x, nb, pr):
        return x[nb * t:(nb + 1) * t, pr * LANES:(pr + 1) * LANES]

    def stack(x):
        return jnp.where(head_sel, jnp.concatenate([x, x], axis=0), 0.0)

    lhs = [jnp.concatenate([stack(blk(at, *gp)), stack(blk(rt, *gp))], axis=0) for gp in groups]
    rk_s = [stack(blk(kt, *gp)) for gp in groups]
    rb_s = [stack(blk(bt, *gp)) for gp in groups]
    v_p = [blk(vr, *gp) for gp in groups]
    v_s = [jnp.concatenate([v, v], axis=0) for v in v_p]
    s_p = [s_scr[i] for i in range(len(groups))]
    gk = [_dot(l, x, NT) for l, x in zip(lhs, rk_s)]
    gb = [_dot(l, x, NT) for l, x in zip(lhs, rb_s)]
    xs = [_dot(l, s, NT) for l, s in zip(lhs, s_p)]
    m_ab = [jnp.where(strict, x[0:t2], 0.0) for x in gb]
    ps, qs = _neumann_inverses(m_ab, t2, t)
    rhs = [x[0:t2] + _dot(jnp.where(strict, y[0:t2], 0.0), v) for x, y, v in zip(xs, gk, v_s)]
    sa_s = [_dot(pm, x) for pm, x in zip(ps, rhs)]
    if qs is not None:
        sa_s = [_dot(qm, x) for qm, x in zip(qs, sa_s)]
    o_s = [x[t2:] + _dot(jnp.where(incl, y[t2:], 0.0), v) + _dot(jnp.where(incl, w[t2:], 0.0), sa)
           for x, y, w, v, sa in zip(xs, gk, gb, v_s, sa_s)]
    o_p = [jnp.where(first_head, o[0:t], o[t:t2]) for o in o_s]
    sa_p = [jnp.where(first_head, sa[0:t], sa[t:t2]) for sa in sa_s]
    upd = [_dot(jnp.concatenate([v, sa], axis=0),
                jnp.concatenate([blk(kend, *gp), blk(bend, *gp)], axis=0), TN)
           for v, sa, gp in zip(v_p, sa_p, groups)]
    for i, (nb, pr) in enumerate(groups):
        dec = jnp.exp(lends[nb][:, pr * LANES:(pr + 1) * LANES])
        s_scr[i] = s_p[i] * dec + jnp.where(blockdiag, upd[i], 0.0)

    o = cat0([jnp.concatenate(o_p[nb * N_PAIRS:(nb + 1) * N_PAIRS], axis=1) for nb in range(nbb)])
    mean = seg_sum(o) * (1.0 / DH_B)
    oc = o - mean
    var = seg_sum(oc * oc) * (1.0 / DH_B)
    out = oc * lax.rsqrt(var + GN_EPS_B) * lng_ref[...] + lnb_ref[...]
    res = (out + bonus) * g
    for nb in range(nbb):
        h_ref[nb] = res[nb * t:(nb + 1) * t, :]

    @pl.when(ci == nc - 1)
    def _():
        fr = lax.broadcasted_iota(jnp.int32, (LANES, DH_B), 0)
        fc = lax.broadcasted_iota(jnp.int32, (LANES, DH_B), 1)
        dup_rows = jnp.where((fr & (DH_B - 1)) == fc, 1.0, 0.0)
        for i, (nb, pr) in enumerate(groups):
            packed = _dot_exact_rhs(s_scr[i], dup_rows)
            s_out_ref[0, nb, 2 * pr:2 * pr + 2] = packed.reshape(2, DH_B, DH_B)


def _rwkv_call(pb, shift0, s_in, mu, w0a0, wlora, g2, kk, ka, rk, lng, lnb, s_stack, *, batch, seq, nbb,
               layer_in, layer):
    t = math.gcd(seq, CHUNK)
    nc = seq // t
    tok = lambda w: pl.BlockSpec((nbb, t, w), lambda b, i: (b, i, 0))
    vec = lambda r, w: pl.BlockSpec((r, w), lambda b, i: (0, 0))
    st5 = lambda l: pl.BlockSpec((1, nbb, H_B, DH_B, DH_B), lambda b, i: (l, b, 0, 0, 0))
    chained = s_stack is not None
    in_specs = [tok(D_B_IN), pl.BlockSpec((nbb, 1, D_B_IN), lambda b, i: (b, 0, 0)), st5(layer_in),
                vec(1, D_B_IN), vec(1, 2 * D_B), vec(LANES, 2 * D_B), vec(R_GATE, D_B),
                vec(1, D_B), vec(1, D_B), vec(1, D_B), vec(1, D_B), vec(1, D_B)]
    args = [pb, shift0, s_in, mu, w0a0, wlora, g2, kk, ka, rk, lng, lnb]
    if chained:
        in_specs.append(pl.BlockSpec(memory_space=pl.ANY))
        args.append(s_stack)
    return pl.pallas_call(
        functools.partial(_rwkv_kernel, t=t, nc=nc, nbb=nbb, chained=chained),
        grid=(batch // nbb, nc),
        in_specs=in_specs,
        out_specs=[tok(D_B), st5(layer)],
        out_shape=[jax.ShapeDtypeStruct((batch, seq, D_B), F32),
                   jax.ShapeDtypeStruct((DEPTH, batch, H_B, DH_B, DH_B), F32)],
        scratch_shapes=[pltpu.VMEM((nbb, SHIFT_PAD + t, D_B_IN), F32),
                        pltpu.VMEM((nbb * N_PAIRS, LANES, LANES), F32)],
        input_output_aliases={len(args) - 1: 1} if chained else {},
        compiler_params=pltpu.CompilerParams(dimension_semantics=("parallel", "arbitrary"),
                                             vmem_limit_bytes=VMEM_LIMIT),
        name="rwkv",
    )(*args)


ROUTE_ROWS = 8


def _outproj_kernel(x_ref, ha_ref, hb_ref, wo_ref, g_ref, b_ref, wr_ref, br_ref, x1_ref, route_ref,
                    cnt_ref, *, tm):
    y = _dot(ha_ref[...], wo_ref[0, 0:D_A, :]) + _dot(hb_ref[...], wo_ref[0, D_A:D_A + D_B, :])
    x1 = _layer_norm(ALPHA * x_ref[...] + y, g_ref[...], b_ref[...], LN_EPS)
    x1_ref[...] = x1
    logits = lax.dot_general(wr_ref[...], x1, NT, precision=lax.Precision.HIGHEST,
                             preferred_element_type=F32)
    mx = jnp.max(logits, axis=0, keepdims=True)
    ex = jnp.exp(logits - mx)
    probs = ex / jnp.sum(ex, axis=0, keepdims=True)
    sel = probs + br_ref[...]
    neg = -jnp.inf

    def top2(rows):
        m1 = functools.reduce(jnp.maximum, rows)
        i1 = jnp.full(m1.shape, len(rows) - 1, jnp.int32)
        for j in range(len(rows) - 2, -1, -1):
            i1 = jnp.where(rows[j] == m1, j, i1)
        rest = [jnp.where(i1 == j, neg, rows[j]) for j in range(len(rows))]
        m2 = functools.reduce(jnp.maximum, rest)
        i2 = jnp.full(m2.shape, len(rows) - 1, jnp.int32)
        for j in range(len(rows) - 2, -1, -1):
            i2 = jnp.where(rest[j] == m2, j, i2)
        return m1, i1, m2, i2

    g_score, g_i1, g_i2 = [], [], []
    for gidx in range(N_GROUPS):
        rows = [sel[gidx * EXPERTS_PER_GROUP + j:gidx * EXPERTS_PER_GROUP + j + 1, :]
                for j in range(EXPERTS_PER_GROUP)]
        m1, i1, m2, i2 = top2(rows)
        g_score.append(m1 + m2)
        g_i1.append(i1)
        g_i2.append(i2)
    best = functools.reduce(jnp.maximum, g_score)
    grp = jnp.full(best.shape, N_GROUPS - 1, jnp.int32)
    for gidx in range(N_GROUPS - 2, -1, -1):
        grp = jnp.where(g_score[gidx] == best, gidx, grp)
    l1 = g_i1[N_GROUPS - 1]
    l2 = g_i2[N_GROUPS - 1]
    for gidx in range(N_GROUPS - 2, -1, -1):
        l1 = jnp.where(grp == gidx, g_i1[gidx], l1)
        l2 = jnp.where(grp == gidx, g_i2[gidx], l2)
    e1 = grp * EXPERTS_PER_GROUP + l1
    e2 = grp * EXPERTS_PER_GROUP + l2
    e_iota = lax.broadcasted_iota(jnp.int32, (N_EXPERTS, tm), 0)
    p1 = jnp.sum(jnp.where(e_iota == e1, probs, 0.0), axis=0, keepdims=True)
    p2 = jnp.sum(jnp.where(e_iota == e2, probs, 0.0), axis=0, keepdims=True)
    tot = p1 + p2
    r8 = lax.broadcasted_iota(jnp.int32, (ROUTE_ROWS, tm), 0)
    route_ref[...] = jnp.where(r8 == 0, e1.astype(F32),
                               jnp.where(r8 == 1, e2.astype(F32),
                                         jnp.where(r8 == 2, p1 / tot, jnp.where(r8 == 3, p2 / tot, 0.0))))
    picked = jnp.where((e_iota == e1) | (e_iota == e2), 1.0, 0.0)
    cnt_ref[0] = jnp.broadcast_to(jnp.sum(picked, axis=1, keepdims=True), (N_EXPERTS, LANES))


def _outproj_call(x, ha, hb, wo, g, b, wr_t, br, tm, layer):
    n = x.shape[0]
    row = lambda w: pl.BlockSpec((tm, w), lambda i: (i, 0))
    full = lambda r, w: pl.BlockSpec((r, w), lambda i: (0, 0))
    return pl.pallas_call(
        functools.partial(_outproj_kernel, tm=tm),
        grid=(n // tm,),
        in_specs=[row(D_MODEL), row(D_A), row(D_B),
                  pl.BlockSpec((1, D_MODEL, D_MODEL), lambda i: (layer, 0, 0)), full(1, D_MODEL),
                  full(1, D_MODEL), full(N_EXPERTS, D_MODEL), full(N_EXPERTS, 1)],
        out_specs=[row(D_MODEL), pl.BlockSpec((ROUTE_ROWS, tm), lambda i: (0, i)),
                   pl.BlockSpec((1, N_EXPERTS, LANES), lambda i: (i, 0, 0))],
        out_shape=[jax.ShapeDtypeStruct((n, D_MODEL), F32), jax.ShapeDtypeStruct((ROUTE_ROWS, n), F32),
                   jax.ShapeDtypeStruct((n // tm, N_EXPERTS, LANES), F32)],
        compiler_params=pltpu.CompilerParams(dimension_semantics=("parallel",),
                                             vmem_limit_bytes=VMEM_LIMIT),
        name="outproj",
    )(x, ha, hb, wo, g, b, wr_t, br)


MOE_BM = 128
MOE_CH = 512
MOE_CCH = 1024
MOE_NBUF = 4


def _moe_max_blocks(t):
    return -(-2 * t // MOE_BM) + N_EXPERTS - 1


def _moe_sort_kernel(nblk_ref, x_ref, route_ref, padoff_ref, tri_ref, xs_ref, cols_ref, *, t, n_tiles,
                     n_chunks):
    i = pl.program_id(0)

    @pl.when(i >= n_tiles)
    def _():
        xs_ref[...] = jnp.zeros_like(xs_ref)

    @pl.when(i < n_tiles)
    def _():
        s_used = nblk_ref[jnp.minimum(i, n_tiles - 1)] * MOE_BM
        xb = x_ref[...].astype(BF16)
        route = route_ref[...]
        e_iota = lax.broadcasted_iota(jnp.int32, (N_EXPERTS, t), 0)
        a1 = e_iota == route[0:1].astype(jnp.int32)
        a2 = e_iota == route[1:2].astype(jnp.int32)
        picked = jnp.where(a1 | a2, 1.0, 0.0).astype(BF16)
        rank = jnp.dot(picked, tri_ref[...], preferred_element_type=F32)
        base = padoff_ref[0][:, 0:1] + rank
        slot1 = jnp.sum(jnp.where(a1, base, 0.0), axis=0, keepdims=True)
        slot2 = jnp.sum(jnp.where(a2, base, 0.0), axis=0, keepdims=True)
        r8 = lax.broadcasted_iota(jnp.int32, (ROUTE_ROWS, t), 0)
        rows = jnp.where(r8 == 0, slot1, jnp.where(r8 == 1, slot2,
                                                   jnp.where(r8 == 2, route[2:3],
                                                             jnp.where(r8 == 3, route[3:4], 0.0))))
        padded = jnp.concatenate([rows, jnp.zeros((LANES - ROUTE_ROWS, t), F32)], axis=0)
        cols_ref[...] = padded.T
        s1 = slot1.astype(jnp.int32)
        s2 = slot2.astype(jnp.int32)
        for c in range(n_chunks):
            used = c * MOE_CH < s_used

            @pl.when(used)
            def _():
                s_iota = lax.broadcasted_iota(jnp.int32, (MOE_CH, t), 0) + c * MOE_CH
                onehot = jnp.where((s_iota == s1) | (s_iota == s2), 1.0, 0.0).astype(BF16)
                xs_ref[0, c * MOE_CH:(c + 1) * MOE_CH, :] = jnp.dot(
                    onehot, xb, preferred_element_type=F32).astype(BF16)

            @pl.when(jnp.logical_not(used))
            def _():
                xs_ref[0, c * MOE_CH:(c + 1) * MOE_CH, :] = jnp.zeros((MOE_CH, D_MODEL), BF16)


def _moe_expert_kernel(sblk_ref, ebase_ref, ecnt_ref, xs_hbm, wg_ref, wu_ref, wd_ref, ys_hbm,
                       wg_scr, wu_scr, wd_scr, xbuf, ybuf, in_sem, out_sem):
    e = pl.program_id(0)
    base = ebase_ref[e]
    cnt = ecnt_ref[e]
    npair = (cnt + 1) // 2
    wg_scr[...] = wg_ref[0, 0].astype(BF16)
    wu_scr[...] = wu_ref[0, 0].astype(BF16)
    wd_scr[...] = wd_ref[0, 0].astype(BF16)
    xbuf[...] = jnp.zeros_like(xbuf)

    def rows(k):
        return pl.ds(pl.multiple_of(sblk_ref[base + k] * MOE_BM, MOE_BM), MOE_BM)

    def in_copy(k, slot, half):
        return pltpu.make_async_copy(xs_hbm.at[rows(k), :], xbuf.at[slot, half * MOE_BM:(half + 1) * MOE_BM, :],
                                     in_sem.at[slot, half])

    def out_copy(k, slot, half):
        return pltpu.make_async_copy(ybuf.at[slot, half * MOE_BM:(half + 1) * MOE_BM, :], ys_hbm.at[rows(k), :],
                                     out_sem.at[slot, half])

    def for_pair(p, make, act):
        slot = p % MOE_NBUF
        act(make(2 * p, slot, 0))

        @pl.when(2 * p + 1 < cnt)
        def _():
            act(make(2 * p + 1, slot, 1))

    start = lambda c: c.start()
    wait = lambda c: c.wait()

    for i in range(MOE_NBUF - 1):
        @pl.when(i < npair)
        def _():
            for_pair(i, in_copy, start)

    def pair(p, carry):
        slot = p % MOE_NBUF
        for_pair(p, in_copy, wait)
        ahead = p + MOE_NBUF - 1

        @pl.when(ahead < npair)
        def _():
            for_pair(ahead, in_copy, start)

        @pl.when(p >= MOE_NBUF)
        def _():
            for_pair(p - MOE_NBUF, out_copy, wait)

        xblk = xbuf[slot]
        hid = _silu(jnp.dot(xblk, wg_scr[...], preferred_element_type=F32)) * jnp.dot(
            xblk, wu_scr[...], preferred_element_type=F32)
        y = jnp.dot(hid.astype(BF16), wd_scr[...], preferred_element_type=F32)
        ybuf[slot] = y.astype(BF16)
        for_pair(p, out_copy, start)
        return carry

    lax.fori_loop(0, npair, pair, 0)

    for i in range(MOE_NBUF):
        @pl.when(npair > i)
        def _():
            for_pair(npair - 1 - i, out_copy, wait)


def _moe_combine_kernel(nblk_ref, x_ref, cols_ref, ys_ref, g_ref, b_ref, o_ref, acc_scr, *, t, n_chunks):
    s_used = nblk_ref[pl.program_id(0)] * MOE_BM
    cols = cols_ref[...]
    s1c = cols[:, 0:1].astype(jnp.int32)
    s2c = cols[:, 1:2].astype(jnp.int32)
    w1c = cols[:, 2:3]
    w2c = cols[:, 3:4]
    for c in range(n_chunks):
        @pl.when(c * MOE_CCH < s_used)
        def _():
            l_iota = lax.broadcasted_iota(jnp.int32, (t, MOE_CCH), 1) + c * MOE_CCH
            weighted = (jnp.where(l_iota == s1c, w1c, 0.0) + jnp.where(l_iota == s2c, w2c, 0.0)).astype(BF16)
            part = jnp.dot(weighted, ys_ref[0, c * MOE_CCH:(c + 1) * MOE_CCH, :],
                           preferred_element_type=F32)
            if c == 0:
                acc_scr[...] = part
            else:
                acc_scr[...] += part
    o_ref[...] = _layer_norm(ALPHA * x_ref[...] + acc_scr[...], g_ref[...], b_ref[...], LN_EPS)


def _moe_schedule(counts, tiles_per, maxb, bpt):
    cnt = counts[:, :, 0].astype(jnp.int32)
    cnt = cnt.reshape(-1, tiles_per, N_EXPERTS).sum(axis=1)
    n_tiles = cnt.shape[0]
    nblk = (cnt + MOE_BM - 1) // MOE_BM
    end = jnp.cumsum(nblk, axis=-1)
    start = end - nblk
    total = end[:, -1]
    padoff = jnp.broadcast_to((start * MOE_BM).astype(F32)[:, :, None], (n_tiles, N_EXPERTS, LANES))
    per_e = jnp.sum(nblk, axis=0)
    e_end = jnp.cumsum(per_e)
    e_base = e_end - per_e
    e_ids = jnp.arange(N_EXPERTS, dtype=jnp.int32)
    s_idx = jnp.arange(n_tiles * maxb, dtype=jnp.int32)
    last_e = jnp.max(jnp.where(per_e > 0, e_ids, 0))
    se = jnp.minimum(jnp.sum((s_idx[:, None] >= e_end[None, :]).astype(jnp.int32), axis=-1), last_e)
    oh_e = se[:, None] == e_ids[None, :]
    r = s_idx - jnp.sum(jnp.where(oh_e, e_base[None, :], 0), axis=-1)
    of_e = lambda a: jnp.sum(jnp.where(oh_e[:, None, :], a[None, :, :], 0), axis=-1)
    incl_s = of_e(jnp.cumsum(nblk, axis=0))
    tile = jnp.minimum(jnp.sum((incl_s <= r[:, None]).astype(jnp.int32), axis=-1), n_tiles - 1)
    oh_t = tile[:, None] == jnp.arange(n_tiles, dtype=jnp.int32)[None, :]
    of_t = lambda a: jnp.sum(jnp.where(oh_t, a, 0), axis=-1)
    j = of_t(of_e(start)) + r - (of_t(incl_s) - of_t(of_e(nblk)))
    sblk = jnp.where(s_idx < e_end[-1], tile * bpt + j, n_tiles * bpt)
    return total, padoff, sblk, e_base, per_e


def _moe_call(x, route, counts, wg, wu, wd, g, b, *, layer, tm):
    n = x.shape[0]
    t = 1024 if n % 1024 == 0 else n
    n_tiles = n // t
    maxb = _moe_max_blocks(t)
    s_alloc = -(-maxb * MOE_BM // MOE_CCH) * MOE_CCH
    n_chunks = s_alloc // MOE_CH
    bpt = s_alloc // MOE_BM
    nblk, padoff, sblk, e_base, e_cnt = _moe_schedule(counts, t // tm, maxb, bpt)
    tri = jnp.triu(jnp.ones((t, t), BF16), k=1)
    last = n_tiles - 1
    xs, cols = pl.pallas_call(
        functools.partial(_moe_sort_kernel, t=t, n_tiles=n_tiles, n_chunks=n_chunks),
        grid_spec=pltpu.PrefetchScalarGridSpec(
            num_scalar_prefetch=1,
            grid=(n_tiles + 1,),
            in_specs=[pl.BlockSpec((t, D_MODEL), lambda i, nb: (jnp.minimum(i, last), 0)),
                      pl.BlockSpec((ROUTE_ROWS, t), lambda i, nb: (0, jnp.minimum(i, last))),
                      pl.BlockSpec((1, N_EXPERTS, LANES), lambda i, nb: (jnp.minimum(i, last), 0, 0)),
                      pl.BlockSpec((t, t), lambda i, nb: (0, 0))],
            out_specs=[pl.BlockSpec((1, s_alloc, D_MODEL), lambda i, nb: (i, 0, 0)),
                       pl.BlockSpec((t, LANES), lambda i, nb: (jnp.minimum(i, last), 0))]),
        out_shape=[jax.ShapeDtypeStruct((n_tiles + 1, s_alloc, D_MODEL), BF16),
                   jax.ShapeDtypeStruct((n, LANES), F32)],
        compiler_params=pltpu.CompilerParams(dimension_semantics=("arbitrary",),
                                             vmem_limit_bytes=VMEM_LIMIT),
        name="moe_sort",
    )(nblk, x, route, padoff, tri)
    wspec = lambda r, c: pl.BlockSpec((1, 1, r, c), lambda e, sb, eb, ec: (layer, e, 0, 0))
    ys = pl.pallas_call(
        _moe_expert_kernel,
        grid_spec=pltpu.PrefetchScalarGridSpec(
            num_scalar_prefetch=3,
            grid=(N_EXPERTS,),
            in_specs=[pl.BlockSpec(memory_space=pl.ANY),
                      wspec(D_MODEL, D_EXP), wspec(D_MODEL, D_EXP), wspec(D_EXP, D_MODEL)],
            out_specs=pl.BlockSpec(memory_space=pl.ANY),
            scratch_shapes=[pltpu.VMEM((D_MODEL, D_EXP), BF16), pltpu.VMEM((D_MODEL, D_EXP), BF16),
                            pltpu.VMEM((D_EXP, D_MODEL), BF16),
                            pltpu.VMEM((MOE_NBUF, 2 * MOE_BM, D_MODEL), BF16),
                            pltpu.VMEM((MOE_NBUF, 2 * MOE_BM, D_MODEL), BF16),
                            pltpu.SemaphoreType.DMA((MOE_NBUF, 2)), pltpu.SemaphoreType.DMA((MOE_NBUF, 2))]),
        out_shape=jax.ShapeDtypeStruct(((n_tiles + 1) * s_alloc, D_MODEL), BF16),
        input_output_aliases={3: 0},
        compiler_params=pltpu.CompilerParams(dimension_semantics=("arbitrary",),
                                             vmem_limit_bytes=VMEM_LIMIT),
        name="moe_experts",
    )(sblk, e_base, e_cnt, xs.reshape((n_tiles + 1) * s_alloc, D_MODEL), wg, wu, wd)
    vec = pl.BlockSpec((1, D_MODEL), lambda i, nb: (0, 0))
    return pl.pallas_call(
        functools.partial(_moe_combine_kernel, t=t, n_chunks=s_alloc // MOE_CCH),
        grid_spec=pltpu.PrefetchScalarGridSpec(
            num_scalar_prefetch=1,
            grid=(n_tiles,),
            in_specs=[pl.BlockSpec((t, D_MODEL), lambda i, nb: (i, 0)),
                      pl.BlockSpec((t, LANES), lambda i, nb: (i, 0)),
                      pl.BlockSpec((1, s_alloc, D_MODEL), lambda i, nb: (i, 0, 0)), vec, vec],
            out_specs=pl.BlockSpec((t, D_MODEL), lambda i, nb: (i, 0)),
            scratch_shapes=[pltpu.VMEM((t, D_MODEL), F32)]),
        out_shape=jax.ShapeDtypeStruct((n, D_MODEL), F32),
        compiler_params=pltpu.CompilerParams(dimension_semantics=("parallel",),
                                             vmem_limit_bytes=VMEM_LIMIT),
        name="moe_combine",
    )(nblk, x, cols, ys.reshape(n_tiles + 1, s_alloc, D_MODEL), g, b)


def _row_tile(n):
    return 512 if n % 512 == 0 else n


def _seqs_per_step(batch, want):
    return want if batch % want == 0 else 1


def kernel(x_prompt, x_sample, state_mlstm_C, state_mlstm_n, state_mlstm_m, state_mlstm_conv, state_rwkv_S, state_rwkv_shift, ln0_g, ln0_b, w_in, conv_w, conv_b, b_i, b_f, gn_a_g, mu_shift, w0, w2, a0, a2, g2, k_k, k_a, r_k, lnx_g, lnx_b, w_out, ln1_g, ln1_b, w_router, b_router, we_gate, we_up, we_down, ln2_g, ln2_b):
    d_a_in = 4 * D_A + 2 * H_A
    zpad = jnp.zeros((DEPTH, D_MODEL, LANES - H_A), F32)
    w_cat = jnp.concatenate(
        [w_in[:, :, 0:4 * D_A], w_in[:, :, d_a_in:], w_in[:, :, 4 * D_A:4 * D_A + H_A], zpad,
         w_in[:, :, 4 * D_A + H_A:d_a_in], zpad], axis=-1).astype(BF16)
    bif = jnp.zeros((DEPTH, 1, W_IF_COLS), F32)
    bif = bif.at[:, 0, 0:H_A].set(b_i).at[:, 0, LANES:LANES + H_A].set(b_f)
    wlora = jnp.zeros((DEPTH, LANES, 2 * D_B), F32)
    wlora = wlora.at[:, 0:R_DECAY, 0:D_B].set(w2).at[:, R_DECAY:, D_B:].set(a2).astype(BF16)
    w0a0 = jnp.concatenate([w0, a0], axis=-1)[:, None, :]
    g2b = g2.astype(BF16)
    wob = w_out.astype(BF16)
    wr_t = w_router.T
    br = b_router[:, None]
    r1 = lambda v: v[None, :]

    def run(x3, states):
        nb, seq, _ = x3.shape
        n = nb * seq
        tm = _row_tile(n)
        rwkv_nbb = _seqs_per_step(nb, 4 if seq >= CHUNK else 8)
        mlstm_nbb = _seqs_per_step(nb, 2 if seq >= CHUNK else 8)
        x = _ln_call(x3.reshape(n, D_MODEL), r1(ln0_g), r1(ln0_b), tm)
        outs = []
        c_stack = jnp.zeros((DEPTH, nb, H_A, DH_A, DH_A), F32)
        s_stack = jnp.zeros((DEPTH, nb, H_B, DH_B, DH_B), F32)
        for l in range(DEPTH):
            if states is None:
                c_in = jnp.zeros((1, nb, H_A, DH_A, DH_A), F32)
                s_in = jnp.zeros((1, nb, H_B, DH_B, DH_B), F32)
                layer_in = 0
                n0 = jnp.zeros((nb, H_A, DH_A), F32)
                m0 = jnp.zeros((nb, H_A), F32)
                conv0 = jnp.zeros((nb, CONV_W - 1, 2 * D_A), F32)
                shift0 = jnp.zeros((nb, D_B_IN), F32)
            else:
                c_in, s_in, layer_in = states[0], states[4], l
                n0, m0, conv0, shift0 = (states[k][l] for k in (1, 2, 3, 5))
            m0p = jnp.zeros((nb, 1, LANES), F32).at[:, 0, 0:H_A].set(m0)
            pa, pb, pif = _inproj_call(x, w_cat, tm, l)
            ha, c_stack, n_new, m_new = _mlstm_call(
                pa.reshape(nb, seq, W_A_COLS), pif.reshape(nb, seq, W_IF_COLS), conv0, c_in, n0, m0p,
                conv_w[l], r1(conv_b[l]), bif[l], r1(gn_a_g[l]), c_stack, batch=nb, seq=seq,
                nbb=mlstm_nbb, layer_in=layer_in, layer=l)
            ha = ha.reshape(n, D_A)
            hb, s_stack = _rwkv_call(
                pb.reshape(nb, seq, D_B_IN), shift0[:, None, :], s_in, r1(mu_shift[l]), w0a0[l], wlora[l],
                g2b[l], r1(k_k[l]), r1(k_a[l]), r1(r_k[l].reshape(D_B)), r1(lnx_g[l]), r1(lnx_b[l]),
                s_stack, batch=nb, seq=seq, nbb=rwkv_nbb, layer_in=layer_in, layer=l)
            hb = hb.reshape(n, D_B)
            x1, route, counts = _outproj_call(x, ha, hb, wob, r1(ln1_g[l]), r1(ln1_b[l]), wr_t, br, tm, l)
            x = _moe_call(x1, route, counts, we_gate, we_up, we_down, r1(ln2_g[l]), r1(ln2_b[l]),
                          layer=l, tm=tm)
            pa3 = pa.reshape(nb, seq, W_A_COLS)
            full = jnp.concatenate([conv0, pa3[:, :, 0:2 * D_A]], axis=1) if seq < CONV_W - 1 else pa3[:, :, 0:2 * D_A]
            conv_new = full[:, -(CONV_W - 1):, :]
            shift_new = pb.reshape(nb, seq, D_B_IN)[:, -1, :]
            outs.append((n_new, m_new[:, 0, 0:H_A], conv_new, shift_new))
        n_all, m_all, conv_all, shift_all = (jnp.stack(s) for s in zip(*outs))
        return x.reshape(nb, seq, D_MODEL), (c_stack, n_all, m_all, conv_all, s_stack, shift_all)

    y_prompt, (p_c, p_n, p_m, p_conv, p_s, p_shift) = run(x_prompt, None)
    y_sample, (s_c, s_n, s_m, s_conv, s_s, s_shift) = run(
        x_sample, (state_mlstm_C, state_mlstm_n, state_mlstm_m, state_mlstm_conv, state_rwkv_S,
                   state_rwkv_shift))
    return (y_prompt, y_sample, p_c, p_n, p_m, p_conv, p_s, p_shift, s_c, s_n, s_m, s_conv, s_s, s_shift)
```

```python
import functools
import math

import jax
import jax.numpy as jnp
from jax import lax
from jax.experimental import pallas as pl
from jax.experimental.pallas import tpu as pltpu

F32 = jnp.float32
BF16 = jnp.bfloat16

D_MODEL = 1024
DEPTH = 4
D_A = 512
D_B = 512
DH_A = 128
H_A = 4
DH_B = 64
H_B = 8
N_PAIRS = H_B // 2
CONV_W = 4
CHUNK = 64
R_DECAY = 64
R_AAA = 64
R_GATE = 128
D_B_IN = 3 * D_B + R_DECAY + R_AAA + R_GATE
N_EXPERTS = 16
N_GROUPS = 4
EXPERTS_PER_GROUP = 4
D_EXP = 512
ALPHA = (2 * DEPTH) ** 0.25
LN_EPS = 1e-5
GN_EPS_A = 1e-6
GN_EPS_B = 64e-5

LANES = 128
SOLVE_BLOCK = 16
VMEM_LIMIT = 48 * 1024 * 1024

NN = (((1,), (0,)), ((), ()))
NT = (((1,), (1,)), ((), ()))
TN = (((0,), (0,)), ((), ()))


def _dot(a, b, dims=NN):
    return lax.dot_general(a.astype(BF16), b.astype(BF16), dims, preferred_element_type=F32)


def _split3(x):
    hi = x.astype(BF16)
    r1 = x - hi.astype(F32)
    mid = r1.astype(BF16)
    lo = (r1 - mid.astype(F32)).astype(BF16)
    return hi, mid, lo


def _dot_exact_lhs(a, x, dims=NN):
    a = a.astype(BF16)
    hi, mid, lo = _split3(x)
    d = lambda p: lax.dot_general(a, p, dims, preferred_element_type=F32)
    return d(hi) + d(mid) + d(lo)


def _dot_exact_rhs(x, b, dims=NN):
    b = b.astype(BF16)
    hi, mid, lo = _split3(x)
    d = lambda p: lax.dot_general(p, b, dims, preferred_element_type=F32)
    return d(hi) + d(mid) + d(lo)


def _layer_norm(x, g, b, eps):
    mu = jnp.mean(x, axis=-1, keepdims=True)
    xc = x - mu
    var = jnp.mean(xc * xc, axis=-1, keepdims=True)
    return xc * lax.rsqrt(var + eps) * g + b


def _sigmoid(x):
    return 1.0 / (1.0 + jnp.exp(-x))


def _log_sigmoid(x):
    return jnp.minimum(x, 0.0) - jnp.log(1.0 + jnp.exp(-jnp.abs(x)))


def _silu(x):
    return x * _sigmoid(x)


def _ln_kernel(x_ref, g_ref, b_ref, o_ref):
    o_ref[...] = _layer_norm(x_ref[...], g_ref[...], b_ref[...], LN_EPS)


def _ln_call(x, g, b, tm):
    n, d = x.shape
    row = pl.BlockSpec((tm, d), lambda i: (i, 0))
    vec = pl.BlockSpec((1, d), lambda i: (0, 0))
    return pl.pallas_call(
        _ln_kernel,
        grid=(n // tm,),
        in_specs=[row, vec, vec],
        out_specs=row,
        out_shape=jax.ShapeDtypeStruct((n, d), F32),
        compiler_params=pltpu.CompilerParams(dimension_semantics=("parallel",)),
        name="ln0",
    )(x, g, b)


W_A_COLS = 4 * D_A
W_IF_COLS = 2 * LANES
W_IN_COLS = W_A_COLS + D_B_IN + W_IF_COLS


def _inproj_kernel(x_ref, w_ref, pa_ref, pb_ref, pif_ref):
    xb = x_ref[...].astype(BF16)
    pa_ref[...] = jnp.dot(xb, w_ref[0, :, 0:W_A_COLS], preferred_element_type=F32)
    pb_ref[...] = jnp.dot(xb, w_ref[0, :, W_A_COLS:W_A_COLS + D_B_IN], preferred_element_type=F32)
    pif_ref[...] = jnp.dot(xb, w_ref[0, :, W_A_COLS + D_B_IN:W_IN_COLS], preferred_element_type=F32)


def _inproj_call(x, w, tm, layer):
    n = x.shape[0]
    return pl.pallas_call(
        _inproj_kernel,
        grid=(n // tm,),
        in_specs=[pl.BlockSpec((tm, D_MODEL), lambda i: (i, 0)),
                  pl.BlockSpec((1, D_MODEL, W_IN_COLS), lambda i: (layer, 0, 0))],
        out_specs=[pl.BlockSpec((tm, W_A_COLS), lambda i: (i, 0)),
                   pl.BlockSpec((tm, D_B_IN), lambda i: (i, 0)),
                   pl.BlockSpec((tm, W_IF_COLS), lambda i: (i, 0))],
        out_shape=[jax.ShapeDtypeStruct((n, W_A_COLS), F32),
                   jax.ShapeDtypeStruct((n, D_B_IN), F32),
                   jax.ShapeDtypeStruct((n, W_IF_COLS), F32)],
        compiler_params=pltpu.CompilerParams(dimension_semantics=("parallel",),
                                             vmem_limit_bytes=VMEM_LIMIT),
        name="inproj",
    )(x, w)


CONV_PAD = 8


def _mlstm_kernel(pa_ref, pif_ref, conv0_ref, c0_ref, n0_ref, m0_ref, convw_ref, convb_ref,
                  bif_ref, gn_ref, *rest, c, nc, nbb, chained):
    h_ref, c_out_ref, n_out_ref, m_out_ref, qk_scr, c_scr, n_scr, m_scr = rest[1:] if chained else rest
    ci = pl.program_id(1)
    rows = nbb * c
    cshift = int(math.log2(c))
    chains = [(nb, h) for nb in range(nbb) for h in range(H_A)]
    prev0 = CONV_PAD - (CONV_W - 1)

    @pl.when(ci == 0)
    def _():
        c_scr[...] = c0_ref[0]
        n_scr[...] = n0_ref[...]
        m_scr[...] = m0_ref[...]
        for nb in range(nbb):
            qk_scr[nb, prev0:CONV_PAD, :] = conv0_ref[nb]

    cat0 = lambda xs: xs[0] if len(xs) == 1 else jnp.concatenate(xs, axis=0)
    acc_l = []
    for nb in range(nbb):
        u = pa_ref[nb, :, 0:2 * D_A]
        qk_scr[nb, CONV_PAD:CONV_PAD + c, :] = u
        acc = convb_ref[...]
        for j in range(CONV_W):
            acc = acc + qk_scr[nb, prev0 + j:prev0 + j + c, :] * convw_ref[j:j + 1, :]
        qk_scr[nb, prev0:CONV_PAD, :] = u[c - (CONV_W - 1):c, :]
        acc_l.append(acc)
    qk = _silu(cat0(acc_l))

    gates = cat0([pif_ref[nb] for nb in range(nbb)]) + bif_ref[...]
    li_all = gates[:, 0:LANES]
    lf_all = _log_sigmoid(gates[:, LANES:2 * LANES])
    row = lax.broadcasted_iota(jnp.int32, (rows, rows), 0)
    col = lax.broadcasted_iota(jnp.int32, (rows, rows), 1)
    tril = jnp.where((col <= row) & ((row >> cshift) == (col >> cshift)), 1.0, 0.0)
    b_all = _dot_exact_lhs(tril, lf_all)
    z_all = li_all - b_all
    crow = lax.broadcasted_iota(jnp.int32, (c, c), 0)
    ccol = lax.broadcasted_iota(jnp.int32, (c, c), 1)
    causal = ccol <= crow
    hrow = lax.broadcasted_iota(jnp.int32, (H_A * c, LANES), 0)
    hlane = lax.broadcasted_iota(jnp.int32, (H_A * c, LANES), 1)
    head_pick = jnp.where((hrow >> cshift) == hlane, 1.0, 0.0)
    z_rows = [_dot_exact_lhs(head_pick, z_all[nb * c:(nb + 1) * c], NT) for nb in range(nbb)]
    lane1 = lax.broadcasted_iota(jnp.int32, (1, LANES), 1)

    rs = lambda nb: slice(nb * c, (nb + 1) * c)
    q_l = [qk[rs(nb), h * DH_A:(h + 1) * DH_A] for nb, h in chains]
    k_l = [qk[rs(nb), D_A + h * DH_A:D_A + (h + 1) * DH_A] * (DH_A ** -0.5) for nb, h in chains]
    v_l = [pa_ref[nb, :, 2 * D_A + h * DH_A:2 * D_A + (h + 1) * DH_A] for nb, h in chains]
    c_l = [c_scr[nb, h] for nb, h in chains]
    n_l = [n_scr[nb, h:h + 1, :] for nb, h in chains]
    b_col = [b_all[rs(nb), h:h + 1] for nb, h in chains]
    li_col = [li_all[rs(nb), h:h + 1] for nb, h in chains]
    m_prev = [m_scr[nb][:, h:h + 1] for nb, h in chains]
    dmat = [jnp.where(causal, bc + z_rows[nb][h * c:(h + 1) * c], -jnp.inf)
            for bc, (nb, h) in zip(b_col, chains)]
    m_inter = [bc + mp for bc, mp in zip(b_col, m_prev)]
    m_t = [jnp.maximum(mi, jnp.max(d, axis=-1, keepdims=True)) for mi, d in zip(m_inter, dmat)]
    qk_dot = [_dot(q, k, NT) for q, k in zip(q_l, k_l)]
    qc = [_dot(q, cm) for q, cm in zip(q_l, c_l)]
    s_l = [x * jnp.exp(d - mt) for x, d, mt in zip(qk_dot, dmat, m_t)]
    sv = [_dot(s, v) for s, v in zip(s_l, v_l)]
    b_last = [bc[c - 1:c, :] for bc in b_col]
    g_s = [bl - bc + li for bl, bc, li in zip(b_last, b_col, li_col)]
    m_new = [jnp.maximum(bl + mp, jnp.max(gs, axis=0, keepdims=True))
             for bl, mp, gs in zip(b_last, m_prev, g_s)]
    wk = [jnp.exp(gs - mn) * k for gs, mn, k in zip(g_s, m_new, k_l)]
    w_old = [jnp.exp(bl + mp - mn) for bl, mp, mn in zip(b_last, m_prev, m_new)]
    kv = [_dot(w, v, TN) for w, v in zip(wk, v_l)]
    for i, (nb, h) in enumerate(chains):
        c_scr[nb, h] = w_old[i] * c_l[i] + kv[i]
        n_scr[nb, h:h + 1, :] = w_old[i] * n_l[i] + jnp.sum(wk[i], axis=0, keepdims=True)
    for nb in range(nbb):
        m_row = m_scr[nb]
        for h in range(H_A):
            m_row = jnp.where(lane1 == h, m_new[nb * H_A + h], m_row)
        m_scr[nb] = m_row
    w_inter = [jnp.exp(mi - mt) for mi, mt in zip(m_inter, m_t)]
    qn = [jnp.sum(q * nv, axis=-1, keepdims=True) for q, nv in zip(q_l, n_l)]
    s_sum = [jnp.sum(s, axis=-1, keepdims=True) for s in s_l]
    den = [w * a + b for w, a, b in zip(w_inter, qn, s_sum)]
    hh = [(w * a + b) / jnp.maximum(jnp.abs(d), jnp.exp(-mt))
          for w, a, b, d, mt in zip(w_inter, qc, sv, den, m_t)]
    mu = [jnp.mean(x, axis=-1, keepdims=True) for x in hh]
    hc = [x - m for x, m in zip(hh, mu)]
    var = [jnp.mean(x * x, axis=-1, keepdims=True) for x in hc]
    for i, (nb, h) in enumerate(chains):
        sl = slice(h * DH_A, (h + 1) * DH_A)
        hn = hc[i] * lax.rsqrt(var[i] + GN_EPS_A)
        o_pre = pa_ref[nb, :, 3 * D_A + h * DH_A:3 * D_A + (h + 1) * DH_A]
        h_ref[nb, :, sl] = hn * gn_ref[:, sl] * _sigmoid(o_pre)

    @pl.when(ci == nc - 1)
    def _():
        c_out_ref[0] = c_scr[...]
        n_out_ref[...] = n_scr[...]
        m_out_ref[...] = m_scr[...]


def _mlstm_call(pa, pif, conv0, c_in, n0, m0, convw, convb, bif, gn, c_stack, *, batch, seq, nbb,
                layer_in, layer):
    c = math.gcd(seq, CHUNK)
    nc = seq // c
    tok = lambda w: pl.BlockSpec((nbb, c, w), lambda b, i: (b, i, 0))
    vec = lambda r, w: pl.BlockSpec((r, w), lambda b, i: (0, 0))
    st5 = lambda l: pl.BlockSpec((1, nbb, H_A, DH_A, DH_A), lambda b, i: (l, b, 0, 0, 0))
    st3 = lambda r, w: pl.BlockSpec((nbb, r, w), lambda b, i: (b, 0, 0))
    chained = c_stack is not None
    in_specs = [tok(W_A_COLS), tok(W_IF_COLS), st3(CONV_W - 1, 2 * D_A), st5(layer_in), st3(H_A, DH_A),
                st3(1, LANES), vec(CONV_W, 2 * D_A), vec(1, 2 * D_A), vec(1, W_IF_COLS), vec(1, D_A)]
    args = [pa, pif, conv0, c_in, n0, m0, convw, convb, bif, gn]
    if chained:
        in_specs.append(pl.BlockSpec(memory_space=pl.ANY))
        args.append(c_stack)
    return pl.pallas_call(
        functools.partial(_mlstm_kernel, c=c, nc=nc, nbb=nbb, chained=chained),
        grid=(batch // nbb, nc),
        in_specs=in_specs,
        out_specs=[tok(D_A), st5(layer), st3(H_A, DH_A), st3(1, LANES)],
        out_shape=[jax.ShapeDtypeStruct((batch, seq, D_A), F32),
                   jax.ShapeDtypeStruct((DEPTH, batch, H_A, DH_A, DH_A), F32),
                   jax.ShapeDtypeStruct((batch, H_A, DH_A), F32),
                   jax.ShapeDtypeStruct((batch, 1, LANES), F32)],
        scratch_shapes=[pltpu.VMEM((nbb, CONV_PAD + c, 2 * D_A), F32),
                        pltpu.VMEM((nbb, H_A, DH_A, DH_A), F32),
                        pltpu.VMEM((nbb, H_A, DH_A), F32),
                        pltpu.VMEM((nbb, 1, LANES), F32)],
        input_output_aliases={len(args) - 1: 1} if chained else {},
        compiler_params=pltpu.CompilerParams(dimension_semantics=("parallel", "arbitrary"),
                                             vmem_limit_bytes=VMEM_LIMIT),
        name="mlstm",
    )(*args)


SHIFT_PAD = 8


def _neumann_inverses(ms, size, t):
    row = lax.broadcasted_iota(jnp.int32, (size, size), 0)
    col = lax.broadcasted_iota(jnp.int32, (size, size), 1)
    eye = jnp.where(row == col, 1.0, 0.0)
    blk = min(SOLVE_BLOCK, t)
    shift = int(math.log2(blk))
    same = (row >> shift) == (col >> shift)
    ds = [jnp.where(same, m, 0.0) for m in ms]
    ps = [eye + d for d in ds]
    xs = ds
    for _ in range(shift - 1):
        xs = [_dot(x, x) for x in xs]
        ps = [p + _dot(p, x) for p, x in zip(ps, xs)]
    nblk = t // blk
    if nblk == 1:
        return ps, None
    ns = [_dot(p, m - d) for p, m, d in zip(ps, ms, ds)]
    qs = [eye + n for n in ns]
    ys = ns
    for _ in range(int(math.log2(nblk)) - 1):
        ys = [_dot(y, y) for y in ys]
        qs = [q + _dot(q, y) for q, y in zip(qs, ys)]
    return ps, qs


def _rwkv_kernel(pb_ref, shift0_ref, s0_ref, mu_ref, w0a0_ref, wlora_ref, g2_ref, kk_ref, ka_ref,
                 rk_ref, lng_ref, lnb_ref, *rest, t, nc, nbb, chained):
    h_ref, s_out_ref, sh_scr, s_scr = rest[1:] if chained else rest
    ci = pl.program_id(1)
    rows = nbb * t
    t2 = 2 * t
    groups = [(nb, pr) for nb in range(nbb) for pr in range(N_PAIRS)]

    brow = lax.broadcasted_iota(jnp.int32, (LANES, LANES), 0)
    bcol = lax.broadcasted_iota(jnp.int32, (LANES, LANES), 1)
    blockdiag = (brow >= DH_B) == (bcol >= DH_B)
    bd2 = jnp.where(blockdiag, 1.0, 0.0).astype(BF16)

    @pl.when(ci == 0)
    def _():
        er = lax.broadcasted_iota(jnp.int32, (DH_B, LANES), 0)
        ec = lax.broadcasted_iota(jnp.int32, (DH_B, LANES), 1)
        dup_cols = jnp.where((ec & (DH_B - 1)) == er, 1.0, 0.0)
        for nb in range(nbb):
            sh_scr[nb, SHIFT_PAD - 1:SHIFT_PAD, :] = shift0_ref[nb]
            for pr in range(N_PAIRS):
                x = s0_ref[0, nb, 2 * pr:2 * pr + 2].reshape(LANES, DH_B)
                s_scr[nb * N_PAIRS + pr] = jnp.where(blockdiag, _dot_exact_rhs(x, dup_cols), 0.0)

    p_l, prev_l = [], []
    for nb in range(nbb):
        p_nb = pb_ref[nb]
        sh_scr[nb, SHIFT_PAD:SHIFT_PAD + t, :] = p_nb
        prev_l.append(sh_scr[nb, SHIFT_PAD - 1:SHIFT_PAD - 1 + t, :])
        sh_scr[nb, SHIFT_PAD - 1:SHIFT_PAD, :] = p_nb[t - 1:t, :]
        p_l.append(p_nb)
    cat0 = lambda xs: xs[0] if len(xs) == 1 else jnp.concatenate(xs, axis=0)
    p = cat0(p_l)
    prev = cat0(prev_l)
    pb = p + (prev - p) * mu_ref[...]

    r = pb[:, 0:D_B]
    kr = pb[:, D_B:2 * D_B]
    vr = pb[:, 2 * D_B:3 * D_B]
    lora_in = pb[:, 3 * D_B:3 * D_B + LANES]
    gl = pb[:, 3 * D_B + LANES:3 * D_B + 2 * LANES]
    lane = lax.broadcasted_iota(jnp.int32, (rows, LANES), 1)
    lora_act = jnp.where(lane < R_DECAY, jnp.tanh(lora_in), lora_in)
    z = _dot(lora_act, wlora_ref[...]) + w0a0_ref[...]
    ld = -jnp.exp(_log_sigmoid(z[:, 0:D_B]) - 0.5)
    a = _sigmoid(z[:, D_B:2 * D_B])
    g = _dot(_sigmoid(gl), g2_ref[...])

    def seg_sum(x):
        xr = jnp.concatenate([x[:, q * LANES:(q + 1) * LANES] for q in range(N_PAIRS)], axis=0)
        hi = xr.astype(BF16)
        lo = (xr - hi.astype(F32)).astype(BF16)
        s = (jnp.dot(hi, bd2, preferred_element_type=F32) + jnp.dot(lo, bd2, preferred_element_type=F32))
        return jnp.concatenate([s[q * rows:(q + 1) * rows] for q in range(N_PAIRS)], axis=1)

    kk_raw = kr * kk_ref[...]
    kk = kk_raw / jnp.maximum(jnp.sqrt(seg_sum(kk_raw * kk_raw)), 1e-12)
    k2 = kr * (1.0 + (a - 1.0) * ka_ref[...])
    bonus = seg_sum(r * k2 * rk_ref[...]) * vr

    row = lax.broadcasted_iota(jnp.int32, (rows, rows), 0)
    col = lax.broadcasted_iota(jnp.int32, (rows, rows), 1)
    tshift = int(math.log2(t))
    tril = jnp.where((col <= row) & ((row >> tshift) == (col >> tshift)), 1.0, 0.0)
    lc = _dot_exact_lhs(tril, ld)
    lends = [lc[(nb + 1) * t - 1:(nb + 1) * t, :] for nb in range(nbb)]
    lend_rows = cat0([jnp.broadcast_to(le, (t, D_B)) for le in lends])
    e_nc = jnp.exp(-lc)
    e_end = jnp.exp(lend_rows - lc)
    b_raw = kk * a
    at = -kk * jnp.exp(lc - ld)
    rt = r * jnp.exp(lc)
    kt = k2 * e_nc
    bt = b_raw * e_nc
    kend = k2 * e_end
    bend = b_raw * e_end

    srow = lax.broadcasted_iota(jnp.int32, (t2, t2), 0)
    scol = lax.broadcasted_iota(jnp.int32, (t2, t2), 1)
    strict = scol < srow
    incl = scol <= srow
    lane2 = lax.broadcasted_iota(jnp.int32, (t2, LANES), 1)
    row2 = lax.broadcasted_iota(jnp.int32, (t2, LANES), 0)
    head_sel = (lane2 >= DH_B) == (row2 >= t)
    first_head = lax.broadcasted_iota(jnp.int32, (t, LANES), 1) < DH_B

    def blk(x, nb, pr):
        return x[nb * t:(nb + 1) * t, pr * LANES:(pr + 1) * LANES]

    def stack(x):
        return jnp.where(head_sel, jnp.concatenate([x, x], axis=0), 0.0)

    lhs = [jnp.concatenate([stack(blk(at, *gp)), stack(blk(rt, *gp))], axis=0) for gp in groups]
    rk_s = [stack(blk(kt, *gp)) for gp in groups]
    rb_s = [stack(blk(bt, *gp)) for gp in groups]
    v_p = [blk(vr, *gp) for gp in groups]
    v_s = [jnp.concatenate([v, v], axis=0) for v in v_p]
    s_p = [s_scr[i] for i in range(len(groups))]
    gk = [_dot(l, x, NT) for l, x in zip(lhs, rk_s)]
    gb = [_dot(l, x, NT) for l, x in zip(lhs, rb_s)]
    xs = [_dot(l, s, NT) for l, s in zip(lhs, s_p)]
    m_ab = [jnp.where(strict, x[0:t2], 0.0) for x in gb]
    ps, qs = _neumann_inverses(m_ab, t2, t)
    rhs = [x[0:t2] + _dot(jnp.where(strict, y[0:t2], 0.0), v) for x, y, v in zip(xs, gk, v_s)]
    sa_s = [_dot(pm, x) for pm, x in zip(ps, rhs)]
    if qs is not None:
        sa_s = [_dot(qm, x) for qm, x in zip(qs, sa_s)]
    o_s = [x[t2:] + _dot(jnp.where(incl, y[t2:], 0.0), v) + _dot(jnp.where(incl, w[t2:], 0.0), sa)
           for x, y, w, v, sa in zip(xs, gk, gb, v_s, sa_s)]
    o_p = [jnp.where(first_head, o[0:t], o[t:t2]) for o in o_s]
    sa_p = [jnp.where(first_head, sa[0:t], sa[t:t2]) for sa in sa_s]
    upd = [_dot(jnp.concatenate([v, sa], axis=0),
                jnp.concatenate([blk(kend, *gp), blk(bend, *gp)], axis=0), TN)
           for v, sa, gp in zip(v_p, sa_p, groups)]
    for i, (nb, pr) in enumerate(groups):
        dec = jnp.exp(lends[nb][:, pr * LANES:(pr + 1) * LANES])
        s_scr[i] = s_p[i] * dec + jnp.where(blockdiag, upd[i], 0.0)

    o = cat0([jnp.concatenate(o_p[nb * N_PAIRS:(nb + 1) * N_PAIRS], axis=1) for nb in range(nbb)])
    mean = seg_sum(o) * (1.0 / DH_B)
    oc = o - mean
    var = seg_sum(oc * oc) * (1.0 / DH_B)
    out = oc * lax.rsqrt(var + GN_EPS_B) * lng_ref[...] + lnb_ref[...]
    res = (out + bonus) * g
    for nb in range(nbb):
        h_ref[nb] = res[nb * t:(nb + 1) * t, :]

    @pl.when(ci == nc - 1)
    def _():
        fr = lax.broadcasted_iota(jnp.int32, (LANES, DH_B), 0)
        fc = lax.broadcasted_iota(jnp.int32, (LANES, DH_B), 1)
        dup_rows = jnp.where((fr & (DH_B - 1)) == fc, 1.0, 0.0)
        for i, (nb, pr) in enumerate(groups):
            packed = _dot_exact_rhs(s_scr[i], dup_rows)
            s_out_ref[0, nb, 2 * pr:2 * pr + 2] = packed.reshape(2, DH_B, DH_B)


def _rwkv_call(pb, shift0, s_in, mu, w0a0, wlora, g2, kk, ka, rk, lng, lnb, s_stack, *, batch, seq, nbb,
               layer_in, layer):
    t = math.gcd(seq, CHUNK)
    nc = seq // t
    tok = lambda w: pl.BlockSpec((nbb, t, w), lambda b, i: (b, i, 0))
    vec = lambda r, w: pl.BlockSpec((r, w), lambda b, i: (0, 0))
    st5 = lambda l: pl.BlockSpec((1, nbb, H_B, DH_B, DH_B), lambda b, i: (l, b, 0, 0, 0))
    chained = s_stack is not None
    in_specs = [tok(D_B_IN), pl.BlockSpec((nbb, 1, D_B_IN), lambda b, i: (b, 0, 0)), st5(layer_in),
                vec(1, D_B_IN), vec(1, 2 * D_B), vec(LANES, 2 * D_B), vec(R_GATE, D_B),
                vec(1, D_B), vec(1, D_B), vec(1, D_B), vec(1, D_B), vec(1, D_B)]
    args = [pb, shift0, s_in, mu, w0a0, wlora, g2, kk, ka, rk, lng, lnb]
    if chained:
        in_specs.append(pl.BlockSpec(memory_space=pl.ANY))
        args.append(s_stack)
    return pl.pallas_call(
        functools.partial(_rwkv_kernel, t=t, nc=nc, nbb=nbb, chained=chained),
        grid=(batch // nbb, nc),
        in_specs=in_specs,
        out_specs=[tok(D_B), st5(layer)],
        out_shape=[jax.ShapeDtypeStruct((batch, seq, D_B), F32),
                   jax.ShapeDtypeStruct((DEPTH, batch, H_B, DH_B, DH_B), F32)],
        scratch_shapes=[pltpu.VMEM((nbb, SHIFT_PAD + t, D_B_IN), F32),
                        pltpu.VMEM((nbb * N_PAIRS, LANES, LANES), F32)],
        input_output_aliases={len(args) - 1: 1} if chained else {},
        compiler_params=pltpu.CompilerParams(dimension_semantics=("parallel", "arbitrary"),
                                             vmem_limit_bytes=VMEM_LIMIT),
        name="rwkv",
    )(*args)


ROUTE_ROWS = 8


def _outproj_kernel(x_ref, ha_ref, hb_ref, wo_ref, g_ref, b_ref, wr_ref, br_ref, x1_ref, route_ref,
                    cnt_ref, *, tm):
    y = _dot(ha_ref[...], wo_ref[0, 0:D_A, :]) + _dot(hb_ref[...], wo_ref[0, D_A:D_A + D_B, :])
    x1 = _layer_norm(ALPHA * x_ref[...] + y, g_ref[...], b_ref[...], LN_EPS)
    x1_ref[...] = x1
    logits = lax.dot_general(wr_ref[...], x1, NT, precision=lax.Precision.HIGHEST,
                             preferred_element_type=F32)
    mx = jnp.max(logits, axis=0, keepdims=True)
    ex = jnp.exp(logits - mx)
    probs = ex / jnp.sum(ex, axis=0, keepdims=True)
    sel = probs + br_ref[...]
    neg = -jnp.inf

    def top2(rows):
        m1 = functools.reduce(jnp.maximum, rows)
        i1 = jnp.full(m1.shape, len(rows) - 1, jnp.int32)
        for j in range(len(rows) - 2, -1, -1):
            i1 = jnp.where(rows[j] == m1, j, i1)
        rest = [jnp.where(i1 == j, neg, rows[j]) for j in range(len(rows))]
        m2 = functools.reduce(jnp.maximum, rest)
        i2 = jnp.full(m2.shape, len(rows) - 1, jnp.int32)
        for j in range(len(rows) - 2, -1, -1):
            i2 = jnp.where(rest[j] == m2, j, i2)
        return m1, i1, m2, i2

    g_score, g_i1, g_i2 = [], [], []
    for gidx in range(N_GROUPS):
        rows = [sel[gidx * EXPERTS_PER_GROUP + j:gidx * EXPERTS_PER_GROUP + j + 1, :]
                for j in range(EXPERTS_PER_GROUP)]
        m1, i1, m2, i2 = top2(rows)
        g_score.append(m1 + m2)
        g_i1.append(i1)
        g_i2.append(i2)
    best = functools.reduce(jnp.maximum, g_score)
    grp = jnp.full(best.shape, N_GROUPS - 1, jnp.int32)
    for gidx in range(N_GROUPS - 2, -1, -1):
        grp = jnp.where(g_score[gidx] == best, gidx, grp)
    l1 = g_i1[N_GROUPS - 1]
    l2 = g_i2[N_GROUPS - 1]
    for gidx in range(N_GROUPS - 2, -1, -1):
        l1 = jnp.where(grp == gidx, g_i1[gidx], l1)
        l2 = jnp.where(grp == gidx, g_i2[gidx], l2)
    e1 = grp * EXPERTS_PER_GROUP + l1
    e2 = grp * EXPERTS_PER_GROUP + l2
    e_iota = lax.broadcasted_iota(jnp.int32, (N_EXPERTS, tm), 0)
    p1 = jnp.sum(jnp.where(e_iota == e1, probs, 0.0), axis=0, keepdims=True)
    p2 = jnp.sum(jnp.where(e_iota == e2, probs, 0.0), axis=0, keepdims=True)
    tot = p1 + p2
    r8 = lax.broadcasted_iota(jnp.int32, (ROUTE_ROWS, tm), 0)
    route_ref[...] = jnp.where(r8 == 0, e1.astype(F32),
                               jnp.where(r8 == 1, e2.astype(F32),
                                         jnp.where(r8 == 2, p1 / tot, jnp.where(r8 == 3, p2 / tot, 0.0))))
    picked = jnp.where((e_iota == e1) | (e_iota == e2), 1.0, 0.0)
    cnt_ref[0] = jnp.broadcast_to(jnp.sum(picked, axis=1, keepdims=True), (N_EXPERTS, LANES))


def _outproj_call(x, ha, hb, wo, g, b, wr_t, br, tm, layer):
    n = x.shape[0]
    row = lambda w: pl.BlockSpec((tm, w), lambda i: (i, 0))
    full = lambda r, w: pl.BlockSpec((r, w), lambda i: (0, 0))
    return pl.pallas_call(
        functools.partial(_outproj_kernel, tm=tm),
        grid=(n // tm,),
        in_specs=[row(D_MODEL), row(D_A), row(D_B),
                  pl.BlockSpec((1, D_MODEL, D_MODEL), lambda i: (layer, 0, 0)), full(1, D_MODEL),
                  full(1, D_MODEL), full(N_EXPERTS, D_MODEL), full(N_EXPERTS, 1)],
        out_specs=[row(D_MODEL), pl.BlockSpec((ROUTE_ROWS, tm), lambda i: (0, i)),
                   pl.BlockSpec((1, N_EXPERTS, LANES), lambda i: (i, 0, 0))],
        out_shape=[jax.ShapeDtypeStruct((n, D_MODEL), F32), jax.ShapeDtypeStruct((ROUTE_ROWS, n), F32),
                   jax.ShapeDtypeStruct((n // tm, N_EXPERTS, LANES), F32)],
        compiler_params=pltpu.CompilerParams(dimension_semantics=("parallel",),
                                             vmem_limit_bytes=VMEM_LIMIT),
        name="outproj",
    )(x, ha, hb, wo, g, b, wr_t, br)


MOE_BM = 128
MOE_CH = 512
MOE_CCH = 1024
MOE_GROUP = 4
MOE_NBUF = 3


def _moe_max_blocks(t):
    return -(-2 * t // MOE_BM) + N_EXPERTS - 1


def _moe_sort_kernel(nblk_ref, x_ref, route_ref, padoff_ref, tri_ref, xs_ref, cols_ref, *, t, n_tiles,
                     n_chunks):
    i = pl.program_id(0)

    @pl.when(i >= n_tiles)
    def _():
        xs_ref[...] = jnp.zeros_like(xs_ref)

    @pl.when(i < n_tiles)
    def _():
        s_used = nblk_ref[jnp.minimum(i, n_tiles - 1)] * MOE_BM
        xb = x_ref[...].astype(BF16)
        route = route_ref[...]
        e_iota = lax.broadcasted_iota(jnp.int32, (N_EXPERTS, t), 0)
        a1 = e_iota == route[0:1].astype(jnp.int32)
        a2 = e_iota == route[1:2].astype(jnp.int32)
        picked = jnp.where(a1 | a2, 1.0, 0.0).astype(BF16)
        rank = jnp.dot(picked, tri_ref[...], preferred_element_type=F32)
        base = padoff_ref[0][:, 0:1] + rank
        slot1 = jnp.sum(jnp.where(a1, base, 0.0), axis=0, keepdims=True)
        slot2 = jnp.sum(jnp.where(a2, base, 0.0), axis=0, keepdims=True)
        r8 = lax.broadcasted_iota(jnp.int32, (ROUTE_ROWS, t), 0)
        rows = jnp.where(r8 == 0, slot1, jnp.where(r8 == 1, slot2,
                                                   jnp.where(r8 == 2, route[2:3],
                                                             jnp.where(r8 == 3, route[3:4], 0.0))))
        padded = jnp.concatenate([rows, jnp.zeros((LANES - ROUTE_ROWS, t), F32)], axis=0)
        cols_ref[...] = padded.T
        s1 = slot1.astype(jnp.int32)
        s2 = slot2.astype(jnp.int32)
        for c in range(n_chunks):
            used = c * MOE_CH < s_used

            @pl.when(used)
            def _():
                s_iota = lax.broadcasted_iota(jnp.int32, (MOE_CH, t), 0) + c * MOE_CH
                onehot = jnp.where((s_iota == s1) | (s_iota == s2), 1.0, 0.0).astype(BF16)
                xs_ref[0, c * MOE_CH:(c + 1) * MOE_CH, :] = jnp.dot(
                    onehot, xb, preferred_element_type=F32).astype(BF16)

            @pl.when(jnp.logical_not(used))
            def _():
                xs_ref[0, c * MOE_CH:(c + 1) * MOE_CH, :] = jnp.zeros((MOE_CH, D_MODEL), BF16)


def _moe_expert_kernel(sblk_ref, ebase_ref, ecnt_ref, xs_hbm, wg_ref, wu_ref, wd_ref, ys_hbm,
                       wg_scr, wu_scr, wd_scr, xbuf, ybuf, in_sem, out_sem):
    e = pl.program_id(0)
    base = ebase_ref[e]
    cnt = ecnt_ref[e]
    npair = (cnt + MOE_GROUP - 1) // MOE_GROUP
    wg_scr[...] = wg_ref[0, 0].astype(BF16)
    wu_scr[...] = wu_ref[0, 0].astype(BF16)
    wd_scr[...] = wd_ref[0, 0].astype(BF16)
    xbuf[...] = jnp.zeros_like(xbuf)

    def rows(k):
        return pl.ds(pl.multiple_of(sblk_ref[base + k] * MOE_BM, MOE_BM), MOE_BM)

    def in_copy(k, slot, half):
        return pltpu.make_async_copy(xs_hbm.at[rows(k), :], xbuf.at[slot, half * MOE_BM:(half + 1) * MOE_BM, :],
                                     in_sem.at[slot, half])

    def out_copy(k, slot, half):
        return pltpu.make_async_copy(ybuf.at[slot, half * MOE_BM:(half + 1) * MOE_BM, :], ys_hbm.at[rows(k), :],
                                     out_sem.at[slot, half])

    def for_pair(p, make, act):
        slot = p % MOE_NBUF
        act(make(MOE_GROUP * p, slot, 0))
        for part in range(1, MOE_GROUP):
            @pl.when(MOE_GROUP * p + part < cnt)
            def _():
                act(make(MOE_GROUP * p + part, slot, part))

    start = lambda c: c.start()
    wait = lambda c: c.wait()

    for i in range(MOE_NBUF - 1):
        @pl.when(i < npair)
        def _():
            for_pair(i, in_copy, start)

    def pair(p, carry):
        slot = p % MOE_NBUF
        for_pair(p, in_copy, wait)
        ahead = p + MOE_NBUF - 1

        @pl.when(ahead < npair)
        def _():
            for_pair(ahead, in_copy, start)

        @pl.when(p >= MOE_NBUF)
        def _():
            for_pair(p - MOE_NBUF, out_copy, wait)

        xblk = xbuf[slot]
        hid = _silu(jnp.dot(xblk, wg_scr[...], preferred_element_type=F32)) * jnp.dot(
            xblk, wu_scr[...], preferred_element_type=F32)
        y = jnp.dot(hid.astype(BF16), wd_scr[...], preferred_element_type=F32)
        ybuf[slot] = y.astype(BF16)
        for_pair(p, out_copy, start)
        return carry

    lax.fori_loop(0, npair, pair, 0)

    for i in range(MOE_NBUF):
        @pl.when(npair > i)
        def _():
            for_pair(npair - 1 - i, out_copy, wait)


def _moe_combine_kernel(nblk_ref, x_ref, cols_ref, ys_ref, g_ref, b_ref, o_ref, acc_scr, *, t, n_chunks):
    s_used = nblk_ref[pl.program_id(0)] * MOE_BM
    cols = cols_ref[...]
    s1c = cols[:, 0:1].astype(jnp.int32)
    s2c = cols[:, 1:2].astype(jnp.int32)
    w1c = cols[:, 2:3]
    w2c = cols[:, 3:4]
    for c in range(n_chunks):
        @pl.when(c * MOE_CCH < s_used)
        def _():
            l_iota = lax.broadcasted_iota(jnp.int32, (t, MOE_CCH), 1) + c * MOE_CCH
            weighted = (jnp.where(l_iota == s1c, w1c, 0.0) + jnp.where(l_iota == s2c, w2c, 0.0)).astype(BF16)
            part = jnp.dot(weighted, ys_ref[0, c * MOE_CCH:(c + 1) * MOE_CCH, :],
                           preferred_element_type=F32)
            if c == 0:
                acc_scr[...] = part
            else:
                acc_scr[...] += part
    o_ref[...] = _layer_norm(ALPHA * x_ref[...] + acc_scr[...], g_ref[...], b_ref[...], LN_EPS)


def _moe_schedule(counts, tiles_per, maxb, bpt):
    cnt = counts[:, :, 0].astype(jnp.int32)
    cnt = cnt.reshape(-1, tiles_per, N_EXPERTS).sum(axis=1)
    n_tiles = cnt.shape[0]
    nblk = (cnt + MOE_BM - 1) // MOE_BM
    end = jnp.cumsum(nblk, axis=-1)
    start = end - nblk
    total = end[:, -1]
    padoff = jnp.broadcast_to((start * MOE_BM).astype(F32)[:, :, None], (n_tiles, N_EXPERTS, LANES))
    per_e = jnp.sum(nblk, axis=0)
    e_end = jnp.cumsum(per_e)
    e_base = e_end - per_e
    e_ids = jnp.arange(N_EXPERTS, dtype=jnp.int32)
    s_idx = jnp.arange(n_tiles * maxb, dtype=jnp.int32)
    last_e = jnp.max(jnp.where(per_e > 0, e_ids, 0))
    se = jnp.minimum(jnp.sum((s_idx[:, None] >= e_end[None, :]).astype(jnp.int32), axis=-1), last_e)
    oh_e = se[:, None] == e_ids[None, :]
    r = s_idx - jnp.sum(jnp.where(oh_e, e_base[None, :], 0), axis=-1)
    of_e = lambda a: jnp.sum(jnp.where(oh_e[:, None, :], a[None, :, :], 0), axis=-1)
    incl_s = of_e(jnp.cumsum(nblk, axis=0))
    tile = jnp.minimum(jnp.sum((incl_s <= r[:, None]).astype(jnp.int32), axis=-1), n_tiles - 1)
    oh_t = tile[:, None] == jnp.arange(n_tiles, dtype=jnp.int32)[None, :]
    of_t = lambda a: jnp.sum(jnp.where(oh_t, a, 0), axis=-1)
    j = of_t(of_e(start)) + r - (of_t(incl_s) - of_t(of_e(nblk)))
    sblk = jnp.where(s_idx < e_end[-1], tile * bpt + j, n_tiles * bpt)
    return total, padoff, sblk, e_base, per_e


def _moe_call(x, route, counts, wg, wu, wd, g, b, *, layer, tm):
    n = x.shape[0]
    t = 1024 if n % 1024 == 0 else n
    n_tiles = n // t
    maxb = _moe_max_blocks(t)
    s_alloc = -(-maxb * MOE_BM // MOE_CCH) * MOE_CCH
    n_chunks = s_alloc // MOE_CH
    bpt = s_alloc // MOE_BM
    nblk, padoff, sblk, e_base, e_cnt = _moe_schedule(counts, t // tm, maxb, bpt)
    tri = jnp.triu(jnp.ones((t, t), BF16), k=1)
    last = n_tiles - 1
    xs, cols = pl.pallas_call(
        functools.partial(_moe_sort_kernel, t=t, n_tiles=n_tiles, n_chunks=n_chunks),
        grid_spec=pltpu.PrefetchScalarGridSpec(
            num_scalar_prefetch=1,
            grid=(n_tiles + 1,),
            in_specs=[pl.BlockSpec((t, D_MODEL), lambda i, nb: (jnp.minimum(i, last), 0)),
                      pl.BlockSpec((ROUTE_ROWS, t), lambda i, nb: (0, jnp.minimum(i, last))),
                      pl.BlockSpec((1, N_EXPERTS, LANES), lambda i, nb: (jnp.minimum(i, last), 0, 0)),
                      pl.BlockSpec((t, t), lambda i, nb: (0, 0))],
            out_specs=[pl.BlockSpec((1, s_alloc, D_MODEL), lambda i, nb: (i, 0, 0)),
                       pl.BlockSpec((t, LANES), lambda i, nb: (jnp.minimum(i, last), 0))]),
        out_shape=[jax.ShapeDtypeStruct((n_tiles + 1, s_alloc, D_MODEL), BF16),
                   jax.ShapeDtypeStruct((n, LANES), F32)],
        compiler_params=pltpu.CompilerParams(dimension_semantics=("arbitrary",),
                                             vmem_limit_bytes=VMEM_LIMIT),
        name="moe_sort",
    )(nblk, x, route, padoff, tri)
    wspec = lambda r, c: pl.BlockSpec((1, 1, r, c), lambda e, sb, eb, ec: (layer, e, 0, 0))
    ys = pl.pallas_call(
        _moe_expert_kernel,
        grid_spec=pltpu.PrefetchScalarGridSpec(
            num_scalar_prefetch=3,
            grid=(N_EXPERTS,),
            in_specs=[pl.BlockSpec(memory_space=pl.ANY),
                      wspec(D_MODEL, D_EXP), wspec(D_MODEL, D_EXP), wspec(D_EXP, D_MODEL)],
            out_specs=pl.BlockSpec(memory_space=pl.ANY),
            scratch_shapes=[pltpu.VMEM((D_MODEL, D_EXP), BF16), pltpu.VMEM((D_MODEL, D_EXP), BF16),
                            pltpu.VMEM((D_EXP, D_MODEL), BF16),
                            pltpu.VMEM((MOE_NBUF, MOE_GROUP * MOE_BM, D_MODEL), BF16),
                            pltpu.VMEM((MOE_NBUF, MOE_GROUP * MOE_BM, D_MODEL), BF16),
                            pltpu.SemaphoreType.DMA((MOE_NBUF, MOE_GROUP)),
                            pltpu.SemaphoreType.DMA((MOE_NBUF, MOE_GROUP))]),
        out_shape=jax.ShapeDtypeStruct(((n_tiles + 1) * s_alloc, D_MODEL), BF16),
        input_output_aliases={3: 0},
        compiler_params=pltpu.CompilerParams(dimension_semantics=("arbitrary",),
                                             vmem_limit_bytes=VMEM_LIMIT),
        name="moe_experts",
    )(sblk, e_base, e_cnt, xs.reshape((n_tiles + 1) * s_alloc, D_MODEL), wg, wu, wd)
    vec = pl.BlockSpec((1, D_MODEL), lambda i, nb: (0, 0))
    return pl.pallas_call(
        functools.partial(_moe_combine_kernel, t=t, n_chunks=s_alloc // MOE_CCH),
        grid_spec=pltpu.PrefetchScalarGridSpec(
            num_scalar_prefetch=1,
            grid=(n_tiles,),
            in_specs=[pl.BlockSpec((t, D_MODEL), lambda i, nb: (i, 0)),
                      pl.BlockSpec((t, LANES), lambda i, nb: (i, 0)),
                      pl.BlockSpec((1, s_alloc, D_MODEL), lambda i, nb: (i, 0, 0)), vec, vec],
            out_specs=pl.BlockSpec((t, D_MODEL), lambda i, nb: (i, 0)),
            scratch_shapes=[pltpu.VMEM((t, D_MODEL), F32)]),
        out_shape=jax.ShapeDtypeStruct((n, D_MODEL), F32),
        compiler_params=pltpu.CompilerParams(dimension_semantics=("parallel",),
                                             vmem_limit_bytes=VMEM_LIMIT),
        name="moe_combine",
    )(nblk, x, cols, ys.reshape(n_tiles + 1, s_alloc, D_MODEL), g, b)


def _row_tile(n):
    return 512 if n % 512 == 0 else n


def _seqs_per_step(batch, want):
    return want if batch % want == 0 else 1


def kernel(x_prompt, x_sample, state_mlstm_C, state_mlstm_n, state_mlstm_m, state_mlstm_conv, state_rwkv_S, state_rwkv_shift, ln0_g, ln0_b, w_in, conv_w, conv_b, b_i, b_f, gn_a_g, mu_shift, w0, w2, a0, a2, g2, k_k, k_a, r_k, lnx_g, lnx_b, w_out, ln1_g, ln1_b, w_router, b_router, we_gate, we_up, we_down, ln2_g, ln2_b):
    d_a_in = 4 * D_A + 2 * H_A
    zpad = jnp.zeros((DEPTH, D_MODEL, LANES - H_A), F32)
    w_cat = jnp.concatenate(
        [w_in[:, :, 0:4 * D_A], w_in[:, :, d_a_in:], w_in[:, :, 4 * D_A:4 * D_A + H_A], zpad,
         w_in[:, :, 4 * D_A + H_A:d_a_in], zpad], axis=-1).astype(BF16)
    bif = jnp.zeros((DEPTH, 1, W_IF_COLS), F32)
    bif = bif.at[:, 0, 0:H_A].set(b_i).at[:, 0, LANES:LANES + H_A].set(b_f)
    wlora = jnp.zeros((DEPTH, LANES, 2 * D_B), F32)
    wlora = wlora.at[:, 0:R_DECAY, 0:D_B].set(w2).at[:, R_DECAY:, D_B:].set(a2).astype(BF16)
    w0a0 = jnp.concatenate([w0, a0], axis=-1)[:, None, :]
    g2b = g2.astype(BF16)
    wob = w_out.astype(BF16)
    wr_t = w_router.T
    br = b_router[:, None]
    r1 = lambda v: v[None, :]

    def run(x3, states):
        nb, seq, _ = x3.shape
        n = nb * seq
        tm = _row_tile(n)
        rwkv_nbb = _seqs_per_step(nb, 4 if seq >= CHUNK else 8)
        mlstm_nbb = _seqs_per_step(nb, 2 if seq >= CHUNK else 8)
        x = _ln_call(x3.reshape(n, D_MODEL), r1(ln0_g), r1(ln0_b), tm)
        outs = []
        c_stack = jnp.zeros((DEPTH, nb, H_A, DH_A, DH_A), F32)
        s_stack = jnp.zeros((DEPTH, nb, H_B, DH_B, DH_B), F32)
        for l in range(DEPTH):
            if states is None:
                c_in = jnp.zeros((1, nb, H_A, DH_A, DH_A), F32)
                s_in = jnp.zeros((1, nb, H_B, DH_B, DH_B), F32)
                layer_in = 0
                n0 = jnp.zeros((nb, H_A, DH_A), F32)
                m0 = jnp.zeros((nb, H_A), F32)
                conv0 = jnp.zeros((nb, CONV_W - 1, 2 * D_A), F32)
                shift0 = jnp.zeros((nb, D_B_IN), F32)
            else:
                c_in, s_in, layer_in = states[0], states[4], l
                n0, m0, conv0, shift0 = (states[k][l] for k in (1, 2, 3, 5))
            m0p = jnp.zeros((nb, 1, LANES), F32).at[:, 0, 0:H_A].set(m0)
            pa, pb, pif = _inproj_call(x, w_cat, tm, l)
            ha, c_stack, n_new, m_new = _mlstm_call(
                pa.reshape(nb, seq, W_A_COLS), pif.reshape(nb, seq, W_IF_COLS), conv0, c_in, n0, m0p,
                conv_w[l], r1(conv_b[l]), bif[l], r1(gn_a_g[l]), c_stack, batch=nb, seq=seq,
                nbb=mlstm_nbb, layer_in=layer_in, layer=l)
            ha = ha.reshape(n, D_A)
            hb, s_stack = _rwkv_call(
                pb.reshape(nb, seq, D_B_IN), shift0[:, None, :], s_in, r1(mu_shift[l]), w0a0[l], wlora[l],
                g2b[l], r1(k_k[l]), r1(k_a[l]), r1(r_k[l].reshape(D_B)), r1(lnx_g[l]), r1(lnx_b[l]),
                s_stack, batch=nb, seq=seq, nbb=rwkv_nbb, layer_in=layer_in, layer=l)
            hb = hb.reshape(n, D_B)
            x1, route, counts = _outproj_call(x, ha, hb, wob, r1(ln1_g[l]), r1(ln1_b[l]), wr_t, br, tm, l)
            x = _moe_call(x1, route, counts, we_gate, we_up, we_down, r1(ln2_g[l]), r1(ln2_b[l]),
                          layer=l, tm=tm)
            pa3 = pa.reshape(nb, seq, W_A_COLS)
            full = jnp.concatenate([conv0, pa3[:, :, 0:2 * D_A]], axis=1) if seq < CONV_W - 1 else pa3[:, :, 0:2 * D_A]
            conv_new = full[:, -(CONV_W - 1):, :]
            shift_new = pb.reshape(nb, seq, D_B_IN)[:, -1, :]
            outs.append((n_new, m_new[:, 0, 0:H_A], conv_new, shift_new))
        n_all, m_all, conv_all, shift_all = (jnp.stack(s) for s in zip(*outs))
        return x.reshape(nb, seq, D_MODEL), (c_stack, n_all, m_all, conv_all, s_stack, shift_all)

    y_prompt, (p_c, p_n, p_m, p_conv, p_s, p_shift) = run(x_prompt, None)
    y_sample, (s_c, s_n, s_m, s_conv, s_s, s_shift) = run(
        x_sample, (state_mlstm_C, state_mlstm_n, state_mlstm_m, state_mlstm_conv, state_rwkv_S,
                   state_rwkv_shift))
    return (y_prompt, y_sample, p_c, p_n, p_m, p_conv, p_s, p_shift, s_c, s_n, s_m, s_conv, s_s, s_shift)
```

```python
import functools
import math

import jax
import jax.numpy as jnp
from jax import lax
from jax.experimental import pallas as pl
from jax.experimental.pallas import tpu as pltpu

F32 = jnp.float32
BF16 = jnp.bfloat16

D_MODEL = 1024
DEPTH = 4
D_A = 512
D_B = 512
DH_A = 128
H_A = 4
DH_B = 64
H_B = 8
N_PAIRS = H_B // 2
CONV_W = 4
CHUNK = 64
R_DECAY = 64
R_AAA = 64
R_GATE = 128
D_B_IN = 3 * D_B + R_DECAY + R_AAA + R_GATE
N_EXPERTS = 16
N_GROUPS = 4
EXPERTS_PER_GROUP = 4
D_EXP = 512
ALPHA = (2 * DEPTH) ** 0.25
LN_EPS = 1e-5
GN_EPS_A = 1e-6
GN_EPS_B = 64e-5

LANES = 128
SOLVE_BLOCK = 16
VMEM_LIMIT = 48 * 1024 * 1024

NN = (((1,), (0,)), ((), ()))
NT = (((1,), (1,)), ((), ()))
TN = (((0,), (0,)), ((), ()))


def _dot(a, b, dims=NN):
    return lax.dot_general(a.astype(BF16), b.astype(BF16), dims, preferred_element_type=F32)


def _split3(x):
    hi = x.astype(BF16)
    r1 = x - hi.astype(F32)
    mid = r1.astype(BF16)
    lo = (r1 - mid.astype(F32)).astype(BF16)
    return hi, mid, lo


def _dot_exact_lhs(a, x, dims=NN):
    a = a.astype(BF16)
    hi, mid, lo = _split3(x)
    d = lambda p: lax.dot_general(a, p, dims, preferred_element_type=F32)
    return d(hi) + d(mid) + d(lo)


def _dot_exact_rhs(x, b, dims=NN):
    b = b.astype(BF16)
    hi, mid, lo = _split3(x)
    d = lambda p: lax.dot_general(p, b, dims, preferred_element_type=F32)
    return d(hi) + d(mid) + d(lo)


def _layer_norm(x, g, b, eps):
    mu = jnp.mean(x, axis=-1, keepdims=True)
    xc = x - mu
    var = jnp.mean(xc * xc, axis=-1, keepdims=True)
    return xc * lax.rsqrt(var + eps) * g + b


def _sigmoid(x):
    return 1.0 / (1.0 + jnp.exp(-x))


def _log_sigmoid(x):
    return jnp.minimum(x, 0.0) - jnp.log(1.0 + jnp.exp(-jnp.abs(x)))


def _silu(x):
    return x * _sigmoid(x)


W_A_COLS = 4 * D_A
W_IF_COLS = 2 * LANES
W_IN_COLS = W_A_COLS + D_B_IN + W_IF_COLS


def _inproj_kernel(x_ref, w_ref, *rest, input_norm):
    if input_norm:
        g_ref, b_ref, pa_ref, pb_ref, pif_ref, xn_ref = rest
        x = _layer_norm(x_ref[...], g_ref[...], b_ref[...], LN_EPS)
        xn_ref[...] = x
    else:
        pa_ref, pb_ref, pif_ref = rest
        x = x_ref[...]
    xb = x.astype(BF16)
    pa_ref[...] = jnp.dot(xb, w_ref[0, :, 0:W_A_COLS], preferred_element_type=F32)
    pb_ref[...] = jnp.dot(xb, w_ref[0, :, W_A_COLS:W_A_COLS + D_B_IN], preferred_element_type=F32)
    pif_ref[...] = jnp.dot(xb, w_ref[0, :, W_A_COLS + D_B_IN:W_IN_COLS], preferred_element_type=F32)


def _inproj_call(x, w, tm, layer, norm=None):
    n = x.shape[0]
    row = lambda c: pl.BlockSpec((tm, c), lambda i: (i, 0))
    vec = pl.BlockSpec((1, D_MODEL), lambda i: (0, 0))
    in_specs = [row(D_MODEL), pl.BlockSpec((1, D_MODEL, W_IN_COLS), lambda i: (layer, 0, 0))]
    out_specs = [row(W_A_COLS), row(D_B_IN), row(W_IF_COLS)]
    out_shape = [jax.ShapeDtypeStruct((n, W_A_COLS), F32), jax.ShapeDtypeStruct((n, D_B_IN), F32),
                 jax.ShapeDtypeStruct((n, W_IF_COLS), F32)]
    args = [x, w]
    if norm is not None:
        in_specs += [vec, vec]
        out_specs.append(row(D_MODEL))
        out_shape.append(jax.ShapeDtypeStruct((n, D_MODEL), F32))
        args += list(norm)
    return pl.pallas_call(
        functools.partial(_inproj_kernel, input_norm=norm is not None),
        grid=(n // tm,),
        in_specs=in_specs,
        out_specs=out_specs,
        out_shape=out_shape,
        compiler_params=pltpu.CompilerParams(dimension_semantics=("parallel",),
                                             vmem_limit_bytes=VMEM_LIMIT),
        name="inproj",
    )(*args)


CONV_PAD = 8


def _mlstm_kernel(pa_ref, pif_ref, conv0_ref, c0_ref, n0_ref, m0_ref, convw_ref, convb_ref,
                  bif_ref, gn_ref, *rest, c, nc, nbb, chained):
    h_ref, c_out_ref, n_out_ref, m_out_ref, qk_scr, c_scr, n_scr, m_scr = rest[1:] if chained else rest
    ci = pl.program_id(1)
    rows = nbb * c
    cshift = int(math.log2(c))
    chains = [(nb, h) for nb in range(nbb) for h in range(H_A)]
    prev0 = CONV_PAD - (CONV_W - 1)

    @pl.when(ci == 0)
    def _():
        c_scr[...] = c0_ref[0]
        n_scr[...] = n0_ref[...]
        m_scr[...] = m0_ref[...]
        for nb in range(nbb):
            qk_scr[nb, prev0:CONV_PAD, :] = conv0_ref[nb]

    cat0 = lambda xs: xs[0] if len(xs) == 1 else jnp.concatenate(xs, axis=0)
    acc_l = []
    for nb in range(nbb):
        u = pa_ref[nb, :, 0:2 * D_A]
        qk_scr[nb, CONV_PAD:CONV_PAD + c, :] = u
        acc = convb_ref[...]
        for j in range(CONV_W):
            acc = acc + qk_scr[nb, prev0 + j:prev0 + j + c, :] * convw_ref[j:j + 1, :]
        qk_scr[nb, prev0:CONV_PAD, :] = u[c - (CONV_W - 1):c, :]
        acc_l.append(acc)
    qk = _silu(cat0(acc_l))

    gates = cat0([pif_ref[nb] for nb in range(nbb)]) + bif_ref[...]
    li_all = gates[:, 0:LANES]
    lf_all = _log_sigmoid(gates[:, LANES:2 * LANES])
    row = lax.broadcasted_iota(jnp.int32, (rows, rows), 0)
    col = lax.broadcasted_iota(jnp.int32, (rows, rows), 1)
    tril = jnp.where((col <= row) & ((row >> cshift) == (col >> cshift)), 1.0, 0.0)
    b_all = _dot_exact_lhs(tril, lf_all)
    z_all = li_all - b_all
    crow = lax.broadcasted_iota(jnp.int32, (c, c), 0)
    ccol = lax.broadcasted_iota(jnp.int32, (c, c), 1)
    causal = ccol <= crow
    hrow = lax.broadcasted_iota(jnp.int32, (H_A * c, LANES), 0)
    hlane = lax.broadcasted_iota(jnp.int32, (H_A * c, LANES), 1)
    head_pick = jnp.where((hrow >> cshift) == hlane, 1.0, 0.0)
    z_rows = [_dot_exact_lhs(head_pick, z_all[nb * c:(nb + 1) * c], NT) for nb in range(nbb)]
    lane1 = lax.broadcasted_iota(jnp.int32, (1, LANES), 1)

    rs = lambda nb: slice(nb * c, (nb + 1) * c)
    q_l = [qk[rs(nb), h * DH_A:(h + 1) * DH_A] for nb, h in chains]
    k_l = [qk[rs(nb), D_A + h * DH_A:D_A + (h + 1) * DH_A] * (DH_A ** -0.5) for nb, h in chains]
    v_l = [pa_ref[nb, :, 2 * D_A + h * DH_A:2 * D_A + (h + 1) * DH_A] for nb, h in chains]
    c_l = [c_scr[nb, h] for nb, h in chains]
    n_l = [n_scr[nb, h:h + 1, :] for nb, h in chains]
    b_col = [b_all[rs(nb), h:h + 1] for nb, h in chains]
    li_col = [li_all[rs(nb), h:h + 1] for nb, h in chains]
    m_prev = [m_scr[nb][:, h:h + 1] for nb, h in chains]
    dmat = [jnp.where(causal, bc + z_rows[nb][h * c:(h + 1) * c], -jnp.inf)
            for bc, (nb, h) in zip(b_col, chains)]
    m_inter = [bc + mp for bc, mp in zip(b_col, m_prev)]
    m_t = [jnp.maximum(mi, jnp.max(d, axis=-1, keepdims=True)) for mi, d in zip(m_inter, dmat)]
    qk_dot = [_dot(q, k, NT) for q, k in zip(q_l, k_l)]
    qc = [_dot(q, cm) for q, cm in zip(q_l, c_l)]
    s_l = [x * jnp.exp(d - mt) for x, d, mt in zip(qk_dot, dmat, m_t)]
    sv = [_dot(s, v) for s, v in zip(s_l, v_l)]
    b_last = [bc[c - 1:c, :] for bc in b_col]
    g_s = [bl - bc + li for bl, bc, li in zip(b_last, b_col, li_col)]
    m_new = [jnp.maximum(bl + mp, jnp.max(gs, axis=0, keepdims=True))
             for bl, mp, gs in zip(b_last, m_prev, g_s)]
    wk = [jnp.exp(gs - mn) * k for gs, mn, k in zip(g_s, m_new, k_l)]
    w_old = [jnp.exp(bl + mp - mn) for bl, mp, mn in zip(b_last, m_prev, m_new)]
    kv = [_dot(w, v, TN) for w, v in zip(wk, v_l)]
    for i, (nb, h) in enumerate(chains):
        c_scr[nb, h] = w_old[i] * c_l[i] + kv[i]
        n_scr[nb, h:h + 1, :] = w_old[i] * n_l[i] + jnp.sum(wk[i], axis=0, keepdims=True)
    for nb in range(nbb):
        m_row = m_scr[nb]
        for h in range(H_A):
            m_row = jnp.where(lane1 == h, m_new[nb * H_A + h], m_row)
        m_scr[nb] = m_row
    w_inter = [jnp.exp(mi - mt) for mi, mt in zip(m_inter, m_t)]
    qn = [jnp.sum(q * nv, axis=-1, keepdims=True) for q, nv in zip(q_l, n_l)]
    s_sum = [jnp.sum(s, axis=-1, keepdims=True) for s in s_l]
    den = [w * a + b for w, a, b in zip(w_inter, qn, s_sum)]
    hh = [(w * a + b) / jnp.maximum(jnp.abs(d), jnp.exp(-mt))
          for w, a, b, d, mt in zip(w_inter, qc, sv, den, m_t)]
    mu = [jnp.mean(x, axis=-1, keepdims=True) for x in hh]
    hc = [x - m for x, m in zip(hh, mu)]
    var = [jnp.mean(x * x, axis=-1, keepdims=True) for x in hc]
    for i, (nb, h) in enumerate(chains):
        sl = slice(h * DH_A, (h + 1) * DH_A)
        hn = hc[i] * lax.rsqrt(var[i] + GN_EPS_A)
        o_pre = pa_ref[nb, :, 3 * D_A + h * DH_A:3 * D_A + (h + 1) * DH_A]
        h_ref[nb, :, sl] = hn * gn_ref[:, sl] * _sigmoid(o_pre)

    @pl.when(ci == nc - 1)
    def _():
        c_out_ref[0] = c_scr[...]
        n_out_ref[...] = n_scr[...]
        m_out_ref[...] = m_scr[...]


def _mlstm_call(pa, pif, conv0, c_in, n0, m0, convw, convb, bif, gn, c_stack, *, batch, seq, nbb,
                layer_in, layer):
    c = math.gcd(seq, CHUNK)
    nc = seq // c
    tok = lambda w: pl.BlockSpec((nbb, c, w), lambda b, i: (b, i, 0))
    vec = lambda r, w: pl.BlockSpec((r, w), lambda b, i: (0, 0))
    st5 = lambda l: pl.BlockSpec((1, nbb, H_A, DH_A, DH_A), lambda b, i: (l, b, 0, 0, 0))
    st3 = lambda r, w: pl.BlockSpec((nbb, r, w), lambda b, i: (b, 0, 0))
    chained = c_stack is not None
    in_specs = [tok(W_A_COLS), tok(W_IF_COLS), st3(CONV_W - 1, 2 * D_A), st5(layer_in), st3(H_A, DH_A),
                st3(1, LANES), vec(CONV_W, 2 * D_A), vec(1, 2 * D_A), vec(1, W_IF_COLS), vec(1, D_A)]
    args = [pa, pif, conv0, c_in, n0, m0, convw, convb, bif, gn]
    if chained:
        in_specs.append(pl.BlockSpec(memory_space=pl.ANY))
        args.append(c_stack)
    return pl.pallas_call(
        functools.partial(_mlstm_kernel, c=c, nc=nc, nbb=nbb, chained=chained),
        grid=(batch // nbb, nc),
        in_specs=in_specs,
        out_specs=[tok(D_A), st5(layer), st3(H_A, DH_A), st3(1, LANES)],
        out_shape=[jax.ShapeDtypeStruct((batch, seq, D_A), F32),
                   jax.ShapeDtypeStruct((DEPTH, batch, H_A, DH_A, DH_A), F32),
                   jax.ShapeDtypeStruct((batch, H_A, DH_A), F32),
                   jax.ShapeDtypeStruct((batch, 1, LANES), F32)],
        scratch_shapes=[pltpu.VMEM((nbb, CONV_PAD + c, 2 * D_A), F32),
                        pltpu.VMEM((nbb, H_A, DH_A, DH_A), F32),
                        pltpu.VMEM((nbb, H_A, DH_A), F32),
                        pltpu.VMEM((nbb, 1, LANES), F32)],
        input_output_aliases={len(args) - 1: 1} if chained else {},
        compiler_params=pltpu.CompilerParams(dimension_semantics=("parallel", "arbitrary"),
                                             vmem_limit_bytes=VMEM_LIMIT),
        name="mlstm",
    )(*args)


SHIFT_PAD = 8


def _neumann_inverses(ms, size, t):
    row = lax.broadcasted_iota(jnp.int32, (size, size), 0)
    col = lax.broadcasted_iota(jnp.int32, (size, size), 1)
    eye = jnp.where(row == col, 1.0, 0.0)
    blk = min(SOLVE_BLOCK, t)
    shift = int(math.log2(blk))
    same = (row >> shift) == (col >> shift)
    ds = [jnp.where(same, m, 0.0) for m in ms]
    ps = [eye + d for d in ds]
    xs = ds
    for _ in range(shift - 1):
        xs = [_dot(x, x) for x in xs]
        ps = [p + _dot(p, x) for p, x in zip(ps, xs)]
    nblk = t // blk
    if nblk == 1:
        return ps, None
    ns = [_dot(p, m - d) for p, m, d in zip(ps, ms, ds)]
    qs = [eye + n for n in ns]
    ys = ns
    for _ in range(int(math.log2(nblk)) - 1):
        ys = [_dot(y, y) for y in ys]
        qs = [q + _dot(q, y) for q, y in zip(qs, ys)]
    return ps, qs


def _rwkv_kernel(pb_ref, shift0_ref, s0_ref, mu_ref, w0a0_ref, wlora_ref, g2_ref, kk_ref, ka_ref,
                 rk_ref, lng_ref, lnb_ref, *rest, t, nc, nbb, chained):
    h_ref, s_out_ref, sh_scr, s_scr = rest[1:] if chained else rest
    ci = pl.program_id(1)
    rows = nbb * t
    t2 = 2 * t
    groups = [(nb, pr) for nb in range(nbb) for pr in range(N_PAIRS)]

    brow = lax.broadcasted_iota(jnp.int32, (LANES, LANES), 0)
    bcol = lax.broadcasted_iota(jnp.int32, (LANES, LANES), 1)
    blockdiag = (brow >= DH_B) == (bcol >= DH_B)
    bd2 = jnp.where(blockdiag, 1.0, 0.0).astype(BF16)

    @pl.when(ci == 0)
    def _():
        er = lax.broadcasted_iota(jnp.int32, (DH_B, LANES), 0)
        ec = lax.broadcasted_iota(jnp.int32, (DH_B, LANES), 1)
        dup_cols = jnp.where((ec & (DH_B - 1)) == er, 1.0, 0.0)
        for nb in range(nbb):
            sh_scr[nb, SHIFT_PAD - 1:SHIFT_PAD, :] = shift0_ref[nb]
            for pr in range(N_PAIRS):
                x = s0_ref[0, nb, 2 * pr:2 * pr + 2].reshape(LANES, DH_B)
                s_scr[nb * N_PAIRS + pr] = jnp.where(blockdiag, _dot_exact_rhs(x, dup_cols), 0.0)

    p_l, prev_l = [], []
    for nb in range(nbb):
        p_nb = pb_ref[nb]
        sh_scr[nb, SHIFT_PAD:SHIFT_PAD + t, :] = p_nb
        prev_l.append(sh_scr[nb, SHIFT_PAD - 1:SHIFT_PAD - 1 + t, :])
        sh_scr[nb, SHIFT_PAD - 1:SHIFT_PAD, :] = p_nb[t - 1:t, :]
        p_l.append(p_nb)
    cat0 = lambda xs: xs[0] if len(xs) == 1 else jnp.concatenate(xs, axis=0)
    p = cat0(p_l)
    prev = cat0(prev_l)
    pb = p + (prev - p) * mu_ref[...]

    r = pb[:, 0:D_B]
    kr = pb[:, D_B:2 * D_B]
    vr = pb[:, 2 * D_B:3 * D_B]
    lora_in = pb[:, 3 * D_B:3 * D_B + LANES]
    gl = pb[:, 3 * D_B + LANES:3 * D_B + 2 * LANES]
    lane = lax.broadcasted_iota(jnp.int32, (rows, LANES), 1)
    lora_act = jnp.where(lane < R_DECAY, jnp.tanh(lora_in), lora_in)
    z = _dot(lora_act, wlora_ref[...]) + w0a0_ref[...]
    ld = -jnp.exp(_log_sigmoid(z[:, 0:D_B]) - 0.5)
    a = _sigmoid(z[:, D_B:2 * D_B])
    g = _dot(_sigmoid(gl), g2_ref[...])

    def seg_sum(x):
        xr = jnp.concatenate([x[:, q * LANES:(q + 1) * LANES] for q in range(N_PAIRS)], axis=0)
        hi = xr.astype(BF16)
        lo = (xr - hi.astype(F32)).astype(BF16)
        s = (jnp.dot(hi, bd2, preferred_element_type=F32) + jnp.dot(lo, bd2, preferred_element_type=F32))
        return jnp.concatenate([s[q * rows:(q + 1) * rows] for q in range(N_PAIRS)], axis=1)

    kk_raw = kr * kk_ref[...]
    kk = kk_raw / jnp.maximum(jnp.sqrt(seg_sum(kk_raw * kk_raw)), 1e-12)
    k2 = kr * (1.0 + (a - 1.0) * ka_ref[...])
    bonus = seg_sum(r * k2 * rk_ref[...]) * vr

    row = lax.broadcasted_iota(jnp.int32, (rows, rows), 0)
    col = lax.broadcasted_iota(jnp.int32, (rows, rows), 1)
    tshift = int(math.log2(t))
    tril = jnp.where((col <= row) & ((row >> tshift) == (col >> tshift)), 1.0, 0.0)
    lc = _dot_exact_lhs(tril, ld)
    lends = [lc[(nb + 1) * t - 1:(nb + 1) * t, :] for nb in range(nbb)]
    lend_rows = cat0([jnp.broadcast_to(le, (t, D_B)) for le in lends])
    e_nc = jnp.exp(-lc)
    e_end = jnp.exp(lend_rows - lc)
    b_raw = kk * a
    at = -kk * jnp.exp(lc - ld)
    rt = r * jnp.exp(lc)
    kt = k2 * e_nc
    bt = b_raw * e_nc
    kend = k2 * e_end
    bend = b_raw * e_end

    srow = lax.broadcasted_iota(jnp.int32, (t2, t2), 0)
    scol = lax.broadcasted_iota(jnp.int32, (t2, t2), 1)
    strict = scol < srow
    incl = scol <= srow
    lane2 = lax.broadcasted_iota(jnp.int32, (t2, LANES), 1)
    row2 = lax.broadcasted_iota(jnp.int32, (t2, LANES), 0)
    head_sel = (lane2 >= DH_B) == (row2 >= t)
    first_head = lax.broadcasted_iota(jnp.int32, (t, LANES), 1) < DH_B

    def blk(x, nb, pr):
        return x[nb * t:(nb + 1) * t, pr * LANES:(pr + 1) * LANES]

    def stack(x):
        return jnp.where(head_sel, jnp.concatenate([x, x], axis=0), 0.0)

    lhs = [jnp.concatenate([stack(blk(at, *gp)), stack(blk(rt, *gp))], axis=0) for gp in groups]
    rk_s = [stack(blk(kt, *gp)) for gp in groups]
    rb_s = [stack(blk(bt, *gp)) for gp in groups]
    v_p = [blk(vr, *gp) for gp in groups]
    v_s = [jnp.concatenate([v, v], axis=0) for v in v_p]
    s_p = [s_scr[i] for i in range(len(groups))]
    gk = [_dot(l, x, NT) for l, x in zip(lhs, rk_s)]
    gb = [_dot(l, x, NT) for l, x in zip(lhs, rb_s)]
    xs = [_dot(l, s, NT) for l, s in zip(lhs, s_p)]
    m_ab = [jnp.where(strict, x[0:t2], 0.0) for x in gb]
    ps, qs = _neumann_inverses(m_ab, t2, t)
    rhs = [x[0:t2] + _dot(jnp.where(strict, y[0:t2], 0.0), v) for x, y, v in zip(xs, gk, v_s)]
    sa_s = [_dot(pm, x) for pm, x in zip(ps, rhs)]
    if qs is not None:
        sa_s = [_dot(qm, x) for qm, x in zip(qs, sa_s)]
    o_s = [x[t2:] + _dot(jnp.where(incl, y[t2:], 0.0), v) + _dot(jnp.where(incl, w[t2:], 0.0), sa)
           for x, y, w, v, sa in zip(xs, gk, gb, v_s, sa_s)]
    o_p = [jnp.where(first_head, o[0:t], o[t:t2]) for o in o_s]
    sa_p = [jnp.where(first_head, sa[0:t], sa[t:t2]) for sa in sa_s]
    upd = [_dot(jnp.concatenate([v, sa], axis=0),
                jnp.concatenate([blk(kend, *gp), blk(bend, *gp)], axis=0), TN)
           for v, sa, gp in zip(v_p, sa_p, groups)]
    for i, (nb, pr) in enumerate(groups):
        dec = jnp.exp(lends[nb][:, pr * LANES:(pr + 1) * LANES])
        s_scr[i] = s_p[i] * dec + jnp.where(blockdiag, upd[i], 0.0)

    o = cat0([jnp.concatenate(o_p[nb * N_PAIRS:(nb + 1) * N_PAIRS], axis=1) for nb in range(nbb)])
    mean = seg_sum(o) * (1.0 / DH_B)
    oc = o - mean
    var = seg_sum(oc * oc) * (1.0 / DH_B)
    out = oc * lax.rsqrt(var + GN_EPS_B) * lng_ref[...] + lnb_ref[...]
    res = (out + bonus) * g
    for nb in range(nbb):
        h_ref[nb] = res[nb * t:(nb + 1) * t, :]

    @pl.when(ci == nc - 1)
    def _():
        fr = lax.broadcasted_iota(jnp.int32, (LANES, DH_B), 0)
        fc = lax.broadcasted_iota(jnp.int32, (LANES, DH_B), 1)
        dup_rows = jnp.where((fr & (DH_B - 1)) == fc, 1.0, 0.0)
        for i, (nb, pr) in enumerate(groups):
            packed = _dot_exact_rhs(s_scr[i], dup_rows)
            s_out_ref[0, nb, 2 * pr:2 * pr + 2] = packed.reshape(2, DH_B, DH_B)


def _rwkv_call(pb, shift0, s_in, mu, w0a0, wlora, g2, kk, ka, rk, lng, lnb, s_stack, *, batch, seq, nbb,
               layer_in, layer):
    t = math.gcd(seq, CHUNK)
    nc = seq // t
    tok = lambda w: pl.BlockSpec((nbb, t, w), lambda b, i: (b, i, 0))
    vec = lambda r, w: pl.BlockSpec((r, w), lambda b, i: (0, 0))
    st5 = lambda l: pl.BlockSpec((1, nbb, H_B, DH_B, DH_B), lambda b, i: (l, b, 0, 0, 0))
    chained = s_stack is not None
    in_specs = [tok(D_B_IN), pl.BlockSpec((nbb, 1, D_B_IN), lambda b, i: (b, 0, 0)), st5(layer_in),
                vec(1, D_B_IN), vec(1, 2 * D_B), vec(LANES, 2 * D_B), vec(R_GATE, D_B),
                vec(1, D_B), vec(1, D_B), vec(1, D_B), vec(1, D_B), vec(1, D_B)]
    args = [pb, shift0, s_in, mu, w0a0, wlora, g2, kk, ka, rk, lng, lnb]
    if chained:
        in_specs.append(pl.BlockSpec(memory_space=pl.ANY))
        args.append(s_stack)
    return pl.pallas_call(
        functools.partial(_rwkv_kernel, t=t, nc=nc, nbb=nbb, chained=chained),
        grid=(batch // nbb, nc),
        in_specs=in_specs,
        out_specs=[tok(D_B), st5(layer)],
        out_shape=[jax.ShapeDtypeStruct((batch, seq, D_B), F32),
                   jax.ShapeDtypeStruct((DEPTH, batch, H_B, DH_B, DH_B), F32)],
        scratch_shapes=[pltpu.VMEM((nbb, SHIFT_PAD + t, D_B_IN), F32),
                        pltpu.VMEM((nbb * N_PAIRS, LANES, LANES), F32)],
        input_output_aliases={len(args) - 1: 1} if chained else {},
        compiler_params=pltpu.CompilerParams(dimension_semantics=("parallel", "arbitrary"),
                                             vmem_limit_bytes=VMEM_LIMIT),
        name="rwkv",
    )(*args)


ROUTE_ROWS = 8


def _outproj_kernel(x_ref, ha_ref, hb_ref, wo_ref, g_ref, b_ref, wr_ref, br_ref, x1_ref, route_ref,
                    cnt_ref, *, tm):
    y = _dot(ha_ref[...], wo_ref[0, 0:D_A, :]) + _dot(hb_ref[...], wo_ref[0, D_A:D_A + D_B, :])
    x1 = _layer_norm(ALPHA * x_ref[...] + y, g_ref[...], b_ref[...], LN_EPS)
    x1_ref[...] = x1
    logits = lax.dot_general(wr_ref[...], x1, NT, precision=lax.Precision.HIGHEST,
                             preferred_element_type=F32)
    mx = jnp.max(logits, axis=0, keepdims=True)
    ex = jnp.exp(logits - mx)
    probs = ex / jnp.sum(ex, axis=0, keepdims=True)
    sel = probs + br_ref[...]
    neg = -jnp.inf

    def top2(rows):
        m1 = functools.reduce(jnp.maximum, rows)
        i1 = jnp.full(m1.shape, len(rows) - 1, jnp.int32)
        for j in range(len(rows) - 2, -1, -1):
            i1 = jnp.where(rows[j] == m1, j, i1)
        rest = [jnp.where(i1 == j, neg, rows[j]) for j in range(len(rows))]
        m2 = functools.reduce(jnp.maximum, rest)
        i2 = jnp.full(m2.shape, len(rows) - 1, jnp.int32)
        for j in range(len(rows) - 2, -1, -1):
            i2 = jnp.where(rest[j] == m2, j, i2)
        return m1, i1, m2, i2

    g_score, g_i1, g_i2 = [], [], []
    for gidx in range(N_GROUPS):
        rows = [sel[gidx * EXPERTS_PER_GROUP + j:gidx * EXPERTS_PER_GROUP + j + 1, :]
                for j in range(EXPERTS_PER_GROUP)]
        m1, i1, m2, i2 = top2(rows)
        g_score.append(m1 + m2)
        g_i1.append(i1)
        g_i2.append(i2)
    best = functools.reduce(jnp.maximum, g_score)
    grp = jnp.full(best.shape, N_GROUPS - 1, jnp.int32)
    for gidx in range(N_GROUPS - 2, -1, -1):
        grp = jnp.where(g_score[gidx] == best, gidx, grp)
    l1 = g_i1[N_GROUPS - 1]
    l2 = g_i2[N_GROUPS - 1]
    for gidx in range(N_GROUPS - 2, -1, -1):
        l1 = jnp.where(grp == gidx, g_i1[gidx], l1)
        l2 = jnp.where(grp == gidx, g_i2[gidx], l2)
    e1 = grp * EXPERTS_PER_GROUP + l1
    e2 = grp * EXPERTS_PER_GROUP + l2
    e_iota = lax.broadcasted_iota(jnp.int32, (N_EXPERTS, tm), 0)
    p1 = jnp.sum(jnp.where(e_iota == e1, probs, 0.0), axis=0, keepdims=True)
    p2 = jnp.sum(jnp.where(e_iota == e2, probs, 0.0), axis=0, keepdims=True)
    tot = p1 + p2
    r8 = lax.broadcasted_iota(jnp.int32, (ROUTE_ROWS, tm), 0)
    route_ref[...] = jnp.where(r8 == 0, e1.astype(F32),
                               jnp.where(r8 == 1, e2.astype(F32),
                                         jnp.where(r8 == 2, p1 / tot, jnp.where(r8 == 3, p2 / tot, 0.0))))
    picked = jnp.where((e_iota == e1) | (e_iota == e2), 1.0, 0.0)
    cnt_ref[0] = jnp.broadcast_to(jnp.sum(picked, axis=1, keepdims=True), (N_EXPERTS, LANES))


def _outproj_call(x, ha, hb, wo, g, b, wr_t, br, tm, layer):
    n = x.shape[0]
    row = lambda w: pl.BlockSpec((tm, w), lambda i: (i, 0))
    full = lambda r, w: pl.BlockSpec((r, w), lambda i: (0, 0))
    return pl.pallas_call(
        functools.partial(_outproj_kernel, tm=tm),
        grid=(n // tm,),
        in_specs=[row(D_MODEL), row(D_A), row(D_B),
                  pl.BlockSpec((1, D_MODEL, D_MODEL), lambda i: (layer, 0, 0)), full(1, D_MODEL),
                  full(1, D_MODEL), full(N_EXPERTS, D_MODEL), full(N_EXPERTS, 1)],
        out_specs=[row(D_MODEL), pl.BlockSpec((ROUTE_ROWS, tm), lambda i: (0, i)),
                   pl.BlockSpec((1, N_EXPERTS, LANES), lambda i: (i, 0, 0))],
        out_shape=[jax.ShapeDtypeStruct((n, D_MODEL), F32), jax.ShapeDtypeStruct((ROUTE_ROWS, n), F32),
                   jax.ShapeDtypeStruct((n // tm, N_EXPERTS, LANES), F32)],
        compiler_params=pltpu.CompilerParams(dimension_semantics=("parallel",),
                                             vmem_limit_bytes=VMEM_LIMIT),
        name="outproj",
    )(x, ha, hb, wo, g, b, wr_t, br)


MOE_BM = 128
MOE_CH = 512
MOE_CCH = 1024
MOE_GROUP = 4
MOE_NBUF = 3


def _moe_max_blocks(t):
    return -(-2 * t // MOE_BM) + N_EXPERTS - 1


def _moe_sort_kernel(nblk_ref, x_ref, route_ref, padoff_ref, tri_ref, xs_ref, cols_ref, *, t, n_tiles,
                     n_chunks):
    i = pl.program_id(0)

    @pl.when(i >= n_tiles)
    def _():
        xs_ref[...] = jnp.zeros_like(xs_ref)

    @pl.when(i < n_tiles)
    def _():
        s_used = nblk_ref[jnp.minimum(i, n_tiles - 1)] * MOE_BM
        xb = x_ref[...].astype(BF16)
        route = route_ref[...]
        e_iota = lax.broadcasted_iota(jnp.int32, (N_EXPERTS, t), 0)
        a1 = e_iota == route[0:1].astype(jnp.int32)
        a2 = e_iota == route[1:2].astype(jnp.int32)
        picked = jnp.where(a1 | a2, 1.0, 0.0).astype(BF16)
        rank = jnp.dot(picked, tri_ref[...], preferred_element_type=F32)
        base = padoff_ref[0][:, 0:1] + rank
        slot1 = jnp.sum(jnp.where(a1, base, 0.0), axis=0, keepdims=True)
        slot2 = jnp.sum(jnp.where(a2, base, 0.0), axis=0, keepdims=True)
        r8 = lax.broadcasted_iota(jnp.int32, (ROUTE_ROWS, t), 0)
        rows = jnp.where(r8 == 0, slot1, jnp.where(r8 == 1, slot2,
                                                   jnp.where(r8 == 2, route[2:3],
                                                             jnp.where(r8 == 3, route[3:4], 0.0))))
        padded = jnp.concatenate([rows, jnp.zeros((LANES - ROUTE_ROWS, t), F32)], axis=0)
        cols_ref[...] = padded.T
        s1 = slot1.astype(jnp.int32)
        s2 = slot2.astype(jnp.int32)
        for c in range(n_chunks):
            used = c * MOE_CH < s_used

            @pl.when(used)
            def _():
                s_iota = lax.broadcasted_iota(jnp.int32, (MOE_CH, t), 0) + c * MOE_CH
                onehot = jnp.where((s_iota == s1) | (s_iota == s2), 1.0, 0.0).astype(BF16)
                xs_ref[0, c * MOE_CH:(c + 1) * MOE_CH, :] = jnp.dot(
                    onehot, xb, preferred_element_type=F32).astype(BF16)

            @pl.when(jnp.logical_not(used))
            def _():
                xs_ref[0, c * MOE_CH:(c + 1) * MOE_CH, :] = jnp.zeros((MOE_CH, D_MODEL), BF16)


def _moe_expert_kernel(sblk_ref, ebase_ref, ecnt_ref, xs_hbm, wg_ref, wu_ref, wd_ref, ys_hbm,
                       wg_scr, wu_scr, wd_scr, xbuf, ybuf, in_sem, out_sem):
    e = pl.program_id(0)
    base = ebase_ref[e]
    cnt = ecnt_ref[e]
    npair = (cnt + MOE_GROUP - 1) // MOE_GROUP
    wg_scr[...] = wg_ref[0, 0].astype(BF16)
    wu_scr[...] = wu_ref[0, 0].astype(BF16)
    wd_scr[...] = wd_ref[0, 0].astype(BF16)

    @pl.when(e == 0)
    def _():
        xbuf[...] = jnp.zeros_like(xbuf)

    def rows(k):
        return pl.ds(pl.multiple_of(sblk_ref[base + k] * MOE_BM, MOE_BM), MOE_BM)

    def in_copy(k, slot, half):
        return pltpu.make_async_copy(xs_hbm.at[rows(k), :], xbuf.at[slot, half * MOE_BM:(half + 1) * MOE_BM, :],
                                     in_sem.at[slot, half])

    def out_copy(k, slot, half):
        return pltpu.make_async_copy(ybuf.at[slot, half * MOE_BM:(half + 1) * MOE_BM, :], ys_hbm.at[rows(k), :],
                                     out_sem.at[slot, half])

    def for_pair(p, make, act):
        slot = p % MOE_NBUF
        act(make(MOE_GROUP * p, slot, 0))
        for part in range(1, MOE_GROUP):
            @pl.when(MOE_GROUP * p + part < cnt)
            def _():
                act(make(MOE_GROUP * p + part, slot, part))

    start = lambda c: c.start()
    wait = lambda c: c.wait()

    for i in range(MOE_NBUF - 1):
        @pl.when(i < npair)
        def _():
            for_pair(i, in_copy, start)

    def pair(p, carry):
        slot = p % MOE_NBUF
        for_pair(p, in_copy, wait)
        ahead = p + MOE_NBUF - 1

        @pl.when(ahead < npair)
        def _():
            for_pair(ahead, in_copy, start)

        @pl.when(p >= MOE_NBUF)
        def _():
            for_pair(p - MOE_NBUF, out_copy, wait)

        def run_expert(n_rows):
            xblk = xbuf[slot, 0:n_rows, :]
            hid = _silu(jnp.dot(xblk, wg_scr[...], preferred_element_type=F32)) * jnp.dot(
                xblk, wu_scr[...], preferred_element_type=F32)
            y = jnp.dot(hid.astype(BF16), wd_scr[...], preferred_element_type=F32)
            ybuf[slot, 0:n_rows, :] = y.astype(BF16)

        short = cnt - MOE_GROUP * p <= MOE_GROUP // 2

        @pl.when(short)
        def _():
            run_expert(MOE_GROUP // 2 * MOE_BM)

        @pl.when(jnp.logical_not(short))
        def _():
            run_expert(MOE_GROUP * MOE_BM)

        for_pair(p, out_copy, start)
        return carry

    lax.fori_loop(0, npair, pair, 0)

    for i in range(MOE_NBUF):
        @pl.when(npair > i)
        def _():
            for_pair(npair - 1 - i, out_copy, wait)


def _moe_combine_kernel(nblk_ref, x_ref, cols_ref, ys_ref, g_ref, b_ref, o_ref, acc_scr, *, t, n_chunks):
    s_used = nblk_ref[pl.program_id(0)] * MOE_BM
    cols = cols_ref[...]
    s1c = cols[:, 0:1].astype(jnp.int32)
    s2c = cols[:, 1:2].astype(jnp.int32)
    w1c = cols[:, 2:3]
    w2c = cols[:, 3:4]
    for c in range(n_chunks):
        @pl.when(c * MOE_CCH < s_used)
        def _():
            l_iota = lax.broadcasted_iota(jnp.int32, (t, MOE_CCH), 1) + c * MOE_CCH
            weighted = (jnp.where(l_iota == s1c, w1c, 0.0) + jnp.where(l_iota == s2c, w2c, 0.0)).astype(BF16)
            part = jnp.dot(weighted, ys_ref[0, c * MOE_CCH:(c + 1) * MOE_CCH, :],
                           preferred_element_type=F32)
            if c == 0:
                acc_scr[...] = part
            else:
                acc_scr[...] += part
    o_ref[...] = _layer_norm(ALPHA * x_ref[...] + acc_scr[...], g_ref[...], b_ref[...], LN_EPS)


def _moe_schedule(counts, tiles_per, maxb, bpt):
    cnt = counts[:, :, 0].astype(jnp.int32)
    cnt = cnt.reshape(-1, tiles_per, N_EXPERTS).sum(axis=1)
    n_tiles = cnt.shape[0]
    nblk = (cnt + MOE_BM - 1) // MOE_BM
    end = jnp.cumsum(nblk, axis=-1)
    start = end - nblk
    total = end[:, -1]
    padoff = jnp.broadcast_to((start * MOE_BM).astype(F32)[:, :, None], (n_tiles, N_EXPERTS, LANES))
    per_e = jnp.sum(nblk, axis=0)
    e_end = jnp.cumsum(per_e)
    e_base = e_end - per_e
    e_ids = jnp.arange(N_EXPERTS, dtype=jnp.int32)
    s_idx = jnp.arange(n_tiles * maxb, dtype=jnp.int32)
    last_e = jnp.max(jnp.where(per_e > 0, e_ids, 0))
    se = jnp.minimum(jnp.sum((s_idx[:, None] >= e_end[None, :]).astype(jnp.int32), axis=-1), last_e)
    oh_e = se[:, None] == e_ids[None, :]
    r = s_idx - jnp.sum(jnp.where(oh_e, e_base[None, :], 0), axis=-1)
    of_e = lambda a: jnp.sum(jnp.where(oh_e[:, None, :], a[None, :, :], 0), axis=-1)
    incl_s = of_e(jnp.cumsum(nblk, axis=0))
    tile = jnp.minimum(jnp.sum((incl_s <= r[:, None]).astype(jnp.int32), axis=-1), n_tiles - 1)
    oh_t = tile[:, None] == jnp.arange(n_tiles, dtype=jnp.int32)[None, :]
    of_t = lambda a: jnp.sum(jnp.where(oh_t, a, 0), axis=-1)
    j = of_t(of_e(start)) + r - (of_t(incl_s) - of_t(of_e(nblk)))
    sblk = jnp.where(s_idx < e_end[-1], tile * bpt + j, n_tiles * bpt)
    return total, padoff, sblk, e_base, per_e


def _moe_call(x, route, counts, wg, wu, wd, g, b, *, layer, tm):
    n = x.shape[0]
    t = 1024 if n % 1024 == 0 else n
    n_tiles = n // t
    maxb = _moe_max_blocks(t)
    s_alloc = -(-maxb * MOE_BM // MOE_CCH) * MOE_CCH
    n_chunks = s_alloc // MOE_CH
    bpt = s_alloc // MOE_BM
    nblk, padoff, sblk, e_base, e_cnt = _moe_schedule(counts, t // tm, maxb, bpt)
    tri = jnp.triu(jnp.ones((t, t), BF16), k=1)
    last = n_tiles - 1
    xs, cols = pl.pallas_call(
        functools.partial(_moe_sort_kernel, t=t, n_tiles=n_tiles, n_chunks=n_chunks),
        grid_spec=pltpu.PrefetchScalarGridSpec(
            num_scalar_prefetch=1,
            grid=(n_tiles + 1,),
            in_specs=[pl.BlockSpec((t, D_MODEL), lambda i, nb: (jnp.minimum(i, last), 0)),
                      pl.BlockSpec((ROUTE_ROWS, t), lambda i, nb: (0, jnp.minimum(i, last))),
                      pl.BlockSpec((1, N_EXPERTS, LANES), lambda i, nb: (jnp.minimum(i, last), 0, 0)),
                      pl.BlockSpec((t, t), lambda i, nb: (0, 0))],
            out_specs=[pl.BlockSpec((1, s_alloc, D_MODEL), lambda i, nb: (i, 0, 0)),
                       pl.BlockSpec((t, LANES), lambda i, nb: (jnp.minimum(i, last), 0))]),
        out_shape=[jax.ShapeDtypeStruct((n_tiles + 1, s_alloc, D_MODEL), BF16),
                   jax.ShapeDtypeStruct((n, LANES), F32)],
        compiler_params=pltpu.CompilerParams(dimension_semantics=("arbitrary",),
                                             vmem_limit_bytes=VMEM_LIMIT),
        name="moe_sort",
    )(nblk, x, route, padoff, tri)
    wspec = lambda r, c: pl.BlockSpec((1, 1, r, c), lambda e, sb, eb, ec: (layer, e, 0, 0))
    ys = pl.pallas_call(
        _moe_expert_kernel,
        grid_spec=pltpu.PrefetchScalarGridSpec(
            num_scalar_prefetch=3,
            grid=(N_EXPERTS,),
            in_specs=[pl.BlockSpec(memory_space=pl.ANY),
                      wspec(D_MODEL, D_EXP), wspec(D_MODEL, D_EXP), wspec(D_EXP, D_MODEL)],
            out_specs=pl.BlockSpec(memory_space=pl.ANY),
            scratch_shapes=[pltpu.VMEM((D_MODEL, D_EXP), BF16), pltpu.VMEM((D_MODEL, D_EXP), BF16),
                            pltpu.VMEM((D_EXP, D_MODEL), BF16),
                            pltpu.VMEM((MOE_NBUF, MOE_GROUP * MOE_BM, D_MODEL), BF16),
                            pltpu.VMEM((MOE_NBUF, MOE_GROUP * MOE_BM, D_MODEL), BF16),
                            pltpu.SemaphoreType.DMA((MOE_NBUF, MOE_GROUP)),
                            pltpu.SemaphoreType.DMA((MOE_NBUF, MOE_GROUP))]),
        out_shape=jax.ShapeDtypeStruct(((n_tiles + 1) * s_alloc, D_MODEL), BF16),
        input_output_aliases={3: 0},
        compiler_params=pltpu.CompilerParams(dimension_semantics=("arbitrary",),
                                             vmem_limit_bytes=VMEM_LIMIT),
        name="moe_experts",
    )(sblk, e_base, e_cnt, xs.reshape((n_tiles + 1) * s_alloc, D_MODEL), wg, wu, wd)
    vec = pl.BlockSpec((1, D_MODEL), lambda i, nb: (0, 0))
    return pl.pallas_call(
        functools.partial(_moe_combine_kernel, t=t, n_chunks=s_alloc // MOE_CCH),
        grid_spec=pltpu.PrefetchScalarGridSpec(
            num_scalar_prefetch=1,
            grid=(n_tiles,),
            in_specs=[pl.BlockSpec((t, D_MODEL), lambda i, nb: (i, 0)),
                      pl.BlockSpec((t, LANES), lambda i, nb: (i, 0)),
                      pl.BlockSpec((1, s_alloc, D_MODEL), lambda i, nb: (i, 0, 0)), vec, vec],
            out_specs=pl.BlockSpec((t, D_MODEL), lambda i, nb: (i, 0)),
            scratch_shapes=[pltpu.VMEM((t, D_MODEL), F32)]),
        out_shape=jax.ShapeDtypeStruct((n, D_MODEL), F32),
        compiler_params=pltpu.CompilerParams(dimension_semantics=("parallel",),
                                             vmem_limit_bytes=VMEM_LIMIT),
        name="moe_combine",
    )(nblk, x, cols, ys.reshape(n_tiles + 1, s_alloc, D_MODEL), g, b)


def _row_tile(n):
    return 512 if n % 512 == 0 else n


def _seqs_per_step(batch, want):
    return want if batch % want == 0 else 1


def kernel(x_prompt, x_sample, state_mlstm_C, state_mlstm_n, state_mlstm_m, state_mlstm_conv, state_rwkv_S, state_rwkv_shift, ln0_g, ln0_b, w_in, conv_w, conv_b, b_i, b_f, gn_a_g, mu_shift, w0, w2, a0, a2, g2, k_k, k_a, r_k, lnx_g, lnx_b, w_out, ln1_g, ln1_b, w_router, b_router, we_gate, we_up, we_down, ln2_g, ln2_b):
    d_a_in = 4 * D_A + 2 * H_A
    zpad = jnp.zeros((DEPTH, D_MODEL, LANES - H_A), F32)
    w_cat = jnp.concatenate(
        [w_in[:, :, 0:4 * D_A], w_in[:, :, d_a_in:], w_in[:, :, 4 * D_A:4 * D_A + H_A], zpad,
         w_in[:, :, 4 * D_A + H_A:d_a_in], zpad], axis=-1).astype(BF16)
    bif = jnp.zeros((DEPTH, 1, W_IF_COLS), F32)
    bif = bif.at[:, 0, 0:H_A].set(b_i).at[:, 0, LANES:LANES + H_A].set(b_f)
    wlora = jnp.zeros((DEPTH, LANES, 2 * D_B), F32)
    wlora = wlora.at[:, 0:R_DECAY, 0:D_B].set(w2).at[:, R_DECAY:, D_B:].set(a2).astype(BF16)
    w0a0 = jnp.concatenate([w0, a0], axis=-1)[:, None, :]
    g2b = g2.astype(BF16)
    wob = w_out.astype(BF16)
    wr_t = w_router.T
    br = b_router[:, None]
    r1 = lambda v: v[None, :]

    def run(x3, states):
        nb, seq, _ = x3.shape
        n = nb * seq
        tm = _row_tile(n)
        rwkv_nbb = _seqs_per_step(nb, 4 if seq >= CHUNK else 8)
        mlstm_nbb = _seqs_per_step(nb, 2 if seq >= CHUNK else 8)
        x = x3.reshape(n, D_MODEL)
        outs = []
        c_stack = jnp.zeros((DEPTH, nb, H_A, DH_A, DH_A), F32)
        s_stack = jnp.zeros((DEPTH, nb, H_B, DH_B, DH_B), F32)
        for l in range(DEPTH):
            if states is None:
                c_in = jnp.zeros((1, nb, H_A, DH_A, DH_A), F32)
                s_in = jnp.zeros((1, nb, H_B, DH_B, DH_B), F32)
                layer_in = 0
                n0 = jnp.zeros((nb, H_A, DH_A), F32)
                m0 = jnp.zeros((nb, H_A), F32)
                conv0 = jnp.zeros((nb, CONV_W - 1, 2 * D_A), F32)
                shift0 = jnp.zeros((nb, D_B_IN), F32)
            else:
                c_in, s_in, layer_in = states[0], states[4], l
                n0, m0, conv0, shift0 = (states[k][l] for k in (1, 2, 3, 5))
            m0p = jnp.zeros((nb, 1, LANES), F32).at[:, 0, 0:H_A].set(m0)
            if l == 0:
                pa, pb, pif, x = _inproj_call(x, w_cat, tm, l, norm=(r1(ln0_g), r1(ln0_b)))
            else:
                pa, pb, pif = _inproj_call(x, w_cat, tm, l)
            ha, c_stack, n_new, m_new = _mlstm_call(
                pa.reshape(nb, seq, W_A_COLS), pif.reshape(nb, seq, W_IF_COLS), conv0, c_in, n0, m0p,
                conv_w[l], r1(conv_b[l]), bif[l], r1(gn_a_g[l]), c_stack, batch=nb, seq=seq,
                nbb=mlstm_nbb, layer_in=layer_in, layer=l)
            ha = ha.reshape(n, D_A)
            hb, s_stack = _rwkv_call(
                pb.reshape(nb, seq, D_B_IN), shift0[:, None, :], s_in, r1(mu_shift[l]), w0a0[l], wlora[l],
                g2b[l], r1(k_k[l]), r1(k_a[l]), r1(r_k[l].reshape(D_B)), r1(lnx_g[l]), r1(lnx_b[l]),
                s_stack, batch=nb, seq=seq, nbb=rwkv_nbb, layer_in=layer_in, layer=l)
            hb = hb.reshape(n, D_B)
            x1, route, counts = _outproj_call(x, ha, hb, wob, r1(ln1_g[l]), r1(ln1_b[l]), wr_t, br, tm, l)
            x = _moe_call(x1, route, counts, we_gate, we_up, we_down, r1(ln2_g[l]), r1(ln2_b[l]),
                          layer=l, tm=tm)
            pa3 = pa.reshape(nb, seq, W_A_COLS)
            full = jnp.concatenate([conv0, pa3[:, :, 0:2 * D_A]], axis=1) if seq < CONV_W - 1 else pa3[:, :, 0:2 * D_A]
            conv_new = full[:, -(CONV_W - 1):, :]
            shift_new = pb.reshape(nb, seq, D_B_IN)[:, -1, :]
            outs.append((n_new, m_new[:, 0, 0:H_A], conv_new, shift_new))
        n_all, m_all, conv_all, shift_all = (jnp.stack(s) for s in zip(*outs))
        return x.reshape(nb, seq, D_MODEL), (c_stack, n_all, m_all, conv_all, s_stack, shift_all)

    y_prompt, (p_c, p_n, p_m, p_conv, p_s, p_shift) = run(x_prompt, None)
    y_sample, (s_c, s_n, s_m, s_conv, s_s, s_shift) = run(
        x_sample, (state_mlstm_C, state_mlstm_n, state_mlstm_m, state_mlstm_conv, state_rwkv_S,
                   state_rwkv_shift))
    return (y_prompt, y_sample, p_c, p_n, p_m, p_conv, p_s, p_shift, s_c, s_n, s_m, s_conv, s_s, s_shift)
```

```python
import functools
import math

import jax
import jax.numpy as jnp
from jax import lax
from jax.experimental import pallas as pl
from jax.experimental.pallas import tpu as pltpu

F32 = jnp.float32
BF16 = jnp.bfloat16

D_MODEL = 1024
DEPTH = 4
D_A = 512
D_B = 512
DH_A = 128
H_A = 4
DH_B = 64
H_B = 8
N_PAIRS = H_B // 2
CONV_W = 4
CHUNK = 64
R_DECAY = 64
R_AAA = 64
R_GATE = 128
D_B_IN = 3 * D_B + R_DECAY + R_AAA + R_GATE
N_EXPERTS = 16
N_GROUPS = 4
EXPERTS_PER_GROUP = 4
D_EXP = 512
ALPHA = (2 * DEPTH) ** 0.25
LN_EPS = 1e-5
GN_EPS_A = 1e-6
GN_EPS_B = 64e-5

LANES = 128
SOLVE_BLOCK = 16
VMEM_LIMIT = 48 * 1024 * 1024

NN = (((1,), (0,)), ((), ()))
NT = (((1,), (1,)), ((), ()))
TN = (((0,), (0,)), ((), ()))


def _dot(a, b, dims=NN):
    return lax.dot_general(a.astype(BF16), b.astype(BF16), dims, preferred_element_type=F32)


def _split3(x):
    hi = x.astype(BF16)
    r1 = x - hi.astype(F32)
    mid = r1.astype(BF16)
    lo = (r1 - mid.astype(F32)).astype(BF16)
    return hi, mid, lo


def _dot_exact_lhs(a, x, dims=NN):
    a = a.astype(BF16)
    hi, mid, lo = _split3(x)
    d = lambda p: lax.dot_general(a, p, dims, preferred_element_type=F32)
    return d(hi) + d(mid) + d(lo)


def _dot_exact_rhs(x, b, dims=NN):
    b = b.astype(BF16)
    hi, mid, lo = _split3(x)
    d = lambda p: lax.dot_general(p, b, dims, preferred_element_type=F32)
    return d(hi) + d(mid) + d(lo)


def _layer_norm(x, g, b, eps):
    mu = jnp.mean(x, axis=-1, keepdims=True)
    xc = x - mu
    var = jnp.mean(xc * xc, axis=-1, keepdims=True)
    return xc * lax.rsqrt(var + eps) * g + b


def _sigmoid(x):
    return 1.0 / (1.0 + jnp.exp(-x))


def _log_sigmoid(x):
    return jnp.minimum(x, 0.0) - jnp.log(1.0 + jnp.exp(-jnp.abs(x)))


def _silu(x):
    return x * _sigmoid(x)


W_A_COLS = 4 * D_A
W_IF_COLS = 2 * LANES
W_IN_COLS = W_A_COLS + D_B_IN + W_IF_COLS


def _inproj_kernel(x_ref, w_ref, *rest, input_norm):
    if input_norm:
        g_ref, b_ref, pa_ref, pb_ref, pif_ref, xn_ref = rest
        x = _layer_norm(x_ref[...], g_ref[...], b_ref[...], LN_EPS)
        xn_ref[...] = x
    else:
        pa_ref, pb_ref, pif_ref = rest
        x = x_ref[...]
    xb = x.astype(BF16)
    pa_ref[...] = jnp.dot(xb, w_ref[0, :, 0:W_A_COLS], preferred_element_type=F32)
    pb_ref[...] = jnp.dot(xb, w_ref[0, :, W_A_COLS:W_A_COLS + D_B_IN], preferred_element_type=F32)
    pif_ref[...] = jnp.dot(xb, w_ref[0, :, W_A_COLS + D_B_IN:W_IN_COLS], preferred_element_type=F32)


def _inproj_call(x, w, tm, layer, norm=None):
    n = x.shape[0]
    row = lambda c: pl.BlockSpec((tm, c), lambda i: (i, 0))
    vec = pl.BlockSpec((1, D_MODEL), lambda i: (0, 0))
    in_specs = [row(D_MODEL), pl.BlockSpec((1, D_MODEL, W_IN_COLS), lambda i: (layer, 0, 0))]
    out_specs = [row(W_A_COLS), row(D_B_IN), row(W_IF_COLS)]
    out_shape = [jax.ShapeDtypeStruct((n, W_A_COLS), F32), jax.ShapeDtypeStruct((n, D_B_IN), F32),
                 jax.ShapeDtypeStruct((n, W_IF_COLS), F32)]
    args = [x, w]
    if norm is not None:
        in_specs += [vec, vec]
        out_specs.append(row(D_MODEL))
        out_shape.append(jax.ShapeDtypeStruct((n, D_MODEL), F32))
        args += list(norm)
    return pl.pallas_call(
        functools.partial(_inproj_kernel, input_norm=norm is not None),
        grid=(n // tm,),
        in_specs=in_specs,
        out_specs=out_specs,
        out_shape=out_shape,
        compiler_params=pltpu.CompilerParams(dimension_semantics=("parallel",),
                                             vmem_limit_bytes=VMEM_LIMIT),
        name="inproj",
    )(*args)


CONV_PAD = 8


def _mlstm_kernel(pa_ref, pif_ref, conv0_ref, c0_ref, n0_ref, m0_ref, convw_ref, convb_ref,
                  bif_ref, gn_ref, *rest, c, nc, nbb, chained):
    h_ref, c_out_ref, n_out_ref, m_out_ref, qk_scr, c_scr, n_scr, m_scr = rest[1:] if chained else rest
    ci = pl.program_id(1)
    rows = nbb * c
    cshift = int(math.log2(c))
    chains = [(nb, h) for nb in range(nbb) for h in range(H_A)]
    prev0 = CONV_PAD - (CONV_W - 1)

    @pl.when(ci == 0)
    def _():
        c_scr[...] = c0_ref[0]
        n_scr[...] = n0_ref[...]
        m_scr[...] = m0_ref[...]
        for nb in range(nbb):
            qk_scr[nb, prev0:CONV_PAD, :] = conv0_ref[nb]

    cat0 = lambda xs: xs[0] if len(xs) == 1 else jnp.concatenate(xs, axis=0)
    acc_l = []
    for nb in range(nbb):
        u = pa_ref[nb, :, 0:2 * D_A]
        qk_scr[nb, CONV_PAD:CONV_PAD + c, :] = u
        acc = convb_ref[...]
        for j in range(CONV_W):
            acc = acc + qk_scr[nb, prev0 + j:prev0 + j + c, :] * convw_ref[j:j + 1, :]
        qk_scr[nb, prev0:CONV_PAD, :] = u[c - (CONV_W - 1):c, :]
        acc_l.append(acc)
    qk = _silu(cat0(acc_l))

    gates = cat0([pif_ref[nb] for nb in range(nbb)]) + bif_ref[...]
    li_all = gates[:, 0:LANES]
    lf_all = _log_sigmoid(gates[:, LANES:2 * LANES])
    row = lax.broadcasted_iota(jnp.int32, (rows, rows), 0)
    col = lax.broadcasted_iota(jnp.int32, (rows, rows), 1)
    tril = jnp.where((col <= row) & ((row >> cshift) == (col >> cshift)), 1.0, 0.0)
    b_all = _dot_exact_lhs(tril, lf_all)
    z_all = li_all - b_all
    crow = lax.broadcasted_iota(jnp.int32, (c, c), 0)
    ccol = lax.broadcasted_iota(jnp.int32, (c, c), 1)
    causal = ccol <= crow
    hrow = lax.broadcasted_iota(jnp.int32, (H_A * c, LANES), 0)
    hlane = lax.broadcasted_iota(jnp.int32, (H_A * c, LANES), 1)
    head_pick = jnp.where((hrow >> cshift) == hlane, 1.0, 0.0)
    z_rows = [_dot_exact_lhs(head_pick, z_all[nb * c:(nb + 1) * c], NT) for nb in range(nbb)]
    lane1 = lax.broadcasted_iota(jnp.int32, (1, LANES), 1)

    rs = lambda nb: slice(nb * c, (nb + 1) * c)
    q_l = [qk[rs(nb), h * DH_A:(h + 1) * DH_A] for nb, h in chains]
    k_l = [qk[rs(nb), D_A + h * DH_A:D_A + (h + 1) * DH_A] * (DH_A ** -0.5) for nb, h in chains]
    v_l = [pa_ref[nb, :, 2 * D_A + h * DH_A:2 * D_A + (h + 1) * DH_A] for nb, h in chains]
    c_l = [c_scr[nb, h] for nb, h in chains]
    n_l = [n_scr[nb, h:h + 1, :] for nb, h in chains]
    b_col = [b_all[rs(nb), h:h + 1] for nb, h in chains]
    li_col = [li_all[rs(nb), h:h + 1] for nb, h in chains]
    m_prev = [m_scr[nb][:, h:h + 1] for nb, h in chains]
    dmat = [jnp.where(causal, bc + z_rows[nb][h * c:(h + 1) * c], -jnp.inf)
            for bc, (nb, h) in zip(b_col, chains)]
    m_inter = [bc + mp for bc, mp in zip(b_col, m_prev)]
    m_t = [jnp.maximum(mi, jnp.max(d, axis=-1, keepdims=True)) for mi, d in zip(m_inter, dmat)]
    qk_dot = [_dot(q, k, NT) for q, k in zip(q_l, k_l)]
    qc = [_dot(q, cm) for q, cm in zip(q_l, c_l)]
    s_l = [x * jnp.exp(d - mt) for x, d, mt in zip(qk_dot, dmat, m_t)]
    sv = [_dot(s, v) for s, v in zip(s_l, v_l)]
    b_last = [bc[c - 1:c, :] for bc in b_col]
    g_s = [bl - bc + li for bl, bc, li in zip(b_last, b_col, li_col)]
    m_new = [jnp.maximum(bl + mp, jnp.max(gs, axis=0, keepdims=True))
             for bl, mp, gs in zip(b_last, m_prev, g_s)]
    wk = [jnp.exp(gs - mn) * k for gs, mn, k in zip(g_s, m_new, k_l)]
    w_old = [jnp.exp(bl + mp - mn) for bl, mp, mn in zip(b_last, m_prev, m_new)]
    kv = [_dot(w, v, TN) for w, v in zip(wk, v_l)]
    for i, (nb, h) in enumerate(chains):
        c_scr[nb, h] = w_old[i] * c_l[i] + kv[i]
        n_scr[nb, h:h + 1, :] = w_old[i] * n_l[i] + jnp.sum(wk[i], axis=0, keepdims=True)
    for nb in range(nbb):
        m_row = m_scr[nb]
        for h in range(H_A):
            m_row = jnp.where(lane1 == h, m_new[nb * H_A + h], m_row)
        m_scr[nb] = m_row
    w_inter = [jnp.exp(mi - mt) for mi, mt in zip(m_inter, m_t)]
    qn = [jnp.sum(q * nv, axis=-1, keepdims=True) for q, nv in zip(q_l, n_l)]
    s_sum = [jnp.sum(s, axis=-1, keepdims=True) for s in s_l]
    den = [w * a + b for w, a, b in zip(w_inter, qn, s_sum)]
    hh = [(w * a + b) / jnp.maximum(jnp.abs(d), jnp.exp(-mt))
          for w, a, b, d, mt in zip(w_inter, qc, sv, den, m_t)]
    mu = [jnp.mean(x, axis=-1, keepdims=True) for x in hh]
    hc = [x - m for x, m in zip(hh, mu)]
    var = [jnp.mean(x * x, axis=-1, keepdims=True) for x in hc]
    for i, (nb, h) in enumerate(chains):
        sl = slice(h * DH_A, (h + 1) * DH_A)
        hn = hc[i] * lax.rsqrt(var[i] + GN_EPS_A)
        o_pre = pa_ref[nb, :, 3 * D_A + h * DH_A:3 * D_A + (h + 1) * DH_A]
        h_ref[nb, :, sl] = hn * gn_ref[:, sl] * _sigmoid(o_pre)

    @pl.when(ci == nc - 1)
    def _():
        c_out_ref[0] = c_scr[...]
        n_out_ref[...] = n_scr[...]
        m_out_ref[...] = m_scr[...]


def _mlstm_call(pa, pif, conv0, c_in, n0, m0, convw, convb, bif, gn, c_stack, *, batch, seq, nbb,
                layer_in, layer):
    c = math.gcd(seq, CHUNK)
    nc = seq // c
    tok = lambda w: pl.BlockSpec((nbb, c, w), lambda b, i: (b, i, 0))
    vec = lambda r, w: pl.BlockSpec((r, w), lambda b, i: (0, 0))
    st5 = lambda l: pl.BlockSpec((1, nbb, H_A, DH_A, DH_A), lambda b, i: (l, b, 0, 0, 0))
    st3 = lambda r, w: pl.BlockSpec((nbb, r, w), lambda b, i: (b, 0, 0))
    chained = c_stack is not None
    in_specs = [tok(W_A_COLS), tok(W_IF_COLS), st3(CONV_W - 1, 2 * D_A), st5(layer_in), st3(H_A, DH_A),
                st3(1, LANES), vec(CONV_W, 2 * D_A), vec(1, 2 * D_A), vec(1, W_IF_COLS), vec(1, D_A)]
    args = [pa, pif, conv0, c_in, n0, m0, convw, convb, bif, gn]
    if chained:
        in_specs.append(pl.BlockSpec(memory_space=pl.ANY))
        args.append(c_stack)
    return pl.pallas_call(
        functools.partial(_mlstm_kernel, c=c, nc=nc, nbb=nbb, chained=chained),
        grid=(batch // nbb, nc),
        in_specs=in_specs,
        out_specs=[tok(D_A), st5(layer), st3(H_A, DH_A), st3(1, LANES)],
        out_shape=[jax.ShapeDtypeStruct((batch, seq, D_A), F32),
                   jax.ShapeDtypeStruct((DEPTH, batch, H_A, DH_A, DH_A), F32),
                   jax.ShapeDtypeStruct((batch, H_A, DH_A), F32),
                   jax.ShapeDtypeStruct((batch, 1, LANES), F32)],
        scratch_shapes=[pltpu.VMEM((nbb, CONV_PAD + c, 2 * D_A), F32),
                        pltpu.VMEM((nbb, H_A, DH_A, DH_A), F32),
                        pltpu.VMEM((nbb, H_A, DH_A), F32),
                        pltpu.VMEM((nbb, 1, LANES), F32)],
        input_output_aliases={len(args) - 1: 1} if chained else {},
        compiler_params=pltpu.CompilerParams(dimension_semantics=("parallel", "arbitrary"),
                                             vmem_limit_bytes=VMEM_LIMIT),
        name="mlstm",
    )(*args)


SHIFT_PAD = 8


def _neumann_inverses(ms, size, t):
    row = lax.broadcasted_iota(jnp.int32, (size, size), 0)
    col = lax.broadcasted_iota(jnp.int32, (size, size), 1)
    eye = jnp.where(row == col, 1.0, 0.0)
    blk = min(SOLVE_BLOCK, t)
    shift = int(math.log2(blk))
    same = (row >> shift) == (col >> shift)
    ds = [jnp.where(same, m, 0.0) for m in ms]
    ps = [eye + d for d in ds]
    xs = ds
    for _ in range(shift - 1):
        xs = [_dot(x, x) for x in xs]
        ps = [p + _dot(p, x) for p, x in zip(ps, xs)]
    nblk = t // blk
    if nblk == 1:
        return ps, None
    ns = [_dot(p, m - d) for p, m, d in zip(ps, ms, ds)]
    qs = [eye + n for n in ns]
    ys = ns
    for _ in range(int(math.log2(nblk)) - 1):
        ys = [_dot(y, y) for y in ys]
        qs = [q + _dot(q, y) for q, y in zip(qs, ys)]
    return ps, qs


def _rwkv_kernel(pb_ref, shift0_ref, s0_ref, mu_ref, w0a0_ref, wlora_ref, g2_ref, kk_ref, ka_ref,
                 rk_ref, lng_ref, lnb_ref, *rest, t, nc, nbb, chained):
    h_ref, s_out_ref, sh_scr, s_scr = rest[1:] if chained else rest
    ci = pl.program_id(1)
    rows = nbb * t
    t2 = 2 * t
    groups = [(nb, pr) for nb in range(nbb) for pr in range(N_PAIRS)]

    brow = lax.broadcasted_iota(jnp.int32, (LANES, LANES), 0)
    bcol = lax.broadcasted_iota(jnp.int32, (LANES, LANES), 1)
    blockdiag = (brow >= DH_B) == (bcol >= DH_B)
    bd2 = jnp.where(blockdiag, 1.0, 0.0).astype(BF16)

    @pl.when(ci == 0)
    def _():
        er = lax.broadcasted_iota(jnp.int32, (DH_B, LANES), 0)
        ec = lax.broadcasted_iota(jnp.int32, (DH_B, LANES), 1)
        dup_cols = jnp.where((ec & (DH_B - 1)) == er, 1.0, 0.0)
        for nb in range(nbb):
            sh_scr[nb, SHIFT_PAD - 1:SHIFT_PAD, :] = shift0_ref[nb]
            for pr in range(N_PAIRS):
                x = s0_ref[0, nb, 2 * pr:2 * pr + 2].reshape(LANES, DH_B)
                s_scr[nb * N_PAIRS + pr] = jnp.where(blockdiag, _dot_exact_rhs(x, dup_cols), 0.0)

    p_l, prev_l = [], []
    for nb in range(nbb):
        p_nb = pb_ref[nb]
        sh_scr[nb, SHIFT_PAD:SHIFT_PAD + t, :] = p_nb
        prev_l.append(sh_scr[nb, SHIFT_PAD - 1:SHIFT_PAD - 1 + t, :])
        sh_scr[nb, SHIFT_PAD - 1:SHIFT_PAD, :] = p_nb[t - 1:t, :]
        p_l.append(p_nb)
    cat0 = lambda xs: xs[0] if len(xs) == 1 else jnp.concatenate(xs, axis=0)
    p = cat0(p_l)
    prev = cat0(prev_l)
    pb = p + (prev - p) * mu_ref[...]

    r = pb[:, 0:D_B]
    kr = pb[:, D_B:2 * D_B]
    vr = pb[:, 2 * D_B:3 * D_B]
    lora_in = pb[:, 3 * D_B:3 * D_B + LANES]
    gl = pb[:, 3 * D_B + LANES:3 * D_B + 2 * LANES]
    lane = lax.broadcasted_iota(jnp.int32, (rows, LANES), 1)
    lora_act = jnp.where(lane < R_DECAY, jnp.tanh(lora_in), lora_in)
    z = _dot(lora_act, wlora_ref[...]) + w0a0_ref[...]
    ld = -jnp.exp(_log_sigmoid(z[:, 0:D_B]) - 0.5)
    a = _sigmoid(z[:, D_B:2 * D_B])
    g = _dot(_sigmoid(gl), g2_ref[...])

    def seg_sum(x):
        xr = jnp.concatenate([x[:, q * LANES:(q + 1) * LANES] for q in range(N_PAIRS)], axis=0)
        hi = xr.astype(BF16)
        lo = (xr - hi.astype(F32)).astype(BF16)
        s = (jnp.dot(hi, bd2, preferred_element_type=F32) + jnp.dot(lo, bd2, preferred_element_type=F32))
        return jnp.concatenate([s[q * rows:(q + 1) * rows] for q in range(N_PAIRS)], axis=1)

    kk_raw = kr * kk_ref[...]
    kk = kk_raw / jnp.maximum(jnp.sqrt(seg_sum(kk_raw * kk_raw)), 1e-12)
    k2 = kr * (1.0 + (a - 1.0) * ka_ref[...])
    bonus = seg_sum(r * k2 * rk_ref[...]) * vr

    row = lax.broadcasted_iota(jnp.int32, (rows, rows), 0)
    col = lax.broadcasted_iota(jnp.int32, (rows, rows), 1)
    tshift = int(math.log2(t))
    tril = jnp.where((col <= row) & ((row >> tshift) == (col >> tshift)), 1.0, 0.0)
    lc = _dot_exact_lhs(tril, ld)
    lends = [lc[(nb + 1) * t - 1:(nb + 1) * t, :] for nb in range(nbb)]
    lend_rows = cat0([jnp.broadcast_to(le, (t, D_B)) for le in lends])
    e_nc = jnp.exp(-lc)
    e_end = jnp.exp(lend_rows - lc)
    b_raw = kk * a
    at = -kk * jnp.exp(lc - ld)
    rt = r * jnp.exp(lc)
    kt = k2 * e_nc
    bt = b_raw * e_nc
    kend = k2 * e_end
    bend = b_raw * e_end

    srow = lax.broadcasted_iota(jnp.int32, (t2, t2), 0)
    scol = lax.broadcasted_iota(jnp.int32, (t2, t2), 1)
    strict = scol < srow
    incl = scol <= srow
    lane2 = lax.broadcasted_iota(jnp.int32, (t2, LANES), 1)
    row2 = lax.broadcasted_iota(jnp.int32, (t2, LANES), 0)
    head_sel = (lane2 >= DH_B) == (row2 >= t)
    first_head = lax.broadcasted_iota(jnp.int32, (t, LANES), 1) < DH_B

    def blk(x, nb, pr):
        return x[nb * t:(nb + 1) * t, pr * LANES:(pr + 1) * LANES]

    def stack(x):
        return jnp.where(head_sel, jnp.concatenate([x, x], axis=0), 0.0)

    lhs = [jnp.concatenate([stack(blk(at, *gp)), stack(blk(rt, *gp))], axis=0) for gp in groups]
    rk_s = [stack(blk(kt, *gp)) for gp in groups]
    rb_s = [stack(blk(bt, *gp)) for gp in groups]
    v_p = [blk(vr, *gp) for gp in groups]
    v_s = [jnp.concatenate([v, v], axis=0) for v in v_p]
    s_p = [s_scr[i] for i in range(len(groups))]
    gk = [_dot(l, x, NT) for l, x in zip(lhs, rk_s)]
    gb = [_dot(l, x, NT) for l, x in zip(lhs, rb_s)]
    xs = [_dot(l, s, NT) for l, s in zip(lhs, s_p)]
    m_ab = [jnp.where(strict, x[0:t2], 0.0) for x in gb]
    ps, qs = _neumann_inverses(m_ab, t2, t)
    rhs = [x[0:t2] + _dot(jnp.where(strict, y[0:t2], 0.0), v) for x, y, v in zip(xs, gk, v_s)]
    sa_s = [_dot(pm, x) for pm, x in zip(ps, rhs)]
    if qs is not None:
        sa_s = [_dot(qm, x) for qm, x in zip(qs, sa_s)]
    o_s = [x[t2:] + _dot(jnp.where(incl, y[t2:], 0.0), v) + _dot(jnp.where(incl, w[t2:], 0.0), sa)
           for x, y, w, v, sa in zip(xs, gk, gb, v_s, sa_s)]
    o_p = [jnp.where(first_head, o[0:t], o[t:t2]) for o in o_s]
    sa_p = [jnp.where(first_head, sa[0:t], sa[t:t2]) for sa in sa_s]
    upd = [_dot(jnp.concatenate([v, sa], axis=0),
                jnp.concatenate([blk(kend, *gp), blk(bend, *gp)], axis=0), TN)
           for v, sa, gp in zip(v_p, sa_p, groups)]
    for i, (nb, pr) in enumerate(groups):
        dec = jnp.exp(lends[nb][:, pr * LANES:(pr + 1) * LANES])
        s_scr[i] = s_p[i] * dec + jnp.where(blockdiag, upd[i], 0.0)

    o = cat0([jnp.concatenate(o_p[nb * N_PAIRS:(nb + 1) * N_PAIRS], axis=1) for nb in range(nbb)])
    mean = seg_sum(o) * (1.0 / DH_B)
    oc = o - mean
    var = seg_sum(oc * oc) * (1.0 / DH_B)
    out = oc * lax.rsqrt(var + GN_EPS_B) * lng_ref[...] + lnb_ref[...]
    res = (out + bonus) * g
    for nb in range(nbb):
        h_ref[nb] = res[nb * t:(nb + 1) * t, :]

    @pl.when(ci == nc - 1)
    def _():
        fr = lax.broadcasted_iota(jnp.int32, (LANES, DH_B), 0)
        fc = lax.broadcasted_iota(jnp.int32, (LANES, DH_B), 1)
        dup_rows = jnp.where((fr & (DH_B - 1)) == fc, 1.0, 0.0)
        for i, (nb, pr) in enumerate(groups):
            packed = _dot_exact_rhs(s_scr[i], dup_rows)
            s_out_ref[0, nb, 2 * pr:2 * pr + 2] = packed.reshape(2, DH_B, DH_B)


def _rwkv_call(pb, shift0, s_in, mu, w0a0, wlora, g2, kk, ka, rk, lng, lnb, s_stack, *, batch, seq, nbb,
               layer_in, layer):
    t = math.gcd(seq, CHUNK)
    nc = seq // t
    tok = lambda w: pl.BlockSpec((nbb, t, w), lambda b, i: (b, i, 0))
    vec = lambda r, w: pl.BlockSpec((r, w), lambda b, i: (0, 0))
    st5 = lambda l: pl.BlockSpec((1, nbb, H_B, DH_B, DH_B), lambda b, i: (l, b, 0, 0, 0))
    chained = s_stack is not None
    in_specs = [tok(D_B_IN), pl.BlockSpec((nbb, 1, D_B_IN), lambda b, i: (b, 0, 0)), st5(layer_in),
                vec(1, D_B_IN), vec(1, 2 * D_B), vec(LANES, 2 * D_B), vec(R_GATE, D_B),
                vec(1, D_B), vec(1, D_B), vec(1, D_B), vec(1, D_B), vec(1, D_B)]
    args = [pb, shift0, s_in, mu, w0a0, wlora, g2, kk, ka, rk, lng, lnb]
    if chained:
        in_specs.append(pl.BlockSpec(memory_space=pl.ANY))
        args.append(s_stack)
    return pl.pallas_call(
        functools.partial(_rwkv_kernel, t=t, nc=nc, nbb=nbb, chained=chained),
        grid=(batch // nbb, nc),
        in_specs=in_specs,
        out_specs=[tok(D_B), st5(layer)],
        out_shape=[jax.ShapeDtypeStruct((batch, seq, D_B), F32),
                   jax.ShapeDtypeStruct((DEPTH, batch, H_B, DH_B, DH_B), F32)],
        scratch_shapes=[pltpu.VMEM((nbb, SHIFT_PAD + t, D_B_IN), F32),
                        pltpu.VMEM((nbb * N_PAIRS, LANES, LANES), F32)],
        input_output_aliases={len(args) - 1: 1} if chained else {},
        compiler_params=pltpu.CompilerParams(dimension_semantics=("parallel", "arbitrary"),
                                             vmem_limit_bytes=VMEM_LIMIT),
        name="rwkv",
    )(*args)


ROUTE_ROWS = 8


def _outproj_kernel(x_ref, ha_ref, hb_ref, wo_ref, g_ref, b_ref, wr_ref, br_ref, x1_ref, route_ref,
                    cnt_ref, *, tm):
    y = _dot(ha_ref[...], wo_ref[0, 0:D_A, :]) + _dot(hb_ref[...], wo_ref[0, D_A:D_A + D_B, :])
    x1 = _layer_norm(ALPHA * x_ref[...] + y, g_ref[...], b_ref[...], LN_EPS)
    x1_ref[...] = x1
    logits = lax.dot_general(wr_ref[...], x1, NT, precision=lax.Precision.HIGHEST,
                             preferred_element_type=F32)
    mx = jnp.max(logits, axis=0, keepdims=True)
    ex = jnp.exp(logits - mx)
    probs = ex / jnp.sum(ex, axis=0, keepdims=True)
    sel = probs + br_ref[...]
    neg = -jnp.inf

    def top2(rows):
        m1 = functools.reduce(jnp.maximum, rows)
        i1 = jnp.full(m1.shape, len(rows) - 1, jnp.int32)
        for j in range(len(rows) - 2, -1, -1):
            i1 = jnp.where(rows[j] == m1, j, i1)
        rest = [jnp.where(i1 == j, neg, rows[j]) for j in range(len(rows))]
        m2 = functools.reduce(jnp.maximum, rest)
        i2 = jnp.full(m2.shape, len(rows) - 1, jnp.int32)
        for j in range(len(rows) - 2, -1, -1):
            i2 = jnp.where(rest[j] == m2, j, i2)
        return m1, i1, m2, i2

    g_score, g_i1, g_i2 = [], [], []
    for gidx in range(N_GROUPS):
        rows = [sel[gidx * EXPERTS_PER_GROUP + j:gidx * EXPERTS_PER_GROUP + j + 1, :]
                for j in range(EXPERTS_PER_GROUP)]
        m1, i1, m2, i2 = top2(rows)
        g_score.append(m1 + m2)
        g_i1.append(i1)
        g_i2.append(i2)
    best = functools.reduce(jnp.maximum, g_score)
    grp = jnp.full(best.shape, N_GROUPS - 1, jnp.int32)
    for gidx in range(N_GROUPS - 2, -1, -1):
        grp = jnp.where(g_score[gidx] == best, gidx, grp)
    l1 = g_i1[N_GROUPS - 1]
    l2 = g_i2[N_GROUPS - 1]
    for gidx in range(N_GROUPS - 2, -1, -1):
        l1 = jnp.where(grp == gidx, g_i1[gidx], l1)
        l2 = jnp.where(grp == gidx, g_i2[gidx], l2)
    e1 = grp * EXPERTS_PER_GROUP + l1
    e2 = grp * EXPERTS_PER_GROUP + l2
    e_iota = lax.broadcasted_iota(jnp.int32, (N_EXPERTS, tm), 0)
    p1 = jnp.sum(jnp.where(e_iota == e1, probs, 0.0), axis=0, keepdims=True)
    p2 = jnp.sum(jnp.where(e_iota == e2, probs, 0.0), axis=0, keepdims=True)
    tot = p1 + p2
    r8 = lax.broadcasted_iota(jnp.int32, (ROUTE_ROWS, tm), 0)
    route_ref[...] = jnp.where(r8 == 0, e1.astype(F32),
                               jnp.where(r8 == 1, e2.astype(F32),
                                         jnp.where(r8 == 2, p1 / tot, jnp.where(r8 == 3, p2 / tot, 0.0))))
    picked = jnp.where((e_iota == e1) | (e_iota == e2), 1.0, 0.0)
    cnt_ref[0] = jnp.broadcast_to(jnp.sum(picked, axis=1, keepdims=True), (N_EXPERTS, LANES))


def _outproj_call(x, ha, hb, wo, g, b, wr_t, br, tm, layer):
    n = x.shape[0]
    row = lambda w: pl.BlockSpec((tm, w), lambda i: (i, 0))
    full = lambda r, w: pl.BlockSpec((r, w), lambda i: (0, 0))
    return pl.pallas_call(
        functools.partial(_outproj_kernel, tm=tm),
        grid=(n // tm,),
        in_specs=[row(D_MODEL), row(D_A), row(D_B),
                  pl.BlockSpec((1, D_MODEL, D_MODEL), lambda i: (layer, 0, 0)), full(1, D_MODEL),
                  full(1, D_MODEL), full(N_EXPERTS, D_MODEL), full(N_EXPERTS, 1)],
        out_specs=[row(D_MODEL), pl.BlockSpec((ROUTE_ROWS, tm), lambda i: (0, i)),
                   pl.BlockSpec((1, N_EXPERTS, LANES), lambda i: (i, 0, 0))],
        out_shape=[jax.ShapeDtypeStruct((n, D_MODEL), F32), jax.ShapeDtypeStruct((ROUTE_ROWS, n), F32),
                   jax.ShapeDtypeStruct((n // tm, N_EXPERTS, LANES), F32)],
        compiler_params=pltpu.CompilerParams(dimension_semantics=("parallel",),
                                             vmem_limit_bytes=VMEM_LIMIT),
        name="outproj",
    )(x, ha, hb, wo, g, b, wr_t, br)


MOE_BM = 128
MOE_CH = 512
MOE_CCH = 1024
MOE_GROUP = 4
MOE_NBUF = 3


def _moe_max_blocks(t):
    return -(-2 * t // MOE_BM) + N_EXPERTS - 1


def _moe_sort_kernel(nblk_ref, x_ref, route_ref, padoff_ref, tri_ref, xs_ref, cols_ref, *, t, n_tiles,
                     n_chunks):
    i = pl.program_id(0)

    @pl.when(i >= n_tiles)
    def _():
        xs_ref[...] = jnp.zeros_like(xs_ref)

    @pl.when(i < n_tiles)
    def _():
        s_used = nblk_ref[jnp.minimum(i, n_tiles - 1)] * MOE_BM
        xb = x_ref[...].astype(BF16)
        route = route_ref[...]
        e_iota = lax.broadcasted_iota(jnp.int32, (N_EXPERTS, t), 0)
        a1 = e_iota == route[0:1].astype(jnp.int32)
        a2 = e_iota == route[1:2].astype(jnp.int32)
        picked = jnp.where(a1 | a2, 1.0, 0.0).astype(BF16)
        rank = jnp.dot(picked, tri_ref[...], preferred_element_type=F32)
        base = padoff_ref[0][:, 0:1] + rank
        slot1 = jnp.sum(jnp.where(a1, base, 0.0), axis=0, keepdims=True)
        slot2 = jnp.sum(jnp.where(a2, base, 0.0), axis=0, keepdims=True)
        r8 = lax.broadcasted_iota(jnp.int32, (ROUTE_ROWS, t), 0)
        rows = jnp.where(r8 == 0, slot1, jnp.where(r8 == 1, slot2,
                                                   jnp.where(r8 == 2, route[2:3],
                                                             jnp.where(r8 == 3, route[3:4], 0.0))))
        padded = jnp.concatenate([rows, jnp.zeros((LANES - ROUTE_ROWS, t), F32)], axis=0)
        cols_ref[...] = padded.T
        s1 = slot1.astype(jnp.int32)
        s2 = slot2.astype(jnp.int32)
        for c in range(n_chunks):
            used = c * MOE_CH < s_used

            @pl.when(used)
            def _():
                s_iota = lax.broadcasted_iota(jnp.int32, (MOE_CH, t), 0) + c * MOE_CH
                onehot = jnp.where((s_iota == s1) | (s_iota == s2), 1.0, 0.0).astype(BF16)
                xs_ref[0, c * MOE_CH:(c + 1) * MOE_CH, :] = jnp.dot(
                    onehot, xb, preferred_element_type=F32).astype(BF16)

            @pl.when(jnp.logical_not(used))
            def _():
                xs_ref[0, c * MOE_CH:(c + 1) * MOE_CH, :] = jnp.zeros((MOE_CH, D_MODEL), BF16)


def _moe_expert_kernel(*refs, n_streams):
    sched = refs[:3 * n_streams]
    xs_list = refs[3 * n_streams:4 * n_streams]
    wg_ref, wu_ref, wd_ref = refs[4 * n_streams:4 * n_streams + 3]
    ys_list = refs[4 * n_streams + 3:5 * n_streams + 3]
    wg_scr, wu_scr, wd_scr, xbuf, ybuf, in_sem, out_sem = refs[5 * n_streams + 3:]
    e = pl.program_id(0)
    wg_scr[...] = wg_ref[0, 0].astype(BF16)
    wu_scr[...] = wu_ref[0, 0].astype(BF16)
    wd_scr[...] = wd_ref[0, 0].astype(BF16)

    @pl.when(e == 0)
    def _():
        xbuf[...] = jnp.zeros_like(xbuf)

    for s in range(n_streams):
        sblk_ref, ebase_ref, ecnt_ref = sched[3 * s:3 * s + 3]
        _moe_expert_blocks(sblk_ref, ebase_ref[e], ecnt_ref[e], xs_list[s], ys_list[s],
                           wg_scr, wu_scr, wd_scr, xbuf, ybuf, in_sem, out_sem)


def _moe_expert_blocks(sblk_ref, base, cnt, xs_hbm, ys_hbm, wg_scr, wu_scr, wd_scr, xbuf, ybuf, in_sem,
                       out_sem):
    npair = (cnt + MOE_GROUP - 1) // MOE_GROUP

    def rows(k):
        return pl.ds(pl.multiple_of(sblk_ref[base + k] * MOE_BM, MOE_BM), MOE_BM)

    def in_copy(k, slot, half):
        return pltpu.make_async_copy(xs_hbm.at[rows(k), :], xbuf.at[slot, half * MOE_BM:(half + 1) * MOE_BM, :],
                                     in_sem.at[slot, half])

    def out_copy(k, slot, half):
        return pltpu.make_async_copy(ybuf.at[slot, half * MOE_BM:(half + 1) * MOE_BM, :], ys_hbm.at[rows(k), :],
                                     out_sem.at[slot, half])

    def for_pair(p, make, act):
        slot = p % MOE_NBUF
        act(make(MOE_GROUP * p, slot, 0))
        for part in range(1, MOE_GROUP):
            @pl.when(MOE_GROUP * p + part < cnt)
            def _():
                act(make(MOE_GROUP * p + part, slot, part))

    start = lambda c: c.start()
    wait = lambda c: c.wait()

    for i in range(MOE_NBUF - 1):
        @pl.when(i < npair)
        def _():
            for_pair(i, in_copy, start)

    def pair(p, carry):
        slot = p % MOE_NBUF
        for_pair(p, in_copy, wait)
        ahead = p + MOE_NBUF - 1

        @pl.when(ahead < npair)
        def _():
            for_pair(ahead, in_copy, start)

        @pl.when(p >= MOE_NBUF)
        def _():
            for_pair(p - MOE_NBUF, out_copy, wait)

        def run_expert(n_rows):
            xblk = xbuf[slot, 0:n_rows, :]
            hid = _silu(jnp.dot(xblk, wg_scr[...], preferred_element_type=F32)) * jnp.dot(
                xblk, wu_scr[...], preferred_element_type=F32)
            y = jnp.dot(hid.astype(BF16), wd_scr[...], preferred_element_type=F32)
            ybuf[slot, 0:n_rows, :] = y.astype(BF16)

        short = cnt - MOE_GROUP * p <= MOE_GROUP // 2

        @pl.when(short)
        def _():
            run_expert(MOE_GROUP // 2 * MOE_BM)

        @pl.when(jnp.logical_not(short))
        def _():
            run_expert(MOE_GROUP * MOE_BM)

        for_pair(p, out_copy, start)
        return carry

    lax.fori_loop(0, npair, pair, 0)

    for i in range(MOE_NBUF):
        @pl.when(npair > i)
        def _():
            for_pair(npair - 1 - i, out_copy, wait)


def _moe_combine_kernel(nblk_ref, x_ref, cols_ref, ys_ref, g_ref, b_ref, o_ref, acc_scr, *, t, n_chunks):
    s_used = nblk_ref[pl.program_id(0)] * MOE_BM
    cols = cols_ref[...]
    s1c = cols[:, 0:1].astype(jnp.int32)
    s2c = cols[:, 1:2].astype(jnp.int32)
    w1c = cols[:, 2:3]
    w2c = cols[:, 3:4]
    for c in range(n_chunks):
        @pl.when(c * MOE_CCH < s_used)
        def _():
            l_iota = lax.broadcasted_iota(jnp.int32, (t, MOE_CCH), 1) + c * MOE_CCH
            weighted = (jnp.where(l_iota == s1c, w1c, 0.0) + jnp.where(l_iota == s2c, w2c, 0.0)).astype(BF16)
            part = jnp.dot(weighted, ys_ref[0, c * MOE_CCH:(c + 1) * MOE_CCH, :],
                           preferred_element_type=F32)
            if c == 0:
                acc_scr[...] = part
            else:
                acc_scr[...] += part
    o_ref[...] = _layer_norm(ALPHA * x_ref[...] + acc_scr[...], g_ref[...], b_ref[...], LN_EPS)


def _moe_schedule(counts, tiles_per, maxb, bpt):
    cnt = counts[:, :, 0].astype(jnp.int32)
    cnt = cnt.reshape(-1, tiles_per, N_EXPERTS).sum(axis=1)
    n_tiles = cnt.shape[0]
    nblk = (cnt + MOE_BM - 1) // MOE_BM
    end = jnp.cumsum(nblk, axis=-1)
    start = end - nblk
    total = end[:, -1]
    padoff = jnp.broadcast_to((start * MOE_BM).astype(F32)[:, :, None], (n_tiles, N_EXPERTS, LANES))
    per_e = jnp.sum(nblk, axis=0)
    e_end = jnp.cumsum(per_e)
    e_base = e_end - per_e
    e_ids = jnp.arange(N_EXPERTS, dtype=jnp.int32)
    s_idx = jnp.arange(n_tiles * maxb, dtype=jnp.int32)
    last_e = jnp.max(jnp.where(per_e > 0, e_ids, 0))
    se = jnp.minimum(jnp.sum((s_idx[:, None] >= e_end[None, :]).astype(jnp.int32), axis=-1), last_e)
    oh_e = se[:, None] == e_ids[None, :]
    r = s_idx - jnp.sum(jnp.where(oh_e, e_base[None, :], 0), axis=-1)
    of_e = lambda a: jnp.sum(jnp.where(oh_e[:, None, :], a[None, :, :], 0), axis=-1)
    incl_s = of_e(jnp.cumsum(nblk, axis=0))
    tile = jnp.minimum(jnp.sum((incl_s <= r[:, None]).astype(jnp.int32), axis=-1), n_tiles - 1)
    oh_t = tile[:, None] == jnp.arange(n_tiles, dtype=jnp.int32)[None, :]
    of_t = lambda a: jnp.sum(jnp.where(oh_t, a, 0), axis=-1)
    j = of_t(of_e(start)) + r - (of_t(incl_s) - of_t(of_e(nblk)))
    sblk = jnp.where(s_idx < e_end[-1], tile * bpt + j, n_tiles * bpt)
    return total, padoff, sblk, e_base, per_e


def _moe_sort_call(x, route, counts, tm):
    n = x.shape[0]
    t = 1024 if n % 1024 == 0 else n
    n_tiles = n // t
    maxb = _moe_max_blocks(t)
    s_alloc = -(-maxb * MOE_BM // MOE_CCH) * MOE_CCH
    n_chunks = s_alloc // MOE_CH
    bpt = s_alloc // MOE_BM
    nblk, padoff, sblk, e_base, e_cnt = _moe_schedule(counts, t // tm, maxb, bpt)
    tri = jnp.triu(jnp.ones((t, t), BF16), k=1)
    last = n_tiles - 1
    xs, cols = pl.pallas_call(
        functools.partial(_moe_sort_kernel, t=t, n_tiles=n_tiles, n_chunks=n_chunks),
        grid_spec=pltpu.PrefetchScalarGridSpec(
            num_scalar_prefetch=1,
            grid=(n_tiles + 1,),
            in_specs=[pl.BlockSpec((t, D_MODEL), lambda i, nb: (jnp.minimum(i, last), 0)),
                      pl.BlockSpec((ROUTE_ROWS, t), lambda i, nb: (0, jnp.minimum(i, last))),
                      pl.BlockSpec((1, N_EXPERTS, LANES), lambda i, nb: (jnp.minimum(i, last), 0, 0)),
                      pl.BlockSpec((t, t), lambda i, nb: (0, 0))],
            out_specs=[pl.BlockSpec((1, s_alloc, D_MODEL), lambda i, nb: (i, 0, 0)),
                       pl.BlockSpec((t, LANES), lambda i, nb: (jnp.minimum(i, last), 0))]),
        out_shape=[jax.ShapeDtypeStruct((n_tiles + 1, s_alloc, D_MODEL), BF16),
                   jax.ShapeDtypeStruct((n, LANES), F32)],
        compiler_params=pltpu.CompilerParams(dimension_semantics=("arbitrary",),
                                             vmem_limit_bytes=VMEM_LIMIT),
        name="moe_sort",
    )(nblk, x, route, padoff, tri)
    return dict(xs=xs.reshape((n_tiles + 1) * s_alloc, D_MODEL), cols=cols, nblk=nblk, sblk=sblk,
                e_base=e_base, e_cnt=e_cnt, t=t, n_tiles=n_tiles, s_alloc=s_alloc)


def _moe_expert_call(sorted_streams, wg, wu, wd, layer):
    ns = len(sorted_streams)
    wspec = lambda r, c: pl.BlockSpec((1, 1, r, c), lambda e, *_: (layer, e, 0, 0))
    sched = [s[k] for s in sorted_streams for k in ("sblk", "e_base", "e_cnt")]
    xs = [s["xs"] for s in sorted_streams]
    return pl.pallas_call(
        functools.partial(_moe_expert_kernel, n_streams=ns),
        grid_spec=pltpu.PrefetchScalarGridSpec(
            num_scalar_prefetch=3 * ns,
            grid=(N_EXPERTS,),
            in_specs=[pl.BlockSpec(memory_space=pl.ANY)] * ns
            + [wspec(D_MODEL, D_EXP), wspec(D_MODEL, D_EXP), wspec(D_EXP, D_MODEL)],
            out_specs=[pl.BlockSpec(memory_space=pl.ANY)] * ns,
            scratch_shapes=[pltpu.VMEM((D_MODEL, D_EXP), BF16), pltpu.VMEM((D_MODEL, D_EXP), BF16),
                            pltpu.VMEM((D_EXP, D_MODEL), BF16),
                            pltpu.VMEM((MOE_NBUF, MOE_GROUP * MOE_BM, D_MODEL), BF16),
                            pltpu.VMEM((MOE_NBUF, MOE_GROUP * MOE_BM, D_MODEL), BF16),
                            pltpu.SemaphoreType.DMA((MOE_NBUF, MOE_GROUP)),
                            pltpu.SemaphoreType.DMA((MOE_NBUF, MOE_GROUP))]),
        out_shape=[jax.ShapeDtypeStruct(x.shape, BF16) for x in xs],
        input_output_aliases={3 * ns + i: i for i in range(ns)},
        compiler_params=pltpu.CompilerParams(dimension_semantics=("arbitrary",),
                                             vmem_limit_bytes=VMEM_LIMIT),
        name="moe_experts",
    )(*sched, *xs, wg, wu, wd)


def _moe_combine_call(x, srt, ys, g, b):
    n = x.shape[0]
    t, n_tiles, s_alloc = srt["t"], srt["n_tiles"], srt["s_alloc"]
    nblk, cols = srt["nblk"], srt["cols"]
    vec = pl.BlockSpec((1, D_MODEL), lambda i, nb: (0, 0))
    return pl.pallas_call(
        functools.partial(_moe_combine_kernel, t=t, n_chunks=s_alloc // MOE_CCH),
        grid_spec=pltpu.PrefetchScalarGridSpec(
            num_scalar_prefetch=1,
            grid=(n_tiles,),
            in_specs=[pl.BlockSpec((t, D_MODEL), lambda i, nb: (i, 0)),
                      pl.BlockSpec((t, LANES), lambda i, nb: (i, 0)),
                      pl.BlockSpec((1, s_alloc, D_MODEL), lambda i, nb: (i, 0, 0)), vec, vec],
            out_specs=pl.BlockSpec((t, D_MODEL), lambda i, nb: (i, 0)),
            scratch_shapes=[pltpu.VMEM((t, D_MODEL), F32)]),
        out_shape=jax.ShapeDtypeStruct((n, D_MODEL), F32),
        compiler_params=pltpu.CompilerParams(dimension_semantics=("parallel",),
                                             vmem_limit_bytes=VMEM_LIMIT),
        name="moe_combine",
    )(nblk, x, cols, ys.reshape(n_tiles + 1, s_alloc, D_MODEL), g, b)


def _row_tile(n):
    return 512 if n % 512 == 0 else n


def _seqs_per_step(batch, want):
    return want if batch % want == 0 else 1


def kernel(x_prompt, x_sample, state_mlstm_C, state_mlstm_n, state_mlstm_m, state_mlstm_conv, state_rwkv_S, state_rwkv_shift, ln0_g, ln0_b, w_in, conv_w, conv_b, b_i, b_f, gn_a_g, mu_shift, w0, w2, a0, a2, g2, k_k, k_a, r_k, lnx_g, lnx_b, w_out, ln1_g, ln1_b, w_router, b_router, we_gate, we_up, we_down, ln2_g, ln2_b):
    d_a_in = 4 * D_A + 2 * H_A
    zpad = jnp.zeros((DEPTH, D_MODEL, LANES - H_A), F32)
    w_cat = jnp.concatenate(
        [w_in[:, :, 0:4 * D_A], w_in[:, :, d_a_in:], w_in[:, :, 4 * D_A:4 * D_A + H_A], zpad,
         w_in[:, :, 4 * D_A + H_A:d_a_in], zpad], axis=-1).astype(BF16)
    bif = jnp.zeros((DEPTH, 1, W_IF_COLS), F32)
    bif = bif.at[:, 0, 0:H_A].set(b_i).at[:, 0, LANES:LANES + H_A].set(b_f)
    wlora = jnp.zeros((DEPTH, LANES, 2 * D_B), F32)
    wlora = wlora.at[:, 0:R_DECAY, 0:D_B].set(w2).at[:, R_DECAY:, D_B:].set(a2).astype(BF16)
    w0a0 = jnp.concatenate([w0, a0], axis=-1)[:, None, :]
    g2b = g2.astype(BF16)
    wob = w_out.astype(BF16)
    wr_t = w_router.T
    br = b_router[:, None]
    r1 = lambda v: v[None, :]

    def new_stream(x3, states):
        nb, seq, _ = x3.shape
        return dict(nb=nb, seq=seq, n=nb * seq, tm=_row_tile(nb * seq), states=states,
                    x=x3.reshape(nb * seq, D_MODEL), outs=[],
                    c_stack=jnp.zeros((DEPTH, nb, H_A, DH_A, DH_A), F32),
                    s_stack=jnp.zeros((DEPTH, nb, H_B, DH_B, DH_B), F32))

    def mixers_and_router(st, l):
        nb, seq, n, tm, states, x = (st[k] for k in ("nb", "seq", "n", "tm", "states", "x"))
        c_stack, s_stack = st["c_stack"], st["s_stack"]
        rwkv_nbb = _seqs_per_step(nb, 4 if seq >= CHUNK else 8)
        mlstm_nbb = _seqs_per_step(nb, 2 if seq >= CHUNK else 8)
        if states is None:
            c_in = jnp.zeros((1, nb, H_A, DH_A, DH_A), F32)
            s_in = jnp.zeros((1, nb, H_B, DH_B, DH_B), F32)
            layer_in = 0
            n0 = jnp.zeros((nb, H_A, DH_A), F32)
            m0 = jnp.zeros((nb, H_A), F32)
            conv0 = jnp.zeros((nb, CONV_W - 1, 2 * D_A), F32)
            shift0 = jnp.zeros((nb, D_B_IN), F32)
        else:
            c_in, s_in, layer_in = states[0], states[4], l
            n0, m0, conv0, shift0 = (states[k][l] for k in (1, 2, 3, 5))
        m0p = jnp.zeros((nb, 1, LANES), F32).at[:, 0, 0:H_A].set(m0)
        if l == 0:
            pa, pb, pif, x = _inproj_call(x, w_cat, tm, l, norm=(r1(ln0_g), r1(ln0_b)))
        else:
            pa, pb, pif = _inproj_call(x, w_cat, tm, l)
        ha, c_stack, n_new, m_new = _mlstm_call(
            pa.reshape(nb, seq, W_A_COLS), pif.reshape(nb, seq, W_IF_COLS), conv0, c_in, n0, m0p,
            conv_w[l], r1(conv_b[l]), bif[l], r1(gn_a_g[l]), c_stack, batch=nb, seq=seq,
            nbb=mlstm_nbb, layer_in=layer_in, layer=l)
        ha = ha.reshape(n, D_A)
        hb, s_stack = _rwkv_call(
            pb.reshape(nb, seq, D_B_IN), shift0[:, None, :], s_in, r1(mu_shift[l]), w0a0[l], wlora[l],
            g2b[l], r1(k_k[l]), r1(k_a[l]), r1(r_k[l].reshape(D_B)), r1(lnx_g[l]), r1(lnx_b[l]),
            s_stack, batch=nb, seq=seq, nbb=rwkv_nbb, layer_in=layer_in, layer=l)
        hb = hb.reshape(n, D_B)
        x1, route, counts = _outproj_call(x, ha, hb, wob, r1(ln1_g[l]), r1(ln1_b[l]), wr_t, br, tm, l)
        pa3 = pa.reshape(nb, seq, W_A_COLS)
        full = jnp.concatenate([conv0, pa3[:, :, 0:2 * D_A]], axis=1) if seq < CONV_W - 1 else pa3[:, :, 0:2 * D_A]
        conv_new = full[:, -(CONV_W - 1):, :]
        shift_new = pb.reshape(nb, seq, D_B_IN)[:, -1, :]
        st["outs"].append((n_new, m_new[:, 0, 0:H_A], conv_new, shift_new))
        st["c_stack"], st["s_stack"] = c_stack, s_stack
        return x1, route, counts

    streams = [new_stream(x_prompt, None),
               new_stream(x_sample, (state_mlstm_C, state_mlstm_n, state_mlstm_m, state_mlstm_conv,
                                     state_rwkv_S, state_rwkv_shift))]
    for l in range(DEPTH):
        routed = [mixers_and_router(st, l) for st in streams]
        srt = [_moe_sort_call(x1, route, counts, st["tm"]) for (x1, route, counts), st in zip(routed, streams)]
        ys = _moe_expert_call(srt, we_gate, we_up, we_down, l)
        for st, (x1, _, _), s, y in zip(streams, routed, srt, ys):
            st["x"] = _moe_combine_call(x1, s, y, r1(ln2_g[l]), r1(ln2_b[l]))

    def finish(st):
        n_all, m_all, conv_all, shift_all = (jnp.stack(s) for s in zip(*st["outs"]))
        return (st["x"].reshape(st["nb"], st["seq"], D_MODEL),
                (st["c_stack"], n_all, m_all, conv_all, st["s_stack"], shift_all))

    y_prompt, (p_c, p_n, p_m, p_conv, p_s, p_shift) = finish(streams[0])
    y_sample, (s_c, s_n, s_m, s_conv, s_s, s_shift) = finish(streams[1])
    return (y_prompt, y_sample, p_c, p_n, p_m, p_conv, p_s, p_shift, s_c, s_n, s_m, s_conv, s_s, s_shift)
```

```python
import functools
import math

import jax
import jax.numpy as jnp
from jax import lax
from jax.experimental import pallas as pl
from jax.experimental.pallas import tpu as pltpu

F32 = jnp.float32
BF16 = jnp.bfloat16

D_MODEL = 1024
DEPTH = 4
D_A = 512
D_B = 512
DH_A = 128
H_A = 4
DH_B = 64
H_B = 8
N_PAIRS = H_B // 2
CONV_W = 4
CHUNK = 64
R_DECAY = 64
R_AAA = 64
R_GATE = 128
D_B_IN = 3 * D_B + R_DECAY + R_AAA + R_GATE
N_EXPERTS = 16
N_GROUPS = 4
EXPERTS_PER_GROUP = 4
D_EXP = 512
ALPHA = (2 * DEPTH) ** 0.25
LN_EPS = 1e-5
GN_EPS_A = 1e-6
GN_EPS_B = 64e-5

LANES = 128
SOLVE_BLOCK = 16
VMEM_LIMIT = 48 * 1024 * 1024

NN = (((1,), (0,)), ((), ()))
NT = (((1,), (1,)), ((), ()))
TN = (((0,), (0,)), ((), ()))


def _dot(a, b, dims=NN):
    return lax.dot_general(a.astype(BF16), b.astype(BF16), dims, preferred_element_type=F32)


def _split3(x):
    hi = x.astype(BF16)
    r1 = x - hi.astype(F32)
    mid = r1.astype(BF16)
    lo = (r1 - mid.astype(F32)).astype(BF16)
    return hi, mid, lo


def _dot_exact_lhs(a, x, dims=NN):
    a = a.astype(BF16)
    hi, mid, lo = _split3(x)
    d = lambda p: lax.dot_general(a, p, dims, preferred_element_type=F32)
    return d(hi) + d(mid) + d(lo)


def _dot_exact_rhs(x, b, dims=NN):
    b = b.astype(BF16)
    hi, mid, lo = _split3(x)
    d = lambda p: lax.dot_general(p, b, dims, preferred_element_type=F32)
    return d(hi) + d(mid) + d(lo)


def _layer_norm(x, g, b, eps):
    mu = jnp.mean(x, axis=-1, keepdims=True)
    xc = x - mu
    var = jnp.mean(xc * xc, axis=-1, keepdims=True)
    return xc * lax.rsqrt(var + eps) * g + b


def _sigmoid(x):
    return 1.0 / (1.0 + jnp.exp(-x))


def _log_sigmoid(x):
    return jnp.minimum(x, 0.0) - jnp.log(1.0 + jnp.exp(-jnp.abs(x)))


def _silu(x):
    return x * _sigmoid(x)


W_A_COLS = 4 * D_A
W_IF_COLS = 2 * LANES
W_IN_COLS = W_A_COLS + D_B_IN + W_IF_COLS


def _inproj_kernel(x_ref, w_ref, *rest, input_norm):
    if input_norm:
        g_ref, b_ref, pa_ref, pb_ref, pif_ref, xn_ref = rest
        x = _layer_norm(x_ref[...], g_ref[...], b_ref[...], LN_EPS)
        xn_ref[...] = x
    else:
        pa_ref, pb_ref, pif_ref = rest
        x = x_ref[...]
    xb = x.astype(BF16)
    pa_ref[...] = jnp.dot(xb, w_ref[0, :, 0:W_A_COLS], preferred_element_type=F32)
    pb_ref[...] = jnp.dot(xb, w_ref[0, :, W_A_COLS:W_A_COLS + D_B_IN], preferred_element_type=F32)
    pif_ref[...] = jnp.dot(xb, w_ref[0, :, W_A_COLS + D_B_IN:W_IN_COLS], preferred_element_type=F32)


def _inproj_call(x, w, tm, layer, norm=None):
    n = x.shape[0]
    row = lambda c: pl.BlockSpec((tm, c), lambda i: (i, 0))
    vec = pl.BlockSpec((1, D_MODEL), lambda i: (0, 0))
    in_specs = [row(D_MODEL), pl.BlockSpec((1, D_MODEL, W_IN_COLS), lambda i: (layer, 0, 0))]
    out_specs = [row(W_A_COLS), row(D_B_IN), row(W_IF_COLS)]
    out_shape = [jax.ShapeDtypeStruct((n, W_A_COLS), F32), jax.ShapeDtypeStruct((n, D_B_IN), F32),
                 jax.ShapeDtypeStruct((n, W_IF_COLS), F32)]
    args = [x, w]
    if norm is not None:
        in_specs += [vec, vec]
        out_specs.append(row(D_MODEL))
        out_shape.append(jax.ShapeDtypeStruct((n, D_MODEL), F32))
        args += list(norm)
    return pl.pallas_call(
        functools.partial(_inproj_kernel, input_norm=norm is not None),
        grid=(n // tm,),
        in_specs=in_specs,
        out_specs=out_specs,
        out_shape=out_shape,
        compiler_params=pltpu.CompilerParams(dimension_semantics=("parallel",),
                                             vmem_limit_bytes=VMEM_LIMIT),
        name="inproj",
    )(*args)


CONV_PAD = 8


def _mlstm_kernel(pa_ref, pif_ref, conv0_ref, c0_ref, n0_ref, m0_ref, convw_ref, convb_ref,
                  bif_ref, gn_ref, *rest, c, nc, nbb, chained):
    h_ref, c_out_ref, n_out_ref, m_out_ref, qk_scr, c_scr, n_scr, m_scr = rest[1:] if chained else rest
    ci = pl.program_id(1)
    rows = nbb * c
    cshift = int(math.log2(c))
    chains = [(nb, h) for nb in range(nbb) for h in range(H_A)]
    prev0 = CONV_PAD - (CONV_W - 1)

    @pl.when(ci == 0)
    def _():
        c_scr[...] = c0_ref[0]
        n_scr[...] = n0_ref[...]
        m_scr[...] = m0_ref[...]
        for nb in range(nbb):
            qk_scr[nb, prev0:CONV_PAD, :] = conv0_ref[nb]

    cat0 = lambda xs: xs[0] if len(xs) == 1 else jnp.concatenate(xs, axis=0)
    acc_l = []
    for nb in range(nbb):
        u = pa_ref[nb, :, 0:2 * D_A]
        qk_scr[nb, CONV_PAD:CONV_PAD + c, :] = u
        acc = convb_ref[...]
        for j in range(CONV_W):
            acc = acc + qk_scr[nb, prev0 + j:prev0 + j + c, :] * convw_ref[j:j + 1, :]
        qk_scr[nb, prev0:CONV_PAD, :] = u[c - (CONV_W - 1):c, :]
        acc_l.append(acc)
    qk = _silu(cat0(acc_l))

    gates = cat0([pif_ref[nb] for nb in range(nbb)]) + bif_ref[...]
    li_all = gates[:, 0:LANES]
    lf_all = _log_sigmoid(gates[:, LANES:2 * LANES])
    row = lax.broadcasted_iota(jnp.int32, (rows, rows), 0)
    col = lax.broadcasted_iota(jnp.int32, (rows, rows), 1)
    tril = jnp.where((col <= row) & ((row >> cshift) == (col >> cshift)), 1.0, 0.0)
    b_all = _dot_exact_lhs(tril, lf_all)
    z_all = li_all - b_all
    crow = lax.broadcasted_iota(jnp.int32, (c, c), 0)
    ccol = lax.broadcasted_iota(jnp.int32, (c, c), 1)
    causal = ccol <= crow
    hrow = lax.broadcasted_iota(jnp.int32, (H_A * c, LANES), 0)
    hlane = lax.broadcasted_iota(jnp.int32, (H_A * c, LANES), 1)
    head_pick = jnp.where((hrow >> cshift) == hlane, 1.0, 0.0)
    z_rows = [_dot_exact_lhs(head_pick, z_all[nb * c:(nb + 1) * c], NT) for nb in range(nbb)]
    lane1 = lax.broadcasted_iota(jnp.int32, (1, LANES), 1)

    rs = lambda nb: slice(nb * c, (nb + 1) * c)
    q_l = [qk[rs(nb), h * DH_A:(h + 1) * DH_A] for nb, h in chains]
    k_l = [qk[rs(nb), D_A + h * DH_A:D_A + (h + 1) * DH_A] * (DH_A ** -0.5) for nb, h in chains]
    v_l = [pa_ref[nb, :, 2 * D_A + h * DH_A:2 * D_A + (h + 1) * DH_A] for nb, h in chains]
    c_l = [c_scr[nb, h] for nb, h in chains]
    n_l = [n_scr[nb, h:h + 1, :] for nb, h in chains]
    b_col = [b_all[rs(nb), h:h + 1] for nb, h in chains]
    li_col = [li_all[rs(nb), h:h + 1] for nb, h in chains]
    m_prev = [m_scr[nb][:, h:h + 1] for nb, h in chains]
    dmat = [jnp.where(causal, bc + z_rows[nb][h * c:(h + 1) * c], -jnp.inf)
            for bc, (nb, h) in zip(b_col, chains)]
    m_inter = [bc + mp for bc, mp in zip(b_col, m_prev)]
    m_t = [jnp.maximum(mi, jnp.max(d, axis=-1, keepdims=True)) for mi, d in zip(m_inter, dmat)]
    qk_dot = [_dot(q, k, NT) for q, k in zip(q_l, k_l)]
    qc = [_dot(q, cm) for q, cm in zip(q_l, c_l)]
    s_l = [x * jnp.exp(d - mt) for x, d, mt in zip(qk_dot, dmat, m_t)]
    sv = [_dot(s, v) for s, v in zip(s_l, v_l)]
    b_last = [bc[c - 1:c, :] for bc in b_col]
    g_s = [bl - bc + li for bl, bc, li in zip(b_last, b_col, li_col)]
    m_new = [jnp.maximum(bl + mp, jnp.max(gs, axis=0, keepdims=True))
             for bl, mp, gs in zip(b_last, m_prev, g_s)]
    wk = [jnp.exp(gs - mn) * k for gs, mn, k in zip(g_s, m_new, k_l)]
    w_old = [jnp.exp(bl + mp - mn) for bl, mp, mn in zip(b_last, m_prev, m_new)]
    kv = [_dot(w, v, TN) for w, v in zip(wk, v_l)]
    for i, (nb, h) in enumerate(chains):
        c_scr[nb, h] = w_old[i] * c_l[i] + kv[i]
        n_scr[nb, h:h + 1, :] = w_old[i] * n_l[i] + jnp.sum(wk[i], axis=0, keepdims=True)
    for nb in range(nbb):
        m_row = m_scr[nb]
        for h in range(H_A):
            m_row = jnp.where(lane1 == h, m_new[nb * H_A + h], m_row)
        m_scr[nb] = m_row
    w_inter = [jnp.exp(mi - mt) for mi, mt in zip(m_inter, m_t)]
    qn = [jnp.sum(q * nv, axis=-1, keepdims=True) for q, nv in zip(q_l, n_l)]
    s_sum = [jnp.sum(s, axis=-1, keepdims=True) for s in s_l]
    den = [w * a + b for w, a, b in zip(w_inter, qn, s_sum)]
    hh = [(w * a + b) / jnp.maximum(jnp.abs(d), jnp.exp(-mt))
          for w, a, b, d, mt in zip(w_inter, qc, sv, den, m_t)]
    mu = [jnp.mean(x, axis=-1, keepdims=True) for x in hh]
    hc = [x - m for x, m in zip(hh, mu)]
    var = [jnp.mean(x * x, axis=-1, keepdims=True) for x in hc]
    for i, (nb, h) in enumerate(chains):
        sl = slice(h * DH_A, (h + 1) * DH_A)
        hn = hc[i] * lax.rsqrt(var[i] + GN_EPS_A)
        o_pre = pa_ref[nb, :, 3 * D_A + h * DH_A:3 * D_A + (h + 1) * DH_A]
        h_ref[nb, :, sl] = hn * gn_ref[:, sl] * _sigmoid(o_pre)

    @pl.when(ci == nc - 1)
    def _():
        c_out_ref[0] = c_scr[...]
        n_out_ref[...] = n_scr[...]
        m_out_ref[...] = m_scr[...]


def _mlstm_call(pa, pif, conv0, c_in, n0, m0, convw, convb, bif, gn, c_stack, *, batch, seq, nbb,
                layer_in, layer):
    c = math.gcd(seq, CHUNK)
    nc = seq // c
    tok = lambda w: pl.BlockSpec((nbb, c, w), lambda b, i: (b, i, 0))
    vec = lambda r, w: pl.BlockSpec((r, w), lambda b, i: (0, 0))
    st5 = lambda l: pl.BlockSpec((1, nbb, H_A, DH_A, DH_A), lambda b, i: (l, b, 0, 0, 0))
    st3 = lambda r, w: pl.BlockSpec((nbb, r, w), lambda b, i: (b, 0, 0))
    chained = c_stack is not None
    in_specs = [tok(W_A_COLS), tok(W_IF_COLS), st3(CONV_W - 1, 2 * D_A), st5(layer_in), st3(H_A, DH_A),
                st3(1, LANES), vec(CONV_W, 2 * D_A), vec(1, 2 * D_A), vec(1, W_IF_COLS), vec(1, D_A)]
    args = [pa, pif, conv0, c_in, n0, m0, convw, convb, bif, gn]
    if chained:
        in_specs.append(pl.BlockSpec(memory_space=pl.ANY))
        args.append(c_stack)
    return pl.pallas_call(
        functools.partial(_mlstm_kernel, c=c, nc=nc, nbb=nbb, chained=chained),
        grid=(batch // nbb, nc),
        in_specs=in_specs,
        out_specs=[tok(D_A), st5(layer), st3(H_A, DH_A), st3(1, LANES)],
        out_shape=[jax.ShapeDtypeStruct((batch, seq, D_A), F32),
                   jax.ShapeDtypeStruct((DEPTH, batch, H_A, DH_A, DH_A), F32),
                   jax.ShapeDtypeStruct((batch, H_A, DH_A), F32),
                   jax.ShapeDtypeStruct((batch, 1, LANES), F32)],
        scratch_shapes=[pltpu.VMEM((nbb, CONV_PAD + c, 2 * D_A), F32),
                        pltpu.VMEM((nbb, H_A, DH_A, DH_A), F32),
                        pltpu.VMEM((nbb, H_A, DH_A), F32),
                        pltpu.VMEM((nbb, 1, LANES), F32)],
        input_output_aliases={len(args) - 1: 1} if chained else {},
        compiler_params=pltpu.CompilerParams(dimension_semantics=("parallel", "arbitrary"),
                                             vmem_limit_bytes=VMEM_LIMIT),
        name="mlstm",
    )(*args)


SHIFT_PAD = 8


def _neumann_inverses(ms, size, t):
    row = lax.broadcasted_iota(jnp.int32, (size, size), 0)
    col = lax.broadcasted_iota(jnp.int32, (size, size), 1)
    eye = jnp.where(row == col, 1.0, 0.0)
    blk = min(SOLVE_BLOCK, t)
    shift = int(math.log2(blk))
    same = (row >> shift) == (col >> shift)
    ds = [jnp.where(same, m, 0.0) for m in ms]
    ps = [eye + d for d in ds]
    xs = ds
    for _ in range(shift - 1):
        xs = [_dot(x, x) for x in xs]
        ps = [p + _dot(p, x) for p, x in zip(ps, xs)]
    nblk = t // blk
    if nblk == 1:
        return ps, None
    ns = [_dot(p, m - d) for p, m, d in zip(ps, ms, ds)]
    qs = [eye + n for n in ns]
    ys = ns
    for _ in range(int(math.log2(nblk)) - 1):
        ys = [_dot(y, y) for y in ys]
        qs = [q + _dot(q, y) for q, y in zip(qs, ys)]
    return ps, qs


def _rwkv_kernel(pb_ref, shift0_ref, s0_ref, mu_ref, w0a0_ref, wlora_ref, g2_ref, kk_ref, ka_ref,
                 rk_ref, lng_ref, lnb_ref, *rest, t, nc, nbb, chained):
    h_ref, s_out_ref, sh_scr, s_scr = rest[1:] if chained else rest
    ci = pl.program_id(1)
    rows = nbb * t
    t2 = 2 * t
    groups = [(nb, pr) for nb in range(nbb) for pr in range(N_PAIRS)]

    brow = lax.broadcasted_iota(jnp.int32, (LANES, LANES), 0)
    bcol = lax.broadcasted_iota(jnp.int32, (LANES, LANES), 1)
    blockdiag = (brow >= DH_B) == (bcol >= DH_B)
    bd2 = jnp.where(blockdiag, 1.0, 0.0).astype(BF16)

    @pl.when(ci == 0)
    def _():
        er = lax.broadcasted_iota(jnp.int32, (DH_B, LANES), 0)
        ec = lax.broadcasted_iota(jnp.int32, (DH_B, LANES), 1)
        dup_cols = jnp.where((ec & (DH_B - 1)) == er, 1.0, 0.0)
        for nb in range(nbb):
            sh_scr[nb, SHIFT_PAD - 1:SHIFT_PAD, :] = shift0_ref[nb]
            for pr in range(N_PAIRS):
                x = s0_ref[0, nb, 2 * pr:2 * pr + 2].reshape(LANES, DH_B)
                s_scr[nb * N_PAIRS + pr] = jnp.where(blockdiag, _dot_exact_rhs(x, dup_cols), 0.0)

    p_l, prev_l = [], []
    for nb in range(nbb):
        p_nb = pb_ref[nb]
        sh_scr[nb, SHIFT_PAD:SHIFT_PAD + t, :] = p_nb
        prev_l.append(sh_scr[nb, SHIFT_PAD - 1:SHIFT_PAD - 1 + t, :])
        sh_scr[nb, SHIFT_PAD - 1:SHIFT_PAD, :] = p_nb[t - 1:t, :]
        p_l.append(p_nb)
    cat0 = lambda xs: xs[0] if len(xs) == 1 else jnp.concatenate(xs, axis=0)
    p = cat0(p_l)
    prev = cat0(prev_l)
    pb = p + (prev - p) * mu_ref[...]

    r = pb[:, 0:D_B]
    kr = pb[:, D_B:2 * D_B]
    vr = pb[:, 2 * D_B:3 * D_B]
    lora_in = pb[:, 3 * D_B:3 * D_B + LANES]
    gl = pb[:, 3 * D_B + LANES:3 * D_B + 2 * LANES]
    lane = lax.broadcasted_iota(jnp.int32, (rows, LANES), 1)
    lora_act = jnp.where(lane < R_DECAY, jnp.tanh(lora_in), lora_in)
    z = _dot(lora_act, wlora_ref[...]) + w0a0_ref[...]
    ld = -jnp.exp(_log_sigmoid(z[:, 0:D_B]) - 0.5)
    a = _sigmoid(z[:, D_B:2 * D_B])
    g = _dot(_sigmoid(gl), g2_ref[...])

    def seg_sum(x):
        xr = jnp.concatenate([x[:, q * LANES:(q + 1) * LANES] for q in range(N_PAIRS)], axis=0)
        hi = xr.astype(BF16)
        lo = (xr - hi.astype(F32)).astype(BF16)
        s = (jnp.dot(hi, bd2, preferred_element_type=F32) + jnp.dot(lo, bd2, preferred_element_type=F32))
        return jnp.concatenate([s[q * rows:(q + 1) * rows] for q in range(N_PAIRS)], axis=1)

    kk_raw = kr * kk_ref[...]
    kk = kk_raw / jnp.maximum(jnp.sqrt(seg_sum(kk_raw * kk_raw)), 1e-12)
    k2 = kr * (1.0 + (a - 1.0) * ka_ref[...])
    bonus = seg_sum(r * k2 * rk_ref[...]) * vr

    row = lax.broadcasted_iota(jnp.int32, (rows, rows), 0)
    col = lax.broadcasted_iota(jnp.int32, (rows, rows), 1)
    tshift = int(math.log2(t))
    tril = jnp.where((col <= row) & ((row >> tshift) == (col >> tshift)), 1.0, 0.0)
    lc = _dot_exact_lhs(tril, ld)
    lends = [lc[(nb + 1) * t - 1:(nb + 1) * t, :] for nb in range(nbb)]
    lend_rows = cat0([jnp.broadcast_to(le, (t, D_B)) for le in lends])
    e_nc = jnp.exp(-lc)
    e_end = jnp.exp(lend_rows - lc)
    b_raw = kk * a
    at = -kk * jnp.exp(lc - ld)
    rt = r * jnp.exp(lc)
    kt = k2 * e_nc
    bt = b_raw * e_nc
    kend = k2 * e_end
    bend = b_raw * e_end

    srow = lax.broadcasted_iota(jnp.int32, (t2, t2), 0)
    scol = lax.broadcasted_iota(jnp.int32, (t2, t2), 1)
    strict = scol < srow
    incl = scol <= srow
    lane2 = lax.broadcasted_iota(jnp.int32, (t2, LANES), 1)
    row2 = lax.broadcasted_iota(jnp.int32, (t2, LANES), 0)
    head_sel = (lane2 >= DH_B) == (row2 >= t)
    first_head = lax.broadcasted_iota(jnp.int32, (t, LANES), 1) < DH_B

    def blk(x, nb, pr):
        return x[nb * t:(nb + 1) * t, pr * LANES:(pr + 1) * LANES]

    def stack(x):
        return jnp.where(head_sel, jnp.concatenate([x, x], axis=0), 0.0)

    lhs = [jnp.concatenate([stack(blk(at, *gp)), stack(blk(rt, *gp))], axis=0) for gp in groups]
    rk_s = [stack(blk(kt, *gp)) for gp in groups]
    rb_s = [stack(blk(bt, *gp)) for gp in groups]
    v_p = [blk(vr, *gp) for gp in groups]
    v_s = [jnp.concatenate([v, v], axis=0) for v in v_p]
    s_p = [s_scr[i] for i in range(len(groups))]
    gk = [_dot(l, x, NT) for l, x in zip(lhs, rk_s)]
    gb = [_dot(l, x, NT) for l, x in zip(lhs, rb_s)]
    xs = [_dot(l, s, NT) for l, s in zip(lhs, s_p)]
    m_ab = [jnp.where(strict, x[0:t2], 0.0) for x in gb]
    ps, qs = _neumann_inverses(m_ab, t2, t)
    rhs = [x[0:t2] + _dot(jnp.where(strict, y[0:t2], 0.0), v) for x, y, v in zip(xs, gk, v_s)]
    sa_s = [_dot(pm, x) for pm, x in zip(ps, rhs)]
    if qs is not None:
        sa_s = [_dot(qm, x) for qm, x in zip(qs, sa_s)]
    o_s = [x[t2:] + _dot(jnp.where(incl, y[t2:], 0.0), v) + _dot(jnp.where(incl, w[t2:], 0.0), sa)
           for x, y, w, v, sa in zip(xs, gk, gb, v_s, sa_s)]
    o_p = [jnp.where(first_head, o[0:t], o[t:t2]) for o in o_s]
    sa_p = [jnp.where(first_head, sa[0:t], sa[t:t2]) for sa in sa_s]
    upd = [_dot(jnp.concatenate([v, sa], axis=0),
                jnp.concatenate([blk(kend, *gp), blk(bend, *gp)], axis=0), TN)
           for v, sa, gp in zip(v_p, sa_p, groups)]
    for i, (nb, pr) in enumerate(groups):
        dec = jnp.exp(lends[nb][:, pr * LANES:(pr + 1) * LANES])
        s_scr[i] = s_p[i] * dec + jnp.where(blockdiag, upd[i], 0.0)

    o = cat0([jnp.concatenate(o_p[nb * N_PAIRS:(nb + 1) * N_PAIRS], axis=1) for nb in range(nbb)])
    mean = seg_sum(o) * (1.0 / DH_B)
    oc = o - mean
    var = seg_sum(oc * oc) * (1.0 / DH_B)
    out = oc * lax.rsqrt(var + GN_EPS_B) * lng_ref[...] + lnb_ref[...]
    res = (out + bonus) * g
    for nb in range(nbb):
        h_ref[nb] = res[nb * t:(nb + 1) * t, :]

    @pl.when(ci == nc - 1)
    def _():
        fr = lax.broadcasted_iota(jnp.int32, (LANES, DH_B), 0)
        fc = lax.broadcasted_iota(jnp.int32, (LANES, DH_B), 1)
        dup_rows = jnp.where((fr & (DH_B - 1)) == fc, 1.0, 0.0)
        for i, (nb, pr) in enumerate(groups):
            packed = _dot_exact_rhs(s_scr[i], dup_rows)
            s_out_ref[0, nb, 2 * pr:2 * pr + 2] = packed.reshape(2, DH_B, DH_B)


def _rwkv_call(pb, shift0, s_in, mu, w0a0, wlora, g2, kk, ka, rk, lng, lnb, s_stack, *, batch, seq, nbb,
               layer_in, layer):
    t = math.gcd(seq, CHUNK)
    nc = seq // t
    tok = lambda w: pl.BlockSpec((nbb, t, w), lambda b, i: (b, i, 0))
    vec = lambda r, w: pl.BlockSpec((r, w), lambda b, i: (0, 0))
    st5 = lambda l: pl.BlockSpec((1, nbb, H_B, DH_B, DH_B), lambda b, i: (l, b, 0, 0, 0))
    chained = s_stack is not None
    in_specs = [tok(D_B_IN), pl.BlockSpec((nbb, 1, D_B_IN), lambda b, i: (b, 0, 0)), st5(layer_in),
                vec(1, D_B_IN), vec(1, 2 * D_B), vec(LANES, 2 * D_B), vec(R_GATE, D_B),
                vec(1, D_B), vec(1, D_B), vec(1, D_B), vec(1, D_B), vec(1, D_B)]
    args = [pb, shift0, s_in, mu, w0a0, wlora, g2, kk, ka, rk, lng, lnb]
    if chained:
        in_specs.append(pl.BlockSpec(memory_space=pl.ANY))
        args.append(s_stack)
    return pl.pallas_call(
        functools.partial(_rwkv_kernel, t=t, nc=nc, nbb=nbb, chained=chained),
        grid=(batch // nbb, nc),
        in_specs=in_specs,
        out_specs=[tok(D_B), st5(layer)],
        out_shape=[jax.ShapeDtypeStruct((batch, seq, D_B), F32),
                   jax.ShapeDtypeStruct((DEPTH, batch, H_B, DH_B, DH_B), F32)],
        scratch_shapes=[pltpu.VMEM((nbb, SHIFT_PAD + t, D_B_IN), F32),
                        pltpu.VMEM((nbb * N_PAIRS, LANES, LANES), F32)],
        input_output_aliases={len(args) - 1: 1} if chained else {},
        compiler_params=pltpu.CompilerParams(dimension_semantics=("parallel", "arbitrary"),
                                             vmem_limit_bytes=VMEM_LIMIT),
        name="rwkv",
    )(*args)


ROUTE_ROWS = 8


def _outproj_kernel(x_ref, ha_ref, hb_ref, wo_ref, g_ref, b_ref, wr_ref, br_ref, x1_ref, route_ref,
                    cnt_ref, *, tm):
    y = _dot(ha_ref[...], wo_ref[0, 0:D_A, :]) + _dot(hb_ref[...], wo_ref[0, D_A:D_A + D_B, :])
    x1 = _layer_norm(ALPHA * x_ref[...] + y, g_ref[...], b_ref[...], LN_EPS)
    x1_ref[...] = x1
    logits = lax.dot_general(wr_ref[...], x1, NT, precision=lax.Precision.HIGHEST,
                             preferred_element_type=F32)
    mx = jnp.max(logits, axis=0, keepdims=True)
    ex = jnp.exp(logits - mx)
    probs = ex / jnp.sum(ex, axis=0, keepdims=True)
    sel = probs + br_ref[...]
    neg = -jnp.inf

    def top2(rows):
        m1 = functools.reduce(jnp.maximum, rows)
        i1 = jnp.full(m1.shape, len(rows) - 1, jnp.int32)
        for j in range(len(rows) - 2, -1, -1):
            i1 = jnp.where(rows[j] == m1, j, i1)
        rest = [jnp.where(i1 == j, neg, rows[j]) for j in range(len(rows))]
        m2 = functools.reduce(jnp.maximum, rest)
        i2 = jnp.full(m2.shape, len(rows) - 1, jnp.int32)
        for j in range(len(rows) - 2, -1, -1):
            i2 = jnp.where(rest[j] == m2, j, i2)
        return m1, i1, m2, i2

    g_score, g_i1, g_i2 = [], [], []
    for gidx in range(N_GROUPS):
        rows = [sel[gidx * EXPERTS_PER_GROUP + j:gidx * EXPERTS_PER_GROUP + j + 1, :]
                for j in range(EXPERTS_PER_GROUP)]
        m1, i1, m2, i2 = top2(rows)
        g_score.append(m1 + m2)
        g_i1.append(i1)
        g_i2.append(i2)
    best = functools.reduce(jnp.maximum, g_score)
    grp = jnp.full(best.shape, N_GROUPS - 1, jnp.int32)
    for gidx in range(N_GROUPS - 2, -1, -1):
        grp = jnp.where(g_score[gidx] == best, gidx, grp)
    l1 = g_i1[N_GROUPS - 1]
    l2 = g_i2[N_GROUPS - 1]
    for gidx in range(N_GROUPS - 2, -1, -1):
        l1 = jnp.where(grp == gidx, g_i1[gidx], l1)
        l2 = jnp.where(grp == gidx, g_i2[gidx], l2)
    e1 = grp * EXPERTS_PER_GROUP + l1
    e2 = grp * EXPERTS_PER_GROUP + l2
    e_iota = lax.broadcasted_iota(jnp.int32, (N_EXPERTS, tm), 0)
    p1 = jnp.sum(jnp.where(e_iota == e1, probs, 0.0), axis=0, keepdims=True)
    p2 = jnp.sum(jnp.where(e_iota == e2, probs, 0.0), axis=0, keepdims=True)
    tot = p1 + p2
    r8 = lax.broadcasted_iota(jnp.int32, (ROUTE_ROWS, tm), 0)
    route_ref[...] = jnp.where(r8 == 0, e1.astype(F32),
                               jnp.where(r8 == 1, e2.astype(F32),
                                         jnp.where(r8 == 2, p1 / tot, jnp.where(r8 == 3, p2 / tot, 0.0))))
    picked = jnp.where((e_iota == e1) | (e_iota == e2), 1.0, 0.0)
    cnt_ref[0] = jnp.broadcast_to(jnp.sum(picked, axis=1, keepdims=True), (N_EXPERTS, LANES))


def _outproj_call(x, ha, hb, wo, g, b, wr_t, br, tm, layer):
    n = x.shape[0]
    row = lambda w: pl.BlockSpec((tm, w), lambda i: (i, 0))
    full = lambda r, w: pl.BlockSpec((r, w), lambda i: (0, 0))
    return pl.pallas_call(
        functools.partial(_outproj_kernel, tm=tm),
        grid=(n // tm,),
        in_specs=[row(D_MODEL), row(D_A), row(D_B),
                  pl.BlockSpec((1, D_MODEL, D_MODEL), lambda i: (layer, 0, 0)), full(1, D_MODEL),
                  full(1, D_MODEL), full(N_EXPERTS, D_MODEL), full(N_EXPERTS, 1)],
        out_specs=[row(D_MODEL), pl.BlockSpec((ROUTE_ROWS, tm), lambda i: (0, i)),
                   pl.BlockSpec((1, N_EXPERTS, LANES), lambda i: (i, 0, 0))],
        out_shape=[jax.ShapeDtypeStruct((n, D_MODEL), F32), jax.ShapeDtypeStruct((ROUTE_ROWS, n), F32),
                   jax.ShapeDtypeStruct((n // tm, N_EXPERTS, LANES), F32)],
        compiler_params=pltpu.CompilerParams(dimension_semantics=("parallel",),
                                             vmem_limit_bytes=VMEM_LIMIT),
        name="outproj",
    )(x, ha, hb, wo, g, b, wr_t, br)


MOE_BM = 128
MOE_CH = 512
MOE_CCH = 1024
MOE_GROUP = 4
MOE_NBUF = 3


def _moe_max_blocks(t):
    return -(-2 * t // MOE_BM) + N_EXPERTS - 1


def _moe_sort_kernel(nblk_ref, x_ref, route_ref, padoff_ref, tri_ref, xs_ref, cols_ref, *, t, n_tiles,
                     n_chunks):
    i = pl.program_id(0)

    @pl.when(i >= n_tiles)
    def _():
        xs_ref[...] = jnp.zeros_like(xs_ref)

    @pl.when(i < n_tiles)
    def _():
        s_used = nblk_ref[jnp.minimum(i, n_tiles - 1)] * MOE_BM
        xb = x_ref[...].astype(BF16)
        route = route_ref[...]
        e_iota = lax.broadcasted_iota(jnp.int32, (N_EXPERTS, t), 0)
        a1 = e_iota == route[0:1].astype(jnp.int32)
        a2 = e_iota == route[1:2].astype(jnp.int32)
        picked = jnp.where(a1 | a2, 1.0, 0.0).astype(BF16)
        rank = jnp.dot(picked, tri_ref[...], preferred_element_type=F32)
        base = padoff_ref[0][:, 0:1] + rank
        slot1 = jnp.sum(jnp.where(a1, base, 0.0), axis=0, keepdims=True)
        slot2 = jnp.sum(jnp.where(a2, base, 0.0), axis=0, keepdims=True)
        r8 = lax.broadcasted_iota(jnp.int32, (ROUTE_ROWS, t), 0)
        rows = jnp.where(r8 == 0, slot1, jnp.where(r8 == 1, slot2,
                                                   jnp.where(r8 == 2, route[2:3],
                                                             jnp.where(r8 == 3, route[3:4], 0.0))))
        padded = jnp.concatenate([rows, jnp.zeros((LANES - ROUTE_ROWS, t), F32)], axis=0)
        cols_ref[...] = padded.T
        s1 = slot1.astype(jnp.int32)
        s2 = slot2.astype(jnp.int32)
        for c in range(n_chunks):
            used = c * MOE_CH < s_used

            @pl.when(used)
            def _():
                s_iota = lax.broadcasted_iota(jnp.int32, (MOE_CH, t), 0) + c * MOE_CH
                onehot = jnp.where((s_iota == s1) | (s_iota == s2), 1.0, 0.0).astype(BF16)
                xs_ref[0, c * MOE_CH:(c + 1) * MOE_CH, :] = jnp.dot(
                    onehot, xb, preferred_element_type=F32).astype(BF16)

            @pl.when(jnp.logical_not(used))
            def _():
                xs_ref[0, c * MOE_CH:(c + 1) * MOE_CH, :] = jnp.zeros((MOE_CH, D_MODEL), BF16)


def _moe_expert_kernel(*refs, n_streams):
    sched = refs[:3 * n_streams]
    xs_list = refs[3 * n_streams:4 * n_streams]
    wg_ref, wu_ref, wd_ref = refs[4 * n_streams:4 * n_streams + 3]
    ys_list = refs[4 * n_streams + 3:5 * n_streams + 3]
    wg_scr, wu_scr, wd_scr = refs[5 * n_streams + 3:5 * n_streams + 6]
    bufs = refs[5 * n_streams + 6:]
    e = pl.program_id(0)

    @pl.when(e == 0)
    def _():
        for s in range(n_streams):
            bufs[4 * s][...] = jnp.zeros_like(bufs[4 * s])

    loops = []
    for s in range(n_streams):
        sblk_ref, ebase_ref, ecnt_ref = sched[3 * s:3 * s + 3]
        loops.append(_moe_expert_blocks(sblk_ref, ebase_ref[e], ecnt_ref[e], xs_list[s], ys_list[s],
                                        wg_scr, wu_scr, wd_scr, *bufs[4 * s:4 * s + 4]))
    wg_scr[...] = wg_ref[0, 0].astype(BF16)
    wu_scr[...] = wu_ref[0, 0].astype(BF16)
    wd_scr[...] = wd_ref[0, 0].astype(BF16)
    for run_blocks in loops:
        run_blocks()


def _moe_expert_blocks(sblk_ref, base, cnt, xs_hbm, ys_hbm, wg_scr, wu_scr, wd_scr, xbuf, ybuf, in_sem,
                       out_sem):
    npair = (cnt + MOE_GROUP - 1) // MOE_GROUP

    def rows(k):
        return pl.ds(pl.multiple_of(sblk_ref[base + k] * MOE_BM, MOE_BM), MOE_BM)

    def in_copy(k, slot, half):
        return pltpu.make_async_copy(xs_hbm.at[rows(k), :], xbuf.at[slot, half * MOE_BM:(half + 1) * MOE_BM, :],
                                     in_sem.at[slot, half])

    def out_copy(k, slot, half):
        return pltpu.make_async_copy(ybuf.at[slot, half * MOE_BM:(half + 1) * MOE_BM, :], ys_hbm.at[rows(k), :],
                                     out_sem.at[slot, half])

    def for_pair(p, make, act):
        slot = p % MOE_NBUF
        act(make(MOE_GROUP * p, slot, 0))
        for part in range(1, MOE_GROUP):
            @pl.when(MOE_GROUP * p + part < cnt)
            def _():
                act(make(MOE_GROUP * p + part, slot, part))

    start = lambda c: c.start()
    wait = lambda c: c.wait()

    for i in range(MOE_NBUF - 1):
        @pl.when(i < npair)
        def _():
            for_pair(i, in_copy, start)

    def pair(p, carry):
        slot = p % MOE_NBUF
        for_pair(p, in_copy, wait)
        ahead = p + MOE_NBUF - 1

        @pl.when(ahead < npair)
        def _():
            for_pair(ahead, in_copy, start)

        @pl.when(p >= MOE_NBUF)
        def _():
            for_pair(p - MOE_NBUF, out_copy, wait)

        def run_expert(n_rows):
            xblk = xbuf[slot, 0:n_rows, :]
            hid = _silu(jnp.dot(xblk, wg_scr[...], preferred_element_type=F32)) * jnp.dot(
                xblk, wu_scr[...], preferred_element_type=F32)
            y = jnp.dot(hid.astype(BF16), wd_scr[...], preferred_element_type=F32)
            ybuf[slot, 0:n_rows, :] = y.astype(BF16)

        short = cnt - MOE_GROUP * p <= MOE_GROUP // 2

        @pl.when(short)
        def _():
            run_expert(MOE_GROUP // 2 * MOE_BM)

        @pl.when(jnp.logical_not(short))
        def _():
            run_expert(MOE_GROUP * MOE_BM)

        for_pair(p, out_copy, start)
        return carry

    def run_blocks():
        lax.fori_loop(0, npair, pair, 0)
        for i in range(MOE_NBUF):
            @pl.when(npair > i)
            def _():
                for_pair(npair - 1 - i, out_copy, wait)

    return run_blocks


def _moe_combine_kernel(nblk_ref, x_ref, cols_ref, ys_ref, g_ref, b_ref, o_ref, acc_scr, *, t, n_chunks):
    s_used = nblk_ref[pl.program_id(0)] * MOE_BM
    cols = cols_ref[...]
    s1c = cols[:, 0:1].astype(jnp.int32)
    s2c = cols[:, 1:2].astype(jnp.int32)
    w1c = cols[:, 2:3]
    w2c = cols[:, 3:4]
    for c in range(n_chunks):
        @pl.when(c * MOE_CCH < s_used)
        def _():
            l_iota = lax.broadcasted_iota(jnp.int32, (t, MOE_CCH), 1) + c * MOE_CCH
            weighted = (jnp.where(l_iota == s1c, w1c, 0.0) + jnp.where(l_iota == s2c, w2c, 0.0)).astype(BF16)
            part = jnp.dot(weighted, ys_ref[0, c * MOE_CCH:(c + 1) * MOE_CCH, :],
                           preferred_element_type=F32)
            if c == 0:
                acc_scr[...] = part
            else:
                acc_scr[...] += part
    o_ref[...] = _layer_norm(ALPHA * x_ref[...] + acc_scr[...], g_ref[...], b_ref[...], LN_EPS)


def _moe_schedule(counts, tiles_per, maxb, bpt):
    cnt = counts[:, :, 0].astype(jnp.int32)
    cnt = cnt.reshape(-1, tiles_per, N_EXPERTS).sum(axis=1)
    n_tiles = cnt.shape[0]
    nblk = (cnt + MOE_BM - 1) // MOE_BM
    end = jnp.cumsum(nblk, axis=-1)
    start = end - nblk
    total = end[:, -1]
    padoff = jnp.broadcast_to((start * MOE_BM).astype(F32)[:, :, None], (n_tiles, N_EXPERTS, LANES))
    per_e = jnp.sum(nblk, axis=0)
    e_end = jnp.cumsum(per_e)
    e_base = e_end - per_e
    e_ids = jnp.arange(N_EXPERTS, dtype=jnp.int32)
    s_idx = jnp.arange(n_tiles * maxb, dtype=jnp.int32)
    last_e = jnp.max(jnp.where(per_e > 0, e_ids, 0))
    se = jnp.minimum(jnp.sum((s_idx[:, None] >= e_end[None, :]).astype(jnp.int32), axis=-1), last_e)
    oh_e = se[:, None] == e_ids[None, :]
    r = s_idx - jnp.sum(jnp.where(oh_e, e_base[None, :], 0), axis=-1)
    of_e = lambda a: jnp.sum(jnp.where(oh_e[:, None, :], a[None, :, :], 0), axis=-1)
    incl_s = of_e(jnp.cumsum(nblk, axis=0))
    tile = jnp.minimum(jnp.sum((incl_s <= r[:, None]).astype(jnp.int32), axis=-1), n_tiles - 1)
    oh_t = tile[:, None] == jnp.arange(n_tiles, dtype=jnp.int32)[None, :]
    of_t = lambda a: jnp.sum(jnp.where(oh_t, a, 0), axis=-1)
    j = of_t(of_e(start)) + r - (of_t(incl_s) - of_t(of_e(nblk)))
    sblk = jnp.where(s_idx < e_end[-1], tile * bpt + j, n_tiles * bpt)
    return total, padoff, sblk, e_base, per_e


def _moe_sort_call(x, route, counts, tm):
    n = x.shape[0]
    t = 1024 if n % 1024 == 0 else n
    n_tiles = n // t
    maxb = _moe_max_blocks(t)
    s_alloc = -(-maxb * MOE_BM // MOE_CCH) * MOE_CCH
    n_chunks = s_alloc // MOE_CH
    bpt = s_alloc // MOE_BM
    nblk, padoff, sblk, e_base, e_cnt = _moe_schedule(counts, t // tm, maxb, bpt)
    tri = jnp.triu(jnp.ones((t, t), BF16), k=1)
    last = n_tiles - 1
    xs, cols = pl.pallas_call(
        functools.partial(_moe_sort_kernel, t=t, n_tiles=n_tiles, n_chunks=n_chunks),
        grid_spec=pltpu.PrefetchScalarGridSpec(
            num_scalar_prefetch=1,
            grid=(n_tiles + 1,),
            in_specs=[pl.BlockSpec((t, D_MODEL), lambda i, nb: (jnp.minimum(i, last), 0)),
                      pl.BlockSpec((ROUTE_ROWS, t), lambda i, nb: (0, jnp.minimum(i, last))),
                      pl.BlockSpec((1, N_EXPERTS, LANES), lambda i, nb: (jnp.minimum(i, last), 0, 0)),
                      pl.BlockSpec((t, t), lambda i, nb: (0, 0))],
            out_specs=[pl.BlockSpec((1, s_alloc, D_MODEL), lambda i, nb: (i, 0, 0)),
                       pl.BlockSpec((t, LANES), lambda i, nb: (jnp.minimum(i, last), 0))]),
        out_shape=[jax.ShapeDtypeStruct((n_tiles + 1, s_alloc, D_MODEL), BF16),
                   jax.ShapeDtypeStruct((n, LANES), F32)],
        compiler_params=pltpu.CompilerParams(dimension_semantics=("arbitrary",),
                                             vmem_limit_bytes=VMEM_LIMIT),
        name="moe_sort",
    )(nblk, x, route, padoff, tri)
    return dict(xs=xs.reshape((n_tiles + 1) * s_alloc, D_MODEL), cols=cols, nblk=nblk, sblk=sblk,
                e_base=e_base, e_cnt=e_cnt, t=t, n_tiles=n_tiles, s_alloc=s_alloc)


def _moe_expert_call(sorted_streams, wg, wu, wd, layer):
    ns = len(sorted_streams)
    wspec = lambda r, c: pl.BlockSpec((1, 1, r, c), lambda e, *_: (layer, e, 0, 0))
    sched = [s[k] for s in sorted_streams for k in ("sblk", "e_base", "e_cnt")]
    xs = [s["xs"] for s in sorted_streams]
    return pl.pallas_call(
        functools.partial(_moe_expert_kernel, n_streams=ns),
        grid_spec=pltpu.PrefetchScalarGridSpec(
            num_scalar_prefetch=3 * ns,
            grid=(N_EXPERTS,),
            in_specs=[pl.BlockSpec(memory_space=pl.ANY)] * ns
            + [wspec(D_MODEL, D_EXP), wspec(D_MODEL, D_EXP), wspec(D_EXP, D_MODEL)],
            out_specs=[pl.BlockSpec(memory_space=pl.ANY)] * ns,
            scratch_shapes=[pltpu.VMEM((D_MODEL, D_EXP), BF16), pltpu.VMEM((D_MODEL, D_EXP), BF16),
                            pltpu.VMEM((D_EXP, D_MODEL), BF16)]
            + [pltpu.VMEM((MOE_NBUF, MOE_GROUP * MOE_BM, D_MODEL), BF16),
               pltpu.VMEM((MOE_NBUF, MOE_GROUP * MOE_BM, D_MODEL), BF16),
               pltpu.SemaphoreType.DMA((MOE_NBUF, MOE_GROUP)),
               pltpu.SemaphoreType.DMA((MOE_NBUF, MOE_GROUP))] * ns),
        out_shape=[jax.ShapeDtypeStruct(x.shape, BF16) for x in xs],
        input_output_aliases={3 * ns + i: i for i in range(ns)},
        compiler_params=pltpu.CompilerParams(dimension_semantics=("arbitrary",),
                                             vmem_limit_bytes=VMEM_LIMIT),
        name="moe_experts",
    )(*sched, *xs, wg, wu, wd)


def _moe_combine_call(x, srt, ys, g, b):
    n = x.shape[0]
    t, n_tiles, s_alloc = srt["t"], srt["n_tiles"], srt["s_alloc"]
    nblk, cols = srt["nblk"], srt["cols"]
    vec = pl.BlockSpec((1, D_MODEL), lambda i, nb: (0, 0))
    return pl.pallas_call(
        functools.partial(_moe_combine_kernel, t=t, n_chunks=s_alloc // MOE_CCH),
        grid_spec=pltpu.PrefetchScalarGridSpec(
            num_scalar_prefetch=1,
            grid=(n_tiles,),
            in_specs=[pl.BlockSpec((t, D_MODEL), lambda i, nb: (i, 0)),
                      pl.BlockSpec((t, LANES), lambda i, nb: (i, 0)),
                      pl.BlockSpec((1, s_alloc, D_MODEL), lambda i, nb: (i, 0, 0)), vec, vec],
            out_specs=pl.BlockSpec((t, D_MODEL), lambda i, nb: (i, 0)),
            scratch_shapes=[pltpu.VMEM((t, D_MODEL), F32)]),
        out_shape=jax.ShapeDtypeStruct((n, D_MODEL), F32),
        compiler_params=pltpu.CompilerParams(dimension_semantics=("parallel",),
                                             vmem_limit_bytes=VMEM_LIMIT),
        name="moe_combine",
    )(nblk, x, cols, ys.reshape(n_tiles + 1, s_alloc, D_MODEL), g, b)


def _row_tile(n):
    return 512 if n % 512 == 0 else n


def _seqs_per_step(batch, want):
    return want if batch % want == 0 else 1


def kernel(x_prompt, x_sample, state_mlstm_C, state_mlstm_n, state_mlstm_m, state_mlstm_conv, state_rwkv_S, state_rwkv_shift, ln0_g, ln0_b, w_in, conv_w, conv_b, b_i, b_f, gn_a_g, mu_shift, w0, w2, a0, a2, g2, k_k, k_a, r_k, lnx_g, lnx_b, w_out, ln1_g, ln1_b, w_router, b_router, we_gate, we_up, we_down, ln2_g, ln2_b):
    d_a_in = 4 * D_A + 2 * H_A
    zpad = jnp.zeros((DEPTH, D_MODEL, LANES - H_A), F32)
    w_cat = jnp.concatenate(
        [w_in[:, :, 0:4 * D_A], w_in[:, :, d_a_in:], w_in[:, :, 4 * D_A:4 * D_A + H_A], zpad,
         w_in[:, :, 4 * D_A + H_A:d_a_in], zpad], axis=-1).astype(BF16)
    bif = jnp.zeros((DEPTH, 1, W_IF_COLS), F32)
    bif = bif.at[:, 0, 0:H_A].set(b_i).at[:, 0, LANES:LANES + H_A].set(b_f)
    wlora = jnp.zeros((DEPTH, LANES, 2 * D_B), F32)
    wlora = wlora.at[:, 0:R_DECAY, 0:D_B].set(w2).at[:, R_DECAY:, D_B:].set(a2).astype(BF16)
    w0a0 = jnp.concatenate([w0, a0], axis=-1)[:, None, :]
    g2b = g2.astype(BF16)
    wob = w_out.astype(BF16)
    wr_t = w_router.T
    br = b_router[:, None]
    r1 = lambda v: v[None, :]

    def new_stream(x3, states):
        nb, seq, _ = x3.shape
        return dict(nb=nb, seq=seq, n=nb * seq, tm=_row_tile(nb * seq), states=states,
                    x=x3.reshape(nb * seq, D_MODEL), outs=[],
                    c_stack=jnp.zeros((DEPTH, nb, H_A, DH_A, DH_A), F32),
                    s_stack=jnp.zeros((DEPTH, nb, H_B, DH_B, DH_B), F32))

    def mixers_and_router(st, l):
        nb, seq, n, tm, states, x = (st[k] for k in ("nb", "seq", "n", "tm", "states", "x"))
        c_stack, s_stack = st["c_stack"], st["s_stack"]
        rwkv_nbb = _seqs_per_step(nb, 4 if seq >= CHUNK else 8)
        mlstm_nbb = _seqs_per_step(nb, 2 if seq >= CHUNK else 8)
        if states is None:
            c_in = jnp.zeros((1, nb, H_A, DH_A, DH_A), F32)
            s_in = jnp.zeros((1, nb, H_B, DH_B, DH_B), F32)
            layer_in = 0
            n0 = jnp.zeros((nb, H_A, DH_A), F32)
            m0 = jnp.zeros((nb, H_A), F32)
            conv0 = jnp.zeros((nb, CONV_W - 1, 2 * D_A), F32)
            shift0 = jnp.zeros((nb, D_B_IN), F32)
        else:
            c_in, s_in, layer_in = states[0], states[4], l
            n0, m0, conv0, shift0 = (states[k][l] for k in (1, 2, 3, 5))
        m0p = jnp.zeros((nb, 1, LANES), F32).at[:, 0, 0:H_A].set(m0)
        if l == 0:
            pa, pb, pif, x = _inproj_call(x, w_cat, tm, l, norm=(r1(ln0_g), r1(ln0_b)))
        else:
            pa, pb, pif = _inproj_call(x, w_cat, tm, l)
        ha, c_stack, n_new, m_new = _mlstm_call(
            pa.reshape(nb, seq, W_A_COLS), pif.reshape(nb, seq, W_IF_COLS), conv0, c_in, n0, m0p,
            conv_w[l], r1(conv_b[l]), bif[l], r1(gn_a_g[l]), c_stack, batch=nb, seq=seq,
            nbb=mlstm_nbb, layer_in=layer_in, layer=l)
        ha = ha.reshape(n, D_A)
        hb, s_stack = _rwkv_call(
            pb.reshape(nb, seq, D_B_IN), shift0[:, None, :], s_in, r1(mu_shift[l]), w0a0[l], wlora[l],
            g2b[l], r1(k_k[l]), r1(k_a[l]), r1(r_k[l].reshape(D_B)), r1(lnx_g[l]), r1(lnx_b[l]),
            s_stack, batch=nb, seq=seq, nbb=rwkv_nbb, layer_in=layer_in, layer=l)
        hb = hb.reshape(n, D_B)
        x1, route, counts = _outproj_call(x, ha, hb, wob, r1(ln1_g[l]), r1(ln1_b[l]), wr_t, br, tm, l)
        pa3 = pa.reshape(nb, seq, W_A_COLS)
        full = jnp.concatenate([conv0, pa3[:, :, 0:2 * D_A]], axis=1) if seq < CONV_W - 1 else pa3[:, :, 0:2 * D_A]
        conv_new = full[:, -(CONV_W - 1):, :]
        shift_new = pb.reshape(nb, seq, D_B_IN)[:, -1, :]
        st["outs"].append((n_new, m_new[:, 0, 0:H_A], conv_new, shift_new))
        st["c_stack"], st["s_stack"] = c_stack, s_stack
        return x1, route, counts

    streams = [new_stream(x_prompt, None),
               new_stream(x_sample, (state_mlstm_C, state_mlstm_n, state_mlstm_m, state_mlstm_conv,
                                     state_rwkv_S, state_rwkv_shift))]
    for l in range(DEPTH):
        routed = [mixers_and_router(st, l) for st in streams]
        srt = [_moe_sort_call(x1, route, counts, st["tm"]) for (x1, route, counts), st in zip(routed, streams)]
        ys = _moe_expert_call(srt, we_gate, we_up, we_down, l)
        for st, (x1, _, _), s, y in zip(streams, routed, srt, ys):
            st["x"] = _moe_combine_call(x1, s, y, r1(ln2_g[l]), r1(ln2_b[l]))

    def finish(st):
        n_all, m_all, conv_all, shift_all = (jnp.stack(s) for s in zip(*st["outs"]))
        return (st["x"].reshape(st["nb"], st["seq"], D_MODEL),
                (st["c_stack"], n_all, m_all, conv_all, st["s_stack"], shift_all))

    y_prompt, (p_c, p_n, p_m, p_conv, p_s, p_shift) = finish(streams[0])
    y_sample, (s_c, s_n, s_m, s_conv, s_s, s_shift) = finish(streams[1])
    return (y_prompt, y_sample, p_c, p_n, p_m, p_conv, p_s, p_shift, s_c, s_n, s_m, s_conv, s_s, s_shift)
```

```python
import functools
import math

import jax
import jax.numpy as jnp
from jax import lax
from jax.experimental import pallas as pl
from jax.experimental.pallas import tpu as pltpu

F32 = jnp.float32
BF16 = jnp.bfloat16

D_MODEL = 1024
DEPTH = 4
D_A = 512
D_B = 512
DH_A = 128
H_A = 4
DH_B = 64
H_B = 8
N_PAIRS = H_B // 2
CONV_W = 4
CHUNK = 64
R_DECAY = 64
R_AAA = 64
R_GATE = 128
D_B_IN = 3 * D_B + R_DECAY + R_AAA + R_GATE
N_EXPERTS = 16
N_GROUPS = 4
EXPERTS_PER_GROUP = 4
D_EXP = 512
ALPHA = (2 * DEPTH) ** 0.25
LN_EPS = 1e-5
GN_EPS_A = 1e-6
GN_EPS_B = 64e-5

LANES = 128
SOLVE_BLOCK = 16
VMEM_LIMIT = 48 * 1024 * 1024

NN = (((1,), (0,)), ((), ()))
NT = (((1,), (1,)), ((), ()))
TN = (((0,), (0,)), ((), ()))


def _dot(a, b, dims=NN):
    return lax.dot_general(a.astype(BF16), b.astype(BF16), dims, preferred_element_type=F32)


def _split3(x):
    hi = x.astype(BF16)
    r1 = x - hi.astype(F32)
    mid = r1.astype(BF16)
    lo = (r1 - mid.astype(F32)).astype(BF16)
    return hi, mid, lo


def _dot_exact_lhs(a, x, dims=NN):
    a = a.astype(BF16)
    hi, mid, lo = _split3(x)
    d = lambda p: lax.dot_general(a, p, dims, preferred_element_type=F32)
    return d(hi) + d(mid) + d(lo)


def _dot_exact_rhs(x, b, dims=NN):
    b = b.astype(BF16)
    hi, mid, lo = _split3(x)
    d = lambda p: lax.dot_general(p, b, dims, preferred_element_type=F32)
    return d(hi) + d(mid) + d(lo)


def _layer_norm(x, g, b, eps):
    mu = jnp.mean(x, axis=-1, keepdims=True)
    xc = x - mu
    var = jnp.mean(xc * xc, axis=-1, keepdims=True)
    return xc * lax.rsqrt(var + eps) * g + b


def _sigmoid(x):
    return 1.0 / (1.0 + jnp.exp(-x))


def _log_sigmoid(x):
    return jnp.minimum(x, 0.0) - jnp.log(1.0 + jnp.exp(-jnp.abs(x)))


def _silu(x):
    return x * _sigmoid(x)


W_A_COLS = 4 * D_A
W_IF_COLS = 2 * LANES
W_IN_COLS = W_A_COLS + D_B_IN + W_IF_COLS


def _inproj_kernel(x_ref, w_ref, *rest, input_norm):
    if input_norm:
        g_ref, b_ref, pa_ref, pb_ref, pif_ref, xn_ref = rest
        x = _layer_norm(x_ref[...], g_ref[...], b_ref[...], LN_EPS)
        xn_ref[...] = x
    else:
        pa_ref, pb_ref, pif_ref = rest
        x = x_ref[...]
    xb = x.astype(BF16)
    pa_ref[...] = jnp.dot(xb, w_ref[0, :, 0:W_A_COLS], preferred_element_type=F32)
    pb_ref[...] = jnp.dot(xb, w_ref[0, :, W_A_COLS:W_A_COLS + D_B_IN], preferred_element_type=F32)
    pif_ref[...] = jnp.dot(xb, w_ref[0, :, W_A_COLS + D_B_IN:W_IN_COLS], preferred_element_type=F32)


def _inproj_call(x, w, tm, layer, norm=None):
    n = x.shape[0]
    row = lambda c: pl.BlockSpec((tm, c), lambda i: (i, 0))
    vec = pl.BlockSpec((1, D_MODEL), lambda i: (0, 0))
    in_specs = [row(D_MODEL), pl.BlockSpec((1, D_MODEL, W_IN_COLS), lambda i: (layer, 0, 0))]
    out_specs = [row(W_A_COLS), row(D_B_IN), row(W_IF_COLS)]
    out_shape = [jax.ShapeDtypeStruct((n, W_A_COLS), F32), jax.ShapeDtypeStruct((n, D_B_IN), F32),
                 jax.ShapeDtypeStruct((n, W_IF_COLS), F32)]
    args = [x, w]
    if norm is not None:
        in_specs += [vec, vec]
        out_specs.append(row(D_MODEL))
        out_shape.append(jax.ShapeDtypeStruct((n, D_MODEL), F32))
        args += list(norm)
    return pl.pallas_call(
        functools.partial(_inproj_kernel, input_norm=norm is not None),
        grid=(n // tm,),
        in_specs=in_specs,
        out_specs=out_specs,
        out_shape=out_shape,
        compiler_params=pltpu.CompilerParams(dimension_semantics=("parallel",),
                                             vmem_limit_bytes=VMEM_LIMIT),
        name="inproj",
    )(*args)


CONV_PAD = 8


def _mlstm_kernel(pa_ref, pif_ref, conv0_ref, c0_ref, n0_ref, m0_ref, convw_ref, convb_ref,
                  bif_ref, gn_ref, *rest, c, nc, nbb, chained):
    h_ref, c_out_ref, n_out_ref, m_out_ref, qk_scr, c_scr, n_scr, m_scr = rest[1:] if chained else rest
    ci = pl.program_id(1)
    rows = nbb * c
    cshift = int(math.log2(c))
    chains = [(nb, h) for nb in range(nbb) for h in range(H_A)]
    prev0 = CONV_PAD - (CONV_W - 1)

    @pl.when(ci == 0)
    def _():
        c_scr[...] = c0_ref[0]
        n_scr[...] = n0_ref[...]
        m_scr[...] = m0_ref[...]
        for nb in range(nbb):
            qk_scr[nb, prev0:CONV_PAD, :] = conv0_ref[nb]

    cat0 = lambda xs: xs[0] if len(xs) == 1 else jnp.concatenate(xs, axis=0)
    acc_l = []
    for nb in range(nbb):
        u = pa_ref[nb, :, 0:2 * D_A]
        qk_scr[nb, CONV_PAD:CONV_PAD + c, :] = u
        acc = convb_ref[...]
        for j in range(CONV_W):
            acc = acc + qk_scr[nb, prev0 + j:prev0 + j + c, :] * convw_ref[j:j + 1, :]
        qk_scr[nb, prev0:CONV_PAD, :] = u[c - (CONV_W - 1):c, :]
        acc_l.append(acc)
    qk = _silu(cat0(acc_l))

    gates = cat0([pif_ref[nb] for nb in range(nbb)]) + bif_ref[...]
    li_all = gates[:, 0:LANES]
    lf_all = _log_sigmoid(gates[:, LANES:2 * LANES])
    row = lax.broadcasted_iota(jnp.int32, (rows, rows), 0)
    col = lax.broadcasted_iota(jnp.int32, (rows, rows), 1)
    tril = jnp.where((col <= row) & ((row >> cshift) == (col >> cshift)), 1.0, 0.0)
    b_all = _dot_exact_lhs(tril, lf_all)
    z_all = li_all - b_all
    crow = lax.broadcasted_iota(jnp.int32, (c, c), 0)
    ccol = lax.broadcasted_iota(jnp.int32, (c, c), 1)
    causal = ccol <= crow
    hrow = lax.broadcasted_iota(jnp.int32, (H_A * c, LANES), 0)
    hlane = lax.broadcasted_iota(jnp.int32, (H_A * c, LANES), 1)
    head_pick = jnp.where((hrow >> cshift) == hlane, 1.0, 0.0)
    z_rows = [_dot_exact_lhs(head_pick, z_all[nb * c:(nb + 1) * c], NT) for nb in range(nbb)]
    lane1 = lax.broadcasted_iota(jnp.int32, (1, LANES), 1)

    rs = lambda nb: slice(nb * c, (nb + 1) * c)
    q_l = [qk[rs(nb), h * DH_A:(h + 1) * DH_A] for nb, h in chains]
    k_l = [qk[rs(nb), D_A + h * DH_A:D_A + (h + 1) * DH_A] * (DH_A ** -0.5) for nb, h in chains]
    v_l = [pa_ref[nb, :, 2 * D_A + h * DH_A:2 * D_A + (h + 1) * DH_A] for nb, h in chains]
    c_l = [c_scr[nb, h] for nb, h in chains]
    n_l = [n_scr[nb, h:h + 1, :] for nb, h in chains]
    b_col = [b_all[rs(nb), h:h + 1] for nb, h in chains]
    li_col = [li_all[rs(nb), h:h + 1] for nb, h in chains]
    m_prev = [m_scr[nb][:, h:h + 1] for nb, h in chains]
    dmat = [jnp.where(causal, bc + z_rows[nb][h * c:(h + 1) * c], -jnp.inf)
            for bc, (nb, h) in zip(b_col, chains)]
    m_inter = [bc + mp for bc, mp in zip(b_col, m_prev)]
    m_t = [jnp.maximum(mi, jnp.max(d, axis=-1, keepdims=True)) for mi, d in zip(m_inter, dmat)]
    qk_dot = [_dot(q, k, NT) for q, k in zip(q_l, k_l)]
    qc = [_dot(q, cm) for q, cm in zip(q_l, c_l)]
    s_l = [x * jnp.exp(d - mt) for x, d, mt in zip(qk_dot, dmat, m_t)]
    sv = [_dot(s, v) for s, v in zip(s_l, v_l)]
    b_last = [bc[c - 1:c, :] for bc in b_col]
    g_s = [bl - bc + li for bl, bc, li in zip(b_last, b_col, li_col)]
    m_new = [jnp.maximum(bl + mp, jnp.max(gs, axis=0, keepdims=True))
             for bl, mp, gs in zip(b_last, m_prev, g_s)]
    wk = [jnp.exp(gs - mn) * k for gs, mn, k in zip(g_s, m_new, k_l)]
    w_old = [jnp.exp(bl + mp - mn) for bl, mp, mn in zip(b_last, m_prev, m_new)]
    kv = [_dot(w, v, TN) for w, v in zip(wk, v_l)]
    for i, (nb, h) in enumerate(chains):
        c_scr[nb, h] = w_old[i] * c_l[i] + kv[i]
        n_scr[nb, h:h + 1, :] = w_old[i] * n_l[i] + jnp.sum(wk[i], axis=0, keepdims=True)
    for nb in range(nbb):
        m_row = m_scr[nb]
        for h in range(H_A):
            m_row = jnp.where(lane1 == h, m_new[nb * H_A + h], m_row)
        m_scr[nb] = m_row
    w_inter = [jnp.exp(mi - mt) for mi, mt in zip(m_inter, m_t)]
    qn = [jnp.sum(q * nv, axis=-1, keepdims=True) for q, nv in zip(q_l, n_l)]
    s_sum = [jnp.sum(s, axis=-1, keepdims=True) for s in s_l]
    den = [w * a + b for w, a, b in zip(w_inter, qn, s_sum)]
    hh = [(w * a + b) / jnp.maximum(jnp.abs(d), jnp.exp(-mt))
          for w, a, b, d, mt in zip(w_inter, qc, sv, den, m_t)]
    mu = [jnp.mean(x, axis=-1, keepdims=True) for x in hh]
    hc = [x - m for x, m in zip(hh, mu)]
    var = [jnp.mean(x * x, axis=-1, keepdims=True) for x in hc]
    for i, (nb, h) in enumerate(chains):
        sl = slice(h * DH_A, (h + 1) * DH_A)
        hn = hc[i] * lax.rsqrt(var[i] + GN_EPS_A)
        o_pre = pa_ref[nb, :, 3 * D_A + h * DH_A:3 * D_A + (h + 1) * DH_A]
        h_ref[nb, :, sl] = hn * gn_ref[:, sl] * _sigmoid(o_pre)

    @pl.when(ci == nc - 1)
    def _():
        c_out_ref[0] = c_scr[...]
        n_out_ref[...] = n_scr[...]
        m_out_ref[...] = m_scr[...]


def _mlstm_call(pa, pif, conv0, c_in, n0, m0, convw, convb, bif, gn, c_stack, *, batch, seq, nbb,
                layer_in, layer):
    c = math.gcd(seq, CHUNK)
    nc = seq // c
    tok = lambda w: pl.BlockSpec((nbb, c, w), lambda b, i: (b, i, 0))
    vec = lambda r, w: pl.BlockSpec((r, w), lambda b, i: (0, 0))
    st5 = lambda l: pl.BlockSpec((1, nbb, H_A, DH_A, DH_A), lambda b, i: (l, b, 0, 0, 0))
    st3 = lambda r, w: pl.BlockSpec((nbb, r, w), lambda b, i: (b, 0, 0))
    chained = c_stack is not None
    in_specs = [tok(W_A_COLS), tok(W_IF_COLS), st3(CONV_W - 1, 2 * D_A), st5(layer_in), st3(H_A, DH_A),
                st3(1, LANES), vec(CONV_W, 2 * D_A), vec(1, 2 * D_A), vec(1, W_IF_COLS), vec(1, D_A)]
    args = [pa, pif, conv0, c_in, n0, m0, convw, convb, bif, gn]
    if chained:
        in_specs.append(pl.BlockSpec(memory_space=pl.ANY))
        args.append(c_stack)
    return pl.pallas_call(
        functools.partial(_mlstm_kernel, c=c, nc=nc, nbb=nbb, chained=chained),
        grid=(batch // nbb, nc),
        in_specs=in_specs,
        out_specs=[tok(D_A), st5(layer), st3(H_A, DH_A), st3(1, LANES)],
        out_shape=[jax.ShapeDtypeStruct((batch, seq, D_A), F32),
                   jax.ShapeDtypeStruct((DEPTH, batch, H_A, DH_A, DH_A), F32),
                   jax.ShapeDtypeStruct((batch, H_A, DH_A), F32),
                   jax.ShapeDtypeStruct((batch, 1, LANES), F32)],
        scratch_shapes=[pltpu.VMEM((nbb, CONV_PAD + c, 2 * D_A), F32),
                        pltpu.VMEM((nbb, H_A, DH_A, DH_A), F32),
                        pltpu.VMEM((nbb, H_A, DH_A), F32),
                        pltpu.VMEM((nbb, 1, LANES), F32)],
        input_output_aliases={len(args) - 1: 1} if chained else {},
        compiler_params=pltpu.CompilerParams(dimension_semantics=("parallel", "arbitrary"),
                                             vmem_limit_bytes=VMEM_LIMIT),
        name="mlstm",
    )(*args)


SHIFT_PAD = 8


def _neumann_inverses(ms, size, t):
    row = lax.broadcasted_iota(jnp.int32, (size, size), 0)
    col = lax.broadcasted_iota(jnp.int32, (size, size), 1)
    eye = jnp.where(row == col, 1.0, 0.0)
    blk = min(SOLVE_BLOCK, t)
    shift = int(math.log2(blk))
    same = (row >> shift) == (col >> shift)
    ds = [jnp.where(same, m, 0.0) for m in ms]
    ps = [eye + d for d in ds]
    xs = ds
    for _ in range(shift - 1):
        xs = [_dot(x, x) for x in xs]
        ps = [p + _dot(p, x) for p, x in zip(ps, xs)]
    nblk = t // blk
    if nblk == 1:
        return ps, None
    ns = [_dot(p, m - d) for p, m, d in zip(ps, ms, ds)]
    qs = [eye + n for n in ns]
    ys = ns
    for _ in range(int(math.log2(nblk)) - 1):
        ys = [_dot(y, y) for y in ys]
        qs = [q + _dot(q, y) for q, y in zip(qs, ys)]
    return ps, qs


def _rwkv_kernel(pb_ref, shift0_ref, s0_ref, mu_ref, w0a0_ref, wlora_ref, g2_ref, kk_ref, ka_ref,
                 rk_ref, lng_ref, lnb_ref, *rest, t, nc, nbb, chained):
    h_ref, s_out_ref, sh_scr, s_scr = rest[1:] if chained else rest
    ci = pl.program_id(1)
    rows = nbb * t
    t2 = 2 * t
    groups = [(nb, pr) for nb in range(nbb) for pr in range(N_PAIRS)]

    brow = lax.broadcasted_iota(jnp.int32, (LANES, LANES), 0)
    bcol = lax.broadcasted_iota(jnp.int32, (LANES, LANES), 1)
    blockdiag = (brow >= DH_B) == (bcol >= DH_B)
    bd2 = jnp.where(blockdiag, 1.0, 0.0).astype(BF16)

    @pl.when(ci == 0)
    def _():
        er = lax.broadcasted_iota(jnp.int32, (DH_B, LANES), 0)
        ec = lax.broadcasted_iota(jnp.int32, (DH_B, LANES), 1)
        dup_cols = jnp.where((ec & (DH_B - 1)) == er, 1.0, 0.0)
        for nb in range(nbb):
            sh_scr[nb, SHIFT_PAD - 1:SHIFT_PAD, :] = shift0_ref[nb]
            for pr in range(N_PAIRS):
                x = s0_ref[0, nb, 2 * pr:2 * pr + 2].reshape(LANES, DH_B)
                s_scr[nb * N_PAIRS + pr] = jnp.where(blockdiag, _dot_exact_rhs(x, dup_cols), 0.0)

    p_l, prev_l = [], []
    for nb in range(nbb):
        p_nb = pb_ref[nb]
        sh_scr[nb, SHIFT_PAD:SHIFT_PAD + t, :] = p_nb
        prev_l.append(sh_scr[nb, SHIFT_PAD - 1:SHIFT_PAD - 1 + t, :])
        sh_scr[nb, SHIFT_PAD - 1:SHIFT_PAD, :] = p_nb[t - 1:t, :]
        p_l.append(p_nb)
    cat0 = lambda xs: xs[0] if len(xs) == 1 else jnp.concatenate(xs, axis=0)
    p = cat0(p_l)
    prev = cat0(prev_l)
    pb = p + (prev - p) * mu_ref[...]

    r = pb[:, 0:D_B]
    kr = pb[:, D_B:2 * D_B]
    vr = pb[:, 2 * D_B:3 * D_B]
    lora_in = pb[:, 3 * D_B:3 * D_B + LANES]
    gl = pb[:, 3 * D_B + LANES:3 * D_B + 2 * LANES]
    lane = lax.broadcasted_iota(jnp.int32, (rows, LANES), 1)
    lora_act = jnp.where(lane < R_DECAY, jnp.tanh(lora_in), lora_in)
    z = _dot(lora_act, wlora_ref[...]) + w0a0_ref[...]
    ld = -jnp.exp(_log_sigmoid(z[:, 0:D_B]) - 0.5)
    a = _sigmoid(z[:, D_B:2 * D_B])
    g = _dot(_sigmoid(gl), g2_ref[...])

    def seg_sum(x):
        xr = jnp.concatenate([x[:, q * LANES:(q + 1) * LANES] for q in range(N_PAIRS)], axis=0)
        hi = xr.astype(BF16)
        lo = (xr - hi.astype(F32)).astype(BF16)
        s = (jnp.dot(hi, bd2, preferred_element_type=F32) + jnp.dot(lo, bd2, preferred_element_type=F32))
        return jnp.concatenate([s[q * rows:(q + 1) * rows] for q in range(N_PAIRS)], axis=1)

    kk_raw = kr * kk_ref[...]
    kk = kk_raw / jnp.maximum(jnp.sqrt(seg_sum(kk_raw * kk_raw)), 1e-12)
    k2 = kr * (1.0 + (a - 1.0) * ka_ref[...])
    bonus = seg_sum(r * k2 * rk_ref[...]) * vr

    row = lax.broadcasted_iota(jnp.int32, (rows, rows), 0)
    col = lax.broadcasted_iota(jnp.int32, (rows, rows), 1)
    tshift = int(math.log2(t))
    tril = jnp.where((col <= row) & ((row >> tshift) == (col >> tshift)), 1.0, 0.0)
    lc = _dot_exact_lhs(tril, ld)
    lends = [lc[(nb + 1) * t - 1:(nb + 1) * t, :] for nb in range(nbb)]
    lend_rows = cat0([jnp.broadcast_to(le, (t, D_B)) for le in lends])
    e_nc = jnp.exp(-lc)
    e_end = jnp.exp(lend_rows - lc)
    b_raw = kk * a
    at = -kk * jnp.exp(lc - ld)
    rt = r * jnp.exp(lc)
    kt = k2 * e_nc
    bt = b_raw * e_nc
    kend = k2 * e_end
    bend = b_raw * e_end

    srow = lax.broadcasted_iota(jnp.int32, (t2, t2), 0)
    scol = lax.broadcasted_iota(jnp.int32, (t2, t2), 1)
    strict = scol < srow
    incl = scol <= srow
    lane2 = lax.broadcasted_iota(jnp.int32, (t2, LANES), 1)
    row2 = lax.broadcasted_iota(jnp.int32, (t2, LANES), 0)
    head_sel = (lane2 >= DH_B) == (row2 >= t)
    first_head = lax.broadcasted_iota(jnp.int32, (t, LANES), 1) < DH_B

    def blk(x, nb, pr):
        return x[nb * t:(nb + 1) * t, pr * LANES:(pr + 1) * LANES]

    def stack(x):
        return jnp.where(head_sel, jnp.concatenate([x, x], axis=0), 0.0)

    lhs = [jnp.concatenate([stack(blk(at, *gp)), stack(blk(rt, *gp))], axis=0) for gp in groups]
    rk_s = [stack(blk(kt, *gp)) for gp in groups]
    rb_s = [stack(blk(bt, *gp)) for gp in groups]
    v_p = [blk(vr, *gp) for gp in groups]
    v_s = [jnp.concatenate([v, v], axis=0) for v in v_p]
    s_p = [s_scr[i] for i in range(len(groups))]
    gk = [_dot(l, x, NT) for l, x in zip(lhs, rk_s)]
    gb = [_dot(l, x, NT) for l, x in zip(lhs, rb_s)]
    xs = [_dot(l, s, NT) for l, s in zip(lhs, s_p)]
    m_ab = [jnp.where(strict, x[0:t2], 0.0) for x in gb]
    ps, qs = _neumann_inverses(m_ab, t2, t)
    rhs = [x[0:t2] + _dot(jnp.where(strict, y[0:t2], 0.0), v) for x, y, v in zip(xs, gk, v_s)]
    sa_s = [_dot(pm, x) for pm, x in zip(ps, rhs)]
    if qs is not None:
        sa_s = [_dot(qm, x) for qm, x in zip(qs, sa_s)]
    o_s = [x[t2:] + _dot(jnp.where(incl, y[t2:], 0.0), v) + _dot(jnp.where(incl, w[t2:], 0.0), sa)
           for x, y, w, v, sa in zip(xs, gk, gb, v_s, sa_s)]
    o_p = [jnp.where(first_head, o[0:t], o[t:t2]) for o in o_s]
    sa_p = [jnp.where(first_head, sa[0:t], sa[t:t2]) for sa in sa_s]
    upd = [_dot(jnp.concatenate([v, sa], axis=0),
                jnp.concatenate([blk(kend, *gp), blk(bend, *gp)], axis=0), TN)
           for v, sa, gp in zip(v_p, sa_p, groups)]
    for i, (nb, pr) in enumerate(groups):
        dec = jnp.exp(lends[nb][:, pr * LANES:(pr + 1) * LANES])
        s_scr[i] = s_p[i] * dec + jnp.where(blockdiag, upd[i], 0.0)

    o = cat0([jnp.concatenate(o_p[nb * N_PAIRS:(nb + 1) * N_PAIRS], axis=1) for nb in range(nbb)])
    mean = seg_sum(o) * (1.0 / DH_B)
    oc = o - mean
    var = seg_sum(oc * oc) * (1.0 / DH_B)
    out = oc * lax.rsqrt(var + GN_EPS_B) * lng_ref[...] + lnb_ref[...]
    res = (out + bonus) * g
    for nb in range(nbb):
        h_ref[nb] = res[nb * t:(nb + 1) * t, :]

    @pl.when(ci == nc - 1)
    def _():
        fr = lax.broadcasted_iota(jnp.int32, (LANES, DH_B), 0)
        fc = lax.broadcasted_iota(jnp.int32, (LANES, DH_B), 1)
        dup_rows = jnp.where((fr & (DH_B - 1)) == fc, 1.0, 0.0)
        for i, (nb, pr) in enumerate(groups):
            packed = _dot_exact_rhs(s_scr[i], dup_rows)
            s_out_ref[0, nb, 2 * pr:2 * pr + 2] = packed.reshape(2, DH_B, DH_B)


def _rwkv_call(pb, shift0, s_in, mu, w0a0, wlora, g2, kk, ka, rk, lng, lnb, s_stack, *, batch, seq, nbb,
               layer_in, layer):
    t = math.gcd(seq, CHUNK)
    nc = seq // t
    tok = lambda w: pl.BlockSpec((nbb, t, w), lambda b, i: (b, i, 0))
    vec = lambda r, w: pl.BlockSpec((r, w), lambda b, i: (0, 0))
    st5 = lambda l: pl.BlockSpec((1, nbb, H_B, DH_B, DH_B), lambda b, i: (l, b, 0, 0, 0))
    chained = s_stack is not None
    in_specs = [tok(D_B_IN), pl.BlockSpec((nbb, 1, D_B_IN), lambda b, i: (b, 0, 0)), st5(layer_in),
                vec(1, D_B_IN), vec(1, 2 * D_B), vec(LANES, 2 * D_B), vec(R_GATE, D_B),
                vec(1, D_B), vec(1, D_B), vec(1, D_B), vec(1, D_B), vec(1, D_B)]
    args = [pb, shift0, s_in, mu, w0a0, wlora, g2, kk, ka, rk, lng, lnb]
    if chained:
        in_specs.append(pl.BlockSpec(memory_space=pl.ANY))
        args.append(s_stack)
    return pl.pallas_call(
        functools.partial(_rwkv_kernel, t=t, nc=nc, nbb=nbb, chained=chained),
        grid=(batch // nbb, nc),
        in_specs=in_specs,
        out_specs=[tok(D_B), st5(layer)],
        out_shape=[jax.ShapeDtypeStruct((batch, seq, D_B), F32),
                   jax.ShapeDtypeStruct((DEPTH, batch, H_B, DH_B, DH_B), F32)],
        scratch_shapes=[pltpu.VMEM((nbb, SHIFT_PAD + t, D_B_IN), F32),
                        pltpu.VMEM((nbb * N_PAIRS, LANES, LANES), F32)],
        input_output_aliases={len(args) - 1: 1} if chained else {},
        compiler_params=pltpu.CompilerParams(dimension_semantics=("parallel", "arbitrary"),
                                             vmem_limit_bytes=VMEM_LIMIT),
        name="rwkv",
    )(*args)


ROUTE_ROWS = 8


def _outproj_kernel(x_ref, ha_ref, hb_ref, wo_ref, g_ref, b_ref, wr_ref, br_ref, x1_ref, route_ref,
                    cnt_ref, *, tm):
    y = _dot(ha_ref[...], wo_ref[0, 0:D_A, :]) + _dot(hb_ref[...], wo_ref[0, D_A:D_A + D_B, :])
    x1 = _layer_norm(ALPHA * x_ref[...] + y, g_ref[...], b_ref[...], LN_EPS)
    x1_ref[...] = x1
    logits = lax.dot_general(wr_ref[...], x1, NT, precision=lax.Precision.HIGHEST,
                             preferred_element_type=F32)
    mx = jnp.max(logits, axis=0, keepdims=True)
    ex = jnp.exp(logits - mx)
    probs = ex / jnp.sum(ex, axis=0, keepdims=True)
    sel = probs + br_ref[...]
    neg = -jnp.inf

    def top2(rows):
        m1 = functools.reduce(jnp.maximum, rows)
        i1 = jnp.full(m1.shape, len(rows) - 1, jnp.int32)
        for j in range(len(rows) - 2, -1, -1):
            i1 = jnp.where(rows[j] == m1, j, i1)
        rest = [jnp.where(i1 == j, neg, rows[j]) for j in range(len(rows))]
        m2 = functools.reduce(jnp.maximum, rest)
        i2 = jnp.full(m2.shape, len(rows) - 1, jnp.int32)
        for j in range(len(rows) - 2, -1, -1):
            i2 = jnp.where(rest[j] == m2, j, i2)
        return m1, i1, m2, i2

    g_score, g_i1, g_i2 = [], [], []
    for gidx in range(N_GROUPS):
        rows = [sel[gidx * EXPERTS_PER_GROUP + j:gidx * EXPERTS_PER_GROUP + j + 1, :]
                for j in range(EXPERTS_PER_GROUP)]
        m1, i1, m2, i2 = top2(rows)
        g_score.append(m1 + m2)
        g_i1.append(i1)
        g_i2.append(i2)
    best = functools.reduce(jnp.maximum, g_score)
    grp = jnp.full(best.shape, N_GROUPS - 1, jnp.int32)
    for gidx in range(N_GROUPS - 2, -1, -1):
        grp = jnp.where(g_score[gidx] == best, gidx, grp)
    l1 = g_i1[N_GROUPS - 1]
    l2 = g_i2[N_GROUPS - 1]
    for gidx in range(N_GROUPS - 2, -1, -1):
        l1 = jnp.where(grp == gidx, g_i1[gidx], l1)
        l2 = jnp.where(grp == gidx, g_i2[gidx], l2)
    e1 = grp * EXPERTS_PER_GROUP + l1
    e2 = grp * EXPERTS_PER_GROUP + l2
    e_iota = lax.broadcasted_iota(jnp.int32, (N_EXPERTS, tm), 0)
    p1 = jnp.sum(jnp.where(e_iota == e1, probs, 0.0), axis=0, keepdims=True)
    p2 = jnp.sum(jnp.where(e_iota == e2, probs, 0.0), axis=0, keepdims=True)
    tot = p1 + p2
    r8 = lax.broadcasted_iota(jnp.int32, (ROUTE_ROWS, tm), 0)
    route_ref[...] = jnp.where(r8 == 0, e1.astype(F32),
                               jnp.where(r8 == 1, e2.astype(F32),
                                         jnp.where(r8 == 2, p1 / tot, jnp.where(r8 == 3, p2 / tot, 0.0))))
    picked = jnp.where((e_iota == e1) | (e_iota == e2), 1.0, 0.0)
    cnt_ref[0] = jnp.broadcast_to(jnp.sum(picked, axis=1, keepdims=True), (N_EXPERTS, LANES))


def _outproj_call(x, ha, hb, wo, g, b, wr_t, br, tm, layer):
    n = x.shape[0]
    row = lambda w: pl.BlockSpec((tm, w), lambda i: (i, 0))
    full = lambda r, w: pl.BlockSpec((r, w), lambda i: (0, 0))
    return pl.pallas_call(
        functools.partial(_outproj_kernel, tm=tm),
        grid=(n // tm,),
        in_specs=[row(D_MODEL), row(D_A), row(D_B),
                  pl.BlockSpec((1, D_MODEL, D_MODEL), lambda i: (layer, 0, 0)), full(1, D_MODEL),
                  full(1, D_MODEL), full(N_EXPERTS, D_MODEL), full(N_EXPERTS, 1)],
        out_specs=[row(D_MODEL), pl.BlockSpec((ROUTE_ROWS, tm), lambda i: (0, i)),
                   pl.BlockSpec((1, N_EXPERTS, LANES), lambda i: (i, 0, 0))],
        out_shape=[jax.ShapeDtypeStruct((n, D_MODEL), F32), jax.ShapeDtypeStruct((ROUTE_ROWS, n), F32),
                   jax.ShapeDtypeStruct((n // tm, N_EXPERTS, LANES), F32)],
        compiler_params=pltpu.CompilerParams(dimension_semantics=("parallel",),
                                             vmem_limit_bytes=VMEM_LIMIT),
        name="outproj",
    )(x, ha, hb, wo, g, b, wr_t, br)


MOE_BM = 128
MOE_CH = 512
MOE_CCH = 1024
MOE_GROUP = 4
MOE_NBUF = 3


def _moe_max_blocks(t):
    return -(-2 * t // MOE_BM) + N_EXPERTS - 1


def _moe_sort_kernel(nblk_ref, x_ref, route_ref, padoff_ref, tri_ref, xs_ref, cols_ref, *, t, n_tiles,
                     n_chunks):
    i = pl.program_id(0)

    @pl.when(i >= n_tiles)
    def _():
        xs_ref[...] = jnp.zeros_like(xs_ref)

    @pl.when(i < n_tiles)
    def _():
        s_used = nblk_ref[jnp.minimum(i, n_tiles - 1)] * MOE_BM
        xb = x_ref[...].astype(BF16)
        route = route_ref[...]
        e_iota = lax.broadcasted_iota(jnp.int32, (N_EXPERTS, t), 0)
        a1 = e_iota == route[0:1].astype(jnp.int32)
        a2 = e_iota == route[1:2].astype(jnp.int32)
        picked = jnp.where(a1 | a2, 1.0, 0.0).astype(BF16)
        rank = jnp.dot(picked, tri_ref[...], preferred_element_type=F32)
        base = padoff_ref[0][:, 0:1] + rank
        slot1 = jnp.sum(jnp.where(a1, base, 0.0), axis=0, keepdims=True)
        slot2 = jnp.sum(jnp.where(a2, base, 0.0), axis=0, keepdims=True)
        r8 = lax.broadcasted_iota(jnp.int32, (ROUTE_ROWS, t), 0)
        rows = jnp.where(r8 == 0, slot1, jnp.where(r8 == 1, slot2,
                                                   jnp.where(r8 == 2, route[2:3],
                                                             jnp.where(r8 == 3, route[3:4], 0.0))))
        padded = jnp.concatenate([rows, jnp.zeros((LANES - ROUTE_ROWS, t), F32)], axis=0)
        cols_ref[...] = padded.T
        s1 = slot1.astype(jnp.int32)
        s2 = slot2.astype(jnp.int32)
        for c in range(n_chunks):
            used = c * MOE_CH < s_used

            @pl.when(used)
            def _():
                s_iota = lax.broadcasted_iota(jnp.int32, (MOE_CH, t), 0) + c * MOE_CH
                onehot = jnp.where((s_iota == s1) | (s_iota == s2), 1.0, 0.0).astype(BF16)
                xs_ref[0, c * MOE_CH:(c + 1) * MOE_CH, :] = jnp.dot(
                    onehot, xb, preferred_element_type=F32).astype(BF16)

            @pl.when(jnp.logical_not(used))
            def _():
                xs_ref[0, c * MOE_CH:(c + 1) * MOE_CH, :] = jnp.zeros((MOE_CH, D_MODEL), BF16)


def _moe_expert_kernel(*refs, n_streams):
    sched = refs[:3 * n_streams]
    xs_list = refs[3 * n_streams:4 * n_streams]
    wg_ref, wu_ref, wd_ref = refs[4 * n_streams:4 * n_streams + 3]
    ys_list = refs[4 * n_streams + 3:5 * n_streams + 3]
    wg_scr, wu_scr, wd_scr = refs[5 * n_streams + 3:5 * n_streams + 6]
    bufs = refs[5 * n_streams + 6:]
    e = pl.program_id(0)

    @pl.when(e == 0)
    def _():
        for s in range(n_streams):
            bufs[4 * s][...] = jnp.zeros_like(bufs[4 * s])

    loops = []
    for s in range(n_streams):
        sblk_ref, ebase_ref, ecnt_ref = sched[3 * s:3 * s + 3]
        loops.append(_moe_expert_blocks(sblk_ref, ebase_ref[e], ecnt_ref[e], xs_list[s], ys_list[s],
                                        wg_scr, wu_scr, wd_scr, *bufs[4 * s:4 * s + 4]))
    wg_scr[...] = wg_ref[0, 0].astype(BF16)
    wu_scr[...] = wu_ref[0, 0].astype(BF16)
    wd_scr[...] = wd_ref[0, 0].astype(BF16)
    for run_blocks in loops:
        run_blocks()


def _moe_expert_blocks(sblk_ref, base, cnt, xs_hbm, ys_hbm, wg_scr, wu_scr, wd_scr, xbuf, ybuf, in_sem,
                       out_sem):
    npair = (cnt + MOE_GROUP - 1) // MOE_GROUP

    def rows(k):
        return pl.ds(pl.multiple_of(sblk_ref[base + k] * MOE_BM, MOE_BM), MOE_BM)

    def in_copy(k, slot, half):
        return pltpu.make_async_copy(xs_hbm.at[rows(k), :], xbuf.at[slot, half * MOE_BM:(half + 1) * MOE_BM, :],
                                     in_sem.at[slot, half])

    def out_copy(k, slot, half):
        return pltpu.make_async_copy(ybuf.at[slot, half * MOE_BM:(half + 1) * MOE_BM, :], ys_hbm.at[rows(k), :],
                                     out_sem.at[slot, half])

    def for_pair(p, make, act):
        slot = p % MOE_NBUF
        act(make(MOE_GROUP * p, slot, 0))
        for part in range(1, MOE_GROUP):
            @pl.when(MOE_GROUP * p + part < cnt)
            def _():
                act(make(MOE_GROUP * p + part, slot, part))

    start = lambda c: c.start()
    wait = lambda c: c.wait()

    for i in range(MOE_NBUF - 1):
        @pl.when(i < npair)
        def _():
            for_pair(i, in_copy, start)

    def pair(p, carry):
        slot = p % MOE_NBUF
        for_pair(p, in_copy, wait)
        ahead = p + MOE_NBUF - 1

        @pl.when(ahead < npair)
        def _():
            for_pair(ahead, in_copy, start)

        @pl.when(p >= MOE_NBUF)
        def _():
            for_pair(p - MOE_NBUF, out_copy, wait)

        def run_expert(n_rows):
            xblk = xbuf[slot, 0:n_rows, :]
            hid = _silu(jnp.dot(xblk, wg_scr[...], preferred_element_type=F32)) * jnp.dot(
                xblk, wu_scr[...], preferred_element_type=F32)
            y = jnp.dot(hid.astype(BF16), wd_scr[...], preferred_element_type=F32)
            ybuf[slot, 0:n_rows, :] = y.astype(BF16)

        short = cnt - MOE_GROUP * p <= MOE_GROUP // 2

        @pl.when(short)
        def _():
            run_expert(MOE_GROUP // 2 * MOE_BM)

        @pl.when(jnp.logical_not(short))
        def _():
            run_expert(MOE_GROUP * MOE_BM)

        for_pair(p, out_copy, start)
        return carry

    def run_blocks():
        lax.fori_loop(0, npair, pair, 0)
        for i in range(MOE_NBUF):
            @pl.when(npair > i)
            def _():
                for_pair(npair - 1 - i, out_copy, wait)

    return run_blocks


def _moe_combine_kernel(nblk_ref, x_ref, cols_ref, ys_ref, g_ref, b_ref, o_ref, acc_scr, *, t, n_chunks):
    s_used = nblk_ref[pl.program_id(0)] * MOE_BM
    cols = cols_ref[...]
    s1c = cols[:, 0:1].astype(jnp.int32)
    s2c = cols[:, 1:2].astype(jnp.int32)
    w1c = cols[:, 2:3]
    w2c = cols[:, 3:4]
    for c in range(n_chunks):
        @pl.when(c * MOE_CCH < s_used)
        def _():
            l_iota = lax.broadcasted_iota(jnp.int32, (t, MOE_CCH), 1) + c * MOE_CCH
            weighted = (jnp.where(l_iota == s1c, w1c, 0.0) + jnp.where(l_iota == s2c, w2c, 0.0)).astype(BF16)
            part = jnp.dot(weighted, ys_ref[0, c * MOE_CCH:(c + 1) * MOE_CCH, :],
                           preferred_element_type=F32)
            if c == 0:
                acc_scr[...] = part
            else:
                acc_scr[...] += part
    o_ref[...] = _layer_norm(ALPHA * x_ref[...] + acc_scr[...], g_ref[...], b_ref[...], LN_EPS)


def _moe_schedule(counts, tiles_per, maxb, bpt):
    cnt = counts[:, :, 0].astype(jnp.int32)
    cnt = cnt.reshape(-1, tiles_per, N_EXPERTS).sum(axis=1)
    n_tiles = cnt.shape[0]
    nblk = (cnt + MOE_BM - 1) // MOE_BM
    end = jnp.cumsum(nblk, axis=-1)
    start = end - nblk
    total = end[:, -1]
    padoff = jnp.broadcast_to((start * MOE_BM).astype(F32)[:, :, None], (n_tiles, N_EXPERTS, LANES))
    per_e = jnp.sum(nblk, axis=0)
    e_end = jnp.cumsum(per_e)
    e_base = e_end - per_e
    e_ids = jnp.arange(N_EXPERTS, dtype=jnp.int32)
    s_idx = jnp.arange(n_tiles * maxb, dtype=jnp.int32)
    last_e = jnp.max(jnp.where(per_e > 0, e_ids, 0))
    se = jnp.minimum(jnp.sum((s_idx[:, None] >= e_end[None, :]).astype(jnp.int32), axis=-1), last_e)
    oh_e = se[:, None] == e_ids[None, :]
    r = s_idx - jnp.sum(jnp.where(oh_e, e_base[None, :], 0), axis=-1)
    of_e = lambda a: jnp.sum(jnp.where(oh_e[:, None, :], a[None, :, :], 0), axis=-1)
    incl_s = of_e(jnp.cumsum(nblk, axis=0))
    tile = jnp.minimum(jnp.sum((incl_s <= r[:, None]).astype(jnp.int32), axis=-1), n_tiles - 1)
    oh_t = tile[:, None] == jnp.arange(n_tiles, dtype=jnp.int32)[None, :]
    of_t = lambda a: jnp.sum(jnp.where(oh_t, a, 0), axis=-1)
    j = of_t(of_e(start)) + r - (of_t(incl_s) - of_t(of_e(nblk)))
    sblk = jnp.where(s_idx < e_end[-1], tile * bpt + j, n_tiles * bpt)
    return total, padoff, sblk, e_base, per_e


def _moe_sort_call(x, route, counts, tm):
    n = x.shape[0]
    t = 1024 if n % 1024 == 0 else n
    n_tiles = n // t
    maxb = _moe_max_blocks(t)
    s_alloc = -(-maxb * MOE_BM // MOE_CCH) * MOE_CCH
    n_chunks = s_alloc // MOE_CH
    bpt = s_alloc // MOE_BM
    nblk, padoff, sblk, e_base, e_cnt = _moe_schedule(counts, t // tm, maxb, bpt)
    tri = jnp.triu(jnp.ones((t, t), BF16), k=1)
    last = n_tiles - 1
    xs, cols = pl.pallas_call(
        functools.partial(_moe_sort_kernel, t=t, n_tiles=n_tiles, n_chunks=n_chunks),
        grid_spec=pltpu.PrefetchScalarGridSpec(
            num_scalar_prefetch=1,
            grid=(n_tiles + 1,),
            in_specs=[pl.BlockSpec((t, D_MODEL), lambda i, nb: (jnp.minimum(i, last), 0)),
                      pl.BlockSpec((ROUTE_ROWS, t), lambda i, nb: (0, jnp.minimum(i, last))),
                      pl.BlockSpec((1, N_EXPERTS, LANES), lambda i, nb: (jnp.minimum(i, last), 0, 0)),
                      pl.BlockSpec((t, t), lambda i, nb: (0, 0))],
            out_specs=[pl.BlockSpec((1, s_alloc, D_MODEL), lambda i, nb: (i, 0, 0)),
                       pl.BlockSpec((t, LANES), lambda i, nb: (jnp.minimum(i, last), 0))]),
        out_shape=[jax.ShapeDtypeStruct((n_tiles + 1, s_alloc, D_MODEL), BF16),
                   jax.ShapeDtypeStruct((n, LANES), F32)],
        compiler_params=pltpu.CompilerParams(dimension_semantics=("arbitrary",),
                                             vmem_limit_bytes=VMEM_LIMIT),
        name="moe_sort",
    )(nblk, x, route, padoff, tri)
    return dict(xs=xs.reshape((n_tiles + 1) * s_alloc, D_MODEL), cols=cols, nblk=nblk, sblk=sblk,
                e_base=e_base, e_cnt=e_cnt, t=t, n_tiles=n_tiles, s_alloc=s_alloc)


def _moe_expert_call(sorted_streams, wg, wu, wd, layer):
    ns = len(sorted_streams)
    wspec = lambda r, c: pl.BlockSpec((1, 1, r, c), lambda e, *_: (layer, e, 0, 0))
    sched = [s[k] for s in sorted_streams for k in ("sblk", "e_base", "e_cnt")]
    xs = [s["xs"] for s in sorted_streams]
    return pl.pallas_call(
        functools.partial(_moe_expert_kernel, n_streams=ns),
        grid_spec=pltpu.PrefetchScalarGridSpec(
            num_scalar_prefetch=3 * ns,
            grid=(N_EXPERTS,),
            in_specs=[pl.BlockSpec(memory_space=pl.ANY)] * ns
            + [wspec(D_MODEL, D_EXP), wspec(D_MODEL, D_EXP), wspec(D_EXP, D_MODEL)],
            out_specs=[pl.BlockSpec(memory_space=pl.ANY)] * ns,
            scratch_shapes=[pltpu.VMEM((D_MODEL, D_EXP), BF16), pltpu.VMEM((D_MODEL, D_EXP), BF16),
                            pltpu.VMEM((D_EXP, D_MODEL), BF16)]
            + [pltpu.VMEM((MOE_NBUF, MOE_GROUP * MOE_BM, D_MODEL), BF16),
               pltpu.VMEM((MOE_NBUF, MOE_GROUP * MOE_BM, D_MODEL), BF16),
               pltpu.SemaphoreType.DMA((MOE_NBUF, MOE_GROUP)),
               pltpu.SemaphoreType.DMA((MOE_NBUF, MOE_GROUP))] * ns),
        out_shape=[jax.ShapeDtypeStruct(x.shape, BF16) for x in xs],
        input_output_aliases={3 * ns + i: i for i in range(ns)},
        compiler_params=pltpu.CompilerParams(dimension_semantics=("arbitrary",),
                                             vmem_limit_bytes=VMEM_LIMIT),
        name="moe_experts",
    )(*sched, *xs, wg, wu, wd)


def _moe_combine_call(x, srt, ys, g, b):
    n = x.shape[0]
    t, n_tiles, s_alloc = srt["t"], srt["n_tiles"], srt["s_alloc"]
    nblk, cols = srt["nblk"], srt["cols"]
    vec = pl.BlockSpec((1, D_MODEL), lambda i, nb: (0, 0))
    return pl.pallas_call(
        functools.partial(_moe_combine_kernel, t=t, n_chunks=s_alloc // MOE_CCH),
        grid_spec=pltpu.PrefetchScalarGridSpec(
            num_scalar_prefetch=1,
            grid=(n_tiles,),
            in_specs=[pl.BlockSpec((t, D_MODEL), lambda i, nb: (i, 0)),
                      pl.BlockSpec((t, LANES), lambda i, nb: (i, 0)),
                      pl.BlockSpec((1, s_alloc, D_MODEL), lambda i, nb: (i, 0, 0)), vec, vec],
            out_specs=pl.BlockSpec((t, D_MODEL), lambda i, nb: (i, 0)),
            scratch_shapes=[pltpu.VMEM((t, D_MODEL), F32)]),
        out_shape=jax.ShapeDtypeStruct((n, D_MODEL), F32),
        compiler_params=pltpu.CompilerParams(dimension_semantics=("parallel",),
                                             vmem_limit_bytes=VMEM_LIMIT),
        name="moe_combine",
    )(nblk, x, cols, ys.reshape(n_tiles + 1, s_alloc, D_MODEL), g, b)


def _row_tile(n):
    return 512 if n % 512 == 0 else n


def _seqs_per_step(batch, want):
    return want if batch % want == 0 else 1


def kernel(x_prompt, x_sample, state_mlstm_C, state_mlstm_n, state_mlstm_m, state_mlstm_conv, state_rwkv_S, state_rwkv_shift, ln0_g, ln0_b, w_in, conv_w, conv_b, b_i, b_f, gn_a_g, mu_shift, w0, w2, a0, a2, g2, k_k, k_a, r_k, lnx_g, lnx_b, w_out, ln1_g, ln1_b, w_router, b_router, we_gate, we_up, we_down, ln2_g, ln2_b):
    d_a_in = 4 * D_A + 2 * H_A
    zpad = jnp.zeros((DEPTH, D_MODEL, LANES - H_A), F32)
    w_cat = jnp.concatenate(
        [w_in[:, :, 0:4 * D_A], w_in[:, :, d_a_in:], w_in[:, :, 4 * D_A:4 * D_A + H_A], zpad,
         w_in[:, :, 4 * D_A + H_A:d_a_in], zpad], axis=-1).astype(BF16)
    bif = jnp.zeros((DEPTH, 1, W_IF_COLS), F32)
    bif = bif.at[:, 0, 0:H_A].set(b_i).at[:, 0, LANES:LANES + H_A].set(b_f)
    wlora = jnp.zeros((DEPTH, LANES, 2 * D_B), F32)
    wlora = wlora.at[:, 0:R_DECAY, 0:D_B].set(w2).at[:, R_DECAY:, D_B:].set(a2).astype(BF16)
    w0a0 = jnp.concatenate([w0, a0], axis=-1)[:, None, :]
    g2b = g2.astype(BF16)
    wob = w_out.astype(BF16)
    wr_t = w_router.T
    br = b_router[:, None]
    r1 = lambda v: v[None, :]

    def new_stream(x3, states):
        nb, seq, _ = x3.shape
        return dict(nb=nb, seq=seq, n=nb * seq, tm=_row_tile(nb * seq), states=states,
                    x=x3.reshape(nb * seq, D_MODEL), outs=[],
                    c_stack=jnp.zeros((DEPTH, nb, H_A, DH_A, DH_A), F32),
                    s_stack=jnp.zeros((DEPTH, nb, H_B, DH_B, DH_B), F32))

    def mixers_and_router(st, l):
        nb, seq, n, tm, states, x = (st[k] for k in ("nb", "seq", "n", "tm", "states", "x"))
        c_stack, s_stack = st["c_stack"], st["s_stack"]
        rwkv_nbb = _seqs_per_step(nb, 4 if seq >= CHUNK else 16)
        mlstm_nbb = _seqs_per_step(nb, 2 if seq >= CHUNK else 16)
        if states is None:
            c_in = jnp.zeros((1, nb, H_A, DH_A, DH_A), F32)
            s_in = jnp.zeros((1, nb, H_B, DH_B, DH_B), F32)
            layer_in = 0
            n0 = jnp.zeros((nb, H_A, DH_A), F32)
            m0 = jnp.zeros((nb, H_A), F32)
            conv0 = jnp.zeros((nb, CONV_W - 1, 2 * D_A), F32)
            shift0 = jnp.zeros((nb, D_B_IN), F32)
        else:
            c_in, s_in, layer_in = states[0], states[4], l
            n0, m0, conv0, shift0 = (states[k][l] for k in (1, 2, 3, 5))
        m0p = jnp.zeros((nb, 1, LANES), F32).at[:, 0, 0:H_A].set(m0)
        if l == 0:
            pa, pb, pif, x = _inproj_call(x, w_cat, tm, l, norm=(r1(ln0_g), r1(ln0_b)))
        else:
            pa, pb, pif = _inproj_call(x, w_cat, tm, l)
        ha, c_stack, n_new, m_new = _mlstm_call(
            pa.reshape(nb, seq, W_A_COLS), pif.reshape(nb, seq, W_IF_COLS), conv0, c_in, n0, m0p,
            conv_w[l], r1(conv_b[l]), bif[l], r1(gn_a_g[l]), c_stack, batch=nb, seq=seq,
            nbb=mlstm_nbb, layer_in=layer_in, layer=l)
        ha = ha.reshape(n, D_A)
        hb, s_stack = _rwkv_call(
            pb.reshape(nb, seq, D_B_IN), shift0[:, None, :], s_in, r1(mu_shift[l]), w0a0[l], wlora[l],
            g2b[l], r1(k_k[l]), r1(k_a[l]), r1(r_k[l].reshape(D_B)), r1(lnx_g[l]), r1(lnx_b[l]),
            s_stack, batch=nb, seq=seq, nbb=rwkv_nbb, layer_in=layer_in, layer=l)
        hb = hb.reshape(n, D_B)
        x1, route, counts = _outproj_call(x, ha, hb, wob, r1(ln1_g[l]), r1(ln1_b[l]), wr_t, br, tm, l)
        pa3 = pa.reshape(nb, seq, W_A_COLS)
        full = jnp.concatenate([conv0, pa3[:, :, 0:2 * D_A]], axis=1) if seq < CONV_W - 1 else pa3[:, :, 0:2 * D_A]
        conv_new = full[:, -(CONV_W - 1):, :]
        shift_new = pb.reshape(nb, seq, D_B_IN)[:, -1, :]
        st["outs"].append((n_new, m_new[:, 0, 0:H_A], conv_new, shift_new))
        st["c_stack"], st["s_stack"] = c_stack, s_stack
        return x1, route, counts

    streams = [new_stream(x_prompt, None),
               new_stream(x_sample, (state_mlstm_C, state_mlstm_n, state_mlstm_m, state_mlstm_conv,
                                     state_rwkv_S, state_rwkv_shift))]
    for l in range(DEPTH):
        routed = [mixers_and_router(st, l) for st in streams]
        srt = [_moe_sort_call(x1, route, counts, st["tm"]) for (x1, route, counts), st in zip(routed, streams)]
        ys = _moe_expert_call(srt, we_gate, we_up, we_down, l)
        for st, (x1, _, _), s, y in zip(streams, routed, srt, ys):
            st["x"] = _moe_combine_call(x1, s, y, r1(ln2_g[l]), r1(ln2_b[l]))

    def finish(st):
        n_all, m_all, conv_all, shift_all = (jnp.stack(s) for s in zip(*st["outs"]))
        return (st["x"].reshape(st["nb"], st["seq"], D_MODEL),
                (st["c_stack"], n_all, m_all, conv_all, st["s_stack"], shift_all))

    y_prompt, (p_c, p_n, p_m, p_conv, p_s, p_shift) = finish(streams[0])
    y_sample, (s_c, s_n, s_m, s_conv, s_s, s_shift) = finish(streams[1])
    return (y_prompt, y_sample, p_c, p_n, p_m, p_conv, p_s, p_shift, s_c, s_n, s_m, s_conv, s_s, s_shift)
```
